```python
import numpy as np
import jax, jax.numpy as jnp
from jax import lax

D_MODEL = 1024
BATCH = 8
SEQ = 2048
DEPTH = 4

HEAD_DIM = 64
ROPE_THETA = 10000.0
RMS_EPS = 1e-6
A_HEADS = 8
A_KV = 1
A_WINDOW = 128
A_QBLOCK = 128
B_HEADS = 8
B_KV = 2
CMP_LEN = 32
CMP_STRIDE = 16
CMP_HIDDEN = 256
SLC_LEN = 64
SLC_TOPN = 8
WIN_LEN = 512
B_QBLOCK = 128
FORCE_BONUS = 1e4
C_HEADS = 16
MOBA_BLOCK = 256
MOBA_TOPK = 3
C_QBLOCK = 32

A_QW = A_HEADS * HEAD_DIM
A_KVW = A_KV * HEAD_DIM
B_QW = B_HEADS * HEAD_DIM
B_KVW = B_KV * HEAD_DIM
B_GATEW = 3 * B_HEADS
C_W = C_HEADS * HEAD_DIM
EVEN_SIZES = (A_QW, A_KVW, A_KVW, A_QW, B_QW, B_KVW, B_KVW, B_KVW, B_KVW, B_KVW, B_KVW, B_GATEW, B_QW)
EVEN_WIDTH = A_QW + 2 * A_KVW + A_QW + B_QW + 6 * B_KVW + B_GATEW + B_QW
EVEN_OUT = A_QW + B_QW
ODD_SIZES = (C_W, C_W, C_W, C_W)
ODD_WIDTH = 4 * C_W
N_EVEN = (DEPTH + 1) // 2
N_ODD = DEPTH // 2

kernel_name = "hybrid_swa_nsa_moba_adaln_trunk"


def _offsets(sizes):
    out, acc = [], 0
    for s in sizes[:-1]:
        acc += s
        out.append(acc)
    return out


def rms_norm(x, g):
    xf = x.astype(jnp.float32)
    y = xf * lax.rsqrt(jnp.mean(xf * xf, axis=-1, keepdims=True) + RMS_EPS)
    return (y * g.astype(jnp.float32)).astype(x.dtype)


def rope_tables(n):
    inv = ROPE_THETA ** (-jnp.arange(0, HEAD_DIM, 2, dtype=jnp.float32) / HEAD_DIM)
    ang = jnp.arange(n, dtype=jnp.float32)[:, None] * inv[None, :]
    return jnp.cos(ang), jnp.sin(ang)


def apply_rope(x, cos, sin):
    half = HEAD_DIM // 2
    xf = x.astype(jnp.float32)
    x1, x2 = xf[..., :half], xf[..., half:]
    return jnp.concatenate([x1 * cos - x2 * sin, x2 * cos + x1 * sin], axis=-1).astype(x.dtype)


def split_heads(t, n):
    b, s, _ = t.shape
    return t.reshape(b, s, n, HEAD_DIM).transpose(0, 2, 1, 3)


def merge_heads(o):
    b, h, s, d = o.shape
    return o.transpose(0, 2, 1, 3).reshape(b, s, h * d)


def masked_softmax(s, mask, sink=None):
    s = jnp.where(mask, s, -jnp.inf)
    m = jnp.max(s, axis=-1, keepdims=True)
    if sink is not None:
        m = jnp.maximum(m, sink)
    m = jnp.where(jnp.isfinite(m), m, 0.0)
    e = jnp.where(mask, jnp.exp(s - m), 0.0)
    den = jnp.sum(e, axis=-1, keepdims=True)
    if sink is not None:
        den = den + jnp.exp(sink - m)
    return e / jnp.where(den > 0, den, 1.0)


def gather_blocks(kb, idx):
    return jax.vmap(jax.vmap(lambda a, i: a[i]))(kb, idx)


def banded_attention(q, k, v, window, block, sink=None):
    b, g, r, s, d = q.shape
    nb = s // block
    n_prev = -(-(window - 1) // block)
    span = (n_prev + 1) * block
    pad = n_prev * block
    kp = jnp.pad(k, ((0, 0), (0, 0), (pad, 0), (0, 0)))
    vp = jnp.pad(v, ((0, 0), (0, 0), (pad, 0), (0, 0)))
    idx = np.arange(nb)[:, None] * block + np.arange(span)[None, :]
    kb = kp[:, :, idx]
    vb = vp[:, :, idx]
    qb = q.reshape(b, g, r, nb, block, d)
    sc = jnp.einsum('bgrnqd,bgnkd->bgrnqk', qb, kb).astype(jnp.float32) * (HEAD_DIM ** -0.5)
    qpos = np.arange(nb)[:, None] * block + np.arange(block)[None, :]
    kpos = idx - pad
    diff = qpos[:, :, None] - kpos[:, None, :]
    mask = jnp.asarray((diff >= 0) & (diff < window) & (kpos[:, None, :] >= 0))
    sk = None if sink is None else sink.astype(jnp.float32).reshape(1, g, r, 1, 1, 1)
    p = masked_softmax(sc, mask, sk)
    o = jnp.einsum('bgrnqk,bgnkd->bgrnqd', p.astype(v.dtype), vb)
    return o.reshape(b, g, r, s, d)


def swa_sink_attention(qa, ka, va, sinks, cos, sin):
    b, s, _ = qa.shape
    q = apply_rope(split_heads(qa, A_HEADS), cos, sin).reshape(b, A_KV, A_HEADS // A_KV, s, HEAD_DIM)
    k = apply_rope(split_heads(ka, A_KV), cos, sin)
    v = split_heads(va, A_KV)
    o = banded_attention(q, k, v, A_WINDOW, A_QBLOCK, sinks.reshape(A_KV, A_HEADS // A_KV))
    return merge_heads(o.reshape(b, A_HEADS, s, HEAD_DIM))


def compress_mlp(blocks, pe, w1, w2):
    b, g, nc, l, d = blocks.shape
    u = (blocks + pe).reshape(b, g, nc, l * d)
    return jax.nn.gelu(u @ w1) @ w2


def nsa_attention(qb, kc, vc, ks, vs, kw, vw, gl, pe_k, w1k, w2k, pe_v, w1v, w2v, cos, sin):
    b, s, _ = qb.shape
    rr = B_HEADS // B_KV
    scale = HEAD_DIM ** -0.5
    q = apply_rope(split_heads(qb, B_HEADS), cos, sin).reshape(b, B_KV, rr, s, HEAD_DIM)
    kc = apply_rope(split_heads(kc, B_KV), cos, sin)
    ks = apply_rope(split_heads(ks, B_KV), cos, sin)
    kw = apply_rope(split_heads(kw, B_KV), cos, sin)
    vc, vs, vw = split_heads(vc, B_KV), split_heads(vs, B_KV), split_heads(vw, B_KV)

    nc = (s - CMP_LEN) // CMP_STRIDE + 1
    cidx = np.arange(nc)[:, None] * CMP_STRIDE + np.arange(CMP_LEN)[None, :]
    k_cmp = compress_mlp(kc[:, :, cidx], pe_k, w1k, w2k)
    v_cmp = compress_mlp(vc[:, :, cidx], pe_v, w1v, w2v)
    sc = jnp.einsum('bgrsd,bgcd->bgrsc', q, k_cmp).astype(jnp.float32) * scale
    mask_c = jnp.asarray(np.arange(s)[:, None] >= cidx[None, :, -1])
    p_cmp = masked_softmax(sc, mask_c)
    o_cmp = jnp.einsum('bgrsc,bgcd->bgrsd', p_cmp.astype(v_cmp.dtype), v_cmp)

    nsb = s // SLC_LEN
    cst = cidx[:, 0]
    jj = np.arange(nsb)
    overlap = ((cst[:, None] < (jj[None, :] + 1) * SLC_LEN) & (cst[:, None] + CMP_LEN > jj[None, :] * SLC_LEN)).astype(np.float32)
    imp = jnp.einsum('bgrsc,cj->bgsj', p_cmp, jnp.asarray(overlap))
    tb = np.arange(s)[:, None] // SLC_LEN
    valid = jnp.asarray(jj[None, :] <= tb)
    forced = jnp.asarray((jj[None, :] == 0) | (jj[None, :] == tb) | (jj[None, :] == tb - 1))
    score = jnp.where(valid, jnp.where(forced, FORCE_BONUS, imp), -jnp.inf)
    top_s, top_i = lax.top_k(score, min(SLC_TOPN, nsb))
    sel_ok = top_s > -jnp.inf
    n_sel = top_i.shape[-1]
    ksb = ks.reshape(b, B_KV, nsb, SLC_LEN, HEAD_DIM)
    vsb = vs.reshape(b, B_KV, nsb, SLC_LEN, HEAD_DIM)

    def sel_chunk(ci):
        start = ci * B_QBLOCK
        qc = lax.dynamic_slice_in_dim(q, start, B_QBLOCK, axis=3)
        ic = lax.dynamic_slice_in_dim(top_i, start, B_QBLOCK, axis=2)
        okc = lax.dynamic_slice_in_dim(sel_ok, start, B_QBLOCK, axis=2)
        kg = gather_blocks(ksb, ic)
        vg = gather_blocks(vsb, ic)
        sc_ = jnp.einsum('bgrqd,bgqnkd->bgrqnk', qc, kg).astype(jnp.float32) * scale
        tpos = start + jnp.arange(B_QBLOCK)
        kpos = ic[..., None] * SLC_LEN + jnp.arange(SLC_LEN)
        msk = okc[..., None] & (kpos <= tpos[None, None, :, None, None])
        m_tot = n_sel * SLC_LEN
        p = masked_softmax(sc_.reshape(b, B_KV, rr, B_QBLOCK, m_tot), msk.reshape(b, B_KV, 1, B_QBLOCK, m_tot))
        return jnp.einsum('bgrqm,bgqmd->bgrqd', p.astype(vg.dtype), vg.reshape(b, B_KV, B_QBLOCK, m_tot, HEAD_DIM))

    o_slc = lax.map(sel_chunk, jnp.arange(s // B_QBLOCK))
    o_slc = o_slc.transpose(1, 2, 3, 0, 4, 5).reshape(b, B_KV, rr, s, HEAD_DIM)

    o_win = banded_attention(q, kw, vw, WIN_LEN, B_QBLOCK)

    g = jax.nn.sigmoid(gl.astype(jnp.float32)).reshape(b, s, B_KV, rr, 3).transpose(0, 2, 3, 1, 4).astype(q.dtype)
    o = g[..., 0:1] * o_cmp + g[..., 1:2] * o_slc + g[..., 2:3] * o_win
    return merge_heads(o.reshape(b, B_HEADS, s, HEAD_DIM))


def moba_attention(qc_, kc_, vc_, cos, sin):
    b, s, _ = qc_.shape
    scale = HEAD_DIM ** -0.5
    q = apply_rope(split_heads(qc_, C_HEADS), cos, sin)
    k = apply_rope(split_heads(kc_, C_HEADS), cos, sin)
    v = split_heads(vc_, C_HEADS)
    sp = -(-s // MOBA_BLOCK) * MOBA_BLOCK
    padw = ((0, 0), (0, 0), (0, sp - s), (0, 0))
    q, k, v = jnp.pad(q, padw), jnp.pad(k, padw), jnp.pad(v, padw)
    nblk = sp // MOBA_BLOCK
    kb = k.reshape(b, C_HEADS, nblk, MOBA_BLOCK, HEAD_DIM)
    vb = v.reshape(b, C_HEADS, nblk, MOBA_BLOCK, HEAD_DIM)
    kmean = jnp.mean(kb.astype(jnp.float32), axis=3)
    gs = jnp.einsum('bhsd,bhjd->bhsj', q.astype(jnp.float32), kmean)
    past = jnp.asarray(np.arange(nblk)[None, :] < (np.arange(sp) // MOBA_BLOCK)[:, None])
    gs = jnp.where(past, gs, -jnp.inf)
    top_s, top_i = lax.top_k(gs, min(MOBA_TOPK, nblk))
    ok = top_s > -jnp.inf
    n_sel = top_i.shape[-1]

    def chunk(ci):
        start = ci * C_QBLOCK
        qc = lax.dynamic_slice_in_dim(q, start, C_QBLOCK, axis=2)
        ic = lax.dynamic_slice_in_dim(top_i, start, C_QBLOCK, axis=2)
        okc = lax.dynamic_slice_in_dim(ok, start, C_QBLOCK, axis=2)
        kg = gather_blocks(kb, ic)
        vg = gather_blocks(vb, ic)
        s_sel = jnp.einsum('bhqd,bhqnkd->bhqnk', qc, kg).astype(jnp.float32).reshape(b, C_HEADS, C_QBLOCK, n_sel * MOBA_BLOCK)
        own = (start // MOBA_BLOCK) * MOBA_BLOCK
        kown = lax.dynamic_slice_in_dim(k, own, MOBA_BLOCK, axis=2)
        vown = lax.dynamic_slice_in_dim(v, own, MOBA_BLOCK, axis=2)
        s_own = jnp.einsum('bhqd,bhkd->bhqk', qc, kown).astype(jnp.float32)
        tpos = start + jnp.arange(C_QBLOCK)
        m_own = (own + jnp.arange(MOBA_BLOCK))[None, :] <= tpos[:, None]
        m_sel = jnp.repeat(okc, MOBA_BLOCK, axis=-1)
        mask = jnp.concatenate([m_sel, jnp.broadcast_to(m_own, (b, C_HEADS, C_QBLOCK, MOBA_BLOCK))], axis=-1)
        p = masked_softmax(jnp.concatenate([s_sel, s_own], axis=-1) * scale, mask).astype(v.dtype)
        nk = n_sel * MOBA_BLOCK
        o = jnp.einsum('bhqm,bhqmd->bhqd', p[..., :nk], vg.reshape(b, C_HEADS, C_QBLOCK, nk, HEAD_DIM))
        return o + jnp.einsum('bhqk,bhkd->bhqd', p[..., nk:], vown)

    o = lax.map(chunk, jnp.arange(sp // C_QBLOCK))
    o = o.transpose(1, 2, 0, 3, 4).reshape(b, C_HEADS, sp, HEAD_DIM)[:, :, :s]
    return merge_heads(o)


def even_mixer(h, w_in, w_out, sinks, pe_k, w1k, w2k, pe_v, w1v, w2v, cos, sin):
    proj = jnp.einsum('bsd,de->bse', h, w_in)
    qa, ka, va, za, qb, kc, vc, ks, vs, kw, vw, gb, zb = jnp.split(proj, _offsets(EVEN_SIZES), axis=-1)
    oa = swa_sink_attention(qa, ka, va, sinks, cos, sin) * jax.nn.silu(za)
    ob = nsa_attention(qb, kc, vc, ks, vs, kw, vw, gb, pe_k, w1k, w2k, pe_v, w1v, w2v, cos, sin) * jax.nn.silu(zb)
    return jnp.einsum('bse,ed->bsd', jnp.concatenate([oa, ob], axis=-1), w_out)


def odd_mixer(h, w_in, w_out, cos, sin):
    proj = jnp.einsum('bsd,de->bse', h, w_in)
    q, k, v, z = jnp.split(proj, _offsets(ODD_SIZES), axis=-1)
    o = moba_attention(q, k, v, cos, sin) * jax.nn.silu(z)
    return jnp.einsum('bse,ed->bsd', o, w_out)


def setup_inputs(seed: int = 0) -> dict:
    key = jax.random.key(seed)
    ks = jax.random.split(key, 20)
    nrm = lambda k, shape, sc: jax.random.normal(k, shape, jnp.float32) * sc
    d = D_MODEL
    fl = CMP_LEN * HEAD_DIM
    return {
        "x": nrm(ks[0], (BATCH, SEQ, d), 1.0),
        "c": nrm(ks[1], (BATCH, d), 1.0),
        "w_ada": nrm(ks[2], (DEPTH, d, 3 * d), 0.5 * d ** -0.5),
        "b_ada": nrm(ks[3], (DEPTH, 3 * d), 0.01),
        "norm_g": 1.0 + nrm(ks[4], (DEPTH, d), 0.02),
        "w_in_even": nrm(ks[5], (N_EVEN, d, EVEN_WIDTH), d ** -0.5),
        "a_sinks": nrm(ks[6], (N_EVEN, A_HEADS), 0.5),
        "cmp_pe_k": nrm(ks[7], (N_EVEN, CMP_LEN, HEAD_DIM), 0.1),
        "cmp_w1_k": nrm(ks[8], (N_EVEN, fl, CMP_HIDDEN), fl ** -0.5),
        "cmp_w2_k": nrm(ks[9], (N_EVEN, CMP_HIDDEN, HEAD_DIM), CMP_HIDDEN ** -0.5),
        "cmp_pe_v": nrm(ks[10], (N_EVEN, CMP_LEN, HEAD_DIM), 0.1),
        "cmp_w1_v": nrm(ks[11], (N_EVEN, fl, CMP_HIDDEN), fl ** -0.5),
        "cmp_w2_v": nrm(ks[12], (N_EVEN, CMP_HIDDEN, HEAD_DIM), CMP_HIDDEN ** -0.5),
        "w_out_even": nrm(ks[13], (N_EVEN, EVEN_OUT, d), EVEN_OUT ** -0.5),
        "w_in_odd": nrm(ks[14], (N_ODD, d, ODD_WIDTH), d ** -0.5),
        "w_out_odd": nrm(ks[15], (N_ODD, C_W, d), C_W ** -0.5),
        "final_g": 1.0 + nrm(ks[16], (d,), 0.02),
    }


def reference(x, c, w_ada, b_ada, norm_g, w_in_even, a_sinks, cmp_pe_k, cmp_w1_k, cmp_w2_k,
              cmp_pe_v, cmp_w1_v, cmp_w2_v, w_out_even, w_in_odd, w_out_odd, final_g):
    cos, sin = rope_tables(x.shape[1])
    c_act = jax.nn.silu(c)
    for layer in range(DEPTH):
        ada = jnp.einsum('bd,de->be', c_act, w_ada[layer]) + b_ada[layer]
        shift, scale, gate = jnp.split(ada, 3, axis=-1)
        h = rms_norm(x, norm_g[layer]) * (1.0 + scale[:, None, :]) + shift[:, None, :]
        i = layer // 2
        if layer % 2 == 0:
            y = even_mixer(h, w_in_even[i], w_out_even[i], a_sinks[i], cmp_pe_k[i], cmp_w1_k[i], cmp_w2_k[i],
                           cmp_pe_v[i], cmp_w1_v[i], cmp_w2_v[i], cos, sin)
        else:
            y = odd_mixer(h, w_in_odd[i], w_out_odd[i], cos, sin)
        x = x + gate[:, None, :] * y
    return rms_norm(x, final_g)
```

```python
import functools

import numpy as np
import jax
import jax.numpy as jnp
from jax import lax
from jax.experimental import pallas as pl
from jax.experimental.pallas import tpu as pltpu

D_MODEL = 1024
HEAD_DIM = 64
PAIR = 2 * HEAD_DIM
ROPE_THETA = 10000.0
RMS_EPS = 1e-6
A_HEADS = 8
A_WINDOW = 128
B_HEADS = 8
B_KV = 2
B_REP = B_HEADS // B_KV
CMP_LEN = 32
CMP_STRIDE = 16
CMP_HIDDEN = 256
SLC_LEN = 64
SLC_TOPN = 8
WIN_LEN = 512
FORCE_BONUS = 1e4
C_HEADS = 16
MOBA_BLOCK = 256
MOBA_TOPK = 3
Q_SCALE = HEAD_DIM ** -0.5

NEG = -1e30
MXU_COLS = 256
ROW_TILE = 512
ATT_TQ = 128
VMEM_LIMIT = 48 * 1024 * 1024

BF16 = jnp.bfloat16
F32 = jnp.float32


def _dot_nt(a, b):
    return lax.dot_general(a, b, (((1,), (1,)), ((), ())), preferred_element_type=F32)


def _dot(a, b):
    return jnp.dot(a, b, preferred_element_type=F32)


def _cparams(sem):
    return pltpu.CompilerParams(dimension_semantics=sem, vmem_limit_bytes=VMEM_LIMIT)


def _ada_kernel(c_ref, w_ref, b_ref, o_ref):
    c = c_ref[...]
    ca = (c * jax.nn.sigmoid(c)).astype(BF16)
    o_ref[0, 0] = _dot(ca, w_ref[0].astype(BF16)) + b_ref[0, 0]


def _ada(c, w_ada, b_ada):
    depth, d, _ = w_ada.shape
    nb = c.shape[0]
    b4 = b_ada.reshape(depth, 3, 1, d)
    out = pl.pallas_call(
        _ada_kernel,
        grid=(depth, 3),
        in_specs=[
            pl.BlockSpec((nb, d), lambda l, j: (0, 0)),
            pl.BlockSpec((1, d, d), lambda l, j: (l, 0, j)),
            pl.BlockSpec((1, 1, 1, d), lambda l, j: (l, j, 0, 0)),
        ],
        out_specs=pl.BlockSpec((1, 1, nb, d), lambda l, j: (l, j, 0, 0)),
        out_shape=jax.ShapeDtypeStruct((depth, 3, nb, d), F32),
        compiler_params=_cparams(("arbitrary", "arbitrary")),
        name="ada",
    )(c, w_ada, b4)
    return out.reshape(depth, 3, nb, 1, d)


def _rope(a, cos, sin_signed):
    w = a.shape[-1]
    lane = lax.broadcasted_iota(jnp.int32, a.shape, 1)
    first_half = (lane % HEAD_DIM) < (HEAD_DIM // 2)
    partner = jnp.where(first_half, pltpu.roll(a, w - HEAD_DIM // 2, 1), pltpu.roll(a, HEAD_DIM // 2, 1))
    return a * cos + partner * sin_signed


def _proj_kernel(groups, moba_mean, x_ref, shift_ref, scale_ref, g_ref, cos_ref, sin_ref, w_ref, *out_refs):
    x = x_ref[...]
    y = x * lax.rsqrt(jnp.mean(x * x, axis=-1, keepdims=True) + RMS_EPS)
    h = (y * g_ref[...]) * (1.0 + scale_ref[0, 0, 0]) + shift_ref[0, 0, 0]
    hb = h.astype(BF16)
    cos = jnp.concatenate([cos_ref[...]] * (MXU_COLS // PAIR), axis=-1)
    sin = jnp.concatenate([sin_ref[...]] * (MXU_COLS // PAIR), axis=-1)
    for (w_start, width, kind, out_idx, out_start) in groups:
        o_ref = out_refs[out_idx]
        for c0 in range(0, width, MXU_COLS):
            cw = min(MXU_COLS, width - c0)
            acc = _dot(hb, w_ref[:, w_start + c0:w_start + c0 + cw])
            if kind in ("rope", "rope_mean"):
                acc = _rope(acc, cos[:, :cw], sin[:, :cw])
            elif kind == "silu":
                acc = acc * jax.nn.sigmoid(acc)
            elif kind == "sigmoid":
                acc = jax.nn.sigmoid(acc)
            o_ref[:, out_start + c0:out_start + c0 + cw] = acc.astype(o_ref.dtype)
            if kind == "rope_mean":
                km_ref = out_refs[moba_mean]
                for j in range(acc.shape[0] // MOBA_BLOCK):
                    blk = acc[j * MOBA_BLOCK:(j + 1) * MOBA_BLOCK]
                    km_ref[0, j, :, c0:c0 + cw] = jnp.sum(blk, axis=0, keepdims=True) * (1.0 / MOBA_BLOCK)


def _proj(x2, ada, layer, norm_g, cos_t, sin_t, w, groups, outs, seq, moba_mean=None):
    rows, d = x2.shape
    tm = ROW_TILE
    per_b = seq // tm
    out_shape = [jax.ShapeDtypeStruct((rows, wd), dt) for (wd, dt) in outs]
    out_specs = [pl.BlockSpec((tm, wd), lambda i: (i, 0)) for (wd, _) in outs]
    if moba_mean is not None:
        nb = rows // seq
        out_shape.append(jax.ShapeDtypeStruct((nb, seq // MOBA_BLOCK, 1, C_HEADS * HEAD_DIM), F32))
        out_specs.append(pl.BlockSpec((1, tm // MOBA_BLOCK, 1, C_HEADS * HEAD_DIM),
                                      lambda i: (i // per_b, i % per_b, 0, 0)))
    return pl.pallas_call(
        functools.partial(_proj_kernel, groups, moba_mean),
        grid=(rows // tm,),
        in_specs=[
            pl.BlockSpec((tm, d), lambda i: (i, 0)),
            pl.BlockSpec((1, 1, 1, 1, d), lambda i: (layer, 0, i // per_b, 0, 0)),
            pl.BlockSpec((1, 1, 1, 1, d), lambda i: (layer, 1, i // per_b, 0, 0)),
            pl.BlockSpec((1, d), lambda i: (0, 0)),
            pl.BlockSpec((tm, PAIR), lambda i: (i % per_b, 0)),
            pl.BlockSpec((tm, PAIR), lambda i: (i % per_b, 0)),
            pl.BlockSpec(w.shape, lambda i: (0, 0)),
        ],
        out_specs=out_specs,
        out_shape=out_shape,
        compiler_params=_cparams(("arbitrary",)),
        name="proj",
    )(x2, ada, ada, norm_g, cos_t, sin_t, w)


def _out_kernel(n_o, final, *refs):
    x_ref, gate_ref = refs[0], refs[1]
    o_refs = refs[2:2 + n_o]
    w_ref = refs[2 + n_o]
    fg_ref = refs[3 + n_o]
    out_ref = refs[4 + n_o]
    y = None
    k0 = 0
    for o_ref in o_refs:
        kw = o_ref.shape[-1]
        part = _dot(o_ref[...], w_ref[k0:k0 + kw, :])
        y = part if y is None else y + part
        k0 += kw
    xn = x_ref[...] + gate_ref[0, 0, 0] * y
    if final:
        xn = (xn * lax.rsqrt(jnp.mean(xn * xn, axis=-1, keepdims=True) + RMS_EPS)) * fg_ref[...]
    out_ref[...] = xn


def _out_proj(x2, ada, layer, o_list, w_out, final_g, final, seq):
    rows, d = x2.shape
    tm = ROW_TILE
    per_b = seq // tm
    n_o = len(o_list)
    in_specs = [
        pl.BlockSpec((tm, d), lambda i: (i, 0)),
        pl.BlockSpec((1, 1, 1, 1, d), lambda i: (layer, 2, i // per_b, 0, 0)),
    ]
    in_specs += [pl.BlockSpec((tm, o.shape[-1]), lambda i: (i, 0)) for o in o_list]
    in_specs += [pl.BlockSpec(w_out.shape, lambda i: (0, 0)), pl.BlockSpec((1, d), lambda i: (0, 0))]
    return pl.pallas_call(
        functools.partial(_out_kernel, n_o, final),
        grid=(rows // tm,),
        in_specs=in_specs,
        out_specs=pl.BlockSpec((tm, d), lambda i: (i, 0)),
        out_shape=jax.ShapeDtypeStruct((rows, d), F32),
        compiler_params=_cparams(("arbitrary",)),
        name="out_proj",
    )(x2, ada, *o_list, w_out, final_g)


def _half_masks():
    lane = lax.broadcasted_iota(jnp.int32, (1, PAIR), 1)
    lo = lane < HEAD_DIM
    return lo, lo.astype(BF16), (~lo).astype(BF16)


def _stack_heads(q_pairs, m_lo, m_hi):
    blocks = []
    for p in range(q_pairs.shape[-1] // PAIR):
        qp = q_pairs[:, p * PAIR:(p + 1) * PAIR]
        blocks += [qp * m_lo, qp * m_hi]
    return jnp.concatenate(blocks, axis=0)


def _merge_heads(o_heads, lo):
    pairs = [jnp.where(lo, o_heads[2 * p], o_heads[2 * p + 1]) for p in range(len(o_heads) // 2)]
    return jnp.concatenate(pairs, axis=-1) if len(pairs) > 1 else pairs[0]


def _swa_kernel(tq, sink_ref, q_ref, k_ref, v_ref, z_ref, o_ref):
    i = pl.program_id(1)
    lo, m_lo, m_hi = _half_masks()
    n_prev = -(-(A_WINDOW - 1) // tq)
    span = (n_prev + 1) * tq
    start = pl.multiple_of(jnp.maximum(i - n_prev, 0) * tq, tq)
    kk = k_ref[0, pl.ds(start, span), :]
    vv = v_ref[0, pl.ds(start, span), :]
    qs = _stack_heads(q_ref[0], m_lo, m_hi)
    s_all = _dot_nt(qs, kk)
    qpos = i * tq + lax.broadcasted_iota(jnp.int32, (tq, span), 0)
    kpos = start + lax.broadcasted_iota(jnp.int32, (tq, span), 1)
    diff = qpos - kpos
    bias = jnp.where(diff >= 0, jnp.where(diff < A_WINDOW, 0.0, NEG), NEG)
    p_list, den_list = [], []
    for hd in range(A_HEADS):
        s = s_all[hd * tq:(hd + 1) * tq] + bias
        sink = sink_ref[hd]
        m = jnp.maximum(jnp.max(s, axis=-1, keepdims=True), sink)
        e = jnp.exp(s - m)
        den_list.append(jnp.sum(e, axis=-1, keepdims=True) + jnp.exp(sink - m))
        p_list.append(e.astype(BF16))
    o_all = _dot(jnp.concatenate(p_list, axis=0), vv)
    heads = [o_all[hd * tq:(hd + 1) * tq] / den_list[hd] for hd in range(A_HEADS)]
    o_ref[0] = (_merge_heads(heads, lo) * z_ref[0]).astype(o_ref.dtype)


def _swa(sinks, q, kr, vv, z, tq):
    nb, seq, _ = q.shape
    aw = A_HEADS * HEAD_DIM
    return pl.pallas_call(
        functools.partial(_swa_kernel, tq),
        grid=(nb, seq // tq),
        in_specs=[
            pl.BlockSpec(memory_space=pltpu.SMEM),
            pl.BlockSpec((1, tq, aw), lambda b, i: (b, i, 0)),
            pl.BlockSpec((1, seq, PAIR), lambda b, i: (b, 0, 0)),
            pl.BlockSpec((1, seq, PAIR), lambda b, i: (b, 0, 0)),
            pl.BlockSpec((1, tq, aw), lambda b, i: (b, i, 0)),
        ],
        out_specs=pl.BlockSpec((1, tq, aw), lambda b, i: (b, i, 0)),
        out_shape=jax.ShapeDtypeStruct((nb, seq, aw), BF16),
        compiler_params=_cparams(("arbitrary", "arbitrary")),
        name="swa",
    )(sinks, q, kr, vv, z)


def _cmp_mlp(c, pe_ref, w1_ref, w2_ref):
    half = CMP_STRIDE * HEAD_DIM
    nch = c.shape[0]
    top = _dot((c + pe_ref[0:1, :]).astype(BF16), w1_ref[0:half, :])
    bot = _dot((c + pe_ref[1:2, :]).astype(BF16), w1_ref[half:2 * half, :])
    hid = jax.nn.gelu(top + pltpu.roll(bot, nch - 1, 0))
    return _dot(hid.astype(BF16), w2_ref[...])


def _compress_kernel(ck_ref, cv_ref, pek_ref, w1k_ref, w2k_ref, pev_ref, w1v_ref, w2v_ref, ko_ref, vo_ref):
    ko_ref[0, 0] = _cmp_mlp(ck_ref[0, 0], pek_ref, w1k_ref, w2k_ref).astype(ko_ref.dtype)
    vo_ref[0, 0] = _cmp_mlp(cv_ref[0, 0], pev_ref, w1v_ref, w2v_ref).astype(vo_ref.dtype)


def _compress(ck, cv, pek, w1k, w2k, pev, w1v, w2v):
    nb, ng, nch, fl = ck.shape
    full = lambda a: pl.BlockSpec(a.shape, lambda b, g: (0,) * a.ndim)
    blk = pl.BlockSpec((1, 1, nch, fl), lambda b, g: (b, g, 0, 0))
    oblk = pl.BlockSpec((1, 1, nch, PAIR), lambda b, g: (b, g, 0, 0))
    return pl.pallas_call(
        _compress_kernel,
        grid=(nb, ng),
        in_specs=[blk, blk, full(pek), full(w1k), full(w2k), full(pev), full(w1v), full(w2v)],
        out_specs=[oblk, oblk],
        out_shape=[jax.ShapeDtypeStruct((nb, ng, nch, PAIR), BF16)] * 2,
        compiler_params=_cparams(("arbitrary", "arbitrary")),
        name="compress",
    )(ck, cv, pek, w1k, w2k, pev, w1v, w2v)


def _topn_mask(score, valid, n_cand, topn):
    lane = lax.broadcasted_iota(jnp.int32, score.shape, 1)
    rank = jnp.zeros(score.shape, F32)
    for jp in range(n_cand):
        col = score[:, jp:jp + 1]
        earlier = jnp.where(lane > jp, 1.0, 0.0)
        rank = rank + jnp.where(col > score, 1.0, jnp.where(col == score, earlier, 0.0))
    return jnp.where(valid, jnp.where(rank < topn, 1.0, 0.0), 0.0)


def _nsa_kernel(tq, nc, nsb, q_ref, kc_ref, vc_ref, ks_ref, vs_ref, kw_ref, vw_ref, gt_ref, z_ref, ovl_ref, o_ref):
    i = pl.program_id(2)
    lo, m_lo, m_hi = _half_masks()
    qs = _stack_heads(q_ref[0], m_lo, m_hi)
    nr = B_REP * tq
    t_col = i * tq + lax.broadcasted_iota(jnp.int32, (tq, 1), 0)

    kc = kc_ref[0, 0]
    nch = kc.shape[0]
    sc = _dot_nt(qs, kc)
    cidx = lax.broadcasted_iota(jnp.int32, (tq, nch), 1)
    ok_c = (cidx < nc) & (t_col >= cidx * CMP_STRIDE + (CMP_LEN - 1))
    p_list = []
    for r in range(B_REP):
        s = jnp.where(ok_c, sc[r * tq:(r + 1) * tq], NEG)
        m = jnp.max(s, axis=-1, keepdims=True)
        m = jnp.where(m > 0.5 * NEG, m, 0.0)
        e = jnp.where(ok_c, jnp.exp(s - m), 0.0)
        den = jnp.sum(e, axis=-1, keepdims=True)
        p_list.append((e / jnp.where(den > 0, den, 1.0)).astype(BF16))
    o_cmp = _dot(jnp.concatenate(p_list, axis=0), vc_ref[0, 0])
    imp = _dot(p_list[0], ovl_ref[...])
    for r in range(1, B_REP):
        imp = imp + _dot(p_list[r], ovl_ref[...])

    jl = lax.broadcasted_iota(jnp.int32, (tq, PAIR), 1)
    tb = t_col // SLC_LEN
    valid = jl <= tb
    forced = (jl == 0) | (jl == tb) | (jl == tb - 1)
    score = jnp.where(valid, jnp.where(forced, FORCE_BONUS, imp), -jnp.inf)
    sel = _topn_mask(score, valid, nsb, SLC_TOPN).astype(BF16)

    blk_per_chunk = PAIR // SLC_LEN
    row = lax.broadcasted_iota(jnp.int32, (PAIR, PAIR), 0)
    lane2 = lax.broadcasted_iota(jnp.int32, (PAIR, PAIR), 1)
    kl = lax.broadcasted_iota(jnp.int32, (tq, PAIR), 1)

    def slc_body(c, carry):
        m, l, acc = carry
        off = pl.multiple_of(c * PAIR, PAIR)
        kch = ks_ref[0, pl.ds(off, PAIR), :]
        vch = vs_ref[0, pl.ds(off, PAIR), :]
        expand = jnp.where(row == c * blk_per_chunk + lane2 // SLC_LEN, 1.0, 0.0).astype(BF16)
        picked = _dot(sel, expand)
        ok = jnp.where(off + kl <= t_col, picked, 0.0) > 0.5
        bias = jnp.where(ok, 0.0, NEG)
        s = _dot_nt(qs, kch) + jnp.concatenate([bias] * B_REP, axis=0)
        m_new = jnp.maximum(m, jnp.max(s, axis=-1, keepdims=True))
        alpha = jnp.exp(m - m_new)
        e = jnp.exp(s - m_new)
        l = alpha * l + jnp.sum(e, axis=-1, keepdims=True)
        acc = alpha * acc + _dot(e.astype(BF16), vch)
        return m_new, l, acc

    init = (jnp.full((nr, 1), NEG, F32), jnp.zeros((nr, 1), F32), jnp.zeros((nr, PAIR), F32))
    n_chunks = ((i + 1) * tq) // PAIR
    _, l_s, acc_s = lax.fori_loop(0, n_chunks, slc_body, init)
    o_slc = acc_s / l_s

    n_prev = -(-(WIN_LEN - 1) // tq)
    span = (n_prev + 1) * tq
    start = pl.multiple_of(jnp.maximum(i - n_prev, 0) * tq, tq)
    kw = kw_ref[0, pl.ds(start, span), :]
    vw = vw_ref[0, pl.ds(start, span), :]
    sw = _dot_nt(qs, kw)
    diff = t_col - (start + lax.broadcasted_iota(jnp.int32, (tq, span), 1))
    bias_w = jnp.where(diff >= 0, jnp.where(diff < WIN_LEN, 0.0, NEG), NEG)
    pw, den_w = [], []
    for r in range(B_REP):
        s = sw[r * tq:(r + 1) * tq] + bias_w
        e = jnp.exp(s - jnp.max(s, axis=-1, keepdims=True))
        den_w.append(jnp.sum(e, axis=-1, keepdims=True))
        pw.append(e.astype(BF16))
    o_win = _dot(jnp.concatenate(pw, axis=0), vw)

    gt = gt_ref[0]
    heads = []
    for r in range(B_REP):
        rs = slice(r * tq, (r + 1) * tq)
        heads.append(gt[:, r:r + 1] * o_cmp[rs]
                     + gt[:, B_REP + r:B_REP + r + 1] * o_slc[rs]
                     + gt[:, 2 * B_REP + r:2 * B_REP + r + 1] * (o_win[rs] / den_w[r]))
    o_ref[0] = (_merge_heads(heads, lo) * z_ref[0]).astype(o_ref.dtype)


def _nsa(q, kcmp, vcmp, kr, vv, gates, z, ovl, nc, nsb, tq):
    nb, seq, _ = q.shape
    gw = B_REP * HEAD_DIM
    a_blocks = (A_HEADS * HEAD_DIM) // gw
    nch = kcmp.shape[2]
    seqblk = lambda off: pl.BlockSpec((1, seq, PAIR), lambda b, g, i: (b, 0, off + g))
    cblk = pl.BlockSpec((1, 1, nch, PAIR), lambda b, g, i: (b, g, 0, 0))
    return pl.pallas_call(
        functools.partial(_nsa_kernel, tq, nc, nsb),
        grid=(nb, B_KV, seq // tq),
        in_specs=[
            pl.BlockSpec((1, tq, gw), lambda b, g, i: (b, i, a_blocks + g)),
            cblk, cblk,
            seqblk(1), seqblk(1), seqblk(1 + B_KV), seqblk(1 + B_KV),
            pl.BlockSpec((1, tq, PAIR), lambda b, g, i: (b, i, g)),
            pl.BlockSpec((1, tq, gw), lambda b, g, i: (b, i, a_blocks + g)),
            pl.BlockSpec(ovl.shape, lambda b, g, i: (0, 0)),
        ],
        out_specs=pl.BlockSpec((1, tq, gw), lambda b, g, i: (b, i, g)),
        out_shape=jax.ShapeDtypeStruct((nb, seq, B_HEADS * HEAD_DIM), BF16),
        compiler_params=_cparams(("arbitrary", "arbitrary", "arbitrary")),
        name="nsa",
    )(q, kcmp, vcmp, kr, vv, kr, vv, gates, z, ovl)


def _moba_kernel(nblk, q_ref, k_ref, v_ref, km_ref, z_ref, o_ref):
    i = pl.program_id(2)
    tq = MOBA_BLOCK
    lo, m_lo, m_hi = _half_masks()
    qs = _stack_heads(q_ref[0], m_lo, m_hi)
    nr = 2 * tq

    km = km_ref[0].astype(BF16)
    km = jnp.concatenate([km, jnp.zeros((PAIR - nblk, PAIR), BF16)], axis=0)
    gs = _dot_nt(qs, km)
    jl = lax.broadcasted_iota(jnp.int32, (nr, PAIR), 1)
    past = jl < i
    score = jnp.where(past, gs, -jnp.inf)
    sel = _topn_mask(score, past, nblk, MOBA_TOPK)

    def attend(carry, kch, vch, bias):
        m, l, acc = carry
        s = _dot_nt(qs, kch) + bias
        m_new = jnp.maximum(m, jnp.max(s, axis=-1, keepdims=True))
        alpha = jnp.exp(m - m_new)
        e = jnp.exp(s - m_new)
        l = alpha * l + jnp.sum(e, axis=-1, keepdims=True)
        acc = alpha * acc + _dot(e.astype(BF16), vch)
        return m_new, l, acc

    own = pl.multiple_of(i * tq, tq)
    rr = lax.broadcasted_iota(jnp.int32, (tq, tq), 0)
    cc = lax.broadcasted_iota(jnp.int32, (tq, tq), 1)
    causal = jnp.where(cc <= rr, 0.0, NEG)
    init = (jnp.full((nr, 1), NEG, F32), jnp.zeros((nr, 1), F32), jnp.zeros((nr, PAIR), F32))
    carry = attend(init, k_ref[0, pl.ds(own, tq), :], v_ref[0, pl.ds(own, tq), :],
                   jnp.concatenate([causal, causal], axis=0))

    def past_body(j, carry):
        off = pl.multiple_of(j * tq, tq)
        picked = jnp.max(jnp.where(jl == j, sel, 0.0), axis=-1, keepdims=True)
        bias = jnp.where(picked > 0.5, 0.0, NEG)
        return attend(carry, k_ref[0, pl.ds(off, tq), :], v_ref[0, pl.ds(off, tq), :], bias)

    _, l_f, acc_f = lax.fori_loop(0, i, past_body, carry)
    o = acc_f / l_f
    o_ref[0] = (jnp.where(lo, o[:tq], o[tq:]) * z_ref[0]).astype(o_ref.dtype)


def _moba(q, k, v, kmean, z):
    nb, seq, width = q.shape
    tq = MOBA_BLOCK
    nblk = seq // MOBA_BLOCK
    qblk = pl.BlockSpec((1, tq, PAIR), lambda b, p, i: (b, i, p))
    sblk = pl.BlockSpec((1, seq, PAIR), lambda b, p, i: (b, 0, p))
    return pl.pallas_call(
        functools.partial(_moba_kernel, nblk),
        grid=(nb, width // PAIR, seq // tq),
        in_specs=[qblk, sblk, sblk, pl.BlockSpec((1, nblk, PAIR), lambda b, p, i: (b, 0, p)), qblk],
        out_specs=qblk,
        out_shape=jax.ShapeDtypeStruct((nb, seq, width), BF16),
        compiler_params=_cparams(("arbitrary", "arbitrary", "arbitrary")),
        name="moba",
    )(q, k, v, kmean, z)


def _dup_heads(w):
    d, wd = w.shape
    w3 = w.reshape(d, wd // HEAD_DIM, 1, HEAD_DIM)
    return jnp.broadcast_to(w3, (d, wd // HEAD_DIM, 2, HEAD_DIM)).reshape(d, 2 * wd)


def _even_weights(w_in):
    aq, akv = A_HEADS * HEAD_DIM, HEAD_DIM
    bq, bkv, bg = B_HEADS * HEAD_DIM, B_KV * HEAD_DIM, 3 * B_HEADS
    sizes = (aq, akv, akv, aq, bq, bkv, bkv, bkv, bkv, bkv, bkv, bg, bq)
    offs = np.concatenate([[0], np.cumsum(sizes)])
    qa, ka, va, za, qb, kc, vc, ks, vs, kw, vw, gb, zb = [w_in[:, offs[n]:offs[n + 1]] for n in range(len(sizes))]
    gb3 = gb.reshape(-1, B_KV, B_REP, 3).transpose(0, 1, 3, 2).reshape(-1, B_KV, 3 * B_REP)
    gb3 = jnp.pad(gb3, ((0, 0), (0, 0), (0, PAIR - 3 * B_REP))).reshape(-1, B_KV * PAIR)
    w_rope = [qa * Q_SCALE, qb * Q_SCALE, _dup_heads(ka), _dup_heads(ks), _dup_heads(kw), kc]
    w_plain = [_dup_heads(va), _dup_heads(vs), _dup_heads(vw), vc]
    w = jnp.concatenate(w_rope + w_plain + [za, zb, gb3], axis=1).astype(BF16)
    qw = aq + bq
    krw = PAIR * (1 + 2 * B_KV)
    groups, col = [], 0
    for width, kind, oi, oc in ((qw, "rope", 0, 0), (krw, "rope", 1, 0), (bkv, "rope", 2, 0),
                                (krw, "plain", 3, 0), (bkv, "plain", 4, 0),
                                (aq + bq, "silu", 5, 0), (B_KV * PAIR, "sigmoid", 6, 0)):
        groups.append((col, width, kind, oi, oc))
        col += width
    outs = [(qw, BF16), (krw, BF16), (bkv, F32), (krw, BF16), (bkv, F32), (aq + bq, F32), (B_KV * PAIR, F32)]
    return w, tuple(groups), outs


def _odd_weights(w_in):
    cw = C_HEADS * HEAD_DIM
    q, k, v, z = [w_in[:, n * cw:(n + 1) * cw] for n in range(4)]
    w = jnp.concatenate([q * Q_SCALE, k, v, z], axis=1).astype(BF16)
    groups = ((0, cw, "rope", 0, 0), (cw, cw, "rope_mean", 1, 0), (2 * cw, cw, "plain", 2, 0), (3 * cw, cw, "silu", 3, 0))
    outs = [(cw, BF16), (cw, BF16), (cw, BF16), (cw, F32)]
    return w, groups, outs


def _rope_tables(seq):
    inv = ROPE_THETA ** (-jnp.arange(0, HEAD_DIM, 2, dtype=F32) / HEAD_DIM)
    ang = jnp.arange(seq, dtype=F32)[:, None] * inv[None, :]
    cos, sin = jnp.cos(ang), jnp.sin(ang)
    reps = PAIR // (HEAD_DIM // 2)
    sign = np.tile(np.concatenate([-np.ones(HEAD_DIM // 2), np.ones(HEAD_DIM // 2)]), PAIR // HEAD_DIM)
    return jnp.tile(cos, (1, reps)), jnp.tile(sin, (1, reps)) * jnp.asarray(sign, F32)[None, :]


def _overlap_matrix(seq, nch):
    nc = (seq - CMP_LEN) // CMP_STRIDE + 1
    nsb = seq // SLC_LEN
    cst = np.arange(nc) * CMP_STRIDE
    jj = np.arange(nsb)
    ov = ((cst[:, None] < (jj[None, :] + 1) * SLC_LEN) & (cst[:, None] + CMP_LEN > jj[None, :] * SLC_LEN))
    full = np.zeros((nch, PAIR), np.float32)
    full[:nc, :nsb] = ov
    return jnp.asarray(full, BF16), nc, nsb


def _chunk_rows(t, nb, seq):
    t5 = t.reshape(nb, seq // CMP_STRIDE, CMP_STRIDE, B_KV, HEAD_DIM)
    return t5.transpose(0, 3, 1, 2, 4).reshape(nb, B_KV, seq // CMP_STRIDE, CMP_STRIDE * HEAD_DIM)


def kernel(x, c, w_ada, b_ada, norm_g, w_in_even, a_sinks, cmp_pe_k, cmp_w1_k, cmp_w2_k, cmp_pe_v, cmp_w1_v, cmp_w2_v, w_out_even, w_in_odd, w_out_odd, final_g):
    nb, seq, d = x.shape
    depth = w_ada.shape[0]
    assert seq % ROW_TILE == 0 and seq % MOBA_BLOCK == 0 and seq // SLC_LEN <= PAIR
    cos_t, sin_t = _rope_tables(seq)
    ada = _ada(c, w_ada, b_ada)
    nch = seq // CMP_STRIDE
    ovl, nc, nsb = _overlap_matrix(seq, nch)
    fg = final_g.reshape(1, d)
    half = CMP_STRIDE * HEAD_DIM
    x2 = x.reshape(nb * seq, d)
    for layer in range(depth):
        li = layer // 2
        final = layer == depth - 1
        ng = norm_g[layer].reshape(1, d)
        if layer % 2 == 0:
            w, groups, outs = _even_weights(w_in_even[li])
            q, kr, kc, vv, vc, z, gates = _proj(x2, ada, layer, ng, cos_t, sin_t, w, groups, outs, seq)
            r3 = lambda a: a.reshape(nb, seq, a.shape[-1])
            q, kr, kc, vv, vc, z, gates = map(r3, (q, kr, kc, vv, vc, z, gates))
            dup2 = lambda w2: jnp.concatenate([w2, w2], axis=1).astype(BF16)
            kcmp, vcmp = _compress(
                _chunk_rows(kc, nb, seq), _chunk_rows(vc, nb, seq),
                cmp_pe_k[li].reshape(2, half), cmp_w1_k[li].astype(BF16), dup2(cmp_w2_k[li]),
                cmp_pe_v[li].reshape(2, half), cmp_w1_v[li].astype(BF16), dup2(cmp_w2_v[li]))
            oa = _swa(a_sinks[li], q, kr, vv, z, ATT_TQ)
            ob = _nsa(q, kcmp, vcmp, kr, vv, gates, z, ovl, nc, nsb, ATT_TQ)
            o_list = [oa.reshape(nb * seq, -1), ob.reshape(nb * seq, -1)]
            w_out = w_out_even[li].astype(BF16)
        else:
            w, groups, outs = _odd_weights(w_in_odd[li])
            q, k, v, z, kmean = _proj(x2, ada, layer, ng, cos_t, sin_t, w, groups, outs, seq, moba_mean=4)
            r3 = lambda a: a.reshape(nb, seq, a.shape[-1])
            o = _moba(r3(q), r3(k), r3(v), kmean.reshape(nb, seq // MOBA_BLOCK, -1), r3(z))
            o_list = [o.reshape(nb * seq, -1)]
            w_out = w_out_odd[li].astype(BF16)
        x2 = _out_proj(x2, ada, layer, o_list, w_out, fg, final, seq)
    return x2.reshape(nb, seq, d)
```

```python
import functools

import numpy as np
import jax
import jax.numpy as jnp
from jax import lax
from jax.experimental import pallas as pl
from jax.experimental.pallas import tpu as pltpu

D_MODEL = 1024
HEAD_DIM = 64
PAIR = 2 * HEAD_DIM
ROPE_THETA = 10000.0
RMS_EPS = 1e-6
A_HEADS = 8
A_WINDOW = 128
B_HEADS = 8
B_KV = 2
B_REP = B_HEADS // B_KV
CMP_LEN = 32
CMP_STRIDE = 16
CMP_HIDDEN = 256
SLC_LEN = 64
SLC_TOPN = 8
WIN_LEN = 512
FORCE_BONUS = 1e4
C_HEADS = 16
MOBA_BLOCK = 256
MOBA_TOPK = 3
Q_SCALE = HEAD_DIM ** -0.5

NEG = -1e30
MXU_COLS = 256
ROW_TILE = 512
ATT_TQ = 128
GATE_ROWS = 16
VMEM_LIMIT = 48 * 1024 * 1024

BF16 = jnp.bfloat16
F32 = jnp.float32


def _dot_nt(a, b):
    return lax.dot_general(a, b, (((1,), (1,)), ((), ())), preferred_element_type=F32)


def _dot(a, b):
    return jnp.dot(a, b, preferred_element_type=F32)


def _cparams(sem):
    return pltpu.CompilerParams(dimension_semantics=sem, vmem_limit_bytes=VMEM_LIMIT)


def _tile_lanes(a, n):
    return jnp.concatenate([a] * n, axis=1) if n > 1 else a


def _ada_kernel(c_ref, w_ref, b_ref, o_ref):
    c = c_ref[...]
    ca = (c * jax.nn.sigmoid(c)).astype(BF16)
    o_ref[0, 0] = _dot(ca, w_ref[0].astype(BF16)) + b_ref[0, 0]


def _ada(c, w_ada, b_ada):
    depth, d, _ = w_ada.shape
    nb = c.shape[0]
    b4 = b_ada.reshape(depth, 3, 1, d)
    out = pl.pallas_call(
        _ada_kernel,
        grid=(depth, 3),
        in_specs=[
            pl.BlockSpec((nb, d), lambda l, j: (0, 0)),
            pl.BlockSpec((1, d, d), lambda l, j: (l, 0, j)),
            pl.BlockSpec((1, 1, 1, d), lambda l, j: (l, j, 0, 0)),
        ],
        out_specs=pl.BlockSpec((1, 1, nb, d), lambda l, j: (l, j, 0, 0)),
        out_shape=jax.ShapeDtypeStruct((depth, 3, nb, d), F32),
        compiler_params=_cparams(("arbitrary", "arbitrary")),
        name="ada",
    )(c, w_ada, b4)
    return out.reshape(depth, 3, nb, 1, d)


def _rope(a, cos, sin_signed):
    w = a.shape[-1]
    lane = lax.broadcasted_iota(jnp.int32, a.shape, 1)
    first_half = (lane % HEAD_DIM) < (HEAD_DIM // 2)
    partner = jnp.where(first_half, pltpu.roll(a, w - HEAD_DIM // 2, 1), pltpu.roll(a, HEAD_DIM // 2, 1))
    return a * cos + partner * sin_signed


def _proj_kernel(groups, groups_t, moba_mean, x_ref, shift_ref, scale_ref, g_ref, cos_ref, sin_ref, w_ref, wt_ref,
                 *out_refs):
    x = x_ref[...]
    y = x * lax.rsqrt(jnp.mean(x * x, axis=-1, keepdims=True) + RMS_EPS)
    h = (y * g_ref[...]) * (1.0 + scale_ref[0, 0, 0]) + shift_ref[0, 0, 0]
    hb = h.astype(BF16)
    tm = hb.shape[0]
    cos = _tile_lanes(cos_ref[...], MXU_COLS // PAIR)
    sin = _tile_lanes(sin_ref[...], MXU_COLS // PAIR)
    for (w_start, width, kind, out_idx) in groups:
        o_ref = out_refs[out_idx]
        for c0 in range(0, width, MXU_COLS):
            cw = min(MXU_COLS, width - c0)
            acc = _dot(hb, w_ref[:, w_start + c0:w_start + c0 + cw])
            if kind in ("rope", "rope_mean"):
                acc = _rope(acc, cos[:, :cw], sin[:, :cw])
            elif kind == "silu":
                acc = acc * jax.nn.sigmoid(acc)
            o_ref[:, c0:c0 + cw] = acc.astype(o_ref.dtype)
            if kind == "rope_mean":
                km_ref = out_refs[moba_mean]
                for j in range(tm // MOBA_BLOCK):
                    blk = acc[j * MOBA_BLOCK:(j + 1) * MOBA_BLOCK]
                    km_ref[0, j, :, c0:c0 + cw] = jnp.sum(blk, axis=0, keepdims=True) * (1.0 / MOBA_BLOCK)
    for (r_start, n_rows, kind, out_idx, chunk) in groups_t:
        o_ref = out_refs[out_idx]
        for r0 in range(0, n_rows, MXU_COLS):
            rw = min(MXU_COLS, n_rows - r0)
            acc_t = _dot_nt(wt_ref[r_start + r0:r_start + r0 + rw, :], hb)
            if kind == "sigmoid":
                o_ref[0, r0:r0 + rw, :] = jax.nn.sigmoid(acc_t)
            else:
                for jj in range(tm // chunk):
                    o_ref[0, jj, r0:r0 + rw, :] = acc_t[:, jj * chunk:(jj + 1) * chunk].astype(o_ref.dtype)


def _proj(x2, ada, layer, norm_g, cos_t, sin_t, w, wt, groups, groups_t, out_shape, out_specs, seq, moba_mean=None):
    rows, d = x2.shape
    tm = ROW_TILE
    per_b = seq // tm
    return pl.pallas_call(
        functools.partial(_proj_kernel, groups, groups_t, moba_mean),
        grid=(rows // tm,),
        in_specs=[
            pl.BlockSpec((tm, d), lambda i: (i, 0)),
            pl.BlockSpec((1, 1, 1, 1, d), lambda i: (layer, 0, i // per_b, 0, 0)),
            pl.BlockSpec((1, 1, 1, 1, d), lambda i: (layer, 1, i // per_b, 0, 0)),
            pl.BlockSpec((1, d), lambda i: (0, 0)),
            pl.BlockSpec((tm, PAIR), lambda i: (i % per_b, 0)),
            pl.BlockSpec((tm, PAIR), lambda i: (i % per_b, 0)),
            pl.BlockSpec(w.shape, lambda i: (0, 0)),
            pl.BlockSpec(wt.shape, lambda i: (0, 0)),
        ],
        out_specs=out_specs,
        out_shape=out_shape,
        compiler_params=_cparams(("arbitrary",)),
        name="proj",
    )(x2, ada, ada, norm_g, cos_t, sin_t, w, wt)


def _out_kernel(n_o, final, *refs):
    x_ref, gate_ref = refs[0], refs[1]
    o_refs = refs[2:2 + n_o]
    w_ref = refs[2 + n_o]
    fg_ref = refs[3 + n_o]
    out_ref = refs[4 + n_o]
    y = None
    k0 = 0
    for o_ref in o_refs:
        kw = o_ref.shape[-1]
        part = _dot(o_ref[...], w_ref[k0:k0 + kw, :])
        y = part if y is None else y + part
        k0 += kw
    xn = x_ref[...] + gate_ref[0, 0, 0] * y
    if final:
        xn = (xn * lax.rsqrt(jnp.mean(xn * xn, axis=-1, keepdims=True) + RMS_EPS)) * fg_ref[...]
    out_ref[...] = xn


def _out_proj(x2, ada, layer, o_list, w_out, final_g, final, seq):
    rows, d = x2.shape
    tm = ROW_TILE
    per_b = seq // tm
    n_o = len(o_list)
    in_specs = [
        pl.BlockSpec((tm, d), lambda i: (i, 0)),
        pl.BlockSpec((1, 1, 1, 1, d), lambda i: (layer, 2, i // per_b, 0, 0)),
    ]
    in_specs += [pl.BlockSpec((tm, o.shape[-1]), lambda i: (i, 0)) for o in o_list]
    in_specs += [pl.BlockSpec(w_out.shape, lambda i: (0, 0)), pl.BlockSpec((1, d), lambda i: (0, 0))]
    return pl.pallas_call(
        functools.partial(_out_kernel, n_o, final),
        grid=(rows // tm,),
        in_specs=in_specs,
        out_specs=pl.BlockSpec((tm, d), lambda i: (i, 0)),
        out_shape=jax.ShapeDtypeStruct((rows, d), F32),
        compiler_params=_cparams(("arbitrary",)),
        name="out_proj",
    )(x2, ada, *o_list, w_out, final_g)


def _stack_heads(q_pairs):
    lane = lax.broadcasted_iota(jnp.int32, (1, PAIR), 1)
    m_lo = (lane < HEAD_DIM).astype(BF16)
    m_hi = (lane >= HEAD_DIM).astype(BF16)
    blocks = []
    for p in range(q_pairs.shape[-1] // PAIR):
        qp = q_pairs[:, p * PAIR:(p + 1) * PAIR]
        blocks += [qp * m_lo, qp * m_hi]
    return jnp.concatenate(blocks, axis=0)


def _softmax_first(s_t, v_t_chunks):
    m = jnp.max(s_t, axis=0, keepdims=True)
    e = jnp.exp(s_t - m)
    l = jnp.sum(e, axis=0, keepdims=True)
    eb = e.astype(BF16)
    acc, k0 = None, 0
    for vt in v_t_chunks:
        part = _dot(vt, eb[k0:k0 + vt.shape[1]])
        acc = part if acc is None else acc + part
        k0 += vt.shape[1]
    return m, l, acc


def _softmax_next(carry, s_t, v_t):
    m, l, acc = carry
    m_new = jnp.maximum(m, jnp.max(s_t, axis=0, keepdims=True))
    alpha = jnp.exp(m - m_new)
    e = jnp.exp(s_t - m_new)
    l = alpha * l + jnp.sum(e, axis=0, keepdims=True)
    acc = alpha * acc + _dot(v_t, e.astype(BF16))
    return m_new, l, acc


def _topn_rows(score, valid, n_cand, topn):
    ridx = lax.broadcasted_iota(jnp.int32, score.shape, 0)
    rank = jnp.zeros(score.shape, F32)
    for jp in range(n_cand):
        row = score[jp:jp + 1, :]
        earlier = jnp.where(ridx > jp, 1.0, 0.0)
        rank = rank + jnp.where(row > score, 1.0, jnp.where(row == score, earlier, 0.0))
    return valid & (rank < topn)


def _swa_kernel(tq, sink_ref, q_ref, k_ref, vt_ref, z_ref, o_ref):
    i = pl.program_id(1)
    n_prev = -(-(A_WINDOW - 1) // tq)
    n_ch = n_prev + 1
    span = n_ch * tq
    c0 = jnp.maximum(i - n_prev, 0)
    start = pl.multiple_of(c0 * tq, tq)
    kk = k_ref[0, pl.ds(start, span), :]
    qs = _stack_heads(q_ref[0])
    s_t = _dot_nt(kk, qs)
    qpos = i * tq + lax.broadcasted_iota(jnp.int32, (span, tq), 1)
    kpos = start + lax.broadcasted_iota(jnp.int32, (span, tq), 0)
    diff = qpos - kpos
    bias = jnp.where(diff >= 0, jnp.where(diff < A_WINDOW, 0.0, NEG), NEG)
    s_t = s_t + _tile_lanes(bias, A_HEADS)
    sink = jnp.concatenate([jnp.full((1, tq), sink_ref[hd], F32) for hd in range(A_HEADS)], axis=1)
    m = jnp.maximum(jnp.max(s_t, axis=0, keepdims=True), sink)
    e = jnp.exp(s_t - m)
    den = jnp.sum(e, axis=0, keepdims=True) + jnp.exp(sink - m)
    eb = e.astype(BF16)
    acc = None
    for cc in range(n_ch):
        part = _dot(vt_ref[0, c0 + cc], eb[cc * tq:(cc + 1) * tq])
        acc = part if acc is None else acc + part
    o_t = acc * (1.0 / den)
    heads = jnp.concatenate([o_t[:, hd * tq:(hd + 1) * tq] for hd in range(A_HEADS)], axis=0)
    o_ref[0] = (heads.T * z_ref[0]).astype(o_ref.dtype)


def _swa(sinks, q, kr, vt, z, tq):
    nb, seq, _ = q.shape
    aw = A_HEADS * HEAD_DIM
    return pl.pallas_call(
        functools.partial(_swa_kernel, tq),
        grid=(nb, seq // tq),
        in_specs=[
            pl.BlockSpec(memory_space=pltpu.SMEM),
            pl.BlockSpec((1, tq, aw), lambda b, i: (b, i, 0)),
            pl.BlockSpec((1, seq, PAIR), lambda b, i: (b, 0, 0)),
            pl.BlockSpec((1, seq // tq, HEAD_DIM, tq), lambda b, i: (b, 0, 0, 0)),
            pl.BlockSpec((1, tq, aw), lambda b, i: (b, i, 0)),
        ],
        out_specs=pl.BlockSpec((1, tq, aw), lambda b, i: (b, i, 0)),
        out_shape=jax.ShapeDtypeStruct((nb, seq, aw), BF16),
        compiler_params=_cparams(("arbitrary", "arbitrary")),
        name="swa",
    )(sinks, q, kr, vt, z)


def _cmp_hidden(x_ref, pe_ref, w1_ref, nch):
    top, bot = None, None
    for l in range(CMP_STRIDE):
        rows = x_ref[0, pl.ds(l, nch, stride=CMP_STRIDE), :]
        t = _dot((rows + pe_ref[l:l + 1, :]).astype(BF16), w1_ref[l])
        b = _dot((rows + pe_ref[CMP_STRIDE + l:CMP_STRIDE + l + 1, :]).astype(BF16), w1_ref[CMP_STRIDE + l])
        top = t if top is None else top + t
        bot = b if bot is None else bot + b
    return jax.nn.gelu(top + pltpu.roll(bot, nch - 1, 0))


def _compress_kernel(kc_ref, vc_ref, pek_ref, w1k_ref, w2k_ref, pev_ref, w1v_ref, w2v_ref, ko_ref, vo_ref):
    nch = ko_ref.shape[2]
    hk = _cmp_hidden(kc_ref, pek_ref, w1k_ref, nch).astype(BF16)
    hv = _cmp_hidden(vc_ref, pev_ref, w1v_ref, nch).astype(BF16)
    for g in range(B_KV):
        gs = slice(g * CMP_HIDDEN, (g + 1) * CMP_HIDDEN)
        ko_ref[0, g] = _dot(hk[:, gs], w2k_ref[...]).astype(ko_ref.dtype)
        vo_ref[0, g] = _dot_nt(w2v_ref[...], hv[:, gs]).astype(vo_ref.dtype)


def _compress(kc, vc, pek, w1k, w2k, pev, w1v, w2v):
    nb, seq, _ = kc.shape
    nch = seq // CMP_STRIDE
    full = lambda a: pl.BlockSpec(a.shape, lambda b: (0,) * a.ndim)
    blk = pl.BlockSpec((1, seq, PAIR), lambda b: (b, 0, 0))
    return pl.pallas_call(
        _compress_kernel,
        grid=(nb,),
        in_specs=[blk, blk, full(pek), full(w1k), full(w2k), full(pev), full(w1v), full(w2v)],
        out_specs=[pl.BlockSpec((1, B_KV, nch, PAIR), lambda b: (b, 0, 0, 0)),
                   pl.BlockSpec((1, B_KV, HEAD_DIM, nch), lambda b: (b, 0, 0, 0))],
        out_shape=[jax.ShapeDtypeStruct((nb, B_KV, nch, PAIR), BF16),
                   jax.ShapeDtypeStruct((nb, B_KV, HEAD_DIM, nch), BF16)],
        compiler_params=_cparams(("arbitrary",)),
        name="compress",
    )(kc, vc, pek, w1k, w2k, pev, w1v, w2v)


def _nsa_kernel(tq, nc, nsb, q_ref, kc_ref, vct_ref, ks_ref, vst_ref, kw_ref, vwt_ref, gt_ref, z_ref, ovl_ref,
                o_ref, bias_ref):
    i = pl.program_id(2)
    qs = _stack_heads(q_ref[0])
    t_row = i * tq + lax.broadcasted_iota(jnp.int32, (1, tq), 1)

    kc = kc_ref[0, 0]
    nch = kc.shape[0]
    cidx = lax.broadcasted_iota(jnp.int32, (nch, tq), 0)
    ok_c = (cidx < nc) & (t_row >= cidx * CMP_STRIDE + (CMP_LEN - 1))
    s = _dot_nt(kc, qs) + _tile_lanes(jnp.where(ok_c, 0.0, NEG), B_REP)
    m = jnp.max(s, axis=0, keepdims=True)
    m = jnp.where(m > 0.5 * NEG, m, 0.0)
    e = jnp.exp(s - m)
    den = jnp.sum(e, axis=0, keepdims=True)
    p_t = (e * (1.0 / jnp.where(den > 0, den, 1.0))).astype(BF16)
    o_cmp = _dot(vct_ref[0, 0], p_t)
    imp = None
    for r in range(B_REP):
        part = _dot(ovl_ref[...], p_t[:, r * tq:(r + 1) * tq])
        imp = part if imp is None else imp + part

    jr = lax.broadcasted_iota(jnp.int32, (nsb, tq), 0)
    tb = t_row // SLC_LEN
    valid = jr <= tb
    forced = (jr == 0) | (jr == tb) | (jr == tb - 1)
    score = jnp.where(valid, jnp.where(forced, FORCE_BONUS, imp), -jnp.inf)
    sel = _topn_rows(score, valid, nsb, SLC_TOPN)
    bias_ref[...] = jnp.where(sel, 0.0, NEG)

    blk_per_chunk = tq // SLC_LEN

    def chunk_bias(c):
        rows = [jnp.broadcast_to(bias_ref[pl.ds(c * blk_per_chunk + bb, 1), :], (SLC_LEN, tq))
                for bb in range(blk_per_chunk)]
        return jnp.concatenate(rows, axis=0)

    def chunk_scores(c, extra=None):
        off = pl.multiple_of(c * tq, tq)
        b = chunk_bias(c)
        if extra is not None:
            b = b + extra
        return _dot_nt(ks_ref[0, pl.ds(off, tq), :], qs) + _tile_lanes(b, B_REP)

    krow = lax.broadcasted_iota(jnp.int32, (tq, tq), 0)
    qcol = lax.broadcasted_iota(jnp.int32, (tq, tq), 1)
    causal = jnp.where(krow <= qcol, 0.0, NEG)
    carry = _softmax_first(chunk_scores(i, causal), [vst_ref[0, i]])

    def slc_body(c, carry):
        return _softmax_next(carry, chunk_scores(c), vst_ref[0, c])

    _, l_s, acc_s = lax.fori_loop(0, i, slc_body, carry)
    o_slc = acc_s * (1.0 / l_s)

    n_prev = -(-(WIN_LEN - 1) // tq)
    n_ch = n_prev + 1
    span = n_ch * tq
    c0 = jnp.maximum(i - n_prev, 0)
    start = pl.multiple_of(c0 * tq, tq)
    sw = _dot_nt(kw_ref[0, pl.ds(start, span), :], qs)
    diff = t_row - (start + lax.broadcasted_iota(jnp.int32, (span, tq), 0))
    bias_w = jnp.where(diff >= 0, jnp.where(diff < WIN_LEN, 0.0, NEG), NEG)
    _, l_w, acc_w = _softmax_first(sw + _tile_lanes(bias_w, B_REP), [vwt_ref[0, c0 + cc] for cc in range(n_ch)])
    o_win = acc_w * (1.0 / l_w)

    gt = gt_ref[0]
    heads = []
    for r in range(B_REP):
        cs = slice(r * tq, (r + 1) * tq)
        heads.append(gt[r:r + 1] * o_cmp[:, cs] + gt[B_REP + r:B_REP + r + 1] * o_slc[:, cs]
                     + gt[2 * B_REP + r:2 * B_REP + r + 1] * o_win[:, cs])
    o_t = jnp.concatenate(heads, axis=0)
    o_ref[0] = (o_t.T * z_ref[0]).astype(o_ref.dtype)


def _nsa(q, kcmp, vcmp_t, kr, vt, gates_t, z, ovl_t, nc, nsb, tq):
    nb, seq, _ = q.shape
    gw = B_REP * HEAD_DIM
    a_blocks = (A_HEADS * HEAD_DIM) // gw
    nch = kcmp.shape[2]
    kblk = lambda off: pl.BlockSpec((1, seq, PAIR), lambda b, g, i: (b, 0, off + g))
    vblk = lambda off: pl.BlockSpec((1, seq // tq, HEAD_DIM, tq), lambda b, g, i: (b, 0, off + g, 0))
    return pl.pallas_call(
        functools.partial(_nsa_kernel, tq, nc, nsb),
        grid=(nb, B_KV, seq // tq),
        in_specs=[
            pl.BlockSpec((1, tq, gw), lambda b, g, i: (b, i, a_blocks + g)),
            pl.BlockSpec((1, 1, nch, PAIR), lambda b, g, i: (b, g, 0, 0)),
            pl.BlockSpec((1, 1, HEAD_DIM, nch), lambda b, g, i: (b, g, 0, 0)),
            kblk(1), vblk(1), kblk(1 + B_KV), vblk(1 + B_KV),
            pl.BlockSpec((1, GATE_ROWS, tq), lambda b, g, i: (b, g, i)),
            pl.BlockSpec((1, tq, gw), lambda b, g, i: (b, i, a_blocks + g)),
            pl.BlockSpec(ovl_t.shape, lambda b, g, i: (0, 0)),
        ],
        out_specs=pl.BlockSpec((1, tq, gw), lambda b, g, i: (b, i, g)),
        out_shape=jax.ShapeDtypeStruct((nb, seq, B_HEADS * HEAD_DIM), BF16),
        scratch_shapes=[pltpu.VMEM((nsb, tq), F32)],
        compiler_params=_cparams(("arbitrary", "arbitrary", "arbitrary")),
        name="nsa",
    )(q, kcmp, vcmp_t, kr, vt, kr, vt, gates_t, z, ovl_t)


def _moba_kernel(nblk, q_ref, k_ref, vt_ref, km_ref, z_ref, o_ref, bias_ref):
    i = pl.program_id(2)
    tq = MOBA_BLOCK
    qs = _stack_heads(q_ref[0])
    nr = 2 * tq
    ncand = bias_ref.shape[0]

    km = km_ref[0]
    km = jnp.concatenate([km, jnp.zeros((ncand - nblk, PAIR), F32)], axis=0).astype(BF16)
    gs = _dot_nt(km, qs)
    jr = lax.broadcasted_iota(jnp.int32, (ncand, nr), 0)
    past = jr < i
    sel = _topn_rows(jnp.where(past, gs, -jnp.inf), past, nblk, MOBA_TOPK)
    bias_ref[...] = jnp.where(sel, 0.0, NEG)

    own = pl.multiple_of(i * tq, tq)
    krow = lax.broadcasted_iota(jnp.int32, (tq, tq), 0)
    qcol = lax.broadcasted_iota(jnp.int32, (tq, tq), 1)
    causal = _tile_lanes(jnp.where(krow <= qcol, 0.0, NEG), 2)
    carry = _softmax_first(_dot_nt(k_ref[0, pl.ds(own, tq), :], qs) + causal, [vt_ref[0, i]])

    def past_body(j, carry):
        off = pl.multiple_of(j * tq, tq)
        s_t = _dot_nt(k_ref[0, pl.ds(off, tq), :], qs) + bias_ref[pl.ds(j, 1), :]
        return _softmax_next(carry, s_t, vt_ref[0, j])

    _, l_f, acc_f = lax.fori_loop(0, i, past_body, carry)
    o_t = acc_f * (1.0 / l_f)
    o_pair = jnp.concatenate([o_t[:HEAD_DIM, :tq], o_t[HEAD_DIM:, tq:]], axis=0)
    o_ref[0] = (o_pair.T * z_ref[0]).astype(o_ref.dtype)


def _moba(q, k, vt, kmean, z):
    nb, seq, width = q.shape
    tq = MOBA_BLOCK
    nblk = seq // MOBA_BLOCK
    ncand = -(-nblk // 16) * 16
    qblk = pl.BlockSpec((1, tq, PAIR), lambda b, p, i: (b, i, p))
    return pl.pallas_call(
        functools.partial(_moba_kernel, nblk),
        grid=(nb, width // PAIR, seq // tq),
        in_specs=[
            qblk,
            pl.BlockSpec((1, seq, PAIR), lambda b, p, i: (b, 0, p)),
            pl.BlockSpec((1, nblk, PAIR, tq), lambda b, p, i: (b, 0, p, 0)),
            pl.BlockSpec((1, nblk, PAIR), lambda b, p, i: (b, 0, p)),
            qblk,
        ],
        out_specs=qblk,
        out_shape=jax.ShapeDtypeStruct((nb, seq, width), BF16),
        scratch_shapes=[pltpu.VMEM((ncand, 2 * tq), F32)],
        compiler_params=_cparams(("arbitrary", "arbitrary", "arbitrary")),
        name="moba",
    )(q, k, vt, kmean, z)


def _dup_heads(w):
    d, wd = w.shape
    w3 = w.reshape(d, wd // HEAD_DIM, 1, HEAD_DIM)
    return jnp.broadcast_to(w3, (d, wd // HEAD_DIM, 2, HEAD_DIM)).reshape(d, 2 * wd)


def _even_plan(w_in, nb, seq):
    aq, akv = A_HEADS * HEAD_DIM, HEAD_DIM
    bq, bkv, bg = B_HEADS * HEAD_DIM, B_KV * HEAD_DIM, 3 * B_HEADS
    sizes = (aq, akv, akv, aq, bq, bkv, bkv, bkv, bkv, bkv, bkv, bg, bq)
    offs = np.concatenate([[0], np.cumsum(sizes)])
    qa, ka, va, za, qb, kc, vc, ks, vs, kw, vw, gb, zb = [w_in[:, offs[n]:offs[n + 1]] for n in range(len(sizes))]
    qw = aq + bq
    krw = PAIR * (1 + 2 * B_KV)
    w = jnp.concatenate([qa * Q_SCALE, qb * Q_SCALE, _dup_heads(ka), _dup_heads(ks), _dup_heads(kw), kc, vc, za, zb],
                        axis=1).astype(BF16)
    gb4 = gb.reshape(-1, B_KV, B_REP, 3).transpose(0, 1, 3, 2).reshape(-1, B_KV, 3 * B_REP)
    gb4 = jnp.pad(gb4, ((0, 0), (0, 0), (0, GATE_ROWS - 3 * B_REP))).reshape(-1, B_KV * GATE_ROWS)
    vrows = akv + 2 * bkv
    wt = jnp.concatenate([va, vs, vw, gb4], axis=1).T.astype(BF16)
    groups, col = [], 0
    for width, kind in ((qw, "rope"), (krw, "rope"), (bkv, "rope"), (bkv, "plain"), (qw, "silu")):
        groups.append((col, width, kind, len(groups)))
        col += width
    groups_t = ((0, vrows, "plain", 5, ATT_TQ), (vrows, B_KV * GATE_ROWS, "sigmoid", 6, None))
    rows, tm = nb * seq, ROW_TILE
    per_b = seq // tm
    widths = [(qw, BF16), (krw, BF16), (bkv, F32), (bkv, F32), (qw, F32)]
    out_shape = [jax.ShapeDtypeStruct((rows, wd), dt) for wd, dt in widths]
    out_specs = [pl.BlockSpec((tm, wd), lambda i: (i, 0)) for wd, _ in widths]
    out_shape += [jax.ShapeDtypeStruct((nb, seq // ATT_TQ, vrows, ATT_TQ), BF16),
                  jax.ShapeDtypeStruct((nb, B_KV * GATE_ROWS, seq), F32)]
    out_specs += [pl.BlockSpec((1, tm // ATT_TQ, vrows, ATT_TQ), lambda i: (i // per_b, i % per_b, 0, 0)),
                  pl.BlockSpec((1, B_KV * GATE_ROWS, tm), lambda i: (i // per_b, 0, i % per_b))]
    return w, wt, tuple(groups), groups_t, out_shape, out_specs


def _odd_plan(w_in, nb, seq):
    cw = C_HEADS * HEAD_DIM
    q, k, v, z = [w_in[:, n * cw:(n + 1) * cw] for n in range(4)]
    w = jnp.concatenate([q * Q_SCALE, k, z], axis=1).astype(BF16)
    wt = v.T.astype(BF16)
    groups = ((0, cw, "rope", 0), (cw, cw, "rope_mean", 1), (2 * cw, cw, "silu", 2))
    groups_t = ((0, cw, "plain", 3, MOBA_BLOCK),)
    rows, tm = nb * seq, ROW_TILE
    per_b = seq // tm
    widths = [(cw, BF16), (cw, BF16), (cw, F32)]
    out_shape = [jax.ShapeDtypeStruct((rows, wd), dt) for wd, dt in widths]
    out_specs = [pl.BlockSpec((tm, wd), lambda i: (i, 0)) for wd, _ in widths]
    out_shape += [jax.ShapeDtypeStruct((nb, seq // MOBA_BLOCK, cw, MOBA_BLOCK), BF16),
                  jax.ShapeDtypeStruct((nb, seq // MOBA_BLOCK, 1, cw), F32)]
    out_specs += [pl.BlockSpec((1, tm // MOBA_BLOCK, cw, MOBA_BLOCK), lambda i: (i // per_b, i % per_b, 0, 0)),
                  pl.BlockSpec((1, tm // MOBA_BLOCK, 1, cw), lambda i: (i // per_b, i % per_b, 0, 0))]
    return w, wt, groups, groups_t, out_shape, out_specs


def _rope_tables(seq):
    inv = ROPE_THETA ** (-jnp.arange(0, HEAD_DIM, 2, dtype=F32) / HEAD_DIM)
    ang = jnp.arange(seq, dtype=F32)[:, None] * inv[None, :]
    cos, sin = jnp.cos(ang), jnp.sin(ang)
    reps = PAIR // (HEAD_DIM // 2)
    sign = np.tile(np.concatenate([-np.ones(HEAD_DIM // 2), np.ones(HEAD_DIM // 2)]), PAIR // HEAD_DIM)
    return jnp.tile(cos, (1, reps)), jnp.tile(sin, (1, reps)) * jnp.asarray(sign, F32)[None, :]


def _overlap_matrix_t(seq, nch):
    nc = (seq - CMP_LEN) // CMP_STRIDE + 1
    nsb = seq // SLC_LEN
    cst = np.arange(nc) * CMP_STRIDE
    jj = np.arange(nsb)
    ov = ((cst[None, :] < (jj[:, None] + 1) * SLC_LEN) & (cst[None, :] + CMP_LEN > jj[:, None] * SLC_LEN))
    full = np.zeros((nsb, nch), np.float32)
    full[:, :nc] = ov
    return jnp.asarray(full, BF16), nc, nsb


def _cmp_weights(pe, w1, w2, transpose_out):
    pe2 = jnp.concatenate([pe, pe], axis=1)
    w1t = w1.reshape(CMP_LEN, HEAD_DIM, CMP_HIDDEN)
    zeros = jnp.zeros_like(w1t)
    w1p = jnp.concatenate([jnp.concatenate([w1t, zeros], axis=2), jnp.concatenate([zeros, w1t], axis=2)], axis=1)
    w2o = w2.T if transpose_out else jnp.concatenate([w2, w2], axis=1)
    return pe2, w1p.astype(BF16), w2o.astype(BF16)


def kernel(x, c, w_ada, b_ada, norm_g, w_in_even, a_sinks, cmp_pe_k, cmp_w1_k, cmp_w2_k, cmp_pe_v, cmp_w1_v, cmp_w2_v, w_out_even, w_in_odd, w_out_odd, final_g):
    nb, seq, d = x.shape
    depth = w_ada.shape[0]
    assert seq % ROW_TILE == 0 and seq % MOBA_BLOCK == 0 and ATT_TQ % SLC_LEN == 0
    assert (seq // SLC_LEN) % 8 == 0 and seq >= WIN_LEN + ATT_TQ
    cos_t, sin_t = _rope_tables(seq)
    ada = _ada(c, w_ada, b_ada)
    nch = seq // CMP_STRIDE
    ovl_t, nc, nsb = _overlap_matrix_t(seq, nch)
    fg = final_g.reshape(1, d)
    x2 = x.reshape(nb * seq, d)
    r3 = lambda a: a.reshape(nb, seq, a.shape[-1])
    for layer in range(depth):
        li = layer // 2
        final = layer == depth - 1
        ng = norm_g[layer].reshape(1, d)
        if layer % 2 == 0:
            w, wt, groups, groups_t, out_shape, out_specs = _even_plan(w_in_even[li], nb, seq)
            q, kr, kc, vc, z, vt, gates_t = _proj(x2, ada, layer, ng, cos_t, sin_t, w, wt, groups, groups_t,
                                                  out_shape, out_specs, seq)
            q, kr, kc, vc, z = map(r3, (q, kr, kc, vc, z))
            kcmp, vcmp_t = _compress(kc, vc, *_cmp_weights(cmp_pe_k[li], cmp_w1_k[li], cmp_w2_k[li], False),
                                     *_cmp_weights(cmp_pe_v[li], cmp_w1_v[li], cmp_w2_v[li], True))
            oa = _swa(a_sinks[li], q, kr, vt, z, ATT_TQ)
            ob = _nsa(q, kcmp, vcmp_t, kr, vt, gates_t, z, ovl_t, nc, nsb, ATT_TQ)
            o_list = [oa.reshape(nb * seq, -1), ob.reshape(nb * seq, -1)]
            w_out = w_out_even[li].astype(BF16)
        else:
            w, wt, groups, groups_t, out_shape, out_specs = _odd_plan(w_in_odd[li], nb, seq)
            q, k, z, vt, kmean = _proj(x2, ada, layer, ng, cos_t, sin_t, w, wt, groups, groups_t,
                                       out_shape, out_specs, seq, moba_mean=4)
            o = _moba(r3(q), r3(k), vt, kmean.reshape(nb, seq // MOBA_BLOCK, -1), r3(z))
            o_list = [o.reshape(nb * seq, -1)]
            w_out = w_out_odd[li].astype(BF16)
        x2 = _out_proj(x2, ada, layer, o_list, w_out, fg, final, seq)
    return x2.reshape(nb, seq, d)
```

```python
import functools

import numpy as np
import jax
import jax.numpy as jnp
from jax import lax
from jax.experimental import pallas as pl
from jax.experimental.pallas import tpu as pltpu

D_MODEL = 1024
HEAD_DIM = 64
PAIR = 2 * HEAD_DIM
ROPE_THETA = 10000.0
RMS_EPS = 1e-6
A_HEADS = 8
A_WINDOW = 128
B_HEADS = 8
B_KV = 2
B_REP = B_HEADS // B_KV
CMP_LEN = 32
CMP_STRIDE = 16
CMP_HIDDEN = 256
SLC_LEN = 64
SLC_TOPN = 8
WIN_LEN = 512
FORCE_BONUS = 1e4
C_HEADS = 16
MOBA_BLOCK = 256
MOBA_TOPK = 3
Q_SCALE = HEAD_DIM ** -0.5
LOG2E = 1.4426950408889634

NEG = -1e30
MXU_COLS = 256
ROW_TILE = 512
ATT_TQ = 128
MOBA_PAIRS_PER_STEP = 4
SUM_ROWS = 16
GATE_ROWS = 16
VMEM_LIMIT = 48 * 1024 * 1024

BF16 = jnp.bfloat16
F32 = jnp.float32


def _dot_nt(a, b):
    return lax.dot_general(a, b, (((1,), (1,)), ((), ())), preferred_element_type=F32)


def _dot(a, b):
    return jnp.dot(a, b, preferred_element_type=F32)


def _cparams(sem):
    return pltpu.CompilerParams(dimension_semantics=sem, vmem_limit_bytes=VMEM_LIMIT)


def _tile_lanes(a, n):
    return jnp.concatenate([a] * n, axis=1) if n > 1 else a


def _ada_kernel(c_ref, w_ref, b_ref, o_ref):
    c = c_ref[...]
    ca = (c * jax.nn.sigmoid(c)).astype(BF16)
    o_ref[0, 0] = _dot(ca, w_ref[0].astype(BF16)) + b_ref[0, 0]


def _ada(c, w_ada, b_ada):
    depth, d, _ = w_ada.shape
    nb = c.shape[0]
    b4 = b_ada.reshape(depth, 3, 1, d)
    out = pl.pallas_call(
        _ada_kernel,
        grid=(depth, 3),
        in_specs=[
            pl.BlockSpec((nb, d), lambda l, j: (0, 0)),
            pl.BlockSpec((1, d, d), lambda l, j: (l, 0, j)),
            pl.BlockSpec((1, 1, 1, d), lambda l, j: (l, j, 0, 0)),
        ],
        out_specs=pl.BlockSpec((1, 1, nb, d), lambda l, j: (l, j, 0, 0)),
        out_shape=jax.ShapeDtypeStruct((depth, 3, nb, d), F32),
        compiler_params=_cparams(("arbitrary", "arbitrary")),
        name="ada",
    )(c, w_ada, b4)
    return out.reshape(depth, 3, nb, 1, d)


def _rope(a, cos, sin_signed):
    w = a.shape[-1]
    lane = lax.broadcasted_iota(jnp.int32, a.shape, 1)
    first_half = (lane % HEAD_DIM) < (HEAD_DIM // 2)
    partner = jnp.where(first_half, pltpu.roll(a, w - HEAD_DIM // 2, 1), pltpu.roll(a, HEAD_DIM // 2, 1))
    return a * cos + partner * sin_signed


def _proj_kernel(groups, groups_t, moba_mean, x_ref, shift_ref, scale_ref, g_ref, cos_ref, sin_ref, w_ref, wt_ref,
                 *out_refs):
    x = x_ref[...]
    y = x * lax.rsqrt(jnp.mean(x * x, axis=-1, keepdims=True) + RMS_EPS)
    h = (y * g_ref[...]) * (1.0 + scale_ref[0, 0, 0]) + shift_ref[0, 0, 0]
    hb = h.astype(BF16)
    tm = hb.shape[0]
    cos = _tile_lanes(cos_ref[...], MXU_COLS // PAIR)
    sin = _tile_lanes(sin_ref[...], MXU_COLS // PAIR)
    for (w_start, width, kind, out_idx) in groups:
        o_ref = out_refs[out_idx]
        for c0 in range(0, width, MXU_COLS):
            cw = min(MXU_COLS, width - c0)
            acc = _dot(hb, w_ref[:, w_start + c0:w_start + c0 + cw])
            if kind in ("rope", "rope_mean"):
                acc = _rope(acc, cos[:, :cw], sin[:, :cw])
            elif kind == "silu":
                acc = acc * jax.nn.sigmoid(acc)
            o_ref[:, c0:c0 + cw] = acc.astype(o_ref.dtype)
            if kind == "rope_mean":
                km_ref = out_refs[moba_mean]
                for j in range(tm // MOBA_BLOCK):
                    blk = acc[j * MOBA_BLOCK:(j + 1) * MOBA_BLOCK]
                    km_ref[0, j, :, c0:c0 + cw] = jnp.sum(blk, axis=0, keepdims=True) * (1.0 / MOBA_BLOCK)
    for (r_start, n_rows, kind, out_idx, chunk) in groups_t:
        o_ref = out_refs[out_idx]
        for r0 in range(0, n_rows, MXU_COLS):
            rw = min(MXU_COLS, n_rows - r0)
            acc_t = _dot_nt(wt_ref[r_start + r0:r_start + r0 + rw, :], hb)
            if kind == "sigmoid":
                o_ref[0, r0:r0 + rw, :] = jax.nn.sigmoid(acc_t)
            else:
                for jj in range(tm // chunk):
                    o_ref[0, jj, r0:r0 + rw, :] = acc_t[:, jj * chunk:(jj + 1) * chunk].astype(o_ref.dtype)


def _proj(x2, ada, layer, norm_g, cos_t, sin_t, w, wt, groups, groups_t, out_shape, out_specs, seq, moba_mean=None):
    rows, d = x2.shape
    tm = ROW_TILE
    per_b = seq // tm
    return pl.pallas_call(
        functools.partial(_proj_kernel, groups, groups_t, moba_mean),
        grid=(rows // tm,),
        in_specs=[
            pl.BlockSpec((tm, d), lambda i: (i, 0)),
            pl.BlockSpec((1, 1, 1, 1, d), lambda i: (layer, 0, i // per_b, 0, 0)),
            pl.BlockSpec((1, 1, 1, 1, d), lambda i: (layer, 1, i // per_b, 0, 0)),
            pl.BlockSpec((1, d), lambda i: (0, 0)),
            pl.BlockSpec((tm, PAIR), lambda i: (i % per_b, 0)),
            pl.BlockSpec((tm, PAIR), lambda i: (i % per_b, 0)),
            pl.BlockSpec(w.shape, lambda i: (0, 0)),
            pl.BlockSpec(wt.shape, lambda i: (0, 0)),
        ],
        out_specs=out_specs,
        out_shape=out_shape,
        compiler_params=_cparams(("arbitrary",)),
        name="proj",
    )(x2, ada, ada, norm_g, cos_t, sin_t, w, wt)


def _out_kernel(n_o, final, *refs):
    x_ref, gate_ref = refs[0], refs[1]
    o_refs = refs[2:2 + n_o]
    w_ref = refs[2 + n_o]
    fg_ref = refs[3 + n_o]
    out_ref = refs[4 + n_o]
    y = None
    k0 = 0
    for o_ref in o_refs:
        kw = o_ref.shape[-1]
        part = _dot(o_ref[...], w_ref[k0:k0 + kw, :])
        y = part if y is None else y + part
        k0 += kw
    xn = x_ref[...] + gate_ref[0, 0, 0] * y
    if final:
        xn = (xn * lax.rsqrt(jnp.mean(xn * xn, axis=-1, keepdims=True) + RMS_EPS)) * fg_ref[...]
    out_ref[...] = xn


def _out_proj(x2, ada, layer, o_list, w_out, final_g, final, seq):
    rows, d = x2.shape
    tm = ROW_TILE
    per_b = seq // tm
    n_o = len(o_list)
    in_specs = [
        pl.BlockSpec((tm, d), lambda i: (i, 0)),
        pl.BlockSpec((1, 1, 1, 1, d), lambda i: (layer, 2, i // per_b, 0, 0)),
    ]
    in_specs += [pl.BlockSpec((tm, o.shape[-1]), lambda i: (i, 0)) for o in o_list]
    in_specs += [pl.BlockSpec(w_out.shape, lambda i: (0, 0)), pl.BlockSpec((1, d), lambda i: (0, 0))]
    return pl.pallas_call(
        functools.partial(_out_kernel, n_o, final),
        grid=(rows // tm,),
        in_specs=in_specs,
        out_specs=pl.BlockSpec((tm, d), lambda i: (i, 0)),
        out_shape=jax.ShapeDtypeStruct((rows, d), F32),
        compiler_params=_cparams(("arbitrary",)),
        name="out_proj",
    )(x2, ada, *o_list, w_out, final_g)


def _stack_heads(q_pairs):
    lane = lax.broadcasted_iota(jnp.int32, (1, PAIR), 1)
    m_lo = (lane < HEAD_DIM).astype(BF16)
    m_hi = (lane >= HEAD_DIM).astype(BF16)
    blocks = []
    for p in range(q_pairs.shape[-1] // PAIR):
        qp = q_pairs[:, p * PAIR:(p + 1) * PAIR]
        blocks += [qp * m_lo, qp * m_hi]
    return jnp.concatenate(blocks, axis=0)


def _flash_init(d_rows, n_cols):
    return jnp.full((1, n_cols), NEG, F32), jnp.zeros((1, n_cols), F32), jnp.zeros((d_rows, n_cols), F32)


def _flash(carry, s_t, v_t_chunks):
    m, l, acc = carry
    m_new = jnp.maximum(m, jnp.max(s_t, axis=0, keepdims=True))
    alpha = jnp.exp2(m - m_new)
    e = jnp.exp2(s_t - m_new).astype(BF16)
    l = alpha * l + _dot(jnp.ones((SUM_ROWS, e.shape[0]), BF16), e)[0:1]
    acc, k0 = alpha * acc, 0
    for vt in v_t_chunks:
        acc = acc + _dot(vt, e[k0:k0 + vt.shape[1]])
        k0 += vt.shape[1]
    return m_new, l, acc


def _topn_rows(score, valid, n_cand, topn):
    ridx = lax.broadcasted_iota(jnp.int32, score.shape, 0)
    rank = jnp.zeros(score.shape, F32)
    for jp in range(n_cand):
        row = score[jp:jp + 1, :]
        earlier = jnp.where(ridx > jp, 1.0, 0.0)
        rank = rank + jnp.where(row > score, 1.0, jnp.where(row == score, earlier, 0.0))
    return valid & (rank < topn)


def _swa_kernel(tq, sink_ref, q_ref, k_ref, vt_ref, z_ref, o_ref):
    i = pl.program_id(1)
    n_prev = -(-(A_WINDOW - 1) // tq)
    n_ch = n_prev + 1
    span = n_ch * tq
    c0 = jnp.maximum(i - n_prev, 0)
    start = pl.multiple_of(c0 * tq, tq)
    kk = k_ref[0, pl.ds(start, span), :]
    qs = _stack_heads(q_ref[0])
    s_t = _dot_nt(kk, qs)
    qpos = i * tq + lax.broadcasted_iota(jnp.int32, (span, tq), 1)
    kpos = start + lax.broadcasted_iota(jnp.int32, (span, tq), 0)
    diff = qpos - kpos
    bias = jnp.where(diff >= 0, jnp.where(diff < A_WINDOW, 0.0, NEG), NEG)
    s_t = s_t + _tile_lanes(bias, A_HEADS)
    sink = jnp.concatenate([jnp.full((1, tq), sink_ref[hd] * LOG2E, F32) for hd in range(A_HEADS)], axis=1)
    m = jnp.maximum(jnp.max(s_t, axis=0, keepdims=True), sink)
    e = jnp.exp2(s_t - m)
    den = jnp.sum(e, axis=0, keepdims=True) + jnp.exp2(sink - m)
    eb = e.astype(BF16)
    acc = None
    for cc in range(n_ch):
        part = _dot(vt_ref[0, c0 + cc], eb[cc * tq:(cc + 1) * tq])
        acc = part if acc is None else acc + part
    o_t = acc * (1.0 / den)
    heads = jnp.concatenate([o_t[:, hd * tq:(hd + 1) * tq] for hd in range(A_HEADS)], axis=0)
    o_ref[0] = (heads.T * z_ref[0]).astype(o_ref.dtype)


def _swa(sinks, q, kr, vt, z, tq):
    nb, seq, _ = q.shape
    aw = A_HEADS * HEAD_DIM
    return pl.pallas_call(
        functools.partial(_swa_kernel, tq),
        grid=(nb, seq // tq),
        in_specs=[
            pl.BlockSpec(memory_space=pltpu.SMEM),
            pl.BlockSpec((1, tq, aw), lambda b, i: (b, i, 0)),
            pl.BlockSpec((1, seq, PAIR), lambda b, i: (b, 0, 0)),
            pl.BlockSpec((1, seq // tq, HEAD_DIM, tq), lambda b, i: (b, 0, 0, 0)),
            pl.BlockSpec((1, tq, aw), lambda b, i: (b, i, 0)),
        ],
        out_specs=pl.BlockSpec((1, tq, aw), lambda b, i: (b, i, 0)),
        out_shape=jax.ShapeDtypeStruct((nb, seq, aw), BF16),
        compiler_params=_cparams(("arbitrary", "arbitrary")),
        name="swa",
    )(sinks, q, kr, vt, z)


def _cmp_hidden(x_ref, pe_ref, w1_ref, nch):
    top, bot = None, None
    for l in range(CMP_STRIDE):
        rows = x_ref[0, pl.ds(l, nch, stride=CMP_STRIDE), :]
        t = _dot((rows + pe_ref[l:l + 1, :]).astype(BF16), w1_ref[l])
        b = _dot((rows + pe_ref[CMP_STRIDE + l:CMP_STRIDE + l + 1, :]).astype(BF16), w1_ref[CMP_STRIDE + l])
        top = t if top is None else top + t
        bot = b if bot is None else bot + b
    return jax.nn.gelu(top + pltpu.roll(bot, nch - 1, 0))


def _compress_kernel(kc_ref, vc_ref, pek_ref, w1k_ref, w2k_ref, pev_ref, w1v_ref, w2v_ref, ko_ref, vo_ref):
    nch = ko_ref.shape[2]
    hk = _cmp_hidden(kc_ref, pek_ref, w1k_ref, nch).astype(BF16)
    hv = _cmp_hidden(vc_ref, pev_ref, w1v_ref, nch).astype(BF16)
    for g in range(B_KV):
        gs = slice(g * CMP_HIDDEN, (g + 1) * CMP_HIDDEN)
        ko_ref[0, g] = _dot(hk[:, gs], w2k_ref[...]).astype(ko_ref.dtype)
        vo_ref[0, g] = _dot_nt(w2v_ref[...], hv[:, gs]).astype(vo_ref.dtype)


def _compress(kc, vc, pek, w1k, w2k, pev, w1v, w2v):
    nb, seq, _ = kc.shape
    nch = seq // CMP_STRIDE
    full = lambda a: pl.BlockSpec(a.shape, lambda b: (0,) * a.ndim)
    blk = pl.BlockSpec((1, seq, PAIR), lambda b: (b, 0, 0))
    return pl.pallas_call(
        _compress_kernel,
        grid=(nb,),
        in_specs=[blk, blk, full(pek), full(w1k), full(w2k), full(pev), full(w1v), full(w2v)],
        out_specs=[pl.BlockSpec((1, B_KV, nch, PAIR), lambda b: (b, 0, 0, 0)),
                   pl.BlockSpec((1, B_KV, HEAD_DIM, nch), lambda b: (b, 0, 0, 0))],
        out_shape=[jax.ShapeDtypeStruct((nb, B_KV, nch, PAIR), BF16),
                   jax.ShapeDtypeStruct((nb, B_KV, HEAD_DIM, nch), BF16)],
        compiler_params=_cparams(("arbitrary",)),
        name="compress",
    )(kc, vc, pek, w1k, w2k, pev, w1v, w2v)


def _nsa_kernel(tq, nc, nsb, q_ref, kc_ref, vct_ref, ks0_ref, ks1_ref, vs0_ref, vs1_ref, kw0_ref, kw1_ref,
                vw0_ref, vw1_ref, gt_ref, z_ref, ovl_ref, o_ref, bias_ref):
    i = pl.program_id(1)
    groups = range(B_KV)
    ks_refs, vs_refs = (ks0_ref, ks1_ref), (vs0_ref, vs1_ref)
    kw_refs, vw_refs = (kw0_ref, kw1_ref), (vw0_ref, vw1_ref)
    gw = B_REP * HEAD_DIM
    nr = B_REP * tq
    qs = [_stack_heads(q_ref[0, :, g * gw:(g + 1) * gw]) for g in groups]
    t_row = i * tq + lax.broadcasted_iota(jnp.int32, (1, tq), 1)

    nch = kc_ref.shape[2]
    cidx = lax.broadcasted_iota(jnp.int32, (nch, tq), 0)
    ok_c = (cidx < nc) & (t_row >= cidx * CMP_STRIDE + (CMP_LEN - 1))
    bias_c = _tile_lanes(jnp.where(ok_c, 0.0, NEG), B_REP)
    jr = lax.broadcasted_iota(jnp.int32, (nsb, tq), 0)
    tb = t_row // SLC_LEN
    valid = jr <= tb
    forced = (jr == 0) | (jr == tb) | (jr == tb - 1)
    o_cmp = []
    for g in groups:
        s = _dot_nt(kc_ref[0, g], qs[g]) + bias_c
        m = jnp.max(s, axis=0, keepdims=True)
        m = jnp.where(m > 0.5 * NEG, m, 0.0)
        e = jnp.exp2(s - m)
        den = jnp.sum(e, axis=0, keepdims=True)
        p_t = (e * (1.0 / jnp.where(den > 0, den, 1.0))).astype(BF16)
        o_cmp.append(_dot(vct_ref[0, g], p_t))
        imp = None
        for r in range(B_REP):
            part = _dot(ovl_ref[...], p_t[:, r * tq:(r + 1) * tq])
            imp = part if imp is None else imp + part
        score = jnp.where(valid, jnp.where(forced, FORCE_BONUS, imp), -jnp.inf)
        bias_ref[g] = jnp.where(_topn_rows(score, valid, nsb, SLC_TOPN), 0.0, NEG)

    n_prev = -(-(WIN_LEN - 1) // tq)
    n_ch = n_prev + 1
    span = n_ch * tq
    c0 = jnp.maximum(i - n_prev, 0)
    start = pl.multiple_of(c0 * tq, tq)
    diff = t_row - (start + lax.broadcasted_iota(jnp.int32, (span, tq), 0))
    bias_w = _tile_lanes(jnp.where(diff >= 0, jnp.where(diff < WIN_LEN, 0.0, NEG), NEG), B_REP)
    o_win = []
    for g in groups:
        sw = _dot_nt(kw_refs[g][0, pl.ds(start, span), :], qs[g]) + bias_w
        _, l_w, acc_w = _flash(_flash_init(HEAD_DIM, nr), sw, [vw_refs[g][0, c0 + cc] for cc in range(n_ch)])
        o_win.append(acc_w * (1.0 / l_w))

    blk_per_chunk = tq // SLC_LEN
    krow = lax.broadcasted_iota(jnp.int32, (tq, tq), 0)
    qcol = lax.broadcasted_iota(jnp.int32, (tq, tq), 1)
    causal = jnp.where(krow <= qcol, 0.0, NEG)

    def chunk_scores(g, c, diagonal):
        rows = []
        for bb in range(blk_per_chunk):
            row = bias_ref[g, pl.ds(c * blk_per_chunk + bb, 1), :]
            if diagonal:
                rows.append(jnp.broadcast_to(row, (SLC_LEN, tq)) + causal[bb * SLC_LEN:(bb + 1) * SLC_LEN])
            else:
                rows.append(jnp.broadcast_to(row + jnp.where(c < i, 0.0, NEG), (SLC_LEN, tq)))
        off = pl.multiple_of(c * tq, tq)
        return _dot_nt(ks_refs[g][0, pl.ds(off, tq), :], qs[g]) + _tile_lanes(jnp.concatenate(rows, axis=0), B_REP)

    def two_chunks(carries, c_a, c_b, first):
        s_a = [chunk_scores(g, c_a, first) for g in groups]
        s_b = [chunk_scores(g, c_b, False) for g in groups]
        return tuple(_flash(_flash(carries[g], s_a[g], [vs_refs[g][0, c_a]]), s_b[g], [vs_refs[g][0, c_b]])
                     for g in groups)

    carries = two_chunks((_flash_init(HEAD_DIM, nr),) * B_KV, i, 0, True)
    carries = lax.fori_loop(1, i // 2 + 1, lambda u, c: two_chunks(c, 2 * u - 1, 2 * u, False), carries)

    heads = []
    for g in groups:
        _, l_s, acc_s = carries[g]
        o_slc = acc_s * (1.0 / l_s)
        gt = gt_ref[0, g * GATE_ROWS:(g + 1) * GATE_ROWS]
        for r in range(B_REP):
            cs = slice(r * tq, (r + 1) * tq)
            heads.append(gt[r:r + 1] * o_cmp[g][:, cs] + gt[B_REP + r:B_REP + r + 1] * o_slc[:, cs]
                         + gt[2 * B_REP + r:2 * B_REP + r + 1] * o_win[g][:, cs])
    o_t = jnp.concatenate(heads, axis=0)
    o_ref[0] = (o_t.T * z_ref[0]).astype(o_ref.dtype)


def _nsa(q, kcmp, vcmp_t, kr, vt, gates_t, z, ovl_t, nc, nsb, tq):
    nb, seq, _ = q.shape
    bw = B_HEADS * HEAD_DIM
    nch = kcmp.shape[2]
    kblk = lambda col: pl.BlockSpec((1, seq, PAIR), lambda b, i: (b, 0, col))
    vblk = lambda row: pl.BlockSpec((1, seq // tq, HEAD_DIM, tq), lambda b, i: (b, 0, row, 0))
    return pl.pallas_call(
        functools.partial(_nsa_kernel, tq, nc, nsb),
        grid=(nb, seq // tq),
        in_specs=[
            pl.BlockSpec((1, tq, bw), lambda b, i: (b, i, 1)),
            pl.BlockSpec((1, B_KV, nch, PAIR), lambda b, i: (b, 0, 0, 0)),
            pl.BlockSpec((1, B_KV, HEAD_DIM, nch), lambda b, i: (b, 0, 0, 0)),
            kblk(1), kblk(2), vblk(1), vblk(2), kblk(3), kblk(4), vblk(3), vblk(4),
            pl.BlockSpec((1, B_KV * GATE_ROWS, tq), lambda b, i: (b, 0, i)),
            pl.BlockSpec((1, tq, bw), lambda b, i: (b, i, 1)),
            pl.BlockSpec(ovl_t.shape, lambda b, i: (0, 0)),
        ],
        out_specs=pl.BlockSpec((1, tq, bw), lambda b, i: (b, i, 0)),
        out_shape=jax.ShapeDtypeStruct((nb, seq, bw), BF16),
        scratch_shapes=[pltpu.VMEM((B_KV, nsb, tq), F32)],
        compiler_params=_cparams(("arbitrary", "arbitrary")),
        name="nsa",
    )(q, kcmp, vcmp_t, kr, kr, vt, vt, kr, kr, vt, vt, gates_t, z, ovl_t)


def _moba_kernel(nblk, npair, q_ref, k_ref, vt_ref, km_ref, z_ref, o_ref, bias_ref):
    i = pl.program_id(2)
    tq = MOBA_BLOCK
    nr = 2 * tq
    ncand = bias_ref.shape[1]
    pairs = range(npair)
    lanes = [slice(p * PAIR, (p + 1) * PAIR) for p in pairs]
    qs = [_stack_heads(q_ref[0, :, lanes[p]]) for p in pairs]

    jr = lax.broadcasted_iota(jnp.int32, (ncand, nr), 0)
    past = jr < i
    for p in pairs:
        km = km_ref[0, :, lanes[p]]
        km = jnp.concatenate([km, jnp.zeros((ncand - nblk, PAIR), F32)], axis=0).astype(BF16)
        gs = _dot_nt(km, qs[p])
        sel = _topn_rows(jnp.where(past, gs, -jnp.inf), past, nblk, MOBA_TOPK)
        bias_ref[p] = jnp.where(sel, 0.0, NEG)

    krow = lax.broadcasted_iota(jnp.int32, (tq, tq), 0)
    qcol = lax.broadcasted_iota(jnp.int32, (tq, tq), 1)
    causal = _tile_lanes(jnp.where(krow <= qcol, 0.0, NEG), 2)
    ones = jnp.ones((SUM_ROWS, tq), BF16)

    def scores(p, blk, bias):
        off = pl.multiple_of(blk * tq, tq)
        return _dot_nt(k_ref[0, pl.ds(off, tq), lanes[p]], qs[p]) + bias

    def past_scores(p, blk):
        return scores(p, blk, bias_ref[p, pl.ds(blk, 1), :])

    def flash(carry, s_t, p, blk):
        m, l, acc = carry
        m_new = jnp.maximum(m, jnp.max(s_t, axis=0, keepdims=True))
        alpha = jnp.exp2(m - m_new)
        e = jnp.exp2(s_t - m_new).astype(BF16)
        pv = jnp.concatenate(
            [_dot(vt_ref[0, blk, pl.ds(p * PAIR + hh * HEAD_DIM, HEAD_DIM), :], e[:, hh * tq:(hh + 1) * tq])
             for hh in range(2)], axis=1)
        return m_new, alpha * l + _dot(ones, e)[0:1], alpha * acc + pv

    def two_chunks(carries, blk_a, blk_b, first):
        s_a = [scores(p, blk_a, causal) if first else past_scores(p, blk_a) for p in pairs]
        s_b = [past_scores(p, blk_b) for p in pairs]
        return tuple(flash(flash(carries[p], s_a[p], p, blk_a), s_b[p], p, blk_b) for p in pairs)

    init = (jnp.full((1, nr), NEG, F32), jnp.zeros((1, nr), F32), jnp.zeros((HEAD_DIM, nr), F32))
    carries = two_chunks((init,) * npair, i, 0, True)
    carries = lax.fori_loop(1, i // 2 + 1, lambda u, c: two_chunks(c, 2 * u - 1, 2 * u, False), carries)
    outs = []
    for p in pairs:
        _, l_f, acc_f = carries[p]
        o_t = acc_f * (1.0 / l_f)
        outs += [o_t[:, :tq], o_t[:, tq:]]
    o_ref[0] = (jnp.concatenate(outs, axis=0).T * z_ref[0]).astype(o_ref.dtype)


def _moba(q, k, vt, kmean, z):
    nb, seq, width = q.shape
    tq = MOBA_BLOCK
    nblk = seq // MOBA_BLOCK
    ncand = -(-nblk // 16) * 16
    npair = MOBA_PAIRS_PER_STEP
    gw = npair * PAIR
    qblk = pl.BlockSpec((1, tq, gw), lambda b, p, i: (b, i, p))
    return pl.pallas_call(
        functools.partial(_moba_kernel, nblk, npair),
        grid=(nb, width // gw, seq // tq),
        in_specs=[
            qblk,
            pl.BlockSpec((1, seq, gw), lambda b, p, i: (b, 0, p)),
            pl.BlockSpec((1, nblk, gw, tq), lambda b, p, i: (b, 0, p, 0)),
            pl.BlockSpec((1, nblk, gw), lambda b, p, i: (b, 0, p)),
            qblk,
        ],
        out_specs=qblk,
        out_shape=jax.ShapeDtypeStruct((nb, seq, width), BF16),
        scratch_shapes=[pltpu.VMEM((npair, ncand, 2 * tq), F32)],
        compiler_params=_cparams(("arbitrary", "arbitrary", "arbitrary")),
        name="moba",
    )(q, k, vt, kmean, z)


def _dup_heads(w):
    d, wd = w.shape
    w3 = w.reshape(d, wd // HEAD_DIM, 1, HEAD_DIM)
    return jnp.broadcast_to(w3, (d, wd // HEAD_DIM, 2, HEAD_DIM)).reshape(d, 2 * wd)


def _even_plan(w_in, nb, seq):
    aq, akv = A_HEADS * HEAD_DIM, HEAD_DIM
    bq, bkv, bg = B_HEADS * HEAD_DIM, B_KV * HEAD_DIM, 3 * B_HEADS
    sizes = (aq, akv, akv, aq, bq, bkv, bkv, bkv, bkv, bkv, bkv, bg, bq)
    offs = np.concatenate([[0], np.cumsum(sizes)])
    qa, ka, va, za, qb, kc, vc, ks, vs, kw, vw, gb, zb = [w_in[:, offs[n]:offs[n + 1]] for n in range(len(sizes))]
    qw = aq + bq
    krw = PAIR * (1 + 2 * B_KV)
    w = jnp.concatenate([qa * (Q_SCALE * LOG2E), qb * (Q_SCALE * LOG2E), _dup_heads(ka), _dup_heads(ks), _dup_heads(kw), kc, vc, za, zb],
                        axis=1).astype(BF16)
    gb4 = gb.reshape(-1, B_KV, B_REP, 3).transpose(0, 1, 3, 2).reshape(-1, B_KV, 3 * B_REP)
    gb4 = jnp.pad(gb4, ((0, 0), (0, 0), (0, GATE_ROWS - 3 * B_REP))).reshape(-1, B_KV * GATE_ROWS)
    vrows = akv + 2 * bkv
    wt = jnp.concatenate([va, vs, vw, gb4], axis=1).T.astype(BF16)
    groups, col = [], 0
    for width, kind in ((qw, "rope"), (krw, "rope"), (bkv, "rope"), (bkv, "plain"), (qw, "silu")):
        groups.append((col, width, kind, len(groups)))
        col += width
    groups_t = ((0, vrows, "plain", 5, ATT_TQ), (vrows, B_KV * GATE_ROWS, "sigmoid", 6, None))
    rows, tm = nb * seq, ROW_TILE
    per_b = seq // tm
    widths = [(qw, BF16), (krw, BF16), (bkv, F32), (bkv, F32), (qw, F32)]
    out_shape = [jax.ShapeDtypeStruct((rows, wd), dt) for wd, dt in widths]
    out_specs = [pl.BlockSpec((tm, wd), lambda i: (i, 0)) for wd, _ in widths]
    out_shape += [jax.ShapeDtypeStruct((nb, seq // ATT_TQ, vrows, ATT_TQ), BF16),
                  jax.ShapeDtypeStruct((nb, B_KV * GATE_ROWS, seq), F32)]
    out_specs += [pl.BlockSpec((1, tm // ATT_TQ, vrows, ATT_TQ), lambda i: (i // per_b, i % per_b, 0, 0)),
                  pl.BlockSpec((1, B_KV * GATE_ROWS, tm), lambda i: (i // per_b, 0, i % per_b))]
    return w, wt, tuple(groups), groups_t, out_shape, out_specs


def _odd_plan(w_in, nb, seq):
    cw = C_HEADS * HEAD_DIM
    q, k, v, z = [w_in[:, n * cw:(n + 1) * cw] for n in range(4)]
    w = jnp.concatenate([q * (Q_SCALE * LOG2E), k, z], axis=1).astype(BF16)
    wt = v.T.astype(BF16)
    groups = ((0, cw, "rope", 0), (cw, cw, "rope_mean", 1), (2 * cw, cw, "silu", 2))
    groups_t = ((0, cw, "plain", 3, MOBA_BLOCK),)
    rows, tm = nb * seq, ROW_TILE
    per_b = seq // tm
    widths = [(cw, BF16), (cw, BF16), (cw, F32)]
    out_shape = [jax.ShapeDtypeStruct((rows, wd), dt) for wd, dt in widths]
    out_specs = [pl.BlockSpec((tm, wd), lambda i: (i, 0)) for wd, _ in widths]
    out_shape += [jax.ShapeDtypeStruct((nb, seq // MOBA_BLOCK, cw, MOBA_BLOCK), BF16),
                  jax.ShapeDtypeStruct((nb, seq // MOBA_BLOCK, 1, cw), F32)]
    out_specs += [pl.BlockSpec((1, tm // MOBA_BLOCK, cw, MOBA_BLOCK), lambda i: (i // per_b, i % per_b, 0, 0)),
                  pl.BlockSpec((1, tm // MOBA_BLOCK, 1, cw), lambda i: (i // per_b, i % per_b, 0, 0))]
    return w, wt, groups, groups_t, out_shape, out_specs


def _rope_tables(seq):
    inv = ROPE_THETA ** (-jnp.arange(0, HEAD_DIM, 2, dtype=F32) / HEAD_DIM)
    ang = jnp.arange(seq, dtype=F32)[:, None] * inv[None, :]
    cos, sin = jnp.cos(ang), jnp.sin(ang)
    reps = PAIR // (HEAD_DIM // 2)
    sign = np.tile(np.concatenate([-np.ones(HEAD_DIM // 2), np.ones(HEAD_DIM // 2)]), PAIR // HEAD_DIM)
    return jnp.tile(cos, (1, reps)), jnp.tile(sin, (1, reps)) * jnp.asarray(sign, F32)[None, :]


def _overlap_matrix_t(seq, nch):
    nc = (seq - CMP_LEN) // CMP_STRIDE + 1
    nsb = seq // SLC_LEN
    cst = np.arange(nc) * CMP_STRIDE
    jj = np.arange(nsb)
    ov = ((cst[None, :] < (jj[:, None] + 1) * SLC_LEN) & (cst[None, :] + CMP_LEN > jj[:, None] * SLC_LEN))
    full = np.zeros((nsb, nch), np.float32)
    full[:, :nc] = ov
    return jnp.asarray(full, BF16), nc, nsb


def _cmp_weights(pe, w1, w2, transpose_out):
    pe2 = jnp.concatenate([pe, pe], axis=1)
    w1t = w1.reshape(CMP_LEN, HEAD_DIM, CMP_HIDDEN)
    zeros = jnp.zeros_like(w1t)
    w1p = jnp.concatenate([jnp.concatenate([w1t, zeros], axis=2), jnp.concatenate([zeros, w1t], axis=2)], axis=1)
    w2o = w2.T if transpose_out else jnp.concatenate([w2, w2], axis=1)
    return pe2, w1p.astype(BF16), w2o.astype(BF16)


def kernel(x, c, w_ada, b_ada, norm_g, w_in_even, a_sinks, cmp_pe_k, cmp_w1_k, cmp_w2_k, cmp_pe_v, cmp_w1_v, cmp_w2_v, w_out_even, w_in_odd, w_out_odd, final_g):
    nb, seq, d = x.shape
    depth = w_ada.shape[0]
    assert seq % ROW_TILE == 0 and seq % MOBA_BLOCK == 0 and ATT_TQ % SLC_LEN == 0
    assert (seq // SLC_LEN) % 8 == 0 and seq >= WIN_LEN + ATT_TQ
    cos_t, sin_t = _rope_tables(seq)
    ada = _ada(c, w_ada, b_ada)
    nch = seq // CMP_STRIDE
    ovl_t, nc, nsb = _overlap_matrix_t(seq, nch)
    fg = final_g.reshape(1, d)
    x2 = x.reshape(nb * seq, d)
    r3 = lambda a: a.reshape(nb, seq, a.shape[-1])
    for layer in range(depth):
        li = layer // 2
        final = layer == depth - 1
        ng = norm_g[layer].reshape(1, d)
        if layer % 2 == 0:
            w, wt, groups, groups_t, out_shape, out_specs = _even_plan(w_in_even[li], nb, seq)
            q, kr, kc, vc, z, vt, gates_t = _proj(x2, ada, layer, ng, cos_t, sin_t, w, wt, groups, groups_t,
                                                  out_shape, out_specs, seq)
            q, kr, kc, vc, z = map(r3, (q, kr, kc, vc, z))
            kcmp, vcmp_t = _compress(kc, vc, *_cmp_weights(cmp_pe_k[li], cmp_w1_k[li], cmp_w2_k[li], False),
                                     *_cmp_weights(cmp_pe_v[li], cmp_w1_v[li], cmp_w2_v[li], True))
            oa = _swa(a_sinks[li], q, kr, vt, z, ATT_TQ)
            ob = _nsa(q, kcmp, vcmp_t, kr, vt, gates_t, z, ovl_t, nc, nsb, ATT_TQ)
            o_list = [oa.reshape(nb * seq, -1), ob.reshape(nb * seq, -1)]
            w_out = w_out_even[li].astype(BF16)
        else:
            w, wt, groups, groups_t, out_shape, out_specs = _odd_plan(w_in_odd[li], nb, seq)
            q, k, z, vt, kmean = _proj(x2, ada, layer, ng, cos_t, sin_t, w, wt, groups, groups_t,
                                       out_shape, out_specs, seq, moba_mean=4)
            o = _moba(r3(q), r3(k), vt, kmean.reshape(nb, seq // MOBA_BLOCK, -1), r3(z))
            o_list = [o.reshape(nb * seq, -1)]
            w_out = w_out_odd[li].astype(BF16)
        x2 = _out_proj(x2, ada, layer, o_list, w_out, fg, final, seq)
    return x2.reshape(nb, seq, d)
```

```python
import functools

import numpy as np
import jax
import jax.numpy as jnp
from jax import lax
from jax.experimental import pallas as pl
from jax.experimental.pallas import tpu as pltpu

D_MODEL = 1024
HEAD_DIM = 64
PAIR = 2 * HEAD_DIM
ROPE_THETA = 10000.0
RMS_EPS = 1e-6
A_HEADS = 8
A_WINDOW = 128
B_HEADS = 8
B_KV = 2
B_REP = B_HEADS // B_KV
CMP_LEN = 32
CMP_STRIDE = 16
CMP_HIDDEN = 256
SLC_LEN = 64
SLC_TOPN = 8
WIN_LEN = 512
FORCE_BONUS = 1e4
C_HEADS = 16
MOBA_BLOCK = 256
MOBA_TOPK = 3
Q_SCALE = HEAD_DIM ** -0.5
LOG2E = 1.4426950408889634

NEG = -1e30
SAFE_SUM = 2.0 ** 64
MXU_COLS = 256
ROW_TILE = 512
ATT_TQ = 128
MOBA_PAIRS_PER_STEP = 4
MXU_LAG = 3
SUM_ROWS = 16
GATE_ROWS = 16
VMEM_LIMIT = 48 * 1024 * 1024

BF16 = jnp.bfloat16
F32 = jnp.float32


def _dot_nt(a, b):
    return lax.dot_general(a, b, (((1,), (1,)), ((), ())), preferred_element_type=F32)


def _dot(a, b):
    return jnp.dot(a, b, preferred_element_type=F32)


def _cparams(sem):
    return pltpu.CompilerParams(dimension_semantics=sem, vmem_limit_bytes=VMEM_LIMIT)


def _tile_lanes(a, n):
    return jnp.concatenate([a] * n, axis=1) if n > 1 else a


def _ada_kernel(c_ref, w_ref, b_ref, o_ref):
    c = c_ref[...]
    ca = (c * jax.nn.sigmoid(c)).astype(BF16)
    o_ref[0, 0] = _dot(ca, w_ref[0].astype(BF16)) + b_ref[0, 0]


def _ada(c, w_ada, b_ada):
    depth, d, _ = w_ada.shape
    nb = c.shape[0]
    b4 = b_ada.reshape(depth, 3, 1, d)
    out = pl.pallas_call(
        _ada_kernel,
        grid=(depth, 3),
        in_specs=[
            pl.BlockSpec((nb, d), lambda l, j: (0, 0)),
            pl.BlockSpec((1, d, d), lambda l, j: (l, 0, j)),
            pl.BlockSpec((1, 1, 1, d), lambda l, j: (l, j, 0, 0)),
        ],
        out_specs=pl.BlockSpec((1, 1, nb, d), lambda l, j: (l, j, 0, 0)),
        out_shape=jax.ShapeDtypeStruct((depth, 3, nb, d), F32),
        compiler_params=_cparams(("arbitrary", "arbitrary")),
        name="ada",
    )(c, w_ada, b4)
    return out.reshape(depth, 3, nb, 1, d)


def _rope(a, cos, sin_signed):
    w = a.shape[-1]
    lane = lax.broadcasted_iota(jnp.int32, a.shape, 1)
    first_half = (lane % HEAD_DIM) < (HEAD_DIM // 2)
    partner = jnp.where(first_half, pltpu.roll(a, w - HEAD_DIM // 2, 1), pltpu.roll(a, HEAD_DIM // 2, 1))
    return a * cos + partner * sin_signed


def _proj_kernel(groups, groups_t, moba_mean, x_ref, shift_ref, scale_ref, g_ref, cos_ref, sin_ref, w_ref, wt_ref,
                 *out_refs):
    x = x_ref[...]
    y = x * lax.rsqrt(jnp.mean(x * x, axis=-1, keepdims=True) + RMS_EPS)
    h = (y * g_ref[...]) * (1.0 + scale_ref[0, 0, 0]) + shift_ref[0, 0, 0]
    hb = h.astype(BF16)
    tm = hb.shape[0]
    cos = _tile_lanes(cos_ref[...], MXU_COLS // PAIR)
    sin = _tile_lanes(sin_ref[...], MXU_COLS // PAIR)
    for (w_start, width, kind, out_idx) in groups:
        o_ref = out_refs[out_idx]
        for c0 in range(0, width, MXU_COLS):
            cw = min(MXU_COLS, width - c0)
            acc = _dot(hb, w_ref[:, w_start + c0:w_start + c0 + cw])
            if kind in ("rope", "rope_mean"):
                acc = _rope(acc, cos[:, :cw], sin[:, :cw])
            elif kind == "silu":
                acc = acc * jax.nn.sigmoid(acc)
            o_ref[:, c0:c0 + cw] = acc.astype(o_ref.dtype)
            if kind == "rope_mean":
                km_ref = out_refs[moba_mean]
                for j in range(tm // MOBA_BLOCK):
                    blk = acc[j * MOBA_BLOCK:(j + 1) * MOBA_BLOCK]
                    km_ref[0, j, :, c0:c0 + cw] = jnp.sum(blk, axis=0, keepdims=True) * (1.0 / MOBA_BLOCK)
    for (r_start, n_rows, kind, out_idx, chunk) in groups_t:
        o_ref = out_refs[out_idx]
        for r0 in range(0, n_rows, MXU_COLS):
            rw = min(MXU_COLS, n_rows - r0)
            acc_t = _dot_nt(wt_ref[r_start + r0:r_start + r0 + rw, :], hb)
            if kind == "sigmoid":
                o_ref[0, r0:r0 + rw, :] = jax.nn.sigmoid(acc_t)
            else:
                for jj in range(tm // chunk):
                    o_ref[0, jj, r0:r0 + rw, :] = acc_t[:, jj * chunk:(jj + 1) * chunk].astype(o_ref.dtype)


def _proj(x2, ada, layer, norm_g, cos_t, sin_t, w, wt, groups, groups_t, out_shape, out_specs, seq, moba_mean=None):
    rows, d = x2.shape
    tm = ROW_TILE
    per_b = seq // tm
    return pl.pallas_call(
        functools.partial(_proj_kernel, groups, groups_t, moba_mean),
        grid=(rows // tm,),
        in_specs=[
            pl.BlockSpec((tm, d), lambda i: (i, 0)),
            pl.BlockSpec((1, 1, 1, 1, d), lambda i: (layer, 0, i // per_b, 0, 0)),
            pl.BlockSpec((1, 1, 1, 1, d), lambda i: (layer, 1, i // per_b, 0, 0)),
            pl.BlockSpec((1, d), lambda i: (0, 0)),
            pl.BlockSpec((tm, PAIR), lambda i: (i % per_b, 0)),
            pl.BlockSpec((tm, PAIR), lambda i: (i % per_b, 0)),
            pl.BlockSpec(w.shape, lambda i: (0, 0)),
            pl.BlockSpec(wt.shape, lambda i: (0, 0)),
        ],
        out_specs=out_specs,
        out_shape=out_shape,
        compiler_params=_cparams(("arbitrary",)),
        name="proj",
    )(x2, ada, ada, norm_g, cos_t, sin_t, w, wt)


def _out_kernel(n_o, final, *refs):
    x_ref, gate_ref = refs[0], refs[1]
    o_refs = refs[2:2 + n_o]
    w_ref = refs[2 + n_o]
    fg_ref = refs[3 + n_o]
    out_ref = refs[4 + n_o]
    y = None
    k0 = 0
    for o_ref in o_refs:
        kw = o_ref.shape[-1]
        part = _dot(o_ref[...], w_ref[k0:k0 + kw, :])
        y = part if y is None else y + part
        k0 += kw
    xn = x_ref[...] + gate_ref[0, 0, 0] * y
    if final:
        xn = (xn * lax.rsqrt(jnp.mean(xn * xn, axis=-1, keepdims=True) + RMS_EPS)) * fg_ref[...]
    out_ref[...] = xn


def _out_proj(x2, ada, layer, o_list, w_out, final_g, final, seq):
    rows, d = x2.shape
    tm = ROW_TILE
    per_b = seq // tm
    n_o = len(o_list)
    in_specs = [
        pl.BlockSpec((tm, d), lambda i: (i, 0)),
        pl.BlockSpec((1, 1, 1, 1, d), lambda i: (layer, 2, i // per_b, 0, 0)),
    ]
    in_specs += [pl.BlockSpec((tm, o.shape[-1]), lambda i: (i, 0)) for o in o_list]
    in_specs += [pl.BlockSpec(w_out.shape, lambda i: (0, 0)), pl.BlockSpec((1, d), lambda i: (0, 0))]
    return pl.pallas_call(
        functools.partial(_out_kernel, n_o, final),
        grid=(rows // tm,),
        in_specs=in_specs,
        out_specs=pl.BlockSpec((tm, d), lambda i: (i, 0)),
        out_shape=jax.ShapeDtypeStruct((rows, d), F32),
        compiler_params=_cparams(("arbitrary",)),
        name="out_proj",
    )(x2, ada, *o_list, w_out, final_g)


def _stack_heads(q_pairs):
    lane = lax.broadcasted_iota(jnp.int32, (1, PAIR), 1)
    m_lo = (lane < HEAD_DIM).astype(BF16)
    m_hi = (lane >= HEAD_DIM).astype(BF16)
    blocks = []
    for p in range(q_pairs.shape[-1] // PAIR):
        qp = q_pairs[:, p * PAIR:(p + 1) * PAIR]
        blocks += [qp * m_lo, qp * m_hi]
    return jnp.concatenate(blocks, axis=0)


def _flash_init(d_rows, n_cols):
    return jnp.full((1, n_cols), NEG, F32), jnp.zeros((1, n_cols), F32), jnp.zeros((d_rows, n_cols), F32)


def _flash(carry, s_t, v_t_chunks):
    m, l, acc = carry
    m_new = jnp.maximum(m, jnp.max(s_t, axis=0, keepdims=True))
    alpha = jnp.exp2(m - m_new)
    e = jnp.exp2(s_t - m_new).astype(BF16)
    l = alpha * l + _dot(jnp.ones((SUM_ROWS, e.shape[0]), BF16), e)[0:1]
    acc, k0 = alpha * acc, 0
    for vt in v_t_chunks:
        acc = acc + _dot(vt, e[k0:k0 + vt.shape[1]])
        k0 += vt.shape[1]
    return m_new, l, acc


def _topn_rows(score, valid, n_cand, topn):
    ridx = lax.broadcasted_iota(jnp.int32, score.shape, 0)
    rank = jnp.zeros(score.shape, F32)
    for jp in range(n_cand):
        row = score[jp:jp + 1, :]
        earlier = jnp.where(ridx > jp, 1.0, 0.0)
        rank = rank + jnp.where(row > score, 1.0, jnp.where(row == score, earlier, 0.0))
    return valid & (rank < topn)


def _swa_kernel(tq, sink_ref, q_ref, k_ref, vt_ref, z_ref, o_ref):
    i = pl.program_id(1)
    n_prev = -(-(A_WINDOW - 1) // tq)
    n_ch = n_prev + 1
    span = n_ch * tq
    c0 = jnp.maximum(i - n_prev, 0)
    start = pl.multiple_of(c0 * tq, tq)
    kk = k_ref[0, pl.ds(start, span), :]
    qs = _stack_heads(q_ref[0])
    s_t = _dot_nt(kk, qs)
    qpos = i * tq + lax.broadcasted_iota(jnp.int32, (span, tq), 1)
    kpos = start + lax.broadcasted_iota(jnp.int32, (span, tq), 0)
    diff = qpos - kpos
    bias = jnp.where(diff >= 0, jnp.where(diff < A_WINDOW, 0.0, NEG), NEG)
    s_t = s_t + _tile_lanes(bias, A_HEADS)
    sink = jnp.concatenate([jnp.full((1, tq), sink_ref[hd] * LOG2E, F32) for hd in range(A_HEADS)], axis=1)
    m = jnp.maximum(jnp.max(s_t, axis=0, keepdims=True), sink)
    e = jnp.exp2(s_t - m)
    den = jnp.sum(e, axis=0, keepdims=True) + jnp.exp2(sink - m)
    eb = e.astype(BF16)
    acc = None
    for cc in range(n_ch):
        part = _dot(vt_ref[0, c0 + cc], eb[cc * tq:(cc + 1) * tq])
        acc = part if acc is None else acc + part
    o_t = acc * (1.0 / den)
    heads = jnp.concatenate([o_t[:, hd * tq:(hd + 1) * tq] for hd in range(A_HEADS)], axis=0)
    o_ref[0] = (heads.T * z_ref[0]).astype(o_ref.dtype)


def _swa(sinks, q, kr, vt, z, tq):
    nb, seq, _ = q.shape
    aw = A_HEADS * HEAD_DIM
    return pl.pallas_call(
        functools.partial(_swa_kernel, tq),
        grid=(nb, seq // tq),
        in_specs=[
            pl.BlockSpec(memory_space=pltpu.SMEM),
            pl.BlockSpec((1, tq, aw), lambda b, i: (b, i, 0)),
            pl.BlockSpec((1, seq, PAIR), lambda b, i: (b, 0, 0)),
            pl.BlockSpec((1, seq // tq, HEAD_DIM, tq), lambda b, i: (b, 0, 0, 0)),
            pl.BlockSpec((1, tq, aw), lambda b, i: (b, i, 0)),
        ],
        out_specs=pl.BlockSpec((1, tq, aw), lambda b, i: (b, i, 0)),
        out_shape=jax.ShapeDtypeStruct((nb, seq, aw), BF16),
        compiler_params=_cparams(("arbitrary", "arbitrary")),
        name="swa",
    )(sinks, q, kr, vt, z)


def _cmp_hidden(x_ref, pe_ref, w1_ref, nch):
    top, bot = None, None
    for l in range(CMP_STRIDE):
        rows = x_ref[0, pl.ds(l, nch, stride=CMP_STRIDE), :]
        t = _dot((rows + pe_ref[l:l + 1, :]).astype(BF16), w1_ref[l])
        b = _dot((rows + pe_ref[CMP_STRIDE + l:CMP_STRIDE + l + 1, :]).astype(BF16), w1_ref[CMP_STRIDE + l])
        top = t if top is None else top + t
        bot = b if bot is None else bot + b
    return jax.nn.gelu(top + pltpu.roll(bot, nch - 1, 0))


def _compress_kernel(kc_ref, vc_ref, pek_ref, w1k_ref, w2k_ref, pev_ref, w1v_ref, w2v_ref, ko_ref, vo_ref):
    nch = ko_ref.shape[2]
    hk = _cmp_hidden(kc_ref, pek_ref, w1k_ref, nch).astype(BF16)
    hv = _cmp_hidden(vc_ref, pev_ref, w1v_ref, nch).astype(BF16)
    for g in range(B_KV):
        gs = slice(g * CMP_HIDDEN, (g + 1) * CMP_HIDDEN)
        ko_ref[0, g] = _dot(hk[:, gs], w2k_ref[...]).astype(ko_ref.dtype)
        vo_ref[0, g] = _dot_nt(w2v_ref[...], hv[:, gs]).astype(vo_ref.dtype)


def _compress(kc, vc, pek, w1k, w2k, pev, w1v, w2v):
    nb, seq, _ = kc.shape
    nch = seq // CMP_STRIDE
    full = lambda a: pl.BlockSpec(a.shape, lambda b: (0,) * a.ndim)
    blk = pl.BlockSpec((1, seq, PAIR), lambda b: (b, 0, 0))
    return pl.pallas_call(
        _compress_kernel,
        grid=(nb,),
        in_specs=[blk, blk, full(pek), full(w1k), full(w2k), full(pev), full(w1v), full(w2v)],
        out_specs=[pl.BlockSpec((1, B_KV, nch, PAIR), lambda b: (b, 0, 0, 0)),
                   pl.BlockSpec((1, B_KV, HEAD_DIM, nch), lambda b: (b, 0, 0, 0))],
        out_shape=[jax.ShapeDtypeStruct((nb, B_KV, nch, PAIR), BF16),
                   jax.ShapeDtypeStruct((nb, B_KV, HEAD_DIM, nch), BF16)],
        compiler_params=_cparams(("arbitrary",)),
        name="compress",
    )(kc, vc, pek, w1k, w2k, pev, w1v, w2v)


def _nsa_kernel(tq, nc, nsb, q_ref, kc_ref, vct_ref, ks0_ref, ks1_ref, vs0_ref, vs1_ref, kw0_ref, kw1_ref,
                vw0_ref, vw1_ref, gt_ref, z_ref, ovl_ref, o_ref, bias_ref):
    i = pl.program_id(1)
    groups = range(B_KV)
    ks_refs, vs_refs = (ks0_ref, ks1_ref), (vs0_ref, vs1_ref)
    kw_refs, vw_refs = (kw0_ref, kw1_ref), (vw0_ref, vw1_ref)
    gw = B_REP * HEAD_DIM
    nr = B_REP * tq
    qs = [_stack_heads(q_ref[0, :, g * gw:(g + 1) * gw]) for g in groups]
    t_row = i * tq + lax.broadcasted_iota(jnp.int32, (1, tq), 1)

    nch = kc_ref.shape[2]
    cidx = lax.broadcasted_iota(jnp.int32, (nch, tq), 0)
    ok_c = (cidx < nc) & (t_row >= cidx * CMP_STRIDE + (CMP_LEN - 1))
    bias_c = _tile_lanes(jnp.where(ok_c, 0.0, NEG), B_REP)
    jr = lax.broadcasted_iota(jnp.int32, (nsb, tq), 0)
    tb = t_row // SLC_LEN
    valid = jr <= tb
    forced = (jr == 0) | (jr == tb) | (jr == tb - 1)
    n_prev = -(-(WIN_LEN - 1) // tq)
    blk_per_chunk = tq // SLC_LEN
    krow = lax.broadcasted_iota(jnp.int32, (tq, tq), 0)
    qcol = lax.broadcasted_iota(jnp.int32, (tq, tq), 1)
    causal = jnp.where(krow <= qcol, 0.0, NEG)
    edge = _tile_lanes(jnp.where(n_prev * tq + qcol - krow < WIN_LEN, 0.0, NEG), B_REP)
    causal_r = _tile_lanes(causal, B_REP)
    ones = jnp.ones((SUM_ROWS, tq), BF16)

    def win_chunk(j):
        return jnp.maximum(i - j, 0)

    def win_scores(g, j):
        off = pl.multiple_of(win_chunk(j) * tq, tq)
        s_t = _dot_nt(kw_refs[g][0, pl.ds(off, tq), :], qs[g])
        return s_t + causal_r if j == 0 else (s_t + edge if j == n_prev else s_t)

    def slc_bias(g, c, diagonal):
        rows = [jnp.broadcast_to(bias_ref[g, pl.ds(c * blk_per_chunk + bb, 1), :], (SLC_LEN, tq))
                for bb in range(blk_per_chunk)]
        b = jnp.concatenate(rows, axis=0)
        return _tile_lanes(b + causal if diagonal else b, B_REP)

    def slc_qk(g, c):
        off = pl.multiple_of(c * tq, tq)
        return _dot_nt(ks_refs[g][0, pl.ds(off, tq), :], qs[g])

    def sums_and_pv(v_t, e):
        res = _dot(jnp.concatenate([v_t, ones], axis=0), e)
        return res[HEAD_DIM:HEAD_DIM + 1], res[:HEAD_DIM]

    def masked_ref(m0, present):
        return m0 + jnp.where(present, 0.0, -NEG)

    s_cmp = [_dot_nt(kc_ref[0, g], qs[g]) + bias_c for g in groups]
    s_win0 = [win_scores(g, 0) for g in groups]
    s_diag = [slc_qk(g, i) for g in groups]

    o_cmp = []
    for g in groups:
        s = s_cmp[g]
        m = jnp.max(s, axis=0, keepdims=True)
        m = jnp.where(m > 0.5 * NEG, m, 0.0)
        e = jnp.exp2(s - m)
        den = jnp.sum(e, axis=0, keepdims=True)
        p_t = (e * (1.0 / jnp.where(den > 0, den, 1.0))).astype(BF16)
        o_cmp.append(_dot(vct_ref[0, g], p_t))
        imp = None
        for r in range(B_REP):
            part = _dot(ovl_ref[...], p_t[:, r * tq:(r + 1) * tq])
            imp = part if imp is None else imp + part
        score = jnp.where(valid, jnp.where(forced, FORCE_BONUS, imp), -jnp.inf)
        bias_ref[g] = jnp.where(_topn_rows(score, valid, nsb, SLC_TOPN), 0.0, NEG)

    def write_out(slc, win):
        heads = []
        for g in groups:
            o_slc = slc[g][1] * (1.0 / slc[g][0])
            o_win = win[g][1] * (1.0 / win[g][0])
            gt = gt_ref[0, g * GATE_ROWS:(g + 1) * GATE_ROWS]
            for r in range(B_REP):
                cs = slice(r * tq, (r + 1) * tq)
                heads.append(gt[r:r + 1] * o_cmp[g][:, cs] + gt[B_REP + r:B_REP + r + 1] * o_slc[:, cs]
                             + gt[2 * B_REP + r:2 * B_REP + r + 1] * o_win[:, cs])
        o_t = jnp.concatenate(heads, axis=0)
        o_ref[0] = (o_t.T * z_ref[0]).astype(o_ref.dtype)

    win_m0 = [jnp.max(s_win0[g], axis=0, keepdims=True) for g in groups]
    win = [list(sums_and_pv(vw_refs[g][0, i], jnp.exp2(s_win0[g] - win_m0[g]).astype(BF16))) for g in groups]
    pending = []

    def finish_win(g, j, arg):
        l_c, pv = sums_and_pv(vw_refs[g][0, win_chunk(j)], jnp.exp2(arg).astype(BF16))
        win[g][0], win[g][1] = win[g][0] + l_c, win[g][1] + pv

    for j in range(1, n_prev + 1):
        for g in groups:
            pending.append((g, j, win_scores(g, j) - masked_ref(win_m0[g], i - j >= 0)))
            if len(pending) > MXU_LAG:
                finish_win(*pending.pop(0))
    while pending:
        finish_win(*pending.pop(0))

    s_diag = [s_diag[g] + slc_bias(g, i, True) for g in groups]
    slc_m0 = [jnp.max(s_diag[g], axis=0, keepdims=True) for g in groups]
    slc0 = tuple(sums_and_pv(vs_refs[g][0, i], jnp.exp2(s_diag[g] - slc_m0[g]).astype(BF16)) for g in groups)

    def fast_pair(u, sums):
        sums = [list(x) for x in sums]
        waiting = []

        def finish(g, c, arg):
            l_c, pv = sums_and_pv(vs_refs[g][0, c], jnp.exp2(arg).astype(BF16))
            sums[g][0], sums[g][1] = sums[g][0] + l_c, sums[g][1] + pv

        for c in (2 * u, 2 * u + 1):
            for g in groups:
                waiting.append((g, c, slc_qk(g, c) + slc_bias(g, c, False) - masked_ref(slc_m0[g], c < i)))
                if len(waiting) > MXU_LAG:
                    finish(*waiting.pop(0))
        while waiting:
            finish(*waiting.pop(0))
        return tuple(tuple(x) for x in sums)

    slc = lax.fori_loop(0, (i + 1) // 2, fast_pair, slc0)
    write_out(slc, win)
    worst = jnp.maximum(jnp.maximum(slc[0][0], slc[1][0]), jnp.maximum(win[0][0], win[1][0]))
    safe = jnp.max(worst) < SAFE_SUM

    @pl.when(jnp.logical_not(safe))
    def _():
        span = (n_prev + 1) * tq
        c0 = jnp.maximum(i - n_prev, 0)
        start = pl.multiple_of(c0 * tq, tq)
        diff = t_row - (start + lax.broadcasted_iota(jnp.int32, (span, tq), 0))
        bias_w = _tile_lanes(jnp.where(diff >= 0, jnp.where(diff < WIN_LEN, 0.0, NEG), NEG), B_REP)
        win_x = []
        for g in groups:
            sw = _dot_nt(kw_refs[g][0, pl.ds(start, span), :], qs[g]) + bias_w
            _, l_w, acc_w = _flash(_flash_init(HEAD_DIM, nr), sw,
                                   [vw_refs[g][0, c0 + cc] for cc in range(n_prev + 1)])
            win_x.append((l_w, acc_w))

        def exact_pair(u, carries):
            out = []
            for g in groups:
                carry = carries[g]
                for c in (2 * u, 2 * u + 1):
                    s_t = slc_qk(g, c) + slc_bias(g, c, False) + jnp.where(c < i, 0.0, NEG)
                    carry = _flash(carry, s_t, [vs_refs[g][0, c]])
                out.append(carry)
            return tuple(out)

        first = tuple(_flash(_flash_init(HEAD_DIM, nr), s_diag[g], [vs_refs[g][0, i]]) for g in groups)
        exact = lax.fori_loop(0, (i + 1) // 2, exact_pair, first)
        write_out([(l, acc) for (_, l, acc) in exact], win_x)


def _nsa(q, kcmp, vcmp_t, kr, vt, gates_t, z, ovl_t, nc, nsb, tq):
    nb, seq, _ = q.shape
    bw = B_HEADS * HEAD_DIM
    nch = kcmp.shape[2]
    kblk = lambda col: pl.BlockSpec((1, seq, PAIR), lambda b, i: (b, 0, col))
    vblk = lambda row: pl.BlockSpec((1, seq // tq, HEAD_DIM, tq), lambda b, i: (b, 0, row, 0))
    return pl.pallas_call(
        functools.partial(_nsa_kernel, tq, nc, nsb),
        grid=(nb, seq // tq),
        in_specs=[
            pl.BlockSpec((1, tq, bw), lambda b, i: (b, i, 1)),
            pl.BlockSpec((1, B_KV, nch, PAIR), lambda b, i: (b, 0, 0, 0)),
            pl.BlockSpec((1, B_KV, HEAD_DIM, nch), lambda b, i: (b, 0, 0, 0)),
            kblk(1), kblk(2), vblk(1), vblk(2), kblk(3), kblk(4), vblk(3), vblk(4),
            pl.BlockSpec((1, B_KV * GATE_ROWS, tq), lambda b, i: (b, 0, i)),
            pl.BlockSpec((1, tq, bw), lambda b, i: (b, i, 1)),
            pl.BlockSpec(ovl_t.shape, lambda b, i: (0, 0)),
        ],
        out_specs=pl.BlockSpec((1, tq, bw), lambda b, i: (b, i, 0)),
        out_shape=jax.ShapeDtypeStruct((nb, seq, bw), BF16),
        scratch_shapes=[pltpu.VMEM((B_KV, nsb, tq), F32)],
        compiler_params=_cparams(("arbitrary", "arbitrary")),
        name="nsa",
    )(q, kcmp, vcmp_t, kr, kr, vt, vt, kr, kr, vt, vt, gates_t, z, ovl_t)


def _moba_kernel(nblk, npair, q_ref, k_ref, vt_ref, km_ref, z_ref, o_ref, bias_ref):
    i = pl.program_id(2)
    tq = MOBA_BLOCK
    nr = 2 * tq
    ncand = bias_ref.shape[1]
    pairs = range(npair)
    lanes = [slice(p * PAIR, (p + 1) * PAIR) for p in pairs]
    qs = [_stack_heads(q_ref[0, :, lanes[p]]) for p in pairs]

    gs = []
    for p in pairs:
        km = km_ref[0, :, lanes[p]]
        km = jnp.concatenate([km, jnp.zeros((ncand - nblk, PAIR), F32)], axis=0).astype(BF16)
        gs.append(_dot_nt(km, qs[p]))

    krow = lax.broadcasted_iota(jnp.int32, (tq, tq), 0)
    qcol = lax.broadcasted_iota(jnp.int32, (tq, tq), 1)
    causal = _tile_lanes(jnp.where(krow <= qcol, 0.0, NEG), 2)
    ones = jnp.ones((SUM_ROWS, tq), BF16)

    def scores(p, blk, bias):
        off = pl.multiple_of(blk * tq, tq)
        return _dot_nt(k_ref[0, pl.ds(off, tq), lanes[p]], qs[p]) + bias

    def past_scores(p, blk):
        return scores(p, blk, bias_ref[p, pl.ds(blk, 1), :])

    def weights_times_v(e, p, blk):
        res = []
        for hh in range(2):
            v_ext = jnp.concatenate([vt_ref[0, blk, pl.ds(p * PAIR + hh * HEAD_DIM, HEAD_DIM), :], ones], axis=0)
            res.append(_dot(v_ext, e[:, hh * tq:(hh + 1) * tq]))
        res = jnp.concatenate(res, axis=1)
        return res[HEAD_DIM:HEAD_DIM + 1], res[:HEAD_DIM]

    def flash(carry, s_t, p, blk):
        m, l, acc = carry
        m_new = jnp.maximum(m, jnp.max(s_t, axis=0, keepdims=True))
        alpha = jnp.exp2(m - m_new)
        l_c, pv = weights_times_v(jnp.exp2(s_t - m_new).astype(BF16), p, blk)
        return m_new, alpha * l + l_c, alpha * acc + pv

    def write_out(results):
        outs = []
        for p in pairs:
            l_f, acc_f = results[p]
            o_t = acc_f * (1.0 / l_f)
            outs += [o_t[:, :tq], o_t[:, tq:]]
        o_ref[0] = (jnp.concatenate(outs, axis=0).T * z_ref[0]).astype(o_ref.dtype)

    s_own = [scores(p, i, causal) for p in pairs]
    jr = lax.broadcasted_iota(jnp.int32, (ncand, nr), 0)
    past = jr < i
    for p in pairs:
        sel = _topn_rows(jnp.where(past, gs[p], -jnp.inf), past, nblk, MOBA_TOPK)
        bias_ref[p] = jnp.where(sel, 0.0, NEG)
    own = []
    for p in pairs:
        m0 = jnp.max(s_own[p], axis=0, keepdims=True)
        l0, acc0 = weights_times_v(jnp.exp2(s_own[p] - m0).astype(BF16), p, i)
        own.append((m0, l0, acc0))

    def fast_pair(u, sums):
        sums = [list(x) for x in sums]
        pending = []

        def finish(p, blk, s_t):
            l_c, pv = weights_times_v(jnp.exp2(s_t - own[p][0]).astype(BF16), p, blk)
            sums[p][0] = sums[p][0] + l_c
            sums[p][1] = sums[p][1] + pv

        for blk in (2 * u, 2 * u + 1):
            for p in pairs:
                pending.append((p, blk, past_scores(p, blk)))
                if len(pending) > MXU_LAG:
                    finish(*pending.pop(0))
        while pending:
            finish(*pending.pop(0))
        return tuple(tuple(x) for x in sums)

    fast = lax.fori_loop(0, (i + 1) // 2, fast_pair, tuple((l0, acc0) for (_, l0, acc0) in own))
    write_out(fast)
    worst = fast[0][0]
    for p in pairs[1:]:
        worst = jnp.maximum(worst, fast[p][0])
    safe = jnp.max(worst) < SAFE_SUM

    @pl.when(jnp.logical_not(safe))
    def _():
        def exact_pair(u, carries):
            s_a = [past_scores(p, 2 * u) for p in pairs]
            s_b = [past_scores(p, 2 * u + 1) for p in pairs]
            return tuple(flash(flash(carries[p], s_a[p], p, 2 * u), s_b[p], p, 2 * u + 1) for p in pairs)

        exact = lax.fori_loop(0, (i + 1) // 2, exact_pair, tuple(own))
        write_out([(l, acc) for (_, l, acc) in exact])


def _moba(q, k, vt, kmean, z):
    nb, seq, width = q.shape
    tq = MOBA_BLOCK
    nblk = seq // MOBA_BLOCK
    ncand = -(-nblk // 16) * 16
    npair = MOBA_PAIRS_PER_STEP
    gw = npair * PAIR
    qblk = pl.BlockSpec((1, tq, gw), lambda b, p, i: (b, i, p))
    return pl.pallas_call(
        functools.partial(_moba_kernel, nblk, npair),
        grid=(nb, width // gw, seq // tq),
        in_specs=[
            qblk,
            pl.BlockSpec((1, seq, gw), lambda b, p, i: (b, 0, p)),
            pl.BlockSpec((1, nblk, gw, tq), lambda b, p, i: (b, 0, p, 0)),
            pl.BlockSpec((1, nblk, gw), lambda b, p, i: (b, 0, p)),
            qblk,
        ],
        out_specs=qblk,
        out_shape=jax.ShapeDtypeStruct((nb, seq, width), BF16),
        scratch_shapes=[pltpu.VMEM((npair, ncand, 2 * tq), F32)],
        compiler_params=_cparams(("arbitrary", "arbitrary", "arbitrary")),
        name="moba",
    )(q, k, vt, kmean, z)


def _dup_heads(w):
    d, wd = w.shape
    w3 = w.reshape(d, wd // HEAD_DIM, 1, HEAD_DIM)
    return jnp.broadcast_to(w3, (d, wd // HEAD_DIM, 2, HEAD_DIM)).reshape(d, 2 * wd)


def _even_plan(w_in, nb, seq):
    aq, akv = A_HEADS * HEAD_DIM, HEAD_DIM
    bq, bkv, bg = B_HEADS * HEAD_DIM, B_KV * HEAD_DIM, 3 * B_HEADS
    sizes = (aq, akv, akv, aq, bq, bkv, bkv, bkv, bkv, bkv, bkv, bg, bq)
    offs = np.concatenate([[0], np.cumsum(sizes)])
    qa, ka, va, za, qb, kc, vc, ks, vs, kw, vw, gb, zb = [w_in[:, offs[n]:offs[n + 1]] for n in range(len(sizes))]
    qw = aq + bq
    krw = PAIR * (1 + 2 * B_KV)
    w = jnp.concatenate([qa * (Q_SCALE * LOG2E), qb * (Q_SCALE * LOG2E), _dup_heads(ka), _dup_heads(ks), _dup_heads(kw), kc, vc, za, zb],
                        axis=1).astype(BF16)
    gb4 = gb.reshape(-1, B_KV, B_REP, 3).transpose(0, 1, 3, 2).reshape(-1, B_KV, 3 * B_REP)
    gb4 = jnp.pad(gb4, ((0, 0), (0, 0), (0, GATE_ROWS - 3 * B_REP))).reshape(-1, B_KV * GATE_ROWS)
    vrows = akv + 2 * bkv
    wt = jnp.concatenate([va, vs, vw, gb4], axis=1).T.astype(BF16)
    groups, col = [], 0
    for width, kind in ((qw, "rope"), (krw, "rope"), (bkv, "rope"), (bkv, "plain"), (qw, "silu")):
        groups.append((col, width, kind, len(groups)))
        col += width
    groups_t = ((0, vrows, "plain", 5, ATT_TQ), (vrows, B_KV * GATE_ROWS, "sigmoid", 6, None))
    rows, tm = nb * seq, ROW_TILE
    per_b = seq // tm
    widths = [(qw, BF16), (krw, BF16), (bkv, F32), (bkv, F32), (qw, F32)]
    out_shape = [jax.ShapeDtypeStruct((rows, wd), dt) for wd, dt in widths]
    out_specs = [pl.BlockSpec((tm, wd), lambda i: (i, 0)) for wd, _ in widths]
    out_shape += [jax.ShapeDtypeStruct((nb, seq // ATT_TQ, vrows, ATT_TQ), BF16),
                  jax.ShapeDtypeStruct((nb, B_KV * GATE_ROWS, seq), F32)]
    out_specs += [pl.BlockSpec((1, tm // ATT_TQ, vrows, ATT_TQ), lambda i: (i // per_b, i % per_b, 0, 0)),
                  pl.BlockSpec((1, B_KV * GATE_ROWS, tm), lambda i: (i // per_b, 0, i % per_b))]
    return w, wt, tuple(groups), groups_t, out_shape, out_specs


def _odd_plan(w_in, nb, seq):
    cw = C_HEADS * HEAD_DIM
    q, k, v, z = [w_in[:, n * cw:(n + 1) * cw] for n in range(4)]
    w = jnp.concatenate([q * (Q_SCALE * LOG2E), k, z], axis=1).astype(BF16)
    wt = v.T.astype(BF16)
    groups = ((0, cw, "rope", 0), (cw, cw, "rope_mean", 1), (2 * cw, cw, "silu", 2))
    groups_t = ((0, cw, "plain", 3, MOBA_BLOCK),)
    rows, tm = nb * seq, ROW_TILE
    per_b = seq // tm
    widths = [(cw, BF16), (cw, BF16), (cw, F32)]
    out_shape = [jax.ShapeDtypeStruct((rows, wd), dt) for wd, dt in widths]
    out_specs = [pl.BlockSpec((tm, wd), lambda i: (i, 0)) for wd, _ in widths]
    out_shape += [jax.ShapeDtypeStruct((nb, seq // MOBA_BLOCK, cw, MOBA_BLOCK), BF16),
                  jax.ShapeDtypeStruct((nb, seq // MOBA_BLOCK, 1, cw), F32)]
    out_specs += [pl.BlockSpec((1, tm // MOBA_BLOCK, cw, MOBA_BLOCK), lambda i: (i // per_b, i % per_b, 0, 0)),
                  pl.BlockSpec((1, tm // MOBA_BLOCK, 1, cw), lambda i: (i // per_b, i % per_b, 0, 0))]
    return w, wt, groups, groups_t, out_shape, out_specs


def _rope_tables(seq):
    inv = ROPE_THETA ** (-jnp.arange(0, HEAD_DIM, 2, dtype=F32) / HEAD_DIM)
    ang = jnp.arange(seq, dtype=F32)[:, None] * inv[None, :]
    cos, sin = jnp.cos(ang), jnp.sin(ang)
    reps = PAIR // (HEAD_DIM // 2)
    sign = np.tile(np.concatenate([-np.ones(HEAD_DIM // 2), np.ones(HEAD_DIM // 2)]), PAIR // HEAD_DIM)
    return jnp.tile(cos, (1, reps)), jnp.tile(sin, (1, reps)) * jnp.asarray(sign, F32)[None, :]


def _overlap_matrix_t(seq, nch):
    nc = (seq - CMP_LEN) // CMP_STRIDE + 1
    nsb = seq // SLC_LEN
    cst = np.arange(nc) * CMP_STRIDE
    jj = np.arange(nsb)
    ov = ((cst[None, :] < (jj[:, None] + 1) * SLC_LEN) & (cst[None, :] + CMP_LEN > jj[:, None] * SLC_LEN))
    full = np.zeros((nsb, nch), np.float32)
    full[:, :nc] = ov
    return jnp.asarray(full, BF16), nc, nsb


def _cmp_weights(pe, w1, w2, transpose_out):
    pe2 = jnp.concatenate([pe, pe], axis=1)
    w1t = w1.reshape(CMP_LEN, HEAD_DIM, CMP_HIDDEN)
    zeros = jnp.zeros_like(w1t)
    w1p = jnp.concatenate([jnp.concatenate([w1t, zeros], axis=2), jnp.concatenate([zeros, w1t], axis=2)], axis=1)
    w2o = w2.T if transpose_out else jnp.concatenate([w2, w2], axis=1)
    return pe2, w1p.astype(BF16), w2o.astype(BF16)


def kernel(x, c, w_ada, b_ada, norm_g, w_in_even, a_sinks, cmp_pe_k, cmp_w1_k, cmp_w2_k, cmp_pe_v, cmp_w1_v, cmp_w2_v, w_out_even, w_in_odd, w_out_odd, final_g):
    nb, seq, d = x.shape
    depth = w_ada.shape[0]
    assert seq % ROW_TILE == 0 and seq % MOBA_BLOCK == 0 and ATT_TQ % SLC_LEN == 0
    assert (seq // SLC_LEN) % 8 == 0 and seq >= WIN_LEN + ATT_TQ
    cos_t, sin_t = _rope_tables(seq)
    ada = _ada(c, w_ada, b_ada)
    nch = seq // CMP_STRIDE
    ovl_t, nc, nsb = _overlap_matrix_t(seq, nch)
    fg = final_g.reshape(1, d)
    x2 = x.reshape(nb * seq, d)
    r3 = lambda a: a.reshape(nb, seq, a.shape[-1])
    for layer in range(depth):
        li = layer // 2
        final = layer == depth - 1
        ng = norm_g[layer].reshape(1, d)
        if layer % 2 == 0:
            w, wt, groups, groups_t, out_shape, out_specs = _even_plan(w_in_even[li], nb, seq)
            q, kr, kc, vc, z, vt, gates_t = _proj(x2, ada, layer, ng, cos_t, sin_t, w, wt, groups, groups_t,
                                                  out_shape, out_specs, seq)
            q, kr, kc, vc, z = map(r3, (q, kr, kc, vc, z))
            kcmp, vcmp_t = _compress(kc, vc, *_cmp_weights(cmp_pe_k[li], cmp_w1_k[li], cmp_w2_k[li], False),
                                     *_cmp_weights(cmp_pe_v[li], cmp_w1_v[li], cmp_w2_v[li], True))
            oa = _swa(a_sinks[li], q, kr, vt, z, ATT_TQ)
            ob = _nsa(q, kcmp, vcmp_t, kr, vt, gates_t, z, ovl_t, nc, nsb, ATT_TQ)
            o_list = [oa.reshape(nb * seq, -1), ob.reshape(nb * seq, -1)]
            w_out = w_out_even[li].astype(BF16)
        else:
            w, wt, groups, groups_t, out_shape, out_specs = _odd_plan(w_in_odd[li], nb, seq)
            q, k, z, vt, kmean = _proj(x2, ada, layer, ng, cos_t, sin_t, w, wt, groups, groups_t,
                                       out_shape, out_specs, seq, moba_mean=4)
            o = _moba(r3(q), r3(k), vt, kmean.reshape(nb, seq // MOBA_BLOCK, -1), r3(z))
            o_list = [o.reshape(nb * seq, -1)]
            w_out = w_out_odd[li].astype(BF16)
        x2 = _out_proj(x2, ada, layer, o_list, w_out, fg, final, seq)
    return x2.reshape(nb, seq, d)
```

```python
import functools

import numpy as np
import jax
import jax.numpy as jnp
from jax import lax
from jax.experimental import pallas as pl
from jax.experimental.pallas import tpu as pltpu

D_MODEL = 1024
HEAD_DIM = 64
PAIR = 2 * HEAD_DIM
ROPE_THETA = 10000.0
RMS_EPS = 1e-6
A_HEADS = 8
A_WINDOW = 128
B_HEADS = 8
B_KV = 2
B_REP = B_HEADS // B_KV
CMP_LEN = 32
CMP_STRIDE = 16
CMP_HIDDEN = 256
SLC_LEN = 64
SLC_TOPN = 8
WIN_LEN = 512
FORCE_BONUS = 1e4
C_HEADS = 16
MOBA_BLOCK = 256
MOBA_TOPK = 3
Q_SCALE = HEAD_DIM ** -0.5
LOG2E = 1.4426950408889634

NEG = -1e30
SAFE_SUM = 2.0 ** 64
MXU_COLS = 256
ROW_TILE = 512
SWA_TQ = 128
NSA_TQ = 256
SWA_BLOCKS_PER_STEP = 4
MOBA_PAIRS_PER_STEP = 8
MXU_LAG = 3
SUM_ROWS = 16
GATE_ROWS = 16
VMEM_LIMIT = 48 * 1024 * 1024

BF16 = jnp.bfloat16
F32 = jnp.float32


def _dot_nt(a, b):
    return lax.dot_general(a, b, (((1,), (1,)), ((), ())), preferred_element_type=F32)


def _dot(a, b):
    return jnp.dot(a, b, preferred_element_type=F32)


def _cparams(sem):
    return pltpu.CompilerParams(dimension_semantics=sem, vmem_limit_bytes=VMEM_LIMIT)


def _tile_lanes(a, n):
    return jnp.concatenate([a] * n, axis=1) if n > 1 else a


def _ada_kernel(c_ref, w_ref, b_ref, o_ref):
    c = c_ref[...]
    ca = (c * jax.nn.sigmoid(c)).astype(BF16)
    o_ref[0, 0] = _dot(ca, w_ref[0].astype(BF16)) + b_ref[0, 0]


def _ada(c, w_ada, b_ada):
    depth, d, _ = w_ada.shape
    nb = c.shape[0]
    b4 = b_ada.reshape(depth, 3, 1, d)
    out = pl.pallas_call(
        _ada_kernel,
        grid=(depth, 3),
        in_specs=[
            pl.BlockSpec((nb, d), lambda l, j: (0, 0)),
            pl.BlockSpec((1, d, d), lambda l, j: (l, 0, j)),
            pl.BlockSpec((1, 1, 1, d), lambda l, j: (l, j, 0, 0)),
        ],
        out_specs=pl.BlockSpec((1, 1, nb, d), lambda l, j: (l, j, 0, 0)),
        out_shape=jax.ShapeDtypeStruct((depth, 3, nb, d), F32),
        compiler_params=_cparams(("arbitrary", "arbitrary")),
        name="ada",
    )(c, w_ada, b4)
    return out.reshape(depth, 3, nb, 1, d)


def _rope(a, cos, sin_signed):
    w = a.shape[-1]
    lane = lax.broadcasted_iota(jnp.int32, a.shape, 1)
    first_half = (lane % HEAD_DIM) < (HEAD_DIM // 2)
    partner = jnp.where(first_half, pltpu.roll(a, w - HEAD_DIM // 2, 1), pltpu.roll(a, HEAD_DIM // 2, 1))
    return a * cos + partner * sin_signed


def _proj_kernel(groups, groups_t, moba_mean, x_ref, shift_ref, scale_ref, g_ref, cos_ref, sin_ref, w_ref, wt_ref,
                 *out_refs):
    x = x_ref[...]
    y = x * lax.rsqrt(jnp.mean(x * x, axis=-1, keepdims=True) + RMS_EPS)
    h = (y * g_ref[...]) * (1.0 + scale_ref[0, 0, 0]) + shift_ref[0, 0, 0]
    hb = h.astype(BF16)
    tm = hb.shape[0]
    cos = _tile_lanes(cos_ref[...], MXU_COLS // PAIR)
    sin = _tile_lanes(sin_ref[...], MXU_COLS // PAIR)
    for (w_start, width, kind, out_idx) in groups:
        o_ref = out_refs[out_idx]
        for c0 in range(0, width, MXU_COLS):
            cw = min(MXU_COLS, width - c0)
            acc = _dot(hb, w_ref[:, w_start + c0:w_start + c0 + cw])
            if kind in ("rope", "rope_mean"):
                acc = _rope(acc, cos[:, :cw], sin[:, :cw])
            elif kind == "silu":
                acc = acc * jax.nn.sigmoid(acc)
            o_ref[:, c0:c0 + cw] = acc.astype(o_ref.dtype)
            if kind == "rope_mean":
                km_ref = out_refs[moba_mean]
                for j in range(tm // MOBA_BLOCK):
                    blk = acc[j * MOBA_BLOCK:(j + 1) * MOBA_BLOCK]
                    km_ref[0, j, :, c0:c0 + cw] = jnp.sum(blk, axis=0, keepdims=True) * (1.0 / MOBA_BLOCK)
    for (r_start, n_rows, kind, out_idx, chunk) in groups_t:
        o_ref = out_refs[out_idx]
        for r0 in range(0, n_rows, MXU_COLS):
            rw = min(MXU_COLS, n_rows - r0)
            acc_t = _dot_nt(wt_ref[r_start + r0:r_start + r0 + rw, :], hb)
            if kind == "sigmoid":
                o_ref[0, r0:r0 + rw, :] = jax.nn.sigmoid(acc_t)
            else:
                for jj in range(tm // chunk):
                    o_ref[0, jj, r0:r0 + rw, :] = acc_t[:, jj * chunk:(jj + 1) * chunk].astype(o_ref.dtype)


def _proj(x2, ada, layer, norm_g, cos_t, sin_t, w, wt, groups, groups_t, out_shape, out_specs, seq, moba_mean=None):
    rows, d = x2.shape
    tm = ROW_TILE
    per_b = seq // tm
    return pl.pallas_call(
        functools.partial(_proj_kernel, groups, groups_t, moba_mean),
        grid=(rows // tm,),
        in_specs=[
            pl.BlockSpec((tm, d), lambda i: (i, 0)),
            pl.BlockSpec((1, 1, 1, 1, d), lambda i: (layer, 0, i // per_b, 0, 0)),
            pl.BlockSpec((1, 1, 1, 1, d), lambda i: (layer, 1, i // per_b, 0, 0)),
            pl.BlockSpec((1, d), lambda i: (0, 0)),
            pl.BlockSpec((tm, PAIR), lambda i: (i % per_b, 0)),
            pl.BlockSpec((tm, PAIR), lambda i: (i % per_b, 0)),
            pl.BlockSpec(w.shape, lambda i: (0, 0)),
            pl.BlockSpec(wt.shape, lambda i: (0, 0)),
        ],
        out_specs=out_specs,
        out_shape=out_shape,
        compiler_params=_cparams(("arbitrary",)),
        name="proj",
    )(x2, ada, ada, norm_g, cos_t, sin_t, w, wt)


def _out_kernel(n_o, final, *refs):
    x_ref, gate_ref = refs[0], refs[1]
    o_refs = refs[2:2 + n_o]
    w_ref = refs[2 + n_o]
    fg_ref = refs[3 + n_o]
    out_ref = refs[4 + n_o]
    y = None
    k0 = 0
    for o_ref in o_refs:
        kw = o_ref.shape[-1]
        part = _dot(o_ref[...], w_ref[k0:k0 + kw, :])
        y = part if y is None else y + part
        k0 += kw
    xn = x_ref[...] + gate_ref[0, 0, 0] * y
    if final:
        xn = (xn * lax.rsqrt(jnp.mean(xn * xn, axis=-1, keepdims=True) + RMS_EPS)) * fg_ref[...]
    out_ref[...] = xn


def _out_proj(x2, ada, layer, o_list, w_out, final_g, final, seq):
    rows, d = x2.shape
    tm = ROW_TILE
    per_b = seq // tm
    n_o = len(o_list)
    in_specs = [
        pl.BlockSpec((tm, d), lambda i: (i, 0)),
        pl.BlockSpec((1, 1, 1, 1, d), lambda i: (layer, 2, i // per_b, 0, 0)),
    ]
    in_specs += [pl.BlockSpec((tm, o.shape[-1]), lambda i: (i, 0)) for o in o_list]
    in_specs += [pl.BlockSpec(w_out.shape, lambda i: (0, 0)), pl.BlockSpec((1, d), lambda i: (0, 0))]
    return pl.pallas_call(
        functools.partial(_out_kernel, n_o, final),
        grid=(rows // tm,),
        in_specs=in_specs,
        out_specs=pl.BlockSpec((tm, d), lambda i: (i, 0)),
        out_shape=jax.ShapeDtypeStruct((rows, d), F32),
        compiler_params=_cparams(("arbitrary",)),
        name="out_proj",
    )(x2, ada, *o_list, w_out, final_g)


def _stack_heads(q_pairs):
    row = lax.broadcasted_iota(jnp.int32, (PAIR, 1), 0)
    m_lo = (row < HEAD_DIM).astype(F32)
    m_hi = (row >= HEAD_DIM).astype(F32)
    blocks = []
    for p in range(q_pairs.shape[-1] // PAIR):
        qp_t = q_pairs[:, p * PAIR:(p + 1) * PAIR].astype(F32).T
        blocks += [qp_t * m_lo, qp_t * m_hi]
    return jnp.concatenate(blocks, axis=1).astype(BF16)


def _flash_init(d_rows, n_cols):
    return jnp.full((1, n_cols), NEG, F32), jnp.zeros((1, n_cols), F32), jnp.zeros((d_rows, n_cols), F32)


def _flash(carry, s_t, v_t_chunks):
    m, l, acc = carry
    m_new = jnp.maximum(m, jnp.max(s_t, axis=0, keepdims=True))
    alpha = jnp.exp2(m - m_new)
    e = jnp.exp2(s_t - m_new).astype(BF16)
    l = alpha * l + _dot(jnp.ones((SUM_ROWS, e.shape[0]), BF16), e)[0:1]
    acc, k0 = alpha * acc, 0
    for vt in v_t_chunks:
        acc = acc + _dot(vt, e[k0:k0 + vt.shape[1]])
        k0 += vt.shape[1]
    return m_new, l, acc


def _topn_rows(score, valid, n_cand, topn):
    ridx = lax.broadcasted_iota(jnp.int32, score.shape, 0)
    rank = jnp.zeros(score.shape, F32)
    for jp in range(n_cand):
        row = score[jp:jp + 1, :]
        earlier = jnp.where(ridx > jp, 1.0, 0.0)
        rank = rank + jnp.where(row > score, 1.0, jnp.where(row == score, earlier, 0.0))
    return valid & (rank < topn)


def _swa_kernel(tq, nqb, sink_ref, q_ref, k_ref, vt_ref, z_ref, o_ref):
    step = pl.program_id(1)
    n_prev = -(-(A_WINDOW - 1) // tq)
    krow = lax.broadcasted_iota(jnp.int32, (tq, tq), 0)
    qcol = lax.broadcasted_iota(jnp.int32, (tq, tq), 1)
    causal = _tile_lanes(jnp.where(krow <= qcol, 0.0, NEG), A_HEADS)
    edge = _tile_lanes(jnp.where(n_prev * tq + qcol - krow < A_WINDOW, 0.0, NEG), A_HEADS)
    ones = jnp.ones((SUM_ROWS, tq), BF16)
    sink = jnp.concatenate([jnp.full((1, tq), sink_ref[hd] * LOG2E, F32) for hd in range(A_HEADS)], axis=1)

    blocks = [step * nqb + bb for bb in range(nqb)]
    qs = [_stack_heads(q_ref[0, bb * tq:(bb + 1) * tq, :]) for bb in range(nqb)]
    scores = []
    for bb, i in enumerate(blocks):
        row = []
        for j in range(n_prev + 1):
            off = pl.multiple_of(jnp.maximum(i - j, 0) * tq, tq)
            s_t = _dot(k_ref[0, pl.ds(off, tq), :], qs[bb])
            s_t = s_t + causal if j == 0 else (s_t + edge if j == n_prev else s_t)
            row.append(s_t if j == 0 else s_t + jnp.where(i - j >= 0, 0.0, NEG))
        scores.append(row)

    outs = []
    for bb, i in enumerate(blocks):
        m = sink
        for s_t in scores[bb]:
            m = jnp.maximum(m, jnp.max(s_t, axis=0, keepdims=True))
        den, acc = jnp.exp2(sink - m), None
        for j, s_t in enumerate(scores[bb]):
            v_ext = jnp.concatenate([vt_ref[0, jnp.maximum(i - j, 0)], ones], axis=0)
            res = _dot(v_ext, jnp.exp2(s_t - m).astype(BF16))
            den = den + res[HEAD_DIM:HEAD_DIM + 1]
            acc = res[:HEAD_DIM] if acc is None else acc + res[:HEAD_DIM]
        o_t = acc * (1.0 / den)
        heads = jnp.concatenate([o_t[:, hd * tq:(hd + 1) * tq] for hd in range(A_HEADS)], axis=0)
        outs.append(heads.T)
    o_ref[0] = (jnp.concatenate(outs, axis=0) * z_ref[0]).astype(o_ref.dtype)


def _swa(sinks, q, kr, vt, z, tq):
    nb, seq, _ = q.shape
    aw = A_HEADS * HEAD_DIM
    nqb = SWA_BLOCKS_PER_STEP
    rows = nqb * tq
    return pl.pallas_call(
        functools.partial(_swa_kernel, tq, nqb),
        grid=(nb, seq // rows),
        in_specs=[
            pl.BlockSpec(memory_space=pltpu.SMEM),
            pl.BlockSpec((1, rows, aw), lambda b, i: (b, i, 0)),
            pl.BlockSpec((1, seq, PAIR), lambda b, i: (b, 0, 0)),
            pl.BlockSpec((1, seq // tq, HEAD_DIM, tq), lambda b, i: (b, 0, 0, 0)),
            pl.BlockSpec((1, rows, aw), lambda b, i: (b, i, 0)),
        ],
        out_specs=pl.BlockSpec((1, rows, aw), lambda b, i: (b, i, 0)),
        out_shape=jax.ShapeDtypeStruct((nb, seq, aw), BF16),
        compiler_params=_cparams(("arbitrary", "arbitrary")),
        name="swa",
    )(sinks, q, kr, vt, z)


def _cmp_hidden(x_ref, pe_ref, w1_ref, nch):
    top, bot = None, None
    for l in range(CMP_STRIDE):
        rows = x_ref[0, pl.ds(l, nch, stride=CMP_STRIDE), :]
        t = _dot((rows + pe_ref[l:l + 1, :]).astype(BF16), w1_ref[l])
        b = _dot((rows + pe_ref[CMP_STRIDE + l:CMP_STRIDE + l + 1, :]).astype(BF16), w1_ref[CMP_STRIDE + l])
        top = t if top is None else top + t
        bot = b if bot is None else bot + b
    return jax.nn.gelu(top + pltpu.roll(bot, nch - 1, 0))


def _compress_kernel(kc_ref, vc_ref, pek_ref, w1k_ref, w2k_ref, pev_ref, w1v_ref, w2v_ref, ko_ref, vo_ref):
    nch = ko_ref.shape[2]
    hk = _cmp_hidden(kc_ref, pek_ref, w1k_ref, nch).astype(BF16)
    hv = _cmp_hidden(vc_ref, pev_ref, w1v_ref, nch).astype(BF16)
    for g in range(B_KV):
        gs = slice(g * CMP_HIDDEN, (g + 1) * CMP_HIDDEN)
        ko_ref[0, g] = _dot(hk[:, gs], w2k_ref[...]).astype(ko_ref.dtype)
        vo_ref[0, g] = _dot_nt(w2v_ref[...], hv[:, gs]).astype(vo_ref.dtype)


def _compress(kc, vc, pek, w1k, w2k, pev, w1v, w2v):
    nb, seq, _ = kc.shape
    nch = seq // CMP_STRIDE
    full = lambda a: pl.BlockSpec(a.shape, lambda b: (0,) * a.ndim)
    blk = pl.BlockSpec((1, seq, PAIR), lambda b: (b, 0, 0))
    return pl.pallas_call(
        _compress_kernel,
        grid=(nb,),
        in_specs=[blk, blk, full(pek), full(w1k), full(w2k), full(pev), full(w1v), full(w2v)],
        out_specs=[pl.BlockSpec((1, B_KV, nch, PAIR), lambda b: (b, 0, 0, 0)),
                   pl.BlockSpec((1, B_KV, HEAD_DIM, nch), lambda b: (b, 0, 0, 0))],
        out_shape=[jax.ShapeDtypeStruct((nb, B_KV, nch, PAIR), BF16),
                   jax.ShapeDtypeStruct((nb, B_KV, HEAD_DIM, nch), BF16)],
        compiler_params=_cparams(("arbitrary",)),
        name="compress",
    )(kc, vc, pek, w1k, w2k, pev, w1v, w2v)


def _nsa_kernel(tq, nc, nsb, q_ref, kc_ref, vct_ref, ks0_ref, ks1_ref, vs0_ref, vs1_ref, kw0_ref, kw1_ref,
                vw0_ref, vw1_ref, gt_ref, z_ref, ovl_ref, o_ref, bias_ref):
    i = pl.program_id(1)
    groups = range(B_KV)
    ks_refs, vs_refs = (ks0_ref, ks1_ref), (vs0_ref, vs1_ref)
    kw_refs, vw_refs = (kw0_ref, kw1_ref), (vw0_ref, vw1_ref)
    gw = B_REP * HEAD_DIM
    nr = B_REP * tq
    qs = [_stack_heads(q_ref[0, :, g * gw:(g + 1) * gw]) for g in groups]
    t_row = i * tq + lax.broadcasted_iota(jnp.int32, (1, tq), 1)

    nch = kc_ref.shape[2]
    cidx = lax.broadcasted_iota(jnp.int32, (nch, tq), 0)
    ok_c = (cidx < nc) & (t_row >= cidx * CMP_STRIDE + (CMP_LEN - 1))
    bias_c = _tile_lanes(jnp.where(ok_c, 0.0, NEG), B_REP)
    jr = lax.broadcasted_iota(jnp.int32, (nsb, tq), 0)
    tb = t_row // SLC_LEN
    valid = jr <= tb
    forced = (jr == 0) | (jr == tb) | (jr == tb - 1)
    n_prev = -(-(WIN_LEN - 1) // tq)
    blk_per_chunk = tq // SLC_LEN
    krow = lax.broadcasted_iota(jnp.int32, (tq, tq), 0)
    qcol = lax.broadcasted_iota(jnp.int32, (tq, tq), 1)
    causal = jnp.where(krow <= qcol, 0.0, NEG)
    edge = _tile_lanes(jnp.where(n_prev * tq + qcol - krow < WIN_LEN, 0.0, NEG), B_REP)
    causal_r = _tile_lanes(causal, B_REP)
    ones = jnp.ones((SUM_ROWS, tq), BF16)

    def win_chunk(j):
        return jnp.maximum(i - j, 0)

    def win_scores(g, j):
        off = pl.multiple_of(win_chunk(j) * tq, tq)
        s_t = _dot(kw_refs[g][0, pl.ds(off, tq), :], qs[g])
        return s_t + causal_r if j == 0 else (s_t + edge if j == n_prev else s_t)

    def slc_bias(g, c, diagonal):
        rows = [jnp.broadcast_to(bias_ref[g, pl.ds(c * blk_per_chunk + bb, 1), :], (SLC_LEN, tq))
                for bb in range(blk_per_chunk)]
        b = jnp.concatenate(rows, axis=0)
        return _tile_lanes(b + causal if diagonal else b, B_REP)

    def slc_qk(g, c):
        off = pl.multiple_of(c * tq, tq)
        return _dot(ks_refs[g][0, pl.ds(off, tq), :], qs[g])

    def sums_and_pv(v_t, e):
        res = _dot(jnp.concatenate([v_t, ones], axis=0), e)
        return res[HEAD_DIM:HEAD_DIM + 1], res[:HEAD_DIM]

    def masked_ref(m0, present):
        return m0 + jnp.where(present, 0.0, -NEG)

    s_cmp = [_dot(kc_ref[0, g], qs[g]) + bias_c for g in groups]
    s_win0 = [win_scores(g, 0) for g in groups]
    s_diag = [slc_qk(g, i) for g in groups]

    o_cmp = []
    for g in groups:
        s = s_cmp[g]
        m = jnp.max(s, axis=0, keepdims=True)
        m = jnp.where(m > 0.5 * NEG, m, 0.0)
        e = jnp.exp2(s - m)
        den = jnp.sum(e, axis=0, keepdims=True)
        p_t = (e * (1.0 / jnp.where(den > 0, den, 1.0))).astype(BF16)
        o_cmp.append(_dot(vct_ref[0, g], p_t))
        imp = None
        for r in range(B_REP):
            part = _dot(ovl_ref[...], p_t[:, r * tq:(r + 1) * tq])
            imp = part if imp is None else imp + part
        score = jnp.where(valid, jnp.where(forced, FORCE_BONUS, imp), -jnp.inf)
        bias_ref[g] = jnp.where(_topn_rows(score, valid, nsb, SLC_TOPN), 0.0, NEG)

    def write_out(slc, win):
        heads = []
        for g in groups:
            o_slc = slc[g][1] * (1.0 / slc[g][0])
            o_win = win[g][1] * (1.0 / win[g][0])
            gt = gt_ref[0, g * GATE_ROWS:(g + 1) * GATE_ROWS]
            for r in range(B_REP):
                cs = slice(r * tq, (r + 1) * tq)
                heads.append(gt[r:r + 1] * o_cmp[g][:, cs] + gt[B_REP + r:B_REP + r + 1] * o_slc[:, cs]
                             + gt[2 * B_REP + r:2 * B_REP + r + 1] * o_win[:, cs])
        o_t = jnp.concatenate(heads, axis=0)
        o_ref[0] = (o_t.T * z_ref[0]).astype(o_ref.dtype)

    win_m0 = [jnp.max(s_win0[g], axis=0, keepdims=True) for g in groups]
    win = [list(sums_and_pv(vw_refs[g][0, i], jnp.exp2(s_win0[g] - win_m0[g]).astype(BF16))) for g in groups]
    pending = []

    def finish_win(g, j, arg):
        l_c, pv = sums_and_pv(vw_refs[g][0, win_chunk(j)], jnp.exp2(arg).astype(BF16))
        win[g][0], win[g][1] = win[g][0] + l_c, win[g][1] + pv

    for j in range(1, n_prev + 1):
        for g in groups:
            pending.append((g, j, win_scores(g, j) - masked_ref(win_m0[g], i - j >= 0)))
            if len(pending) > MXU_LAG:
                finish_win(*pending.pop(0))
    while pending:
        finish_win(*pending.pop(0))

    s_diag = [s_diag[g] + slc_bias(g, i, True) for g in groups]
    slc_m0 = [jnp.max(s_diag[g], axis=0, keepdims=True) for g in groups]
    slc0 = tuple(sums_and_pv(vs_refs[g][0, i], jnp.exp2(s_diag[g] - slc_m0[g]).astype(BF16)) for g in groups)

    def fast_pair(u, sums):
        sums = [list(x) for x in sums]
        waiting = []

        def finish(g, c, arg):
            l_c, pv = sums_and_pv(vs_refs[g][0, c], jnp.exp2(arg).astype(BF16))
            sums[g][0], sums[g][1] = sums[g][0] + l_c, sums[g][1] + pv

        for c in (2 * u, 2 * u + 1):
            for g in groups:
                waiting.append((g, c, slc_qk(g, c) + slc_bias(g, c, False) - masked_ref(slc_m0[g], c < i)))
                if len(waiting) > MXU_LAG:
                    finish(*waiting.pop(0))
        while waiting:
            finish(*waiting.pop(0))
        return tuple(tuple(x) for x in sums)

    slc = lax.fori_loop(0, (i + 1) // 2, fast_pair, slc0)
    write_out(slc, win)
    worst = jnp.maximum(jnp.maximum(slc[0][0], slc[1][0]), jnp.maximum(win[0][0], win[1][0]))
    safe = jnp.max(worst) < SAFE_SUM

    @pl.when(jnp.logical_not(safe))
    def _():
        span = (n_prev + 1) * tq
        c0 = jnp.maximum(i - n_prev, 0)
        start = pl.multiple_of(c0 * tq, tq)
        diff = t_row - (start + lax.broadcasted_iota(jnp.int32, (span, tq), 0))
        bias_w = _tile_lanes(jnp.where(diff >= 0, jnp.where(diff < WIN_LEN, 0.0, NEG), NEG), B_REP)
        win_x = []
        for g in groups:
            sw = _dot(kw_refs[g][0, pl.ds(start, span), :], qs[g]) + bias_w
            _, l_w, acc_w = _flash(_flash_init(HEAD_DIM, nr), sw,
                                   [vw_refs[g][0, c0 + cc] for cc in range(n_prev + 1)])
            win_x.append((l_w, acc_w))

        def exact_pair(u, carries):
            out = []
            for g in groups:
                carry = carries[g]
                for c in (2 * u, 2 * u + 1):
                    s_t = slc_qk(g, c) + slc_bias(g, c, False) + jnp.where(c < i, 0.0, NEG)
                    carry = _flash(carry, s_t, [vs_refs[g][0, c]])
                out.append(carry)
            return tuple(out)

        first = tuple(_flash(_flash_init(HEAD_DIM, nr), s_diag[g], [vs_refs[g][0, i]]) for g in groups)
        exact = lax.fori_loop(0, (i + 1) // 2, exact_pair, first)
        write_out([(l, acc) for (_, l, acc) in exact], win_x)


def _nsa(q, kcmp, vcmp_t, kr, vt, gates_t, z, ovl_t, nc, nsb, tq):
    nb, seq, _ = q.shape
    bw = B_HEADS * HEAD_DIM
    nch = kcmp.shape[2]
    kblk = lambda col: pl.BlockSpec((1, seq, PAIR), lambda b, i: (b, 0, col))
    vblk = lambda row: pl.BlockSpec((1, seq // tq, HEAD_DIM, tq), lambda b, i: (b, 0, row, 0))
    return pl.pallas_call(
        functools.partial(_nsa_kernel, tq, nc, nsb),
        grid=(nb, seq // tq),
        in_specs=[
            pl.BlockSpec((1, tq, bw), lambda b, i: (b, i, 1)),
            pl.BlockSpec((1, B_KV, nch, PAIR), lambda b, i: (b, 0, 0, 0)),
            pl.BlockSpec((1, B_KV, HEAD_DIM, nch), lambda b, i: (b, 0, 0, 0)),
            kblk(1), kblk(2), vblk(0), vblk(1), kblk(3), kblk(4), vblk(2), vblk(3),
            pl.BlockSpec((1, B_KV * GATE_ROWS, tq), lambda b, i: (b, 0, i)),
            pl.BlockSpec((1, tq, bw), lambda b, i: (b, i, 1)),
            pl.BlockSpec(ovl_t.shape, lambda b, i: (0, 0)),
        ],
        out_specs=pl.BlockSpec((1, tq, bw), lambda b, i: (b, i, 0)),
        out_shape=jax.ShapeDtypeStruct((nb, seq, bw), BF16),
        scratch_shapes=[pltpu.VMEM((B_KV, nsb, tq), F32)],
        compiler_params=_cparams(("arbitrary", "arbitrary")),
        name="nsa",
    )(q, kcmp, vcmp_t, kr, kr, vt, vt, kr, kr, vt, vt, gates_t, z, ovl_t)


def _moba_kernel(nblk, npair, q_ref, k_ref, vt_ref, km_ref, z_ref, o_ref, bias_ref):
    i = pl.program_id(2)
    tq = MOBA_BLOCK
    nr = 2 * tq
    ncand = bias_ref.shape[1]
    pairs = range(npair)
    lanes = [slice(p * PAIR, (p + 1) * PAIR) for p in pairs]
    qs = [_stack_heads(q_ref[0, :, lanes[p]]) for p in pairs]

    gs = []
    for p in pairs:
        km = km_ref[0, :, lanes[p]]
        km = jnp.concatenate([km, jnp.zeros((ncand - nblk, PAIR), F32)], axis=0).astype(BF16)
        gs.append(_dot(km, qs[p]))

    krow = lax.broadcasted_iota(jnp.int32, (tq, tq), 0)
    qcol = lax.broadcasted_iota(jnp.int32, (tq, tq), 1)
    causal = _tile_lanes(jnp.where(krow <= qcol, 0.0, NEG), 2)
    ones = jnp.ones((SUM_ROWS, tq), BF16)

    def scores(p, blk, bias):
        off = pl.multiple_of(blk * tq, tq)
        return _dot(k_ref[0, pl.ds(off, tq), lanes[p]], qs[p]) + bias

    def past_scores(p, blk):
        return scores(p, blk, bias_ref[p, pl.ds(blk, 1), :])

    def weights_times_v(e, p, blk):
        res = []
        for hh in range(2):
            v_ext = jnp.concatenate([vt_ref[0, blk, pl.ds(p * PAIR + hh * HEAD_DIM, HEAD_DIM), :], ones], axis=0)
            res.append(_dot(v_ext, e[:, hh * tq:(hh + 1) * tq]))
        res = jnp.concatenate(res, axis=1)
        return res[HEAD_DIM:HEAD_DIM + 1], res[:HEAD_DIM]

    def flash(carry, s_t, p, blk):
        m, l, acc = carry
        m_new = jnp.maximum(m, jnp.max(s_t, axis=0, keepdims=True))
        alpha = jnp.exp2(m - m_new)
        l_c, pv = weights_times_v(jnp.exp2(s_t - m_new).astype(BF16), p, blk)
        return m_new, alpha * l + l_c, alpha * acc + pv

    def write_out(results):
        outs = []
        for p in pairs:
            l_f, acc_f = results[p]
            o_t = acc_f * (1.0 / l_f)
            outs += [o_t[:, :tq], o_t[:, tq:]]
        o_ref[0] = (jnp.concatenate(outs, axis=0).T * z_ref[0]).astype(o_ref.dtype)

    s_own = [scores(p, i, causal) for p in pairs]
    jr = lax.broadcasted_iota(jnp.int32, (ncand, nr), 0)
    past = jr < i
    for p in pairs:
        sel = _topn_rows(jnp.where(past, gs[p], -jnp.inf), past, nblk, MOBA_TOPK)
        bias_ref[p] = jnp.where(sel, 0.0, NEG)
    own = []
    for p in pairs:
        m0 = jnp.max(s_own[p], axis=0, keepdims=True)
        l0, acc0 = weights_times_v(jnp.exp2(s_own[p] - m0).astype(BF16), p, i)
        own.append((m0, l0, acc0))

    def fast_pair(u, sums):
        sums = [list(x) for x in sums]
        pending = []

        def finish(p, blk, s_t):
            l_c, pv = weights_times_v(jnp.exp2(s_t - own[p][0]).astype(BF16), p, blk)
            sums[p][0] = sums[p][0] + l_c
            sums[p][1] = sums[p][1] + pv

        for blk in (2 * u, 2 * u + 1):
            for p in pairs:
                pending.append((p, blk, past_scores(p, blk)))
                if len(pending) > MXU_LAG:
                    finish(*pending.pop(0))
        while pending:
            finish(*pending.pop(0))
        return tuple(tuple(x) for x in sums)

    fast = lax.fori_loop(0, (i + 1) // 2, fast_pair, tuple((l0, acc0) for (_, l0, acc0) in own))
    write_out(fast)
    worst = fast[0][0]
    for p in pairs[1:]:
        worst = jnp.maximum(worst, fast[p][0])
    safe = jnp.max(worst) < SAFE_SUM

    @pl.when(jnp.logical_not(safe))
    def _():
        def exact_pair(u, carries):
            s_a = [past_scores(p, 2 * u) for p in pairs]
            s_b = [past_scores(p, 2 * u + 1) for p in pairs]
            return tuple(flash(flash(carries[p], s_a[p], p, 2 * u), s_b[p], p, 2 * u + 1) for p in pairs)

        exact = lax.fori_loop(0, (i + 1) // 2, exact_pair, tuple(own))
        write_out([(l, acc) for (_, l, acc) in exact])


def _moba(q, k, vt, kmean, z):
    nb, seq, width = q.shape
    tq = MOBA_BLOCK
    nblk = seq // MOBA_BLOCK
    ncand = -(-nblk // 16) * 16
    npair = MOBA_PAIRS_PER_STEP
    gw = npair * PAIR
    qblk = pl.BlockSpec((1, tq, gw), lambda b, p, i: (b, i, p))
    return pl.pallas_call(
        functools.partial(_moba_kernel, nblk, npair),
        grid=(nb, width // gw, seq // tq),
        in_specs=[
            qblk,
            pl.BlockSpec((1, seq, gw), lambda b, p, i: (b, 0, p)),
            pl.BlockSpec((1, nblk, gw, tq), lambda b, p, i: (b, 0, p, 0)),
            pl.BlockSpec((1, nblk, gw), lambda b, p, i: (b, 0, p)),
            qblk,
        ],
        out_specs=qblk,
        out_shape=jax.ShapeDtypeStruct((nb, seq, width), BF16),
        scratch_shapes=[pltpu.VMEM((npair, ncand, 2 * tq), F32)],
        compiler_params=_cparams(("arbitrary", "arbitrary", "arbitrary")),
        name="moba",
    )(q, k, vt, kmean, z)


def _dup_heads(w):
    d, wd = w.shape
    w3 = w.reshape(d, wd // HEAD_DIM, 1, HEAD_DIM)
    return jnp.broadcast_to(w3, (d, wd // HEAD_DIM, 2, HEAD_DIM)).reshape(d, 2 * wd)


def _even_plan(w_in, nb, seq):
    aq, akv = A_HEADS * HEAD_DIM, HEAD_DIM
    bq, bkv, bg = B_HEADS * HEAD_DIM, B_KV * HEAD_DIM, 3 * B_HEADS
    sizes = (aq, akv, akv, aq, bq, bkv, bkv, bkv, bkv, bkv, bkv, bg, bq)
    offs = np.concatenate([[0], np.cumsum(sizes)])
    qa, ka, va, za, qb, kc, vc, ks, vs, kw, vw, gb, zb = [w_in[:, offs[n]:offs[n + 1]] for n in range(len(sizes))]
    qw = aq + bq
    krw = PAIR * (1 + 2 * B_KV)
    w = jnp.concatenate([qa * (Q_SCALE * LOG2E), qb * (Q_SCALE * LOG2E), _dup_heads(ka), _dup_heads(ks), _dup_heads(kw), kc, vc, za, zb],
                        axis=1).astype(BF16)
    gb4 = gb.reshape(-1, B_KV, B_REP, 3).transpose(0, 1, 3, 2).reshape(-1, B_KV, 3 * B_REP)
    gb4 = jnp.pad(gb4, ((0, 0), (0, 0), (0, GATE_ROWS - 3 * B_REP))).reshape(-1, B_KV * GATE_ROWS)
    vrows = akv + 2 * bkv
    wt = jnp.concatenate([va, vs, vw, gb4], axis=1).T.astype(BF16)
    groups, col = [], 0
    for width, kind in ((qw, "rope"), (krw, "rope"), (bkv, "rope"), (bkv, "plain"), (qw, "silu")):
        groups.append((col, width, kind, len(groups)))
        col += width
    groups_t = ((0, akv, "plain", 5, SWA_TQ), (akv, 2 * bkv, "plain", 6, NSA_TQ),
                (vrows, B_KV * GATE_ROWS, "sigmoid", 7, None))
    rows, tm = nb * seq, ROW_TILE
    per_b = seq // tm
    widths = [(qw, BF16), (krw, BF16), (bkv, F32), (bkv, F32), (qw, F32)]
    out_shape = [jax.ShapeDtypeStruct((rows, wd), dt) for wd, dt in widths]
    out_specs = [pl.BlockSpec((tm, wd), lambda i: (i, 0)) for wd, _ in widths]
    for n_rows, chunk in ((akv, SWA_TQ), (2 * bkv, NSA_TQ)):
        out_shape.append(jax.ShapeDtypeStruct((nb, seq // chunk, n_rows, chunk), BF16))
        out_specs.append(pl.BlockSpec((1, tm // chunk, n_rows, chunk), lambda i: (i // per_b, i % per_b, 0, 0)))
    out_shape.append(jax.ShapeDtypeStruct((nb, B_KV * GATE_ROWS, seq), F32))
    out_specs.append(pl.BlockSpec((1, B_KV * GATE_ROWS, tm), lambda i: (i // per_b, 0, i % per_b)))
    return w, wt, tuple(groups), groups_t, out_shape, out_specs


def _odd_plan(w_in, nb, seq):
    cw = C_HEADS * HEAD_DIM
    q, k, v, z = [w_in[:, n * cw:(n + 1) * cw] for n in range(4)]
    w = jnp.concatenate([q * (Q_SCALE * LOG2E), k, z], axis=1).astype(BF16)
    wt = v.T.astype(BF16)
    groups = ((0, cw, "rope", 0), (cw, cw, "rope_mean", 1), (2 * cw, cw, "silu", 2))
    groups_t = ((0, cw, "plain", 3, MOBA_BLOCK),)
    rows, tm = nb * seq, ROW_TILE
    per_b = seq // tm
    widths = [(cw, BF16), (cw, BF16), (cw, F32)]
    out_shape = [jax.ShapeDtypeStruct((rows, wd), dt) for wd, dt in widths]
    out_specs = [pl.BlockSpec((tm, wd), lambda i: (i, 0)) for wd, _ in widths]
    out_shape += [jax.ShapeDtypeStruct((nb, seq // MOBA_BLOCK, cw, MOBA_BLOCK), BF16),
                  jax.ShapeDtypeStruct((nb, seq // MOBA_BLOCK, 1, cw), F32)]
    out_specs += [pl.BlockSpec((1, tm // MOBA_BLOCK, cw, MOBA_BLOCK), lambda i: (i // per_b, i % per_b, 0, 0)),
                  pl.BlockSpec((1, tm // MOBA_BLOCK, 1, cw), lambda i: (i // per_b, i % per_b, 0, 0))]
    return w, wt, groups, groups_t, out_shape, out_specs


def _rope_tables(seq):
    inv = ROPE_THETA ** (-jnp.arange(0, HEAD_DIM, 2, dtype=F32) / HEAD_DIM)
    ang = jnp.arange(seq, dtype=F32)[:, None] * inv[None, :]
    cos, sin = jnp.cos(ang), jnp.sin(ang)
    reps = PAIR // (HEAD_DIM // 2)
    sign = np.tile(np.concatenate([-np.ones(HEAD_DIM // 2), np.ones(HEAD_DIM // 2)]), PAIR // HEAD_DIM)
    return jnp.tile(cos, (1, reps)), jnp.tile(sin, (1, reps)) * jnp.asarray(sign, F32)[None, :]


def _overlap_matrix_t(seq, nch):
    nc = (seq - CMP_LEN) // CMP_STRIDE + 1
    nsb = seq // SLC_LEN
    cst = np.arange(nc) * CMP_STRIDE
    jj = np.arange(nsb)
    ov = ((cst[None, :] < (jj[:, None] + 1) * SLC_LEN) & (cst[None, :] + CMP_LEN > jj[:, None] * SLC_LEN))
    full = np.zeros((nsb, nch), np.float32)
    full[:, :nc] = ov
    return jnp.asarray(full, BF16), nc, nsb


def _cmp_weights(pe, w1, w2, transpose_out):
    pe2 = jnp.concatenate([pe, pe], axis=1)
    w1t = w1.reshape(CMP_LEN, HEAD_DIM, CMP_HIDDEN)
    zeros = jnp.zeros_like(w1t)
    w1p = jnp.concatenate([jnp.concatenate([w1t, zeros], axis=2), jnp.concatenate([zeros, w1t], axis=2)], axis=1)
    w2o = w2.T if transpose_out else jnp.concatenate([w2, w2], axis=1)
    return pe2, w1p.astype(BF16), w2o.astype(BF16)


def kernel(x, c, w_ada, b_ada, norm_g, w_in_even, a_sinks, cmp_pe_k, cmp_w1_k, cmp_w2_k, cmp_pe_v, cmp_w1_v, cmp_w2_v, w_out_even, w_in_odd, w_out_odd, final_g):
    nb, seq, d = x.shape
    depth = w_ada.shape[0]
    assert seq % ROW_TILE == 0 and seq % MOBA_BLOCK == 0 and NSA_TQ % SLC_LEN == 0
    assert (seq // SLC_LEN) % 8 == 0 and seq % (SWA_TQ * SWA_BLOCKS_PER_STEP) == 0 and ROW_TILE % NSA_TQ == 0
    cos_t, sin_t = _rope_tables(seq)
    ada = _ada(c, w_ada, b_ada)
    nch = seq // CMP_STRIDE
    ovl_t, nc, nsb = _overlap_matrix_t(seq, nch)
    fg = final_g.reshape(1, d)
    x2 = x.reshape(nb * seq, d)
    r3 = lambda a: a.reshape(nb, seq, a.shape[-1])
    for layer in range(depth):
        li = layer // 2
        final = layer == depth - 1
        ng = norm_g[layer].reshape(1, d)
        if layer % 2 == 0:
            w, wt, groups, groups_t, out_shape, out_specs = _even_plan(w_in_even[li], nb, seq)
            q, kr, kc, vc, z, vt_a, vt_b, gates_t = _proj(x2, ada, layer, ng, cos_t, sin_t, w, wt, groups, groups_t,
                                                          out_shape, out_specs, seq)
            q, kr, kc, vc, z = map(r3, (q, kr, kc, vc, z))
            kcmp, vcmp_t = _compress(kc, vc, *_cmp_weights(cmp_pe_k[li], cmp_w1_k[li], cmp_w2_k[li], False),
                                     *_cmp_weights(cmp_pe_v[li], cmp_w1_v[li], cmp_w2_v[li], True))
            oa = _swa(a_sinks[li], q, kr, vt_a, z, SWA_TQ)
            ob = _nsa(q, kcmp, vcmp_t, kr, vt_b, gates_t, z, ovl_t, nc, nsb, NSA_TQ)
            o_list = [oa.reshape(nb * seq, -1), ob.reshape(nb * seq, -1)]
            w_out = w_out_even[li].astype(BF16)
        else:
            w, wt, groups, groups_t, out_shape, out_specs = _odd_plan(w_in_odd[li], nb, seq)
            q, k, z, vt, kmean = _proj(x2, ada, layer, ng, cos_t, sin_t, w, wt, groups, groups_t,
                                       out_shape, out_specs, seq, moba_mean=4)
            o = _moba(r3(q), r3(k), vt, kmean.reshape(nb, seq // MOBA_BLOCK, -1), r3(z))
            o_list = [o.reshape(nb * seq, -1)]
            w_out = w_out_odd[li].astype(BF16)
        x2 = _out_proj(x2, ada, layer, o_list, w_out, fg, final, seq)
    return x2.reshape(nb, seq, d)
```

```python
import functools

import numpy as np
import jax
import jax.numpy as jnp
from jax import lax
from jax.experimental import pallas as pl
from jax.experimental.pallas import tpu as pltpu

D_MODEL = 1024
HEAD_DIM = 64
PAIR = 2 * HEAD_DIM
ROPE_THETA = 10000.0
RMS_EPS = 1e-6
A_HEADS = 8
A_WINDOW = 128
B_HEADS = 8
B_KV = 2
B_REP = B_HEADS // B_KV
CMP_LEN = 32
CMP_STRIDE = 16
CMP_HIDDEN = 256
SLC_LEN = 64
SLC_TOPN = 8
WIN_LEN = 512
FORCE_BONUS = 1e4
C_HEADS = 16
MOBA_BLOCK = 256
MOBA_TOPK = 3
Q_SCALE = HEAD_DIM ** -0.5
LOG2E = 1.4426950408889634

NEG = -1e30
SAFE_SUM = 2.0 ** 64
MXU_COLS = 256
ROW_TILE = 512
SWA_TQ = 128
NSA_TQ = 256
SWA_BLOCKS_PER_STEP = 4
MOBA_PAIRS_PER_STEP = 8
MXU_LAG = 3
SUM_ROWS = 16
GATE_ROWS = 16
VMEM_LIMIT = 56 * 1024 * 1024

BF16 = jnp.bfloat16
F32 = jnp.float32


def _dot_nt(a, b):
    return lax.dot_general(a, b, (((1,), (1,)), ((), ())), preferred_element_type=F32)


def _dot(a, b):
    return jnp.dot(a, b, preferred_element_type=F32)


def _cparams(sem):
    return pltpu.CompilerParams(dimension_semantics=sem, vmem_limit_bytes=VMEM_LIMIT)


def _tile_lanes(a, n):
    return jnp.concatenate([a] * n, axis=1) if n > 1 else a


def _ada_kernel(c_ref, w_ref, b_ref, o_ref):
    c = c_ref[...]
    ca = (c * jax.nn.sigmoid(c)).astype(BF16)
    o_ref[0, 0] = _dot(ca, w_ref[0].astype(BF16)) + b_ref[0, 0]


def _ada(c, w_ada, b_ada):
    depth, d, _ = w_ada.shape
    nb = c.shape[0]
    b4 = b_ada.reshape(depth, 3, 1, d)
    out = pl.pallas_call(
        _ada_kernel,
        grid=(depth, 3),
        in_specs=[
            pl.BlockSpec((nb, d), lambda l, j: (0, 0)),
            pl.BlockSpec((1, d, d), lambda l, j: (l, 0, j)),
            pl.BlockSpec((1, 1, 1, d), lambda l, j: (l, j, 0, 0)),
        ],
        out_specs=pl.BlockSpec((1, 1, nb, d), lambda l, j: (l, j, 0, 0)),
        out_shape=jax.ShapeDtypeStruct((depth, 3, nb, d), F32),
        compiler_params=_cparams(("arbitrary", "arbitrary")),
        name="ada",
    )(c, w_ada, b4)
    return out.reshape(depth, 3, nb, 1, d)


def _rope(a, cos, sin_signed):
    w = a.shape[-1]
    lane = lax.broadcasted_iota(jnp.int32, a.shape, 1)
    first_half = (lane % HEAD_DIM) < (HEAD_DIM // 2)
    partner = jnp.where(first_half, pltpu.roll(a, w - HEAD_DIM // 2, 1), pltpu.roll(a, HEAD_DIM // 2, 1))
    return a * cos + partner * sin_signed


def _proj_kernel(groups, groups_t, t_cols, moba_mean, n_prev_o, *refs):
    x_ref = refs[0]
    prev_refs, refs = refs[1:1 + (n_prev_o + 2 if n_prev_o else 0)], refs[1 + (n_prev_o + 2 if n_prev_o else 0):]
    shift_ref, scale_ref, g_ref, cos_ref, sin_ref, w_ref = refs[:6]
    out_refs, wt_ref = refs[6:-1], refs[-1]
    t_start, t_width = t_cols

    @pl.when(pl.program_id(0) == 0)
    def _():
        for c0 in range(0, t_width, MXU_COLS):
            cw = min(MXU_COLS, t_width - c0)
            wt_ref[c0:c0 + cw, :] = w_ref[:, t_start + c0:t_start + c0 + cw].astype(F32).T.astype(BF16)

    x = x_ref[...]
    if n_prev_o:
        gate_ref, wout_ref = prev_refs[0], prev_refs[-1]
        mix, k0 = None, 0
        for o_ref in prev_refs[1:-1]:
            kw = o_ref.shape[-1]
            part = _dot(o_ref[...], wout_ref[k0:k0 + kw, :])
            mix = part if mix is None else mix + part
            k0 += kw
        x = x + gate_ref[0, 0, 0] * mix
        out_refs[0][...] = x
        out_refs = out_refs[1:]
    y = x * lax.rsqrt(jnp.mean(x * x, axis=-1, keepdims=True) + RMS_EPS)
    h = (y * g_ref[...]) * (1.0 + scale_ref[0, 0, 0]) + shift_ref[0, 0, 0]
    hb = h.astype(BF16)
    tm = hb.shape[0]
    cos = _tile_lanes(cos_ref[...], MXU_COLS // PAIR)
    sin = _tile_lanes(sin_ref[...], MXU_COLS // PAIR)
    for (w_start, width, kind, out_idx) in groups:
        o_ref = out_refs[out_idx]
        for c0 in range(0, width, MXU_COLS):
            cw = min(MXU_COLS, width - c0)
            acc = _dot(hb, w_ref[:, w_start + c0:w_start + c0 + cw])
            if kind in ("rope", "rope_mean", "rope_dup"):
                acc = _rope(acc, cos[:, :cw], sin[:, :cw])
            elif kind == "silu":
                acc = acc * jax.nn.sigmoid(acc)
            if kind == "rope_dup":
                lo = lax.broadcasted_iota(jnp.int32, (tm, PAIR), 1) < HEAD_DIM
                for t in range(cw // PAIR):
                    ab = acc[:, t * PAIR:(t + 1) * PAIR]
                    ba = pltpu.roll(ab, HEAD_DIM, 1)
                    for hh, dup in enumerate((jnp.where(lo, ab, ba), jnp.where(lo, ba, ab))):
                        dst = (2 * (c0 // PAIR + t) + hh) * PAIR
                        if dst < o_ref.shape[1]:
                            o_ref[:, dst:dst + PAIR] = dup.astype(o_ref.dtype)
                continue
            o_ref[:, c0:c0 + cw] = acc.astype(o_ref.dtype)
            if kind == "rope_mean":
                km_ref = out_refs[moba_mean]
                for j in range(tm // MOBA_BLOCK):
                    blk = acc[j * MOBA_BLOCK:(j + 1) * MOBA_BLOCK]
                    km_ref[0, j, :, c0:c0 + cw] = jnp.sum(blk, axis=0, keepdims=True) * (1.0 / MOBA_BLOCK)
    for (r_start, n_rows, kind, out_idx, chunk) in groups_t:
        o_ref = out_refs[out_idx]
        for r0 in range(0, n_rows, MXU_COLS):
            rw = min(MXU_COLS, n_rows - r0)
            acc_t = _dot_nt(wt_ref[r_start + r0:r_start + r0 + rw, :], hb)
            if kind == "sigmoid":
                o_ref[0, r0:r0 + rw, :] = jax.nn.sigmoid(acc_t)
            else:
                for jj in range(tm // chunk):
                    o_ref[0, jj, r0:r0 + rw, :] = acc_t[:, jj * chunk:(jj + 1) * chunk].astype(o_ref.dtype)


def _proj(x2, ada, layer, norm_g, cos_t, sin_t, w, t_cols, groups, groups_t, out_shape, out_specs, seq,
          moba_mean=None, prev=None):
    rows, d = x2.shape
    tm = ROW_TILE
    per_b = seq // tm
    args, in_specs = [x2], [pl.BlockSpec((tm, d), lambda i: (i, 0))]
    n_prev_o = 0
    if prev is not None:
        o_list, w_out = prev
        n_prev_o = len(o_list)
        args += [ada, *o_list, w_out]
        in_specs.append(pl.BlockSpec((1, 1, 1, 1, d), lambda i: (layer - 1, 2, i // per_b, 0, 0)))
        in_specs += [pl.BlockSpec((tm, o.shape[-1]), lambda i: (i, 0)) for o in o_list]
        in_specs.append(pl.BlockSpec(w_out.shape, lambda i: (0, 0)))
        out_shape = [jax.ShapeDtypeStruct((rows, d), F32)] + list(out_shape)
        out_specs = [pl.BlockSpec((tm, d), lambda i: (i, 0))] + list(out_specs)
    args += [ada, ada, norm_g, cos_t, sin_t, w]
    in_specs += [
        pl.BlockSpec((1, 1, 1, 1, d), lambda i: (layer, 0, i // per_b, 0, 0)),
        pl.BlockSpec((1, 1, 1, 1, d), lambda i: (layer, 1, i // per_b, 0, 0)),
        pl.BlockSpec((1, d), lambda i: (0, 0)),
        pl.BlockSpec((tm, PAIR), lambda i: (i % per_b, 0)),
        pl.BlockSpec((tm, PAIR), lambda i: (i % per_b, 0)),
        pl.BlockSpec(w.shape, lambda i: (0, 0)),
    ]
    return pl.pallas_call(
        functools.partial(_proj_kernel, groups, groups_t, t_cols, moba_mean, n_prev_o),
        grid=(rows // tm,),
        in_specs=in_specs,
        out_specs=out_specs,
        out_shape=out_shape,
        scratch_shapes=[pltpu.VMEM((t_cols[1], d), BF16)],
        compiler_params=_cparams(("arbitrary",)),
        name="proj",
    )(*args)


def _out_kernel(n_o, final, *refs):
    x_ref, gate_ref = refs[0], refs[1]
    o_refs = refs[2:2 + n_o]
    w_ref = refs[2 + n_o]
    fg_ref = refs[3 + n_o]
    out_ref = refs[4 + n_o]
    y = None
    k0 = 0
    for o_ref in o_refs:
        kw = o_ref.shape[-1]
        part = _dot(o_ref[...], w_ref[k0:k0 + kw, :])
        y = part if y is None else y + part
        k0 += kw
    xn = x_ref[...] + gate_ref[0, 0, 0] * y
    if final:
        xn = (xn * lax.rsqrt(jnp.mean(xn * xn, axis=-1, keepdims=True) + RMS_EPS)) * fg_ref[...]
    out_ref[...] = xn


def _out_proj(x2, ada, layer, o_list, w_out, final_g, final, seq):
    rows, d = x2.shape
    tm = ROW_TILE
    per_b = seq // tm
    n_o = len(o_list)
    in_specs = [
        pl.BlockSpec((tm, d), lambda i: (i, 0)),
        pl.BlockSpec((1, 1, 1, 1, d), lambda i: (layer, 2, i // per_b, 0, 0)),
    ]
    in_specs += [pl.BlockSpec((tm, o.shape[-1]), lambda i: (i, 0)) for o in o_list]
    in_specs += [pl.BlockSpec(w_out.shape, lambda i: (0, 0)), pl.BlockSpec((1, d), lambda i: (0, 0))]
    return pl.pallas_call(
        functools.partial(_out_kernel, n_o, final),
        grid=(rows // tm,),
        in_specs=in_specs,
        out_specs=pl.BlockSpec((tm, d), lambda i: (i, 0)),
        out_shape=jax.ShapeDtypeStruct((rows, d), F32),
        compiler_params=_cparams(("arbitrary",)),
        name="out_proj",
    )(x2, ada, *o_list, w_out, final_g)


def _stack_heads(q_pairs):
    row = lax.broadcasted_iota(jnp.int32, (PAIR, 1), 0)
    m_lo = (row < HEAD_DIM).astype(F32)
    m_hi = (row >= HEAD_DIM).astype(F32)
    blocks = []
    for p in range(q_pairs.shape[-1] // PAIR):
        qp_t = q_pairs[:, p * PAIR:(p + 1) * PAIR].astype(F32).T
        blocks += [qp_t * m_lo, qp_t * m_hi]
    return jnp.concatenate(blocks, axis=1).astype(BF16)


def _flash_init(d_rows, n_cols):
    return jnp.full((1, n_cols), NEG, F32), jnp.zeros((1, n_cols), F32), jnp.zeros((d_rows, n_cols), F32)


def _flash(carry, s_t, v_t_chunks):
    m, l, acc = carry
    m_new = jnp.maximum(m, jnp.max(s_t, axis=0, keepdims=True))
    alpha = jnp.exp2(m - m_new)
    e = jnp.exp2(s_t - m_new).astype(BF16)
    l = alpha * l + _dot(jnp.ones((SUM_ROWS, e.shape[0]), BF16), e)[0:1]
    acc, k0 = alpha * acc, 0
    for vt in v_t_chunks:
        acc = acc + _dot(vt, e[k0:k0 + vt.shape[1]])
        k0 += vt.shape[1]
    return m_new, l, acc


def _topn_rows(score, valid, n_cand, topn):
    ridx = lax.broadcasted_iota(jnp.int32, score.shape, 0)
    rank = jnp.zeros(score.shape, F32)
    for jp in range(n_cand):
        row = score[jp:jp + 1, :]
        earlier = jnp.where(ridx > jp, 1.0, 0.0)
        rank = rank + jnp.where(row > score, 1.0, jnp.where(row == score, earlier, 0.0))
    return valid & (rank < topn)


def _swa_kernel(tq, nqb, sink_ref, q_ref, k_ref, vt_ref, z_ref, o_ref):
    step = pl.program_id(1)
    n_prev = -(-(A_WINDOW - 1) // tq)
    krow = lax.broadcasted_iota(jnp.int32, (tq, tq), 0)
    qcol = lax.broadcasted_iota(jnp.int32, (tq, tq), 1)
    causal = _tile_lanes(jnp.where(krow <= qcol, 0.0, NEG), A_HEADS)
    edge = _tile_lanes(jnp.where(n_prev * tq + qcol - krow < A_WINDOW, 0.0, NEG), A_HEADS)
    ones = jnp.ones((SUM_ROWS, tq), BF16)
    sink = jnp.concatenate([jnp.full((1, tq), sink_ref[hd] * LOG2E, F32) for hd in range(A_HEADS)], axis=1)

    blocks = [step * nqb + bb for bb in range(nqb)]
    qs = [_stack_heads(q_ref[0, bb * tq:(bb + 1) * tq, :]) for bb in range(nqb)]
    scores = []
    for bb, i in enumerate(blocks):
        row = []
        for j in range(n_prev + 1):
            off = pl.multiple_of(jnp.maximum(i - j, 0) * tq, tq)
            s_t = _dot(k_ref[0, pl.ds(off, tq), :], qs[bb])
            s_t = s_t + causal if j == 0 else (s_t + edge if j == n_prev else s_t)
            row.append(s_t if j == 0 else s_t + jnp.where(i - j >= 0, 0.0, NEG))
        scores.append(row)

    outs = []
    for bb, i in enumerate(blocks):
        m = sink
        for s_t in scores[bb]:
            m = jnp.maximum(m, jnp.max(s_t, axis=0, keepdims=True))
        den, acc = jnp.exp2(sink - m), None
        for j, s_t in enumerate(scores[bb]):
            v_ext = jnp.concatenate([vt_ref[0, jnp.maximum(i - j, 0)], ones], axis=0)
            res = _dot(v_ext, jnp.exp2(s_t - m).astype(BF16))
            den = den + res[HEAD_DIM:HEAD_DIM + 1]
            acc = res[:HEAD_DIM] if acc is None else acc + res[:HEAD_DIM]
        o_t = acc * (1.0 / den)
        heads = jnp.concatenate([o_t[:, hd * tq:(hd + 1) * tq] for hd in range(A_HEADS)], axis=0)
        outs.append(heads.T)
    o_ref[0] = (jnp.concatenate(outs, axis=0) * z_ref[0]).astype(o_ref.dtype)


def _swa(sinks, q, kr, vt, z, tq):
    nb, seq, _ = q.shape
    aw = A_HEADS * HEAD_DIM
    nqb = SWA_BLOCKS_PER_STEP
    rows = nqb * tq
    return pl.pallas_call(
        functools.partial(_swa_kernel, tq, nqb),
        grid=(nb, seq // rows),
        in_specs=[
            pl.BlockSpec(memory_space=pltpu.SMEM),
            pl.BlockSpec((1, rows, aw), lambda b, i: (b, i, 0)),
            pl.BlockSpec((1, seq, PAIR), lambda b, i: (b, 0, 0)),
            pl.BlockSpec((1, seq // tq, HEAD_DIM, tq), lambda b, i: (b, 0, 0, 0)),
            pl.BlockSpec((1, rows, aw), lambda b, i: (b, i, 0)),
        ],
        out_specs=pl.BlockSpec((1, rows, aw), lambda b, i: (b, i, 0)),
        out_shape=jax.ShapeDtypeStruct((nb, seq, aw), BF16),
        compiler_params=_cparams(("arbitrary", "arbitrary")),
        name="swa",
    )(sinks, q, kr, vt, z)


def _cmp_hidden(x_ref, pe_ref, w1_ref, nch):
    top, bot = None, None
    for l in range(CMP_STRIDE):
        rows = x_ref[0, pl.ds(l, nch, stride=CMP_STRIDE), :]
        t = _dot((rows + pe_ref[l:l + 1, :]).astype(BF16), w1_ref[l])
        b = _dot((rows + pe_ref[CMP_STRIDE + l:CMP_STRIDE + l + 1, :]).astype(BF16), w1_ref[CMP_STRIDE + l])
        top = t if top is None else top + t
        bot = b if bot is None else bot + b
    return jax.nn.gelu(top + pltpu.roll(bot, nch - 1, 0))


def _compress_kernel(kc_ref, vc_ref, pek_ref, w1k_ref, w2k_ref, pev_ref, w1v_ref, w2v_ref, ko_ref, vo_ref):
    nch = ko_ref.shape[2]
    hk = _cmp_hidden(kc_ref, pek_ref, w1k_ref, nch).astype(BF16)
    hv = _cmp_hidden(vc_ref, pev_ref, w1v_ref, nch).astype(BF16)
    for g in range(B_KV):
        gs = slice(g * CMP_HIDDEN, (g + 1) * CMP_HIDDEN)
        ko_ref[0, g] = _dot(hk[:, gs], w2k_ref[...]).astype(ko_ref.dtype)
        vo_ref[0, g] = _dot_nt(w2v_ref[...], hv[:, gs]).astype(vo_ref.dtype)


def _compress(kc, vc, pek, w1k, w2k, pev, w1v, w2v):
    nb, seq, _ = kc.shape
    nch = seq // CMP_STRIDE
    full = lambda a: pl.BlockSpec(a.shape, lambda b: (0,) * a.ndim)
    blk = pl.BlockSpec((1, seq, PAIR), lambda b: (b, 0, 0))
    return pl.pallas_call(
        _compress_kernel,
        grid=(nb,),
        in_specs=[blk, blk, full(pek), full(w1k), full(w2k), full(pev), full(w1v), full(w2v)],
        out_specs=[pl.BlockSpec((1, B_KV, nch, PAIR), lambda b: (b, 0, 0, 0)),
                   pl.BlockSpec((1, B_KV, HEAD_DIM, nch), lambda b: (b, 0, 0, 0))],
        out_shape=[jax.ShapeDtypeStruct((nb, B_KV, nch, PAIR), BF16),
                   jax.ShapeDtypeStruct((nb, B_KV, HEAD_DIM, nch), BF16)],
        compiler_params=_cparams(("arbitrary",)),
        name="compress",
    )(kc, vc, pek, w1k, w2k, pev, w1v, w2v)


def _nsa_kernel(tq, nc, nsb, q_ref, kc_ref, vct_ref, ks0_ref, ks1_ref, vs0_ref, vs1_ref, kw0_ref, kw1_ref,
                vw0_ref, vw1_ref, gt_ref, z_ref, ovl_ref, o_ref, bias_ref):
    i = pl.program_id(1)
    groups = range(B_KV)
    ks_refs, vs_refs = (ks0_ref, ks1_ref), (vs0_ref, vs1_ref)
    kw_refs, vw_refs = (kw0_ref, kw1_ref), (vw0_ref, vw1_ref)
    gw = B_REP * HEAD_DIM
    nr = B_REP * tq
    qs = [_stack_heads(q_ref[0, :, g * gw:(g + 1) * gw]) for g in groups]
    t_row = i * tq + lax.broadcasted_iota(jnp.int32, (1, tq), 1)

    nch = kc_ref.shape[2]
    cidx = lax.broadcasted_iota(jnp.int32, (nch, tq), 0)
    ok_c = (cidx < nc) & (t_row >= cidx * CMP_STRIDE + (CMP_LEN - 1))
    bias_c = _tile_lanes(jnp.where(ok_c, 0.0, NEG), B_REP)
    jr = lax.broadcasted_iota(jnp.int32, (nsb, tq), 0)
    tb = t_row // SLC_LEN
    valid = jr <= tb
    forced = (jr == 0) | (jr == tb) | (jr == tb - 1)
    n_prev = -(-(WIN_LEN - 1) // tq)
    blk_per_chunk = tq // SLC_LEN
    krow = lax.broadcasted_iota(jnp.int32, (tq, tq), 0)
    qcol = lax.broadcasted_iota(jnp.int32, (tq, tq), 1)
    causal = jnp.where(krow <= qcol, 0.0, NEG)
    edge = _tile_lanes(jnp.where(n_prev * tq + qcol - krow < WIN_LEN, 0.0, NEG), B_REP)
    causal_r = _tile_lanes(causal, B_REP)
    ones = jnp.ones((SUM_ROWS, tq), BF16)

    def win_chunk(j):
        return jnp.maximum(i - j, 0)

    def win_scores(g, j):
        off = pl.multiple_of(win_chunk(j) * tq, tq)
        s_t = _dot(kw_refs[g][0, pl.ds(off, tq), :], qs[g])
        return s_t + causal_r if j == 0 else (s_t + edge if j == n_prev else s_t)

    def slc_bias(g, c, diagonal):
        rows = [jnp.broadcast_to(bias_ref[g, pl.ds(c * blk_per_chunk + bb, 1), :], (SLC_LEN, tq))
                for bb in range(blk_per_chunk)]
        b = jnp.concatenate(rows, axis=0)
        return _tile_lanes(b + causal if diagonal else b, B_REP)

    def slc_qk(g, c):
        off = pl.multiple_of(c * tq, tq)
        return _dot(ks_refs[g][0, pl.ds(off, tq), :], qs[g])

    def sums_and_pv(v_t, e):
        res = _dot(jnp.concatenate([v_t, ones], axis=0), e)
        return res[HEAD_DIM:HEAD_DIM + 1], res[:HEAD_DIM]

    def masked_ref(m0, present):
        return m0 + jnp.where(present, 0.0, -NEG)

    s_cmp = [_dot(kc_ref[0, g], qs[g]) + bias_c for g in groups]
    s_win0 = [win_scores(g, 0) for g in groups]
    s_diag = [slc_qk(g, i) for g in groups]

    o_cmp = []
    for g in groups:
        s = s_cmp[g]
        m = jnp.max(s, axis=0, keepdims=True)
        m = jnp.where(m > 0.5 * NEG, m, 0.0)
        e = jnp.exp2(s - m)
        den = jnp.sum(e, axis=0, keepdims=True)
        p_t = (e * (1.0 / jnp.where(den > 0, den, 1.0))).astype(BF16)
        o_cmp.append(_dot(vct_ref[0, g], p_t))
        imp = None
        for r in range(B_REP):
            part = _dot(ovl_ref[...], p_t[:, r * tq:(r + 1) * tq])
            imp = part if imp is None else imp + part
        score = jnp.where(valid, jnp.where(forced, FORCE_BONUS, imp), -jnp.inf)
        bias_ref[g] = jnp.where(_topn_rows(score, valid, nsb, SLC_TOPN), 0.0, NEG)

    def write_out(slc, win):
        heads = []
        for g in groups:
            o_slc = slc[g][1] * (1.0 / slc[g][0])
            o_win = win[g][1] * (1.0 / win[g][0])
            gt = gt_ref[0, g * GATE_ROWS:(g + 1) * GATE_ROWS]
            for r in range(B_REP):
                cs = slice(r * tq, (r + 1) * tq)
                heads.append(gt[r:r + 1] * o_cmp[g][:, cs] + gt[B_REP + r:B_REP + r + 1] * o_slc[:, cs]
                             + gt[2 * B_REP + r:2 * B_REP + r + 1] * o_win[:, cs])
        o_t = jnp.concatenate(heads, axis=0)
        o_ref[0] = (o_t.T * z_ref[0]).astype(o_ref.dtype)

    win_m0 = [jnp.max(s_win0[g], axis=0, keepdims=True) for g in groups]
    win = [list(sums_and_pv(vw_refs[g][0, i], jnp.exp2(s_win0[g] - win_m0[g]).astype(BF16))) for g in groups]
    pending = []

    def finish_win(g, j, arg):
        l_c, pv = sums_and_pv(vw_refs[g][0, win_chunk(j)], jnp.exp2(arg).astype(BF16))
        win[g][0], win[g][1] = win[g][0] + l_c, win[g][1] + pv

    for j in range(1, n_prev + 1):
        for g in groups:
            pending.append((g, j, win_scores(g, j) - masked_ref(win_m0[g], i - j >= 0)))
            if len(pending) > MXU_LAG:
                finish_win(*pending.pop(0))
    while pending:
        finish_win(*pending.pop(0))

    s_diag = [s_diag[g] + slc_bias(g, i, True) for g in groups]
    slc_m0 = [jnp.max(s_diag[g], axis=0, keepdims=True) for g in groups]
    slc0 = tuple(sums_and_pv(vs_refs[g][0, i], jnp.exp2(s_diag[g] - slc_m0[g]).astype(BF16)) for g in groups)

    def fast_pair(u, sums):
        sums = [list(x) for x in sums]
        waiting = []

        def finish(g, c, arg):
            l_c, pv = sums_and_pv(vs_refs[g][0, c], jnp.exp2(arg).astype(BF16))
            sums[g][0], sums[g][1] = sums[g][0] + l_c, sums[g][1] + pv

        for c in (2 * u, 2 * u + 1):
            for g in groups:
                waiting.append((g, c, slc_qk(g, c) + slc_bias(g, c, False) - masked_ref(slc_m0[g], c < i)))
                if len(waiting) > MXU_LAG:
                    finish(*waiting.pop(0))
        while waiting:
            finish(*waiting.pop(0))
        return tuple(tuple(x) for x in sums)

    slc = lax.fori_loop(0, (i + 1) // 2, fast_pair, slc0)
    write_out(slc, win)
    worst = jnp.maximum(jnp.maximum(slc[0][0], slc[1][0]), jnp.maximum(win[0][0], win[1][0]))
    safe = jnp.max(worst) < SAFE_SUM

    @pl.when(jnp.logical_not(safe))
    def _():
        span = (n_prev + 1) * tq
        c0 = jnp.maximum(i - n_prev, 0)
        start = pl.multiple_of(c0 * tq, tq)
        diff = t_row - (start + lax.broadcasted_iota(jnp.int32, (span, tq), 0))
        bias_w = _tile_lanes(jnp.where(diff >= 0, jnp.where(diff < WIN_LEN, 0.0, NEG), NEG), B_REP)
        win_x = []
        for g in groups:
            sw = _dot(kw_refs[g][0, pl.ds(start, span), :], qs[g]) + bias_w
            _, l_w, acc_w = _flash(_flash_init(HEAD_DIM, nr), sw,
                                   [vw_refs[g][0, c0 + cc] for cc in range(n_prev + 1)])
            win_x.append((l_w, acc_w))

        def exact_pair(u, carries):
            out = []
            for g in groups:
                carry = carries[g]
                for c in (2 * u, 2 * u + 1):
                    s_t = slc_qk(g, c) + slc_bias(g, c, False) + jnp.where(c < i, 0.0, NEG)
                    carry = _flash(carry, s_t, [vs_refs[g][0, c]])
                out.append(carry)
            return tuple(out)

        first = tuple(_flash(_flash_init(HEAD_DIM, nr), s_diag[g], [vs_refs[g][0, i]]) for g in groups)
        exact = lax.fori_loop(0, (i + 1) // 2, exact_pair, first)
        write_out([(l, acc) for (_, l, acc) in exact], win_x)


def _nsa(q, kcmp, vcmp_t, kr, vt, gates_t, z, ovl_t, nc, nsb, tq):
    nb, seq, _ = q.shape
    bw = B_HEADS * HEAD_DIM
    nch = kcmp.shape[2]
    kblk = lambda col: pl.BlockSpec((1, seq, PAIR), lambda b, i: (b, 0, col))
    vblk = lambda row: pl.BlockSpec((1, seq // tq, HEAD_DIM, tq), lambda b, i: (b, 0, row, 0))
    return pl.pallas_call(
        functools.partial(_nsa_kernel, tq, nc, nsb),
        grid=(nb, seq // tq),
        in_specs=[
            pl.BlockSpec((1, tq, bw), lambda b, i: (b, i, 1)),
            pl.BlockSpec((1, B_KV, nch, PAIR), lambda b, i: (b, 0, 0, 0)),
            pl.BlockSpec((1, B_KV, HEAD_DIM, nch), lambda b, i: (b, 0, 0, 0)),
            kblk(1), kblk(2), vblk(0), vblk(1), kblk(3), kblk(4), vblk(2), vblk(3),
            pl.BlockSpec((1, B_KV * GATE_ROWS, tq), lambda b, i: (b, 0, i)),
            pl.BlockSpec((1, tq, bw), lambda b, i: (b, i, 1)),
            pl.BlockSpec(ovl_t.shape, lambda b, i: (0, 0)),
        ],
        out_specs=pl.BlockSpec((1, tq, bw), lambda b, i: (b, i, 0)),
        out_shape=jax.ShapeDtypeStruct((nb, seq, bw), BF16),
        scratch_shapes=[pltpu.VMEM((B_KV, nsb, tq), F32)],
        compiler_params=_cparams(("arbitrary", "arbitrary")),
        name="nsa",
    )(q, kcmp, vcmp_t, kr, kr, vt, vt, kr, kr, vt, vt, gates_t, z, ovl_t)


def _moba_kernel(nblk, npair, q_ref, k_ref, vt_ref, km_ref, z_ref, o_ref, bias_ref):
    i = pl.program_id(2)
    tq = MOBA_BLOCK
    nr = 2 * tq
    ncand = bias_ref.shape[1]
    pairs = range(npair)
    lanes = [slice(p * PAIR, (p + 1) * PAIR) for p in pairs]
    qs = [_stack_heads(q_ref[0, :, lanes[p]]) for p in pairs]

    gs = []
    for p in pairs:
        km = km_ref[0, :, lanes[p]]
        km = jnp.concatenate([km, jnp.zeros((ncand - nblk, PAIR), F32)], axis=0).astype(BF16)
        gs.append(_dot(km, qs[p]))

    krow = lax.broadcasted_iota(jnp.int32, (tq, tq), 0)
    qcol = lax.broadcasted_iota(jnp.int32, (tq, tq), 1)
    causal = _tile_lanes(jnp.where(krow <= qcol, 0.0, NEG), 2)
    ones = jnp.ones((SUM_ROWS, tq), BF16)

    def scores(p, blk, bias):
        off = pl.multiple_of(blk * tq, tq)
        return _dot(k_ref[0, pl.ds(off, tq), lanes[p]], qs[p]) + bias

    def past_scores(p, blk):
        return scores(p, blk, bias_ref[p, pl.ds(blk, 1), :])

    def weights_times_v(e, p, blk):
        res = []
        for hh in range(2):
            v_ext = jnp.concatenate([vt_ref[0, blk, pl.ds(p * PAIR + hh * HEAD_DIM, HEAD_DIM), :], ones], axis=0)
            res.append(_dot(v_ext, e[:, hh * tq:(hh + 1) * tq]))
        res = jnp.concatenate(res, axis=1)
        return res[HEAD_DIM:HEAD_DIM + 1], res[:HEAD_DIM]

    def flash(carry, s_t, p, blk):
        m, l, acc = carry
        m_new = jnp.maximum(m, jnp.max(s_t, axis=0, keepdims=True))
        alpha = jnp.exp2(m - m_new)
        l_c, pv = weights_times_v(jnp.exp2(s_t - m_new).astype(BF16), p, blk)
        return m_new, alpha * l + l_c, alpha * acc + pv

    def write_out(results):
        outs = []
        for p in pairs:
            l_f, acc_f = results[p]
            o_t = acc_f * (1.0 / l_f)
            outs += [o_t[:, :tq], o_t[:, tq:]]
        o_ref[0] = (jnp.concatenate(outs, axis=0).T * z_ref[0]).astype(o_ref.dtype)

    s_own = [scores(p, i, causal) for p in pairs]
    jr = lax.broadcasted_iota(jnp.int32, (ncand, nr), 0)
    past = jr < i
    for p in pairs:
        sel = _topn_rows(jnp.where(past, gs[p], -jnp.inf), past, nblk, MOBA_TOPK)
        bias_ref[p] = jnp.where(sel, 0.0, NEG)
    own = []
    for p in pairs:
        m0 = jnp.max(s_own[p], axis=0, keepdims=True)
        l0, acc0 = weights_times_v(jnp.exp2(s_own[p] - m0).astype(BF16), p, i)
        own.append((m0, l0, acc0))

    def fast_blocks(blks, sums):
        sums = [list(x) for x in sums]
        pending = []

        def finish(p, blk, s_t):
            l_c, pv = weights_times_v(jnp.exp2(s_t - own[p][0]).astype(BF16), p, blk)
            sums[p][0] = sums[p][0] + l_c
            sums[p][1] = sums[p][1] + pv

        for blk in blks:
            for p in pairs:
                pending.append((p, blk, past_scores(p, blk)))
                if len(pending) > MXU_LAG:
                    finish(*pending.pop(0))
        while pending:
            finish(*pending.pop(0))
        return tuple(tuple(x) for x in sums)

    fast = lax.fori_loop(0, (i + 1) // 2, lambda u, s: fast_blocks((2 * u, 2 * u + 1), s),
                         tuple((l0, acc0) for (_, l0, acc0) in own))
    write_out(fast)
    worst = fast[0][0]
    for p in pairs[1:]:
        worst = jnp.maximum(worst, fast[p][0])
    safe = jnp.max(worst) < SAFE_SUM

    @pl.when(jnp.logical_not(safe))
    def _():
        def exact_block(blk, carries):
            return tuple(flash(carries[p], past_scores(p, blk), p, blk) for p in pairs)

        exact = lax.fori_loop(0, i, exact_block, tuple(own))
        write_out([(l, acc) for (_, l, acc) in exact])


def _moba(q, k, vt, kmean, z):
    nb, seq, width = q.shape
    tq = MOBA_BLOCK
    nblk = seq // MOBA_BLOCK
    ncand = -(-nblk // 16) * 16
    npair = MOBA_PAIRS_PER_STEP
    gw = npair * PAIR
    qblk = pl.BlockSpec((1, tq, gw), lambda b, p, i: (b, i, p))
    return pl.pallas_call(
        functools.partial(_moba_kernel, nblk, npair),
        grid=(nb, width // gw, seq // tq),
        in_specs=[
            qblk,
            pl.BlockSpec((1, seq, gw), lambda b, p, i: (b, 0, p)),
            pl.BlockSpec((1, nblk, gw, tq), lambda b, p, i: (b, 0, p, 0)),
            pl.BlockSpec((1, nblk, gw), lambda b, p, i: (b, 0, p)),
            qblk,
        ],
        out_specs=qblk,
        out_shape=jax.ShapeDtypeStruct((nb, seq, width), BF16),
        scratch_shapes=[pltpu.VMEM((npair, ncand, 2 * tq), F32)],
        compiler_params=_cparams(("arbitrary", "arbitrary", "arbitrary")),
        name="moba",
    )(q, k, vt, kmean, z)


def _dup_heads(w):
    d, wd = w.shape
    w3 = w.reshape(d, wd // HEAD_DIM, 1, HEAD_DIM)
    return jnp.broadcast_to(w3, (d, wd // HEAD_DIM, 2, HEAD_DIM)).reshape(d, 2 * wd)


def _even_plan(w_in, nb, seq):
    aq, akv = A_HEADS * HEAD_DIM, HEAD_DIM
    bq, bkv, bg = B_HEADS * HEAD_DIM, B_KV * HEAD_DIM, 3 * B_HEADS
    sizes = (aq, akv, akv, aq, bq, bkv, bkv, bkv, bkv, bkv, bkv, bg, bq)
    offs = np.concatenate([[0], np.cumsum(sizes)])
    qa, ka, va, za, qb, kc, vc, ks, vs, kw, vw, gb, zb = [w_in[:, offs[n]:offs[n + 1]] for n in range(len(sizes))]
    qw = aq + bq
    krw = PAIR * (1 + 2 * B_KV)
    gb4 = gb.reshape(-1, B_KV, B_REP, 3).transpose(0, 1, 3, 2).reshape(-1, B_KV, 3 * B_REP)
    gb4 = jnp.pad(gb4, ((0, 0), (0, 0), (0, GATE_ROWS - 3 * B_REP))).reshape(-1, B_KV * GATE_ROWS)
    vrows = akv + 2 * bkv
    t_width = -(-(vrows + B_KV * GATE_ROWS) // PAIR) * PAIR
    t_pad = jnp.zeros((w_in.shape[0], t_width - vrows - B_KV * GATE_ROWS), w_in.dtype)
    kheads = akv + 2 * bkv
    k_width = -(-kheads // PAIR) * PAIR
    k_pad = jnp.zeros((w_in.shape[0], k_width - kheads), w_in.dtype)
    w = jnp.concatenate([qa * (Q_SCALE * LOG2E), qb * (Q_SCALE * LOG2E), ka, ks, kw, k_pad, kc, vc, za, zb,
                         va, vs, vw, gb4, t_pad], axis=1).astype(BF16)
    groups, col = [], 0
    for width, kind in ((qw, "rope"), (k_width, "rope_dup"), (bkv, "rope"), (bkv, "plain"), (qw, "silu")):
        groups.append((col, width, kind, len(groups)))
        col += width
    groups_t = ((0, akv, "plain", 5, SWA_TQ), (akv, 2 * bkv, "plain", 6, NSA_TQ),
                (vrows, B_KV * GATE_ROWS, "sigmoid", 7, None))
    rows, tm = nb * seq, ROW_TILE
    per_b = seq // tm
    widths = [(qw, BF16), (krw, BF16), (bkv, F32), (bkv, F32), (qw, F32)]
    out_shape = [jax.ShapeDtypeStruct((rows, wd), dt) for wd, dt in widths]
    out_specs = [pl.BlockSpec((tm, wd), lambda i: (i, 0)) for wd, _ in widths]
    for n_rows, chunk in ((akv, SWA_TQ), (2 * bkv, NSA_TQ)):
        out_shape.append(jax.ShapeDtypeStruct((nb, seq // chunk, n_rows, chunk), BF16))
        out_specs.append(pl.BlockSpec((1, tm // chunk, n_rows, chunk), lambda i: (i // per_b, i % per_b, 0, 0)))
    out_shape.append(jax.ShapeDtypeStruct((nb, B_KV * GATE_ROWS, seq), F32))
    out_specs.append(pl.BlockSpec((1, B_KV * GATE_ROWS, tm), lambda i: (i // per_b, 0, i % per_b)))
    return w, (col, t_width), tuple(groups), groups_t, out_shape, out_specs


def _odd_plan(w_in, nb, seq):
    cw = C_HEADS * HEAD_DIM
    q, k, v, z = [w_in[:, n * cw:(n + 1) * cw] for n in range(4)]
    w = jnp.concatenate([q * (Q_SCALE * LOG2E), k, z, v], axis=1).astype(BF16)
    groups = ((0, cw, "rope", 0), (cw, cw, "rope_mean", 1), (2 * cw, cw, "silu", 2))
    groups_t = ((0, cw, "plain", 3, MOBA_BLOCK),)
    rows, tm = nb * seq, ROW_TILE
    per_b = seq // tm
    widths = [(cw, BF16), (cw, BF16), (cw, F32)]
    out_shape = [jax.ShapeDtypeStruct((rows, wd), dt) for wd, dt in widths]
    out_specs = [pl.BlockSpec((tm, wd), lambda i: (i, 0)) for wd, _ in widths]
    out_shape += [jax.ShapeDtypeStruct((nb, seq // MOBA_BLOCK, cw, MOBA_BLOCK), BF16),
                  jax.ShapeDtypeStruct((nb, seq // MOBA_BLOCK, 1, cw), F32)]
    out_specs += [pl.BlockSpec((1, tm // MOBA_BLOCK, cw, MOBA_BLOCK), lambda i: (i // per_b, i % per_b, 0, 0)),
                  pl.BlockSpec((1, tm // MOBA_BLOCK, 1, cw), lambda i: (i // per_b, i % per_b, 0, 0))]
    return w, (3 * cw, cw), groups, groups_t, out_shape, out_specs


def _rope_tables(seq):
    inv = ROPE_THETA ** (-jnp.arange(0, HEAD_DIM, 2, dtype=F32) / HEAD_DIM)
    ang = jnp.arange(seq, dtype=F32)[:, None] * inv[None, :]
    cos, sin = jnp.cos(ang), jnp.sin(ang)
    reps = PAIR // (HEAD_DIM // 2)
    sign = np.tile(np.concatenate([-np.ones(HEAD_DIM // 2), np.ones(HEAD_DIM // 2)]), PAIR // HEAD_DIM)
    return jnp.tile(cos, (1, reps)), jnp.tile(sin, (1, reps)) * jnp.asarray(sign, F32)[None, :]


def _overlap_matrix_t(seq, nch):
    nc = (seq - CMP_LEN) // CMP_STRIDE + 1
    nsb = seq // SLC_LEN
    cst = np.arange(nc) * CMP_STRIDE
    jj = np.arange(nsb)
    ov = ((cst[None, :] < (jj[:, None] + 1) * SLC_LEN) & (cst[None, :] + CMP_LEN > jj[:, None] * SLC_LEN))
    full = np.zeros((nsb, nch), np.float32)
    full[:, :nc] = ov
    return jnp.asarray(full, BF16), nc, nsb


def _cmp_weights(pe, w1, w2, transpose_out):
    pe2 = jnp.concatenate([pe, pe], axis=1)
    w1t = w1.reshape(CMP_LEN, HEAD_DIM, CMP_HIDDEN)
    zeros = jnp.zeros_like(w1t)
    w1p = jnp.concatenate([jnp.concatenate([w1t, zeros], axis=2), jnp.concatenate([zeros, w1t], axis=2)], axis=1)
    w2o = w2.T if transpose_out else jnp.concatenate([w2, w2], axis=1)
    return pe2, w1p.astype(BF16), w2o.astype(BF16)


def kernel(x, c, w_ada, b_ada, norm_g, w_in_even, a_sinks, cmp_pe_k, cmp_w1_k, cmp_w2_k, cmp_pe_v, cmp_w1_v, cmp_w2_v, w_out_even, w_in_odd, w_out_odd, final_g):
    nb, seq, d = x.shape
    depth = w_ada.shape[0]
    assert seq % ROW_TILE == 0 and seq % MOBA_BLOCK == 0 and NSA_TQ % SLC_LEN == 0
    assert (seq // SLC_LEN) % 8 == 0 and seq % (SWA_TQ * SWA_BLOCKS_PER_STEP) == 0 and ROW_TILE % NSA_TQ == 0
    cos_t, sin_t = _rope_tables(seq)
    ada = _ada(c, w_ada, b_ada)
    nch = seq // CMP_STRIDE
    ovl_t, nc, nsb = _overlap_matrix_t(seq, nch)
    fg = final_g.reshape(1, d)
    x2 = x.reshape(nb * seq, d)
    r3 = lambda a: a.reshape(nb, seq, a.shape[-1])
    prev = None
    for layer in range(depth):
        li = layer // 2
        ng = norm_g[layer].reshape(1, d)
        plan = _even_plan(w_in_even[li], nb, seq) if layer % 2 == 0 else _odd_plan(w_in_odd[li], nb, seq)
        w, t_cols, groups, groups_t, out_shape, out_specs = plan
        outs = _proj(x2, ada, layer, ng, cos_t, sin_t, w, t_cols, groups, groups_t, out_shape, out_specs, seq,
                     moba_mean=None if layer % 2 == 0 else 4, prev=prev)
        if prev is not None:
            x2, outs = outs[0], outs[1:]
        if layer % 2 == 0:
            q, kr, kc, vc, z, vt_a, vt_b, gates_t = outs
            q, kr, kc, vc, z = map(r3, (q, kr, kc, vc, z))
            kcmp, vcmp_t = _compress(kc, vc, *_cmp_weights(cmp_pe_k[li], cmp_w1_k[li], cmp_w2_k[li], False),
                                     *_cmp_weights(cmp_pe_v[li], cmp_w1_v[li], cmp_w2_v[li], True))
            oa = _swa(a_sinks[li], q, kr, vt_a, z, SWA_TQ)
            ob = _nsa(q, kcmp, vcmp_t, kr, vt_b, gates_t, z, ovl_t, nc, nsb, NSA_TQ)
            o_list = [oa.reshape(nb * seq, -1), ob.reshape(nb * seq, -1)]
            w_out = w_out_even[li].astype(BF16)
        else:
            q, k, z, vt, kmean = outs
            o = _moba(r3(q), r3(k), vt, kmean.reshape(nb, seq // MOBA_BLOCK, -1), r3(z))
            o_list = [o.reshape(nb * seq, -1)]
            w_out = w_out_odd[li].astype(BF16)
        prev = (o_list, w_out)
    x2 = _out_proj(x2, ada, depth - 1, prev[0], prev[1], fg, True, seq)
    return x2.reshape(nb, seq, d)
```

```python
import functools

import numpy as np
import jax
import jax.numpy as jnp
from jax import lax
from jax.experimental import pallas as pl
from jax.experimental.pallas import tpu as pltpu

D_MODEL = 1024
HEAD_DIM = 64
PAIR = 2 * HEAD_DIM
ROPE_THETA = 10000.0
RMS_EPS = 1e-6
A_HEADS = 8
A_WINDOW = 128
B_HEADS = 8
B_KV = 2
B_REP = B_HEADS // B_KV
CMP_LEN = 32
CMP_STRIDE = 16
CMP_HIDDEN = 256
SLC_LEN = 64
SLC_TOPN = 8
WIN_LEN = 512
FORCE_BONUS = 1e4
C_HEADS = 16
MOBA_BLOCK = 256
MOBA_TOPK = 3
Q_SCALE = HEAD_DIM ** -0.5
LOG2E = 1.4426950408889634

NEG = -1e30
SAFE_SUM = 2.0 ** 64
MXU_COLS = 256
ROW_TILE = 512
SWA_TQ = 128
NSA_TQ = 256
SWA_BLOCKS_PER_STEP = 4
MOBA_PAIRS_PER_STEP = 8
MXU_LAG = 3
SUM_ROWS = 16
GATE_ROWS = 16
VMEM_LIMIT = 56 * 1024 * 1024

BF16 = jnp.bfloat16
F32 = jnp.float32


def _dot_nt(a, b):
    return lax.dot_general(a, b, (((1,), (1,)), ((), ())), preferred_element_type=F32)


def _dot(a, b):
    return jnp.dot(a, b, preferred_element_type=F32)


def _cparams(sem):
    return pltpu.CompilerParams(dimension_semantics=sem, vmem_limit_bytes=VMEM_LIMIT)


def _tile_lanes(a, n):
    return jnp.concatenate([a] * n, axis=1) if n > 1 else a


def _ada_kernel(c_ref, w_ref, b_ref, o_ref):
    c = c_ref[...]
    ca = (c * jax.nn.sigmoid(c)).astype(BF16)
    o_ref[0, 0] = _dot(ca, w_ref[0].astype(BF16)) + b_ref[0, 0]


def _ada(c, w_ada, b_ada):
    depth, d, _ = w_ada.shape
    nb = c.shape[0]
    b4 = b_ada.reshape(depth, 3, 1, d)
    out = pl.pallas_call(
        _ada_kernel,
        grid=(depth, 3),
        in_specs=[
            pl.BlockSpec((nb, d), lambda l, j: (0, 0)),
            pl.BlockSpec((1, d, d), lambda l, j: (l, 0, j)),
            pl.BlockSpec((1, 1, 1, d), lambda l, j: (l, j, 0, 0)),
        ],
        out_specs=pl.BlockSpec((1, 1, nb, d), lambda l, j: (l, j, 0, 0)),
        out_shape=jax.ShapeDtypeStruct((depth, 3, nb, d), F32),
        compiler_params=_cparams(("arbitrary", "arbitrary")),
        name="ada",
    )(c, w_ada, b4)
    return out.reshape(depth, 3, nb, 1, d)


def _rope(a, cos, sin_signed):
    w = a.shape[-1]
    lane = lax.broadcasted_iota(jnp.int32, a.shape, 1)
    first_half = (lane % HEAD_DIM) < (HEAD_DIM // 2)
    partner = jnp.where(first_half, pltpu.roll(a, w - HEAD_DIM // 2, 1), pltpu.roll(a, HEAD_DIM // 2, 1))
    return a * cos + partner * sin_signed


def _proj_kernel(groups, groups_t, t_cols, moba_mean, n_prev_o, *refs):
    x_ref = refs[0]
    prev_refs, refs = refs[1:1 + (n_prev_o + 2 if n_prev_o else 0)], refs[1 + (n_prev_o + 2 if n_prev_o else 0):]
    shift_ref, scale_ref, g_ref, cos_ref, sin_ref, w_ref = refs[:6]
    out_refs = refs[6:]
    t_start, t_width = t_cols

    x = x_ref[...]
    if n_prev_o:
        gate_ref, wout_ref = prev_refs[0], prev_refs[-1]
        mix, k0 = None, 0
        for o_ref in prev_refs[1:-1]:
            kw = o_ref.shape[-1]
            part = _dot(o_ref[...], wout_ref[k0:k0 + kw, :])
            mix = part if mix is None else mix + part
            k0 += kw
        x = x + gate_ref[0, 0, 0] * mix
        out_refs[0][...] = x
        out_refs = out_refs[1:]
    y = x * lax.rsqrt(jnp.mean(x * x, axis=-1, keepdims=True) + RMS_EPS)
    h = (y * g_ref[...]) * (1.0 + scale_ref[0, 0, 0]) + shift_ref[0, 0, 0]
    hb = h.astype(BF16)
    tm = hb.shape[0]
    cos = _tile_lanes(cos_ref[...], MXU_COLS // PAIR)
    sin = _tile_lanes(sin_ref[...], MXU_COLS // PAIR)
    for (w_start, width, kind, out_idx) in groups:
        o_ref = out_refs[out_idx]
        for c0 in range(0, width, MXU_COLS):
            cw = min(MXU_COLS, width - c0)
            acc = _dot(hb, w_ref[:, w_start + c0:w_start + c0 + cw])
            if kind in ("rope", "rope_mean", "rope_dup"):
                acc = _rope(acc, cos[:, :cw], sin[:, :cw])
            elif kind == "silu":
                acc = acc * jax.nn.sigmoid(acc)
            if kind == "rope_dup":
                lo = lax.broadcasted_iota(jnp.int32, (tm, PAIR), 1) < HEAD_DIM
                for t in range(cw // PAIR):
                    ab = acc[:, t * PAIR:(t + 1) * PAIR]
                    ba = pltpu.roll(ab, HEAD_DIM, 1)
                    for hh, dup in enumerate((jnp.where(lo, ab, ba), jnp.where(lo, ba, ab))):
                        dst = (2 * (c0 // PAIR + t) + hh) * PAIR
                        if dst < o_ref.shape[1]:
                            o_ref[:, dst:dst + PAIR] = dup.astype(o_ref.dtype)
                continue
            o_ref[:, c0:c0 + cw] = acc.astype(o_ref.dtype)
            if kind == "rope_mean":
                km_ref = out_refs[moba_mean]
                for j in range(tm // MOBA_BLOCK):
                    blk = acc[j * MOBA_BLOCK:(j + 1) * MOBA_BLOCK]
                    km_ref[0, j, :, c0:c0 + cw] = jnp.sum(blk, axis=0, keepdims=True) * (1.0 / MOBA_BLOCK)
    for c0 in range(0, t_width, MXU_COLS):
        cw = min(MXU_COLS, t_width - c0)
        acc_t = _dot(hb, w_ref[:, t_start + c0:t_start + c0 + cw]).T
        for (r_start, n_rows, kind, out_idx, chunk) in groups_t:
            lo, hi = max(r_start, c0), min(r_start + n_rows, c0 + cw)
            if lo >= hi:
                continue
            o_ref = out_refs[out_idx]
            part = acc_t[lo - c0:hi - c0]
            if kind == "sigmoid":
                o_ref[0, lo - r_start:hi - r_start, :] = jax.nn.sigmoid(part)
            else:
                for jj in range(tm // chunk):
                    o_ref[0, jj, lo - r_start:hi - r_start, :] = part[:, jj * chunk:(jj + 1) * chunk].astype(o_ref.dtype)


def _proj(x2, ada, layer, norm_g, cos_t, sin_t, w, t_cols, groups, groups_t, out_shape, out_specs, seq,
          moba_mean=None, prev=None):
    rows, d = x2.shape
    tm = ROW_TILE
    per_b = seq // tm
    args, in_specs = [x2], [pl.BlockSpec((tm, d), lambda i: (i, 0))]
    n_prev_o = 0
    if prev is not None:
        o_list, w_out = prev
        n_prev_o = len(o_list)
        args += [ada, *o_list, w_out]
        in_specs.append(pl.BlockSpec((1, 1, 1, 1, d), lambda i: (layer - 1, 2, i // per_b, 0, 0)))
        in_specs += [pl.BlockSpec((tm, o.shape[-1]), lambda i: (i, 0)) for o in o_list]
        in_specs.append(pl.BlockSpec(w_out.shape, lambda i: (0, 0)))
        out_shape = [jax.ShapeDtypeStruct((rows, d), F32)] + list(out_shape)
        out_specs = [pl.BlockSpec((tm, d), lambda i: (i, 0))] + list(out_specs)
    args += [ada, ada, norm_g, cos_t, sin_t, w]
    in_specs += [
        pl.BlockSpec((1, 1, 1, 1, d), lambda i: (layer, 0, i // per_b, 0, 0)),
        pl.BlockSpec((1, 1, 1, 1, d), lambda i: (layer, 1, i // per_b, 0, 0)),
        pl.BlockSpec((1, d), lambda i: (0, 0)),
        pl.BlockSpec((tm, PAIR), lambda i: (i % per_b, 0)),
        pl.BlockSpec((tm, PAIR), lambda i: (i % per_b, 0)),
        pl.BlockSpec(w.shape, lambda i: (0, 0)),
    ]
    return pl.pallas_call(
        functools.partial(_proj_kernel, groups, groups_t, t_cols, moba_mean, n_prev_o),
        grid=(rows // tm,),
        in_specs=in_specs,
        out_specs=out_specs,
        out_shape=out_shape,
        compiler_params=_cparams(("arbitrary",)),
        name="proj",
    )(*args)


def _out_kernel(n_o, final, *refs):
    x_ref, gate_ref = refs[0], refs[1]
    o_refs = refs[2:2 + n_o]
    w_ref = refs[2 + n_o]
    fg_ref = refs[3 + n_o]
    out_ref = refs[4 + n_o]
    y = None
    k0 = 0
    for o_ref in o_refs:
        kw = o_ref.shape[-1]
        part = _dot(o_ref[...], w_ref[k0:k0 + kw, :])
        y = part if y is None else y + part
        k0 += kw
    xn = x_ref[...] + gate_ref[0, 0, 0] * y
    if final:
        xn = (xn * lax.rsqrt(jnp.mean(xn * xn, axis=-1, keepdims=True) + RMS_EPS)) * fg_ref[...]
    out_ref[...] = xn


def _out_proj(x2, ada, layer, o_list, w_out, final_g, final, seq):
    rows, d = x2.shape
    tm = ROW_TILE
    per_b = seq // tm
    n_o = len(o_list)
    in_specs = [
        pl.BlockSpec((tm, d), lambda i: (i, 0)),
        pl.BlockSpec((1, 1, 1, 1, d), lambda i: (layer, 2, i // per_b, 0, 0)),
    ]
    in_specs += [pl.BlockSpec((tm, o.shape[-1]), lambda i: (i, 0)) for o in o_list]
    in_specs += [pl.BlockSpec(w_out.shape, lambda i: (0, 0)), pl.BlockSpec((1, d), lambda i: (0, 0))]
    return pl.pallas_call(
        functools.partial(_out_kernel, n_o, final),
        grid=(rows // tm,),
        in_specs=in_specs,
        out_specs=pl.BlockSpec((tm, d), lambda i: (i, 0)),
        out_shape=jax.ShapeDtypeStruct((rows, d), F32),
        compiler_params=_cparams(("arbitrary",)),
        name="out_proj",
    )(x2, ada, *o_list, w_out, final_g)


def _stack_heads(q_pairs):
    row = lax.broadcasted_iota(jnp.int32, (PAIR, 1), 0)
    m_lo = (row < HEAD_DIM).astype(F32)
    m_hi = (row >= HEAD_DIM).astype(F32)
    blocks = []
    for p in range(q_pairs.shape[-1] // PAIR):
        qp_t = q_pairs[:, p * PAIR:(p + 1) * PAIR].astype(F32).T
        blocks += [qp_t * m_lo, qp_t * m_hi]
    return jnp.concatenate(blocks, axis=1).astype(BF16)


def _flash_init(d_rows, n_cols):
    return jnp.full((1, n_cols), NEG, F32), jnp.zeros((1, n_cols), F32), jnp.zeros((d_rows, n_cols), F32)


def _flash(carry, s_t, v_t_chunks):
    m, l, acc = carry
    m_new = jnp.maximum(m, jnp.max(s_t, axis=0, keepdims=True))
    alpha = jnp.exp2(m - m_new)
    e = jnp.exp2(s_t - m_new).astype(BF16)
    l = alpha * l + _dot(jnp.ones((SUM_ROWS, e.shape[0]), BF16), e)[0:1]
    acc, k0 = alpha * acc, 0
    for vt in v_t_chunks:
        acc = acc + _dot(vt, e[k0:k0 + vt.shape[1]])
        k0 += vt.shape[1]
    return m_new, l, acc


def _topn_rows(score, valid, n_cand, topn):
    ridx = lax.broadcasted_iota(jnp.int32, score.shape, 0)
    rank = jnp.zeros(score.shape, F32)
    for jp in range(n_cand):
        row = score[jp:jp + 1, :]
        earlier = jnp.where(ridx > jp, 1.0, 0.0)
        rank = rank + jnp.where(row > score, 1.0, jnp.where(row == score, earlier, 0.0))
    return valid & (rank < topn)


def _swa_kernel(tq, nqb, sink_ref, q_ref, k_ref, vt_ref, z_ref, o_ref):
    step = pl.program_id(1)
    n_prev = -(-(A_WINDOW - 1) // tq)
    krow = lax.broadcasted_iota(jnp.int32, (tq, tq), 0)
    qcol = lax.broadcasted_iota(jnp.int32, (tq, tq), 1)
    causal = _tile_lanes(jnp.where(krow <= qcol, 0.0, NEG), A_HEADS)
    edge = _tile_lanes(jnp.where(n_prev * tq + qcol - krow < A_WINDOW, 0.0, NEG), A_HEADS)
    ones = jnp.ones((SUM_ROWS, tq), BF16)
    sink = jnp.concatenate([jnp.full((1, tq), sink_ref[hd] * LOG2E, F32) for hd in range(A_HEADS)], axis=1)

    blocks = [step * nqb + bb for bb in range(nqb)]
    qs = [_stack_heads(q_ref[0, bb * tq:(bb + 1) * tq, :]) for bb in range(nqb)]
    scores = []
    for bb, i in enumerate(blocks):
        row = []
        for j in range(n_prev + 1):
            off = pl.multiple_of(jnp.maximum(i - j, 0) * tq, tq)
            s_t = _dot(k_ref[0, pl.ds(off, tq), :], qs[bb])
            s_t = s_t + causal if j == 0 else (s_t + edge if j == n_prev else s_t)
            row.append(s_t if j == 0 else s_t + jnp.where(i - j >= 0, 0.0, NEG))
        scores.append(row)

    outs = []
    for bb, i in enumerate(blocks):
        m = sink
        for s_t in scores[bb]:
            m = jnp.maximum(m, jnp.max(s_t, axis=0, keepdims=True))
        den, acc = jnp.exp2(sink - m), None
        for j, s_t in enumerate(scores[bb]):
            v_ext = jnp.concatenate([vt_ref[0, jnp.maximum(i - j, 0)], ones], axis=0)
            res = _dot(v_ext, jnp.exp2(s_t - m).astype(BF16))
            den = den + res[HEAD_DIM:HEAD_DIM + 1]
            acc = res[:HEAD_DIM] if acc is None else acc + res[:HEAD_DIM]
        o_t = acc * (1.0 / den)
        heads = jnp.concatenate([o_t[:, hd * tq:(hd + 1) * tq] for hd in range(A_HEADS)], axis=0)
        outs.append(heads.T)
    o_ref[0] = (jnp.concatenate(outs, axis=0) * z_ref[0]).astype(o_ref.dtype)


def _swa(sinks, q, kr, vt, z, tq):
    nb, seq, _ = q.shape
    aw = A_HEADS * HEAD_DIM
    nqb = SWA_BLOCKS_PER_STEP
    rows = nqb * tq
    return pl.pallas_call(
        functools.partial(_swa_kernel, tq, nqb),
        grid=(nb, seq // rows),
        in_specs=[
            pl.BlockSpec(memory_space=pltpu.SMEM),
            pl.BlockSpec((1, rows, aw), lambda b, i: (b, i, 0)),
            pl.BlockSpec((1, seq, PAIR), lambda b, i: (b, 0, 0)),
            pl.BlockSpec((1, seq // tq, HEAD_DIM, tq), lambda b, i: (b, 0, 0, 0)),
            pl.BlockSpec((1, rows, aw), lambda b, i: (b, i, 0)),
        ],
        out_specs=pl.BlockSpec((1, rows, aw), lambda b, i: (b, i, 0)),
        out_shape=jax.ShapeDtypeStruct((nb, seq, aw), BF16),
        compiler_params=_cparams(("arbitrary", "arbitrary")),
        name="swa",
    )(sinks, q, kr, vt, z)


def _cmp_hidden(x_ref, pe_ref, w1_ref, nch):
    top, bot = None, None
    for l in range(CMP_STRIDE):
        rows = x_ref[0, pl.ds(l, nch, stride=CMP_STRIDE), :]
        t = _dot((rows + pe_ref[l:l + 1, :]).astype(BF16), w1_ref[l])
        b = _dot((rows + pe_ref[CMP_STRIDE + l:CMP_STRIDE + l + 1, :]).astype(BF16), w1_ref[CMP_STRIDE + l])
        top = t if top is None else top + t
        bot = b if bot is None else bot + b
    return jax.nn.gelu(top + pltpu.roll(bot, nch - 1, 0))


def _compress_kernel(kc_ref, vc_ref, pek_ref, w1k_ref, w2k_ref, pev_ref, w1v_ref, w2v_ref, ko_ref, vo_ref):
    nch = ko_ref.shape[2]
    hk = _cmp_hidden(kc_ref, pek_ref, w1k_ref, nch).astype(BF16)
    hv = _cmp_hidden(vc_ref, pev_ref, w1v_ref, nch).astype(BF16)
    for g in range(B_KV):
        gs = slice(g * CMP_HIDDEN, (g + 1) * CMP_HIDDEN)
        ko_ref[0, g] = _dot(hk[:, gs], w2k_ref[...]).astype(ko_ref.dtype)
        vo_ref[0, g] = _dot_nt(w2v_ref[...], hv[:, gs]).astype(vo_ref.dtype)


def _compress(kc, vc, pek, w1k, w2k, pev, w1v, w2v):
    nb, seq, _ = kc.shape
    nch = seq // CMP_STRIDE
    full = lambda a: pl.BlockSpec(a.shape, lambda b: (0,) * a.ndim)
    blk = pl.BlockSpec((1, seq, PAIR), lambda b: (b, 0, 0))
    return pl.pallas_call(
        _compress_kernel,
        grid=(nb,),
        in_specs=[blk, blk, full(pek), full(w1k), full(w2k), full(pev), full(w1v), full(w2v)],
        out_specs=[pl.BlockSpec((1, B_KV, nch, PAIR), lambda b: (b, 0, 0, 0)),
                   pl.BlockSpec((1, B_KV, HEAD_DIM, nch), lambda b: (b, 0, 0, 0))],
        out_shape=[jax.ShapeDtypeStruct((nb, B_KV, nch, PAIR), BF16),
                   jax.ShapeDtypeStruct((nb, B_KV, HEAD_DIM, nch), BF16)],
        compiler_params=_cparams(("arbitrary",)),
        name="compress",
    )(kc, vc, pek, w1k, w2k, pev, w1v, w2v)


def _nsa_kernel(tq, nc, nsb, q_ref, kc_ref, vct_ref, ks0_ref, ks1_ref, vs0_ref, vs1_ref, kw0_ref, kw1_ref,
                vw0_ref, vw1_ref, gt_ref, z_ref, ovl_ref, o_ref, bias_ref):
    i = pl.program_id(1)
    groups = range(B_KV)
    ks_refs, vs_refs = (ks0_ref, ks1_ref), (vs0_ref, vs1_ref)
    kw_refs, vw_refs = (kw0_ref, kw1_ref), (vw0_ref, vw1_ref)
    gw = B_REP * HEAD_DIM
    nr = B_REP * tq
    qs = [_stack_heads(q_ref[0, :, g * gw:(g + 1) * gw]) for g in groups]
    t_row = i * tq + lax.broadcasted_iota(jnp.int32, (1, tq), 1)

    nch = kc_ref.shape[2]
    cidx = lax.broadcasted_iota(jnp.int32, (nch, tq), 0)
    ok_c = (cidx < nc) & (t_row >= cidx * CMP_STRIDE + (CMP_LEN - 1))
    bias_c = _tile_lanes(jnp.where(ok_c, 0.0, NEG), B_REP)
    jr = lax.broadcasted_iota(jnp.int32, (nsb, tq), 0)
    tb = t_row // SLC_LEN
    valid = jr <= tb
    forced = (jr == 0) | (jr == tb) | (jr == tb - 1)
    n_prev = -(-(WIN_LEN - 1) // tq)
    blk_per_chunk = tq // SLC_LEN
    krow = lax.broadcasted_iota(jnp.int32, (tq, tq), 0)
    qcol = lax.broadcasted_iota(jnp.int32, (tq, tq), 1)
    causal = jnp.where(krow <= qcol, 0.0, NEG)
    edge = _tile_lanes(jnp.where(n_prev * tq + qcol - krow < WIN_LEN, 0.0, NEG), B_REP)
    causal_r = _tile_lanes(causal, B_REP)
    ones = jnp.ones((SUM_ROWS, tq), BF16)

    def win_chunk(j):
        return jnp.maximum(i - j, 0)

    def win_scores(g, j):
        off = pl.multiple_of(win_chunk(j) * tq, tq)
        s_t = _dot(kw_refs[g][0, pl.ds(off, tq), :], qs[g])
        return s_t + causal_r if j == 0 else (s_t + edge if j == n_prev else s_t)

    def slc_bias(g, c, diagonal):
        rows = [jnp.broadcast_to(bias_ref[g, pl.ds(c * blk_per_chunk + bb, 1), :], (SLC_LEN, tq))
                for bb in range(blk_per_chunk)]
        b = jnp.concatenate(rows, axis=0)
        return _tile_lanes(b + causal if diagonal else b, B_REP)

    def slc_qk(g, c):
        off = pl.multiple_of(c * tq, tq)
        return _dot(ks_refs[g][0, pl.ds(off, tq), :], qs[g])

    def sums_and_pv(v_t, e):
        res = _dot(jnp.concatenate([v_t, ones], axis=0), e)
        return res[HEAD_DIM:HEAD_DIM + 1], res[:HEAD_DIM]

    def masked_ref(m0, present):
        return m0 + jnp.where(present, 0.0, -NEG)

    s_cmp = [_dot(kc_ref[0, g], qs[g]) + bias_c for g in groups]
    s_win0 = [win_scores(g, 0) for g in groups]
    s_diag = [slc_qk(g, i) for g in groups]

    o_cmp = []
    for g in groups:
        s = s_cmp[g]
        m = jnp.max(s, axis=0, keepdims=True)
        m = jnp.where(m > 0.5 * NEG, m, 0.0)
        e = jnp.exp2(s - m)
        den = jnp.sum(e, axis=0, keepdims=True)
        p_t = (e * (1.0 / jnp.where(den > 0, den, 1.0))).astype(BF16)
        o_cmp.append(_dot(vct_ref[0, g], p_t))
        imp = None
        for r in range(B_REP):
            part = _dot(ovl_ref[...], p_t[:, r * tq:(r + 1) * tq])
            imp = part if imp is None else imp + part
        score = jnp.where(valid, jnp.where(forced, FORCE_BONUS, imp), -jnp.inf)
        bias_ref[g] = jnp.where(_topn_rows(score, valid, nsb, SLC_TOPN), 0.0, NEG)

    def write_out(slc, win):
        heads = []
        for g in groups:
            o_slc = slc[g][1] * (1.0 / slc[g][0])
            o_win = win[g][1] * (1.0 / win[g][0])
            gt = gt_ref[0, g * GATE_ROWS:(g + 1) * GATE_ROWS]
            for r in range(B_REP):
                cs = slice(r * tq, (r + 1) * tq)
                heads.append(gt[r:r + 1] * o_cmp[g][:, cs] + gt[B_REP + r:B_REP + r + 1] * o_slc[:, cs]
                             + gt[2 * B_REP + r:2 * B_REP + r + 1] * o_win[:, cs])
        o_t = jnp.concatenate(heads, axis=0)
        o_ref[0] = (o_t.T * z_ref[0]).astype(o_ref.dtype)

    win_m0 = [jnp.max(s_win0[g], axis=0, keepdims=True) for g in groups]
    win = [list(sums_and_pv(vw_refs[g][0, i], jnp.exp2(s_win0[g] - win_m0[g]).astype(BF16))) for g in groups]
    pending = []

    def finish_win(g, j, arg):
        l_c, pv = sums_and_pv(vw_refs[g][0, win_chunk(j)], jnp.exp2(arg).astype(BF16))
        win[g][0], win[g][1] = win[g][0] + l_c, win[g][1] + pv

    for j in range(1, n_prev + 1):
        for g in groups:
            pending.append((g, j, win_scores(g, j) - masked_ref(win_m0[g], i - j >= 0)))
            if len(pending) > MXU_LAG:
                finish_win(*pending.pop(0))
    while pending:
        finish_win(*pending.pop(0))

    s_diag = [s_diag[g] + slc_bias(g, i, True) for g in groups]
    slc_m0 = [jnp.max(s_diag[g], axis=0, keepdims=True) for g in groups]
    slc0 = tuple(sums_and_pv(vs_refs[g][0, i], jnp.exp2(s_diag[g] - slc_m0[g]).astype(BF16)) for g in groups)

    def fast_pair(u, sums):
        sums = [list(x) for x in sums]
        waiting = []

        def finish(g, c, arg):
            l_c, pv = sums_and_pv(vs_refs[g][0, c], jnp.exp2(arg).astype(BF16))
            sums[g][0], sums[g][1] = sums[g][0] + l_c, sums[g][1] + pv

        for c in (2 * u, 2 * u + 1):
            for g in groups:
                waiting.append((g, c, slc_qk(g, c) + slc_bias(g, c, False) - masked_ref(slc_m0[g], c < i)))
                if len(waiting) > MXU_LAG:
                    finish(*waiting.pop(0))
        while waiting:
            finish(*waiting.pop(0))
        return tuple(tuple(x) for x in sums)

    slc = lax.fori_loop(0, (i + 1) // 2, fast_pair, slc0)
    write_out(slc, win)
    worst = jnp.maximum(jnp.maximum(slc[0][0], slc[1][0]), jnp.maximum(win[0][0], win[1][0]))
    safe = jnp.max(worst) < SAFE_SUM

    @pl.when(jnp.logical_not(safe))
    def _():
        span = (n_prev + 1) * tq
        c0 = jnp.maximum(i - n_prev, 0)
        start = pl.multiple_of(c0 * tq, tq)
        diff = t_row - (start + lax.broadcasted_iota(jnp.int32, (span, tq), 0))
        bias_w = _tile_lanes(jnp.where(diff >= 0, jnp.where(diff < WIN_LEN, 0.0, NEG), NEG), B_REP)
        win_x = []
        for g in groups:
            sw = _dot(kw_refs[g][0, pl.ds(start, span), :], qs[g]) + bias_w
            _, l_w, acc_w = _flash(_flash_init(HEAD_DIM, nr), sw,
                                   [vw_refs[g][0, c0 + cc] for cc in range(n_prev + 1)])
            win_x.append((l_w, acc_w))

        def exact_pair(u, carries):
            out = []
            for g in groups:
                carry = carries[g]
                for c in (2 * u, 2 * u + 1):
                    s_t = slc_qk(g, c) + slc_bias(g, c, False) + jnp.where(c < i, 0.0, NEG)
                    carry = _flash(carry, s_t, [vs_refs[g][0, c]])
                out.append(carry)
            return tuple(out)

        first = tuple(_flash(_flash_init(HEAD_DIM, nr), s_diag[g], [vs_refs[g][0, i]]) for g in groups)
        exact = lax.fori_loop(0, (i + 1) // 2, exact_pair, first)
        write_out([(l, acc) for (_, l, acc) in exact], win_x)


def _nsa(q, kcmp, vcmp_t, kr, vt, gates_t, z, ovl_t, nc, nsb, tq):
    nb, seq, _ = q.shape
    bw = B_HEADS * HEAD_DIM
    nch = kcmp.shape[2]
    kblk = lambda col: pl.BlockSpec((1, seq, PAIR), lambda b, i: (b, 0, col))
    vblk = lambda row: pl.BlockSpec((1, seq // tq, HEAD_DIM, tq), lambda b, i: (b, 0, row, 0))
    return pl.pallas_call(
        functools.partial(_nsa_kernel, tq, nc, nsb),
        grid=(nb, seq // tq),
        in_specs=[
            pl.BlockSpec((1, tq, bw), lambda b, i: (b, i, 1)),
            pl.BlockSpec((1, B_KV, nch, PAIR), lambda b, i: (b, 0, 0, 0)),
            pl.BlockSpec((1, B_KV, HEAD_DIM, nch), lambda b, i: (b, 0, 0, 0)),
            kblk(1), kblk(2), vblk(0), vblk(1), kblk(3), kblk(4), vblk(2), vblk(3),
            pl.BlockSpec((1, B_KV * GATE_ROWS, tq), lambda b, i: (b, 0, i)),
            pl.BlockSpec((1, tq, bw), lambda b, i: (b, i, 1)),
            pl.BlockSpec(ovl_t.shape, lambda b, i: (0, 0)),
        ],
        out_specs=pl.BlockSpec((1, tq, bw), lambda b, i: (b, i, 0)),
        out_shape=jax.ShapeDtypeStruct((nb, seq, bw), BF16),
        scratch_shapes=[pltpu.VMEM((B_KV, nsb, tq), F32)],
        compiler_params=_cparams(("arbitrary", "arbitrary")),
        name="nsa",
    )(q, kcmp, vcmp_t, kr, kr, vt, vt, kr, kr, vt, vt, gates_t, z, ovl_t)


def _moba_kernel(nblk, npair, q_ref, k_ref, vt_ref, km_ref, z_ref, o_ref, bias_ref):
    i = pl.program_id(2)
    tq = MOBA_BLOCK
    nr = 2 * tq
    ncand = bias_ref.shape[1]
    pairs = range(npair)
    lanes = [slice(p * PAIR, (p + 1) * PAIR) for p in pairs]
    qs = [_stack_heads(q_ref[0, :, lanes[p]]) for p in pairs]

    gs = []
    for p in pairs:
        km = km_ref[0, :, lanes[p]]
        km = jnp.concatenate([km, jnp.zeros((ncand - nblk, PAIR), F32)], axis=0).astype(BF16)
        gs.append(_dot(km, qs[p]))

    krow = lax.broadcasted_iota(jnp.int32, (tq, tq), 0)
    qcol = lax.broadcasted_iota(jnp.int32, (tq, tq), 1)
    causal = _tile_lanes(jnp.where(krow <= qcol, 0.0, NEG), 2)
    ones = jnp.ones((SUM_ROWS, tq), BF16)

    def scores(p, blk, bias):
        off = pl.multiple_of(blk * tq, tq)
        return _dot(k_ref[0, pl.ds(off, tq), lanes[p]], qs[p]) + bias

    def past_scores(p, blk):
        return scores(p, blk, bias_ref[p, pl.ds(blk, 1), :])

    def weights_times_v(e, p, blk):
        res = []
        for hh in range(2):
            v_ext = jnp.concatenate([vt_ref[0, blk, pl.ds(p * PAIR + hh * HEAD_DIM, HEAD_DIM), :], ones], axis=0)
            res.append(_dot(v_ext, e[:, hh * tq:(hh + 1) * tq]))
        res = jnp.concatenate(res, axis=1)
        return res[HEAD_DIM:HEAD_DIM + 1], res[:HEAD_DIM]

    def flash(carry, s_t, p, blk):
        m, l, acc = carry
        m_new = jnp.maximum(m, jnp.max(s_t, axis=0, keepdims=True))
        alpha = jnp.exp2(m - m_new)
        l_c, pv = weights_times_v(jnp.exp2(s_t - m_new).astype(BF16), p, blk)
        return m_new, alpha * l + l_c, alpha * acc + pv

    def write_out(results):
        outs = []
        for p in pairs:
            l_f, acc_f = results[p]
            o_t = acc_f * (1.0 / l_f)
            outs += [o_t[:, :tq], o_t[:, tq:]]
        o_ref[0] = (jnp.concatenate(outs, axis=0).T * z_ref[0]).astype(o_ref.dtype)

    s_own = [scores(p, i, causal) for p in pairs]
    jr = lax.broadcasted_iota(jnp.int32, (ncand, nr), 0)
    past = jr < i
    for p in pairs:
        sel = _topn_rows(jnp.where(past, gs[p], -jnp.inf), past, nblk, MOBA_TOPK)
        bias_ref[p] = jnp.where(sel, 0.0, NEG)
    own = []
    for p in pairs:
        m0 = jnp.max(s_own[p], axis=0, keepdims=True)
        l0, acc0 = weights_times_v(jnp.exp2(s_own[p] - m0).astype(BF16), p, i)
        own.append((m0, l0, acc0))

    def fast_blocks(blks, sums):
        sums = [list(x) for x in sums]
        pending = []

        def finish(p, blk, s_t):
            l_c, pv = weights_times_v(jnp.exp2(s_t - own[p][0]).astype(BF16), p, blk)
            sums[p][0] = sums[p][0] + l_c
            sums[p][1] = sums[p][1] + pv

        for blk in blks:
            for p in pairs:
                pending.append((p, blk, past_scores(p, blk)))
                if len(pending) > MXU_LAG:
                    finish(*pending.pop(0))
        while pending:
            finish(*pending.pop(0))
        return tuple(tuple(x) for x in sums)

    fast = lax.fori_loop(0, (i + 1) // 2, lambda u, s: fast_blocks((2 * u, 2 * u + 1), s),
                         tuple((l0, acc0) for (_, l0, acc0) in own))
    write_out(fast)
    worst = fast[0][0]
    for p in pairs[1:]:
        worst = jnp.maximum(worst, fast[p][0])
    safe = jnp.max(worst) < SAFE_SUM

    @pl.when(jnp.logical_not(safe))
    def _():
        def exact_block(blk, carries):
            return tuple(flash(carries[p], past_scores(p, blk), p, blk) for p in pairs)

        exact = lax.fori_loop(0, i, exact_block, tuple(own))
        write_out([(l, acc) for (_, l, acc) in exact])


def _moba(q, k, vt, kmean, z):
    nb, seq, width = q.shape
    tq = MOBA_BLOCK
    nblk = seq // MOBA_BLOCK
    ncand = -(-nblk // 16) * 16
    npair = MOBA_PAIRS_PER_STEP
    gw = npair * PAIR
    qblk = pl.BlockSpec((1, tq, gw), lambda b, p, i: (b, i, p))
    return pl.pallas_call(
        functools.partial(_moba_kernel, nblk, npair),
        grid=(nb, width // gw, seq // tq),
        in_specs=[
            qblk,
            pl.BlockSpec((1, seq, gw), lambda b, p, i: (b, 0, p)),
            pl.BlockSpec((1, nblk, gw, tq), lambda b, p, i: (b, 0, p, 0)),
            pl.BlockSpec((1, nblk, gw), lambda b, p, i: (b, 0, p)),
            qblk,
        ],
        out_specs=qblk,
        out_shape=jax.ShapeDtypeStruct((nb, seq, width), BF16),
        scratch_shapes=[pltpu.VMEM((npair, ncand, 2 * tq), F32)],
        compiler_params=_cparams(("arbitrary", "arbitrary", "arbitrary")),
        name="moba",
    )(q, k, vt, kmean, z)


def _even_plan(w_in, nb, seq):
    aq, akv = A_HEADS * HEAD_DIM, HEAD_DIM
    bq, bkv, bg = B_HEADS * HEAD_DIM, B_KV * HEAD_DIM, 3 * B_HEADS
    sizes = (aq, akv, akv, aq, bq, bkv, bkv, bkv, bkv, bkv, bkv, bg, bq)
    offs = np.concatenate([[0], np.cumsum(sizes)])
    qa, ka, va, za, qb, kc, vc, ks, vs, kw, vw, gb, zb = [w_in[:, offs[n]:offs[n + 1]] for n in range(len(sizes))]
    qw = aq + bq
    krw = PAIR * (1 + 2 * B_KV)
    gb4 = gb.reshape(-1, B_KV, B_REP, 3).transpose(0, 1, 3, 2).reshape(-1, B_KV, 3 * B_REP)
    gb4 = jnp.pad(gb4, ((0, 0), (0, 0), (0, GATE_ROWS - 3 * B_REP))).reshape(-1, B_KV * GATE_ROWS)
    vrows = akv + 2 * bkv
    t_width = -(-(vrows + B_KV * GATE_ROWS) // PAIR) * PAIR
    t_pad = jnp.zeros((w_in.shape[0], t_width - vrows - B_KV * GATE_ROWS), w_in.dtype)
    kheads = akv + 2 * bkv
    k_width = -(-kheads // PAIR) * PAIR
    k_pad = jnp.zeros((w_in.shape[0], k_width - kheads), w_in.dtype)
    w = jnp.concatenate([qa * (Q_SCALE * LOG2E), qb * (Q_SCALE * LOG2E), ka, ks, kw, k_pad, kc, vc, za, zb,
                         va, vs, vw, gb4, t_pad], axis=1).astype(BF16)
    groups, col = [], 0
    for width, kind in ((qw, "rope"), (k_width, "rope_dup"), (bkv, "rope"), (bkv, "plain"), (qw, "silu")):
        groups.append((col, width, kind, len(groups)))
        col += width
    groups_t = ((0, akv, "plain", 5, SWA_TQ), (akv, 2 * bkv, "plain", 6, NSA_TQ),
                (vrows, B_KV * GATE_ROWS, "sigmoid", 7, None))
    rows, tm = nb * seq, ROW_TILE
    per_b = seq // tm
    widths = [(qw, BF16), (krw, BF16), (bkv, F32), (bkv, F32), (qw, F32)]
    out_shape = [jax.ShapeDtypeStruct((rows, wd), dt) for wd, dt in widths]
    out_specs = [pl.BlockSpec((tm, wd), lambda i: (i, 0)) for wd, _ in widths]
    for n_rows, chunk in ((akv, SWA_TQ), (2 * bkv, NSA_TQ)):
        out_shape.append(jax.ShapeDtypeStruct((nb, seq // chunk, n_rows, chunk), BF16))
        out_specs.append(pl.BlockSpec((1, tm // chunk, n_rows, chunk), lambda i: (i // per_b, i % per_b, 0, 0)))
    out_shape.append(jax.ShapeDtypeStruct((nb, B_KV * GATE_ROWS, seq), F32))
    out_specs.append(pl.BlockSpec((1, B_KV * GATE_ROWS, tm), lambda i: (i // per_b, 0, i % per_b)))
    return w, (col, t_width), tuple(groups), groups_t, out_shape, out_specs


def _odd_plan(w_in, nb, seq):
    cw = C_HEADS * HEAD_DIM
    col_scale = np.ones((1, 4 * cw), np.float32)
    col_scale[:, :cw] = Q_SCALE * LOG2E
    w = (w_in * col_scale).astype(BF16)
    groups = ((0, cw, "rope", 0), (cw, cw, "rope_mean", 1), (3 * cw, cw, "silu", 2))
    groups_t = ((0, cw, "plain", 3, MOBA_BLOCK),)
    rows, tm = nb * seq, ROW_TILE
    per_b = seq // tm
    widths = [(cw, BF16), (cw, BF16), (cw, F32)]
    out_shape = [jax.ShapeDtypeStruct((rows, wd), dt) for wd, dt in widths]
    out_specs = [pl.BlockSpec((tm, wd), lambda i: (i, 0)) for wd, _ in widths]
    out_shape += [jax.ShapeDtypeStruct((nb, seq // MOBA_BLOCK, cw, MOBA_BLOCK), BF16),
                  jax.ShapeDtypeStruct((nb, seq // MOBA_BLOCK, 1, cw), F32)]
    out_specs += [pl.BlockSpec((1, tm // MOBA_BLOCK, cw, MOBA_BLOCK), lambda i: (i // per_b, i % per_b, 0, 0)),
                  pl.BlockSpec((1, tm // MOBA_BLOCK, 1, cw), lambda i: (i // per_b, i % per_b, 0, 0))]
    return w, (2 * cw, cw), groups, groups_t, out_shape, out_specs


def _rope_tables(seq):
    inv = ROPE_THETA ** (-jnp.arange(0, HEAD_DIM, 2, dtype=F32) / HEAD_DIM)
    ang = jnp.arange(seq, dtype=F32)[:, None] * inv[None, :]
    cos, sin = jnp.cos(ang), jnp.sin(ang)
    reps = PAIR // (HEAD_DIM // 2)
    sign = np.tile(np.concatenate([-np.ones(HEAD_DIM // 2), np.ones(HEAD_DIM // 2)]), PAIR // HEAD_DIM)
    return jnp.tile(cos, (1, reps)), jnp.tile(sin, (1, reps)) * jnp.asarray(sign, F32)[None, :]


def _overlap_matrix_t(seq, nch):
    nc = (seq - CMP_LEN) // CMP_STRIDE + 1
    nsb = seq // SLC_LEN
    cst = np.arange(nc) * CMP_STRIDE
    jj = np.arange(nsb)
    ov = ((cst[None, :] < (jj[:, None] + 1) * SLC_LEN) & (cst[None, :] + CMP_LEN > jj[:, None] * SLC_LEN))
    full = np.zeros((nsb, nch), np.float32)
    full[:, :nc] = ov
    return jnp.asarray(full, BF16), nc, nsb


def _cmp_weights(pe, w1, w2, transpose_out):
    pe2 = jnp.concatenate([pe, pe], axis=1)
    w1t = w1.reshape(CMP_LEN, HEAD_DIM, CMP_HIDDEN)
    zeros = jnp.zeros_like(w1t)
    w1p = jnp.concatenate([jnp.concatenate([w1t, zeros], axis=2), jnp.concatenate([zeros, w1t], axis=2)], axis=1)
    w2o = w2.T if transpose_out else jnp.concatenate([w2, w2], axis=1)
    return pe2, w1p.astype(BF16), w2o.astype(BF16)


def kernel(x, c, w_ada, b_ada, norm_g, w_in_even, a_sinks, cmp_pe_k, cmp_w1_k, cmp_w2_k, cmp_pe_v, cmp_w1_v, cmp_w2_v, w_out_even, w_in_odd, w_out_odd, final_g):
    nb, seq, d = x.shape
    depth = w_ada.shape[0]
    assert seq % ROW_TILE == 0 and seq % MOBA_BLOCK == 0 and NSA_TQ % SLC_LEN == 0
    assert (seq // SLC_LEN) % 8 == 0 and seq % (SWA_TQ * SWA_BLOCKS_PER_STEP) == 0 and ROW_TILE % NSA_TQ == 0
    cos_t, sin_t = _rope_tables(seq)
    ada = _ada(c, w_ada, b_ada)
    nch = seq // CMP_STRIDE
    ovl_t, nc, nsb = _overlap_matrix_t(seq, nch)
    fg = final_g.reshape(1, d)
    x2 = x.reshape(nb * seq, d)
    r3 = lambda a: a.reshape(nb, seq, a.shape[-1])
    prev = None
    for layer in range(depth):
        li = layer // 2
        ng = norm_g[layer].reshape(1, d)
        plan = _even_plan(w_in_even[li], nb, seq) if layer % 2 == 0 else _odd_plan(w_in_odd[li], nb, seq)
        w, t_cols, groups, groups_t, out_shape, out_specs = plan
        outs = _proj(x2, ada, layer, ng, cos_t, sin_t, w, t_cols, groups, groups_t, out_shape, out_specs, seq,
                     moba_mean=None if layer % 2 == 0 else 4, prev=prev)
        if prev is not None:
            x2, outs = outs[0], outs[1:]
        if layer % 2 == 0:
            q, kr, kc, vc, z, vt_a, vt_b, gates_t = outs
            q, kr, kc, vc, z = map(r3, (q, kr, kc, vc, z))
            kcmp, vcmp_t = _compress(kc, vc, *_cmp_weights(cmp_pe_k[li], cmp_w1_k[li], cmp_w2_k[li], False),
                                     *_cmp_weights(cmp_pe_v[li], cmp_w1_v[li], cmp_w2_v[li], True))
            oa = _swa(a_sinks[li], q, kr, vt_a, z, SWA_TQ)
            ob = _nsa(q, kcmp, vcmp_t, kr, vt_b, gates_t, z, ovl_t, nc, nsb, NSA_TQ)
            o_list = [oa.reshape(nb * seq, -1), ob.reshape(nb * seq, -1)]
            w_out = w_out_even[li].astype(BF16)
        else:
            q, k, z, vt, kmean = outs
            o = _moba(r3(q), r3(k), vt, kmean.reshape(nb, seq // MOBA_BLOCK, -1), r3(z))
            o_list = [o.reshape(nb * seq, -1)]
            w_out = w_out_odd[li].astype(BF16)
        prev = (o_list, w_out)
    x2 = _out_proj(x2, ada, depth - 1, prev[0], prev[1], fg, True, seq)
    return x2.reshape(nb, seq, d)
```

```python
import functools

import numpy as np
import jax
import jax.numpy as jnp
from jax import lax
from jax.experimental import pallas as pl
from jax.experimental.pallas import tpu as pltpu

D_MODEL = 1024
HEAD_DIM = 64
PAIR = 2 * HEAD_DIM
ROPE_THETA = 10000.0
RMS_EPS = 1e-6
A_HEADS = 8
A_WINDOW = 128
B_HEADS = 8
B_KV = 2
B_REP = B_HEADS // B_KV
CMP_LEN = 32
CMP_STRIDE = 16
CMP_HIDDEN = 256
SLC_LEN = 64
SLC_TOPN = 8
WIN_LEN = 512
FORCE_BONUS = 1e4
C_HEADS = 16
MOBA_BLOCK = 256
MOBA_TOPK = 3
Q_SCALE = HEAD_DIM ** -0.5
LOG2E = 1.4426950408889634

NEG = -1e30
SAFE_SUM = 2.0 ** 64
MXU_COLS = 256
ROW_TILE = 512
SWA_TQ = 128
NSA_TQ = 256
SWA_BLOCKS_PER_STEP = 4
MOBA_PAIRS_PER_STEP = 8
MXU_LAG = 3
SUM_ROWS = 16
GATE_ROWS = 16
VMEM_LIMIT = 56 * 1024 * 1024

BF16 = jnp.bfloat16
F32 = jnp.float32


def _dot_nt(a, b):
    return lax.dot_general(a, b, (((1,), (1,)), ((), ())), preferred_element_type=F32)


def _dot(a, b):
    return jnp.dot(a, b, preferred_element_type=F32)


def _cparams(sem):
    return pltpu.CompilerParams(dimension_semantics=sem, vmem_limit_bytes=VMEM_LIMIT)


def _tile_lanes(a, n):
    return jnp.concatenate([a] * n, axis=1) if n > 1 else a


def _ada_kernel(c_ref, w_ref, b_ref, o_ref):
    c = c_ref[...]
    ca = (c * jax.nn.sigmoid(c)).astype(BF16)
    o_ref[0, 0] = _dot(ca, w_ref[0].astype(BF16)) + b_ref[0, 0]


def _ada(c, w_ada, b_ada):
    depth, d, _ = w_ada.shape
    nb = c.shape[0]
    b4 = b_ada.reshape(depth, 3, 1, d)
    out = pl.pallas_call(
        _ada_kernel,
        grid=(depth, 3),
        in_specs=[
            pl.BlockSpec((nb, d), lambda l, j: (0, 0)),
            pl.BlockSpec((1, d, d), lambda l, j: (l, 0, j)),
            pl.BlockSpec((1, 1, 1, d), lambda l, j: (l, j, 0, 0)),
        ],
        out_specs=pl.BlockSpec((1, 1, nb, d), lambda l, j: (l, j, 0, 0)),
        out_shape=jax.ShapeDtypeStruct((depth, 3, nb, d), F32),
        compiler_params=_cparams(("arbitrary", "arbitrary")),
        name="ada",
    )(c, w_ada, b4)
    return out.reshape(depth, 3, nb, 1, d)


def _rope(a, cos, sin_signed):
    w = a.shape[-1]
    lane = lax.broadcasted_iota(jnp.int32, a.shape, 1)
    first_half = (lane % HEAD_DIM) < (HEAD_DIM // 2)
    partner = jnp.where(first_half, pltpu.roll(a, w - HEAD_DIM // 2, 1), pltpu.roll(a, HEAD_DIM // 2, 1))
    return a * cos + partner * sin_signed


def _proj_kernel(groups, groups_t, t_cols, moba_mean, n_prev_o, *refs):
    x_ref = refs[0]
    prev_refs, refs = refs[1:1 + (n_prev_o + 2 if n_prev_o else 0)], refs[1 + (n_prev_o + 2 if n_prev_o else 0):]
    shift_ref, scale_ref, g_ref, cos_ref, sin_ref, w_ref = refs[:6]
    out_refs = refs[6:]
    t_start, t_width = t_cols

    x = x_ref[...]
    if n_prev_o:
        gate_ref, wout_ref = prev_refs[0], prev_refs[-1]
        mix, k0 = None, 0
        for o_ref in prev_refs[1:-1]:
            kw = o_ref.shape[-1]
            part = _dot(o_ref[...], wout_ref[k0:k0 + kw, :])
            mix = part if mix is None else mix + part
            k0 += kw
        x = x + gate_ref[0, 0, 0] * mix
        out_refs[0][...] = x
        out_refs = out_refs[1:]
    y = x * lax.rsqrt(jnp.mean(x * x, axis=-1, keepdims=True) + RMS_EPS)
    h = (y * g_ref[...]) * (1.0 + scale_ref[0, 0, 0]) + shift_ref[0, 0, 0]
    hb = h.astype(BF16)
    tm = hb.shape[0]
    cos, sin = cos_ref[...], sin_ref[...]
    lo_half = lax.broadcasted_iota(jnp.int32, (tm, PAIR), 1) < HEAD_DIM
    pair_plan = {}
    for (w_start, width, kind, out_idx) in groups:
        for c in range(0, width, PAIR):
            pair_plan[(w_start + c) // PAIR] = (kind, out_idx, c)
    n_cols = w_ref.shape[1]
    for s0 in range(0, n_cols, MXU_COLS):
        sw = min(MXU_COLS, n_cols - s0)
        acc = _dot(hb, w_ref[:, s0:s0 + sw])
        for h0 in range(0, sw, PAIR):
            col = s0 + h0
            a = acc[:, h0:h0 + PAIR]
            if t_start <= col < t_start + t_width:
                f0 = col - t_start
                a_t = a.T
                for (r_start, n_rows, kind, out_idx, chunk) in groups_t:
                    lo, hi = max(r_start, f0), min(r_start + n_rows, f0 + PAIR)
                    if lo >= hi:
                        continue
                    o_ref = out_refs[out_idx]
                    part = a_t[lo - f0:hi - f0]
                    if kind == "sigmoid":
                        o_ref[0, lo - r_start:hi - r_start, :] = jax.nn.sigmoid(part)
                    else:
                        for jj in range(tm // chunk):
                            o_ref[0, jj, lo - r_start:hi - r_start, :] = (
                                part[:, jj * chunk:(jj + 1) * chunk].astype(o_ref.dtype))
                continue
            kind, out_idx, c = pair_plan[col // PAIR]
            o_ref = out_refs[out_idx]
            if kind in ("rope", "rope_mean", "rope_dup"):
                a = _rope(a, cos, sin)
            elif kind == "silu":
                a = a * jax.nn.sigmoid(a)
            if kind == "rope_dup":
                b = pltpu.roll(a, HEAD_DIM, 1)
                for hh, dup in enumerate((jnp.where(lo_half, a, b), jnp.where(lo_half, b, a))):
                    dst = (2 * (c // PAIR) + hh) * PAIR
                    if dst < o_ref.shape[1]:
                        o_ref[:, dst:dst + PAIR] = dup.astype(o_ref.dtype)
                continue
            o_ref[:, c:c + PAIR] = a.astype(o_ref.dtype)
            if kind == "rope_mean":
                km_ref = out_refs[moba_mean]
                for j in range(tm // MOBA_BLOCK):
                    blk = a[j * MOBA_BLOCK:(j + 1) * MOBA_BLOCK]
                    km_ref[0, j, :, c:c + PAIR] = jnp.sum(blk, axis=0, keepdims=True) * (1.0 / MOBA_BLOCK)


def _proj(x2, ada, layer, norm_g, cos_t, sin_t, w, t_cols, groups, groups_t, out_shape, out_specs, seq,
          moba_mean=None, prev=None):
    rows, d = x2.shape
    tm = ROW_TILE
    per_b = seq // tm
    args, in_specs = [x2], [pl.BlockSpec((tm, d), lambda i: (i, 0))]
    n_prev_o = 0
    if prev is not None:
        o_list, w_out = prev
        n_prev_o = len(o_list)
        args += [ada, *o_list, w_out]
        in_specs.append(pl.BlockSpec((1, 1, 1, 1, d), lambda i: (layer - 1, 2, i // per_b, 0, 0)))
        in_specs += [pl.BlockSpec((tm, o.shape[-1]), lambda i: (i, 0)) for o in o_list]
        in_specs.append(pl.BlockSpec(w_out.shape, lambda i: (0, 0)))
        out_shape = [jax.ShapeDtypeStruct((rows, d), F32)] + list(out_shape)
        out_specs = [pl.BlockSpec((tm, d), lambda i: (i, 0))] + list(out_specs)
    args += [ada, ada, norm_g, cos_t, sin_t, w]
    in_specs += [
        pl.BlockSpec((1, 1, 1, 1, d), lambda i: (layer, 0, i // per_b, 0, 0)),
        pl.BlockSpec((1, 1, 1, 1, d), lambda i: (layer, 1, i // per_b, 0, 0)),
        pl.BlockSpec((1, d), lambda i: (0, 0)),
        pl.BlockSpec((tm, PAIR), lambda i: (i % per_b, 0)),
        pl.BlockSpec((tm, PAIR), lambda i: (i % per_b, 0)),
        pl.BlockSpec(w.shape, lambda i: (0, 0)),
    ]
    return pl.pallas_call(
        functools.partial(_proj_kernel, groups, groups_t, t_cols, moba_mean, n_prev_o),
        grid=(rows // tm,),
        in_specs=in_specs,
        out_specs=out_specs,
        out_shape=out_shape,
        compiler_params=_cparams(("arbitrary",)),
        name="proj",
    )(*args)


def _out_kernel(n_o, final, *refs):
    x_ref, gate_ref = refs[0], refs[1]
    o_refs = refs[2:2 + n_o]
    w_ref = refs[2 + n_o]
    fg_ref = refs[3 + n_o]
    out_ref = refs[4 + n_o]
    y = None
    k0 = 0
    for o_ref in o_refs:
        kw = o_ref.shape[-1]
        part = _dot(o_ref[...], w_ref[k0:k0 + kw, :])
        y = part if y is None else y + part
        k0 += kw
    xn = x_ref[...] + gate_ref[0, 0, 0] * y
    if final:
        xn = (xn * lax.rsqrt(jnp.mean(xn * xn, axis=-1, keepdims=True) + RMS_EPS)) * fg_ref[...]
    out_ref[...] = xn


def _out_proj(x2, ada, layer, o_list, w_out, final_g, final, seq):
    rows, d = x2.shape
    tm = ROW_TILE
    per_b = seq // tm
    n_o = len(o_list)
    in_specs = [
        pl.BlockSpec((tm, d), lambda i: (i, 0)),
        pl.BlockSpec((1, 1, 1, 1, d), lambda i: (layer, 2, i // per_b, 0, 0)),
    ]
    in_specs += [pl.BlockSpec((tm, o.shape[-1]), lambda i: (i, 0)) for o in o_list]
    in_specs += [pl.BlockSpec(w_out.shape, lambda i: (0, 0)), pl.BlockSpec((1, d), lambda i: (0, 0))]
    return pl.pallas_call(
        functools.partial(_out_kernel, n_o, final),
        grid=(rows // tm,),
        in_specs=in_specs,
        out_specs=pl.BlockSpec((tm, d), lambda i: (i, 0)),
        out_shape=jax.ShapeDtypeStruct((rows, d), F32),
        compiler_params=_cparams(("arbitrary",)),
        name="out_proj",
    )(x2, ada, *o_list, w_out, final_g)


def _stack_heads(q_pairs):
    row = lax.broadcasted_iota(jnp.int32, (PAIR, 1), 0)
    m_lo = (row < HEAD_DIM).astype(F32)
    m_hi = (row >= HEAD_DIM).astype(F32)
    blocks = []
    for p in range(q_pairs.shape[-1] // PAIR):
        qp_t = q_pairs[:, p * PAIR:(p + 1) * PAIR].astype(F32).T
        blocks += [qp_t * m_lo, qp_t * m_hi]
    return jnp.concatenate(blocks, axis=1).astype(BF16)


def _flash_init(d_rows, n_cols):
    return jnp.full((1, n_cols), NEG, F32), jnp.zeros((1, n_cols), F32), jnp.zeros((d_rows, n_cols), F32)


def _flash(carry, s_t, v_t_chunks):
    m, l, acc = carry
    m_new = jnp.maximum(m, jnp.max(s_t, axis=0, keepdims=True))
    alpha = jnp.exp2(m - m_new)
    e = jnp.exp2(s_t - m_new).astype(BF16)
    l = alpha * l + _dot(jnp.ones((SUM_ROWS, e.shape[0]), BF16), e)[0:1]
    acc, k0 = alpha * acc, 0
    for vt in v_t_chunks:
        acc = acc + _dot(vt, e[k0:k0 + vt.shape[1]])
        k0 += vt.shape[1]
    return m_new, l, acc


def _topn_rows(score, valid, n_cand, topn):
    ridx = lax.broadcasted_iota(jnp.int32, score.shape, 0)
    rank = jnp.zeros(score.shape, F32)
    for jp in range(n_cand):
        row = score[jp:jp + 1, :]
        earlier = jnp.where(ridx > jp, 1.0, 0.0)
        rank = rank + jnp.where(row > score, 1.0, jnp.where(row == score, earlier, 0.0))
    return valid & (rank < topn)


def _swa_kernel(tq, nqb, sink_ref, q_ref, k_ref, vt_ref, z_ref, o_ref):
    step = pl.program_id(1)
    n_prev = -(-(A_WINDOW - 1) // tq)
    krow = lax.broadcasted_iota(jnp.int32, (tq, tq), 0)
    qcol = lax.broadcasted_iota(jnp.int32, (tq, tq), 1)
    causal = _tile_lanes(jnp.where(krow <= qcol, 0.0, NEG), A_HEADS)
    edge = _tile_lanes(jnp.where(n_prev * tq + qcol - krow < A_WINDOW, 0.0, NEG), A_HEADS)
    ones = jnp.ones((SUM_ROWS, tq), BF16)
    sink = jnp.concatenate([jnp.full((1, tq), sink_ref[hd] * LOG2E, F32) for hd in range(A_HEADS)], axis=1)

    blocks = [step * nqb + bb for bb in range(nqb)]
    qs = [_stack_heads(q_ref[0, bb * tq:(bb + 1) * tq, :]) for bb in range(nqb)]
    scores = []
    for bb, i in enumerate(blocks):
        row = []
        for j in range(n_prev + 1):
            off = pl.multiple_of(jnp.maximum(i - j, 0) * tq, tq)
            s_t = _dot(k_ref[0, pl.ds(off, tq), :], qs[bb])
            s_t = s_t + causal if j == 0 else (s_t + edge if j == n_prev else s_t)
            row.append(s_t if j == 0 else s_t + jnp.where(i - j >= 0, 0.0, NEG))
        scores.append(row)

    outs = []
    for bb, i in enumerate(blocks):
        m = sink
        for s_t in scores[bb]:
            m = jnp.maximum(m, jnp.max(s_t, axis=0, keepdims=True))
        den, acc = jnp.exp2(sink - m), None
        for j, s_t in enumerate(scores[bb]):
            v_ext = jnp.concatenate([vt_ref[0, jnp.maximum(i - j, 0)], ones], axis=0)
            res = _dot(v_ext, jnp.exp2(s_t - m).astype(BF16))
            den = den + res[HEAD_DIM:HEAD_DIM + 1]
            acc = res[:HEAD_DIM] if acc is None else acc + res[:HEAD_DIM]
        o_t = acc * (1.0 / den)
        heads = jnp.concatenate([o_t[:, hd * tq:(hd + 1) * tq] for hd in range(A_HEADS)], axis=0)
        outs.append(heads.T)
    o_ref[0] = (jnp.concatenate(outs, axis=0) * z_ref[0]).astype(o_ref.dtype)


def _swa(sinks, q, kr, vt, z, tq):
    nb, seq, _ = q.shape
    aw = A_HEADS * HEAD_DIM
    nqb = SWA_BLOCKS_PER_STEP
    rows = nqb * tq
    return pl.pallas_call(
        functools.partial(_swa_kernel, tq, nqb),
        grid=(nb, seq // rows),
        in_specs=[
            pl.BlockSpec(memory_space=pltpu.SMEM),
            pl.BlockSpec((1, rows, aw), lambda b, i: (b, i, 0)),
            pl.BlockSpec((1, seq, PAIR), lambda b, i: (b, 0, 0)),
            pl.BlockSpec((1, seq // tq, HEAD_DIM, tq), lambda b, i: (b, 0, 0, 0)),
            pl.BlockSpec((1, rows, aw), lambda b, i: (b, i, 0)),
        ],
        out_specs=pl.BlockSpec((1, rows, aw), lambda b, i: (b, i, 0)),
        out_shape=jax.ShapeDtypeStruct((nb, seq, aw), BF16),
        compiler_params=_cparams(("arbitrary", "arbitrary")),
        name="swa",
    )(sinks, q, kr, vt, z)


def _cmp_hidden(x_ref, pe_ref, w1_ref, nch):
    top, bot = None, None
    for l in range(CMP_STRIDE):
        rows = x_ref[0, pl.ds(l, nch, stride=CMP_STRIDE), :]
        t = _dot((rows + pe_ref[l:l + 1, :]).astype(BF16), w1_ref[l])
        b = _dot((rows + pe_ref[CMP_STRIDE + l:CMP_STRIDE + l + 1, :]).astype(BF16), w1_ref[CMP_STRIDE + l])
        top = t if top is None else top + t
        bot = b if bot is None else bot + b
    return jax.nn.gelu(top + pltpu.roll(bot, nch - 1, 0))


def _compress_kernel(kc_ref, vc_ref, pek_ref, w1k_ref, w2k_ref, pev_ref, w1v_ref, w2v_ref, ko_ref, vo_ref):
    nch = ko_ref.shape[2]
    hk = _cmp_hidden(kc_ref, pek_ref, w1k_ref, nch).astype(BF16)
    hv = _cmp_hidden(vc_ref, pev_ref, w1v_ref, nch).astype(BF16)
    for g in range(B_KV):
        gs = slice(g * CMP_HIDDEN, (g + 1) * CMP_HIDDEN)
        ko_ref[0, g] = _dot(hk[:, gs], w2k_ref[...]).astype(ko_ref.dtype)
        vo_ref[0, g] = _dot_nt(w2v_ref[...], hv[:, gs]).astype(vo_ref.dtype)


def _compress(kc, vc, pek, w1k, w2k, pev, w1v, w2v):
    nb, seq, _ = kc.shape
    nch = seq // CMP_STRIDE
    full = lambda a: pl.BlockSpec(a.shape, lambda b: (0,) * a.ndim)
    blk = pl.BlockSpec((1, seq, PAIR), lambda b: (b, 0, 0))
    return pl.pallas_call(
        _compress_kernel,
        grid=(nb,),
        in_specs=[blk, blk, full(pek), full(w1k), full(w2k), full(pev), full(w1v), full(w2v)],
        out_specs=[pl.BlockSpec((1, B_KV, nch, PAIR), lambda b: (b, 0, 0, 0)),
                   pl.BlockSpec((1, B_KV, HEAD_DIM, nch), lambda b: (b, 0, 0, 0))],
        out_shape=[jax.ShapeDtypeStruct((nb, B_KV, nch, PAIR), BF16),
                   jax.ShapeDtypeStruct((nb, B_KV, HEAD_DIM, nch), BF16)],
        compiler_params=_cparams(("arbitrary",)),
        name="compress",
    )(kc, vc, pek, w1k, w2k, pev, w1v, w2v)


def _nsa_kernel(tq, nc, nsb, q_ref, kc_ref, vct_ref, ks0_ref, ks1_ref, vs0_ref, vs1_ref, kw0_ref, kw1_ref,
                vw0_ref, vw1_ref, gt_ref, z_ref, ovl_ref, o_ref, bias_ref):
    i = pl.program_id(1)
    groups = range(B_KV)
    ks_refs, vs_refs = (ks0_ref, ks1_ref), (vs0_ref, vs1_ref)
    kw_refs, vw_refs = (kw0_ref, kw1_ref), (vw0_ref, vw1_ref)
    gw = B_REP * HEAD_DIM
    nr = B_REP * tq
    qs = [_stack_heads(q_ref[0, :, g * gw:(g + 1) * gw]) for g in groups]
    t_row = i * tq + lax.broadcasted_iota(jnp.int32, (1, tq), 1)

    nch = kc_ref.shape[2]
    cidx = lax.broadcasted_iota(jnp.int32, (nch, tq), 0)
    ok_c = (cidx < nc) & (t_row >= cidx * CMP_STRIDE + (CMP_LEN - 1))
    bias_c = _tile_lanes(jnp.where(ok_c, 0.0, NEG), B_REP)
    jr = lax.broadcasted_iota(jnp.int32, (nsb, tq), 0)
    tb = t_row // SLC_LEN
    valid = jr <= tb
    forced = (jr == 0) | (jr == tb) | (jr == tb - 1)
    n_prev = -(-(WIN_LEN - 1) // tq)
    blk_per_chunk = tq // SLC_LEN
    krow = lax.broadcasted_iota(jnp.int32, (tq, tq), 0)
    qcol = lax.broadcasted_iota(jnp.int32, (tq, tq), 1)
    causal = jnp.where(krow <= qcol, 0.0, NEG)
    edge = _tile_lanes(jnp.where(n_prev * tq + qcol - krow < WIN_LEN, 0.0, NEG), B_REP)
    causal_r = _tile_lanes(causal, B_REP)
    ones = jnp.ones((SUM_ROWS, tq), BF16)

    def win_chunk(j):
        return jnp.maximum(i - j, 0)

    def win_scores(g, j):
        off = pl.multiple_of(win_chunk(j) * tq, tq)
        s_t = _dot(kw_refs[g][0, pl.ds(off, tq), :], qs[g])
        return s_t + causal_r if j == 0 else (s_t + edge if j == n_prev else s_t)

    def slc_bias(g, c, diagonal):
        rows = [jnp.broadcast_to(bias_ref[g, pl.ds(c * blk_per_chunk + bb, 1), :], (SLC_LEN, tq))
                for bb in range(blk_per_chunk)]
        b = jnp.concatenate(rows, axis=0)
        return _tile_lanes(b + causal if diagonal else b, B_REP)

    def slc_qk(g, c):
        off = pl.multiple_of(c * tq, tq)
        return _dot(ks_refs[g][0, pl.ds(off, tq), :], qs[g])

    def sums_and_pv(v_t, e):
        res = _dot(jnp.concatenate([v_t, ones], axis=0), e)
        return res[HEAD_DIM:HEAD_DIM + 1], res[:HEAD_DIM]

    def masked_ref(m0, present):
        return m0 + jnp.where(present, 0.0, -NEG)

    s_cmp = [_dot(kc_ref[0, g], qs[g]) + bias_c for g in groups]
    s_win0 = [win_scores(g, 0) for g in groups]
    s_diag = [slc_qk(g, i) for g in groups]

    o_cmp = []
    for g in groups:
        s = s_cmp[g]
        m = jnp.max(s, axis=0, keepdims=True)
        m = jnp.where(m > 0.5 * NEG, m, 0.0)
        e = jnp.exp2(s - m)
        den = jnp.sum(e, axis=0, keepdims=True)
        p_t = (e * (1.0 / jnp.where(den > 0, den, 1.0))).astype(BF16)
        res = _dot(jnp.concatenate([vct_ref[0, g], ovl_ref[...]], axis=0), p_t)
        o_cmp.append(res[:HEAD_DIM])
        imp = res[HEAD_DIM:, 0:tq]
        for r in range(1, B_REP):
            imp = imp + res[HEAD_DIM:, r * tq:(r + 1) * tq]
        score = jnp.where(valid, jnp.where(forced, FORCE_BONUS, imp), -jnp.inf)
        bias_ref[g] = jnp.where(_topn_rows(score, valid, nsb, SLC_TOPN), 0.0, NEG)

    def write_out(slc, win):
        heads = []
        for g in groups:
            o_slc = slc[g][1] * (1.0 / slc[g][0])
            o_win = win[g][1] * (1.0 / win[g][0])
            gt = gt_ref[0, g * GATE_ROWS:(g + 1) * GATE_ROWS]
            for r in range(B_REP):
                cs = slice(r * tq, (r + 1) * tq)
                heads.append(gt[r:r + 1] * o_cmp[g][:, cs] + gt[B_REP + r:B_REP + r + 1] * o_slc[:, cs]
                             + gt[2 * B_REP + r:2 * B_REP + r + 1] * o_win[:, cs])
        o_t = jnp.concatenate(heads, axis=0)
        o_ref[0] = (o_t.T * z_ref[0]).astype(o_ref.dtype)

    win_m0 = [jnp.max(s_win0[g], axis=0, keepdims=True) for g in groups]
    win = [list(sums_and_pv(vw_refs[g][0, i], jnp.exp2(s_win0[g] - win_m0[g]).astype(BF16))) for g in groups]
    pending = []

    def finish_win(g, j, arg):
        l_c, pv = sums_and_pv(vw_refs[g][0, win_chunk(j)], jnp.exp2(arg).astype(BF16))
        win[g][0], win[g][1] = win[g][0] + l_c, win[g][1] + pv

    for j in range(1, n_prev + 1):
        for g in groups:
            pending.append((g, j, win_scores(g, j) - masked_ref(win_m0[g], i - j >= 0)))
            if len(pending) > MXU_LAG:
                finish_win(*pending.pop(0))
    while pending:
        finish_win(*pending.pop(0))

    s_diag = [s_diag[g] + slc_bias(g, i, True) for g in groups]
    slc_m0 = [jnp.max(s_diag[g], axis=0, keepdims=True) for g in groups]
    slc0 = tuple(sums_and_pv(vs_refs[g][0, i], jnp.exp2(s_diag[g] - slc_m0[g]).astype(BF16)) for g in groups)

    def fast_pair(u, sums):
        sums = [list(x) for x in sums]
        waiting = []

        def finish(g, c, arg):
            l_c, pv = sums_and_pv(vs_refs[g][0, c], jnp.exp2(arg).astype(BF16))
            sums[g][0], sums[g][1] = sums[g][0] + l_c, sums[g][1] + pv

        for c in (2 * u, 2 * u + 1):
            for g in groups:
                waiting.append((g, c, slc_qk(g, c) + slc_bias(g, c, False) - masked_ref(slc_m0[g], c < i)))
                if len(waiting) > MXU_LAG:
                    finish(*waiting.pop(0))
        while waiting:
            finish(*waiting.pop(0))
        return tuple(tuple(x) for x in sums)

    slc = lax.fori_loop(0, (i + 1) // 2, fast_pair, slc0)
    write_out(slc, win)
    worst = jnp.maximum(jnp.maximum(slc[0][0], slc[1][0]), jnp.maximum(win[0][0], win[1][0]))
    safe = jnp.max(worst) < SAFE_SUM

    @pl.when(jnp.logical_not(safe))
    def _():
        span = (n_prev + 1) * tq
        c0 = jnp.maximum(i - n_prev, 0)
        start = pl.multiple_of(c0 * tq, tq)
        diff = t_row - (start + lax.broadcasted_iota(jnp.int32, (span, tq), 0))
        bias_w = _tile_lanes(jnp.where(diff >= 0, jnp.where(diff < WIN_LEN, 0.0, NEG), NEG), B_REP)
        win_x = []
        for g in groups:
            sw = _dot(kw_refs[g][0, pl.ds(start, span), :], qs[g]) + bias_w
            _, l_w, acc_w = _flash(_flash_init(HEAD_DIM, nr), sw,
                                   [vw_refs[g][0, c0 + cc] for cc in range(n_prev + 1)])
            win_x.append((l_w, acc_w))

        def exact_pair(u, carries):
            out = []
            for g in groups:
                carry = carries[g]
                for c in (2 * u, 2 * u + 1):
                    s_t = slc_qk(g, c) + slc_bias(g, c, False) + jnp.where(c < i, 0.0, NEG)
                    carry = _flash(carry, s_t, [vs_refs[g][0, c]])
                out.append(carry)
            return tuple(out)

        first = tuple(_flash(_flash_init(HEAD_DIM, nr), s_diag[g], [vs_refs[g][0, i]]) for g in groups)
        exact = lax.fori_loop(0, (i + 1) // 2, exact_pair, first)
        write_out([(l, acc) for (_, l, acc) in exact], win_x)


def _nsa(q, kcmp, vcmp_t, kr, vt, gates_t, z, ovl_t, nc, nsb, tq):
    nb, seq, _ = q.shape
    bw = B_HEADS * HEAD_DIM
    nch = kcmp.shape[2]
    kblk = lambda col: pl.BlockSpec((1, seq, PAIR), lambda b, i: (b, 0, col))
    vblk = lambda row: pl.BlockSpec((1, seq // tq, HEAD_DIM, tq), lambda b, i: (b, 0, row, 0))
    return pl.pallas_call(
        functools.partial(_nsa_kernel, tq, nc, nsb),
        grid=(nb, seq // tq),
        in_specs=[
            pl.BlockSpec((1, tq, bw), lambda b, i: (b, i, 1)),
            pl.BlockSpec((1, B_KV, nch, PAIR), lambda b, i: (b, 0, 0, 0)),
            pl.BlockSpec((1, B_KV, HEAD_DIM, nch), lambda b, i: (b, 0, 0, 0)),
            kblk(1), kblk(2), vblk(0), vblk(1), kblk(3), kblk(4), vblk(2), vblk(3),
            pl.BlockSpec((1, B_KV * GATE_ROWS, tq), lambda b, i: (b, 0, i)),
            pl.BlockSpec((1, tq, bw), lambda b, i: (b, i, 1)),
            pl.BlockSpec(ovl_t.shape, lambda b, i: (0, 0)),
        ],
        out_specs=pl.BlockSpec((1, tq, bw), lambda b, i: (b, i, 0)),
        out_shape=jax.ShapeDtypeStruct((nb, seq, bw), BF16),
        scratch_shapes=[pltpu.VMEM((B_KV, nsb, tq), F32)],
        compiler_params=_cparams(("arbitrary", "arbitrary")),
        name="nsa",
    )(q, kcmp, vcmp_t, kr, kr, vt, vt, kr, kr, vt, vt, gates_t, z, ovl_t)


def _moba_kernel(nblk, npair, q_ref, k_ref, vt_ref, km_ref, z_ref, o_ref, bias_ref):
    i = pl.program_id(2)
    tq = MOBA_BLOCK
    nr = 2 * tq
    ncand = bias_ref.shape[1]
    pairs = range(npair)
    lanes = [slice(p * PAIR, (p + 1) * PAIR) for p in pairs]
    qs = [_stack_heads(q_ref[0, :, lanes[p]]) for p in pairs]

    gs = []
    for p in pairs:
        km = km_ref[0, :, lanes[p]]
        km = jnp.concatenate([km, jnp.zeros((ncand - nblk, PAIR), F32)], axis=0).astype(BF16)
        gs.append(_dot(km, qs[p]))

    krow = lax.broadcasted_iota(jnp.int32, (tq, tq), 0)
    qcol = lax.broadcasted_iota(jnp.int32, (tq, tq), 1)
    causal = _tile_lanes(jnp.where(krow <= qcol, 0.0, NEG), 2)
    ones = jnp.ones((SUM_ROWS, tq), BF16)

    def scores(p, blk, bias):
        off = pl.multiple_of(blk * tq, tq)
        return _dot(k_ref[0, pl.ds(off, tq), lanes[p]], qs[p]) + bias

    def past_scores(p, blk):
        return scores(p, blk, bias_ref[p, pl.ds(blk, 1), :])

    def weights_times_v(e, p, blk):
        res = []
        for hh in range(2):
            v_ext = jnp.concatenate([vt_ref[0, blk, pl.ds(p * PAIR + hh * HEAD_DIM, HEAD_DIM), :], ones], axis=0)
            res.append(_dot(v_ext, e[:, hh * tq:(hh + 1) * tq]))
        res = jnp.concatenate(res, axis=1)
        return res[HEAD_DIM:HEAD_DIM + 1], res[:HEAD_DIM]

    def flash(carry, s_t, p, blk):
        m, l, acc = carry
        m_new = jnp.maximum(m, jnp.max(s_t, axis=0, keepdims=True))
        alpha = jnp.exp2(m - m_new)
        l_c, pv = weights_times_v(jnp.exp2(s_t - m_new).astype(BF16), p, blk)
        return m_new, alpha * l + l_c, alpha * acc + pv

    def write_out(results):
        outs = []
        for p in pairs:
            l_f, acc_f = results[p]
            o_t = acc_f * (1.0 / l_f)
            outs += [o_t[:, :tq], o_t[:, tq:]]
        o_ref[0] = (jnp.concatenate(outs, axis=0).T * z_ref[0]).astype(o_ref.dtype)

    s_own = [scores(p, i, causal) for p in pairs]
    jr = lax.broadcasted_iota(jnp.int32, (ncand, nr), 0)
    past = jr < i
    for p in pairs:
        sel = _topn_rows(jnp.where(past, gs[p], -jnp.inf), past, nblk, MOBA_TOPK)
        bias_ref[p] = jnp.where(sel, 0.0, NEG)
    own = []
    for p in pairs:
        m0 = jnp.max(s_own[p], axis=0, keepdims=True)
        l0, acc0 = weights_times_v(jnp.exp2(s_own[p] - m0).astype(BF16), p, i)
        own.append((m0, l0, acc0))

    def fast_blocks(blks, sums):
        sums = [list(x) for x in sums]
        pending = []

        def finish(p, blk, s_t):
            l_c, pv = weights_times_v(jnp.exp2(s_t - own[p][0]).astype(BF16), p, blk)
            sums[p][0] = sums[p][0] + l_c
            sums[p][1] = sums[p][1] + pv

        for blk in blks:
            for p in pairs:
                pending.append((p, blk, past_scores(p, blk)))
                if len(pending) > MXU_LAG:
                    finish(*pending.pop(0))
        while pending:
            finish(*pending.pop(0))
        return tuple(tuple(x) for x in sums)

    fast = lax.fori_loop(0, (i + 1) // 2, lambda u, s: fast_blocks((2 * u, 2 * u + 1), s),
                         tuple((l0, acc0) for (_, l0, acc0) in own))
    write_out(fast)
    worst = fast[0][0]
    for p in pairs[1:]:
        worst = jnp.maximum(worst, fast[p][0])
    safe = jnp.max(worst) < SAFE_SUM

    @pl.when(jnp.logical_not(safe))
    def _():
        def exact_block(blk, carries):
            return tuple(flash(carries[p], past_scores(p, blk), p, blk) for p in pairs)

        exact = lax.fori_loop(0, i, exact_block, tuple(own))
        write_out([(l, acc) for (_, l, acc) in exact])


def _moba(q, k, vt, kmean, z):
    nb, seq, width = q.shape
    tq = MOBA_BLOCK
    nblk = seq // MOBA_BLOCK
    ncand = -(-nblk // 16) * 16
    npair = MOBA_PAIRS_PER_STEP
    gw = npair * PAIR
    qblk = pl.BlockSpec((1, tq, gw), lambda b, p, i: (b, i, p))
    return pl.pallas_call(
        functools.partial(_moba_kernel, nblk, npair),
        grid=(nb, width // gw, seq // tq),
        in_specs=[
            qblk,
            pl.BlockSpec((1, seq, gw), lambda b, p, i: (b, 0, p)),
            pl.BlockSpec((1, nblk, gw, tq), lambda b, p, i: (b, 0, p, 0)),
            pl.BlockSpec((1, nblk, gw), lambda b, p, i: (b, 0, p)),
            qblk,
        ],
        out_specs=qblk,
        out_shape=jax.ShapeDtypeStruct((nb, seq, width), BF16),
        scratch_shapes=[pltpu.VMEM((npair, ncand, 2 * tq), F32)],
        compiler_params=_cparams(("arbitrary", "arbitrary", "arbitrary")),
        name="moba",
    )(q, k, vt, kmean, z)


def _even_plan(w_in, nb, seq):
    aq, akv = A_HEADS * HEAD_DIM, HEAD_DIM
    bq, bkv, bg = B_HEADS * HEAD_DIM, B_KV * HEAD_DIM, 3 * B_HEADS
    sizes = (aq, akv, akv, aq, bq, bkv, bkv, bkv, bkv, bkv, bkv, bg, bq)
    offs = np.concatenate([[0], np.cumsum(sizes)])
    qa, ka, va, za, qb, kc, vc, ks, vs, kw, vw, gb, zb = [w_in[:, offs[n]:offs[n + 1]] for n in range(len(sizes))]
    qw = aq + bq
    krw = PAIR * (1 + 2 * B_KV)
    gb4 = gb.reshape(-1, B_KV, B_REP, 3).transpose(0, 1, 3, 2).reshape(-1, B_KV, 3 * B_REP)
    gb4 = jnp.pad(gb4, ((0, 0), (0, 0), (0, GATE_ROWS - 3 * B_REP))).reshape(-1, B_KV * GATE_ROWS)
    vrows = akv + 2 * bkv
    t_width = -(-(vrows + B_KV * GATE_ROWS) // PAIR) * PAIR
    t_pad = jnp.zeros((w_in.shape[0], t_width - vrows - B_KV * GATE_ROWS), w_in.dtype)
    kheads = akv + 2 * bkv
    k_width = -(-kheads // PAIR) * PAIR
    k_pad = jnp.zeros((w_in.shape[0], k_width - kheads), w_in.dtype)
    w = jnp.concatenate([qa * (Q_SCALE * LOG2E), qb * (Q_SCALE * LOG2E), ka, ks, kw, k_pad, kc, vc, za, zb,
                         va, vs, vw, gb4, t_pad], axis=1).astype(BF16)
    groups, col = [], 0
    for width, kind in ((qw, "rope"), (k_width, "rope_dup"), (bkv, "rope"), (bkv, "plain"), (qw, "silu")):
        groups.append((col, width, kind, len(groups)))
        col += width
    groups_t = ((0, akv, "plain", 5, SWA_TQ), (akv, 2 * bkv, "plain", 6, NSA_TQ),
                (vrows, B_KV * GATE_ROWS, "sigmoid", 7, None))
    rows, tm = nb * seq, ROW_TILE
    per_b = seq // tm
    widths = [(qw, BF16), (krw, BF16), (bkv, F32), (bkv, F32), (qw, F32)]
    out_shape = [jax.ShapeDtypeStruct((rows, wd), dt) for wd, dt in widths]
    out_specs = [pl.BlockSpec((tm, wd), lambda i: (i, 0)) for wd, _ in widths]
    for n_rows, chunk in ((akv, SWA_TQ), (2 * bkv, NSA_TQ)):
        out_shape.append(jax.ShapeDtypeStruct((nb, seq // chunk, n_rows, chunk), BF16))
        out_specs.append(pl.BlockSpec((1, tm // chunk, n_rows, chunk), lambda i: (i // per_b, i % per_b, 0, 0)))
    out_shape.append(jax.ShapeDtypeStruct((nb, B_KV * GATE_ROWS, seq), F32))
    out_specs.append(pl.BlockSpec((1, B_KV * GATE_ROWS, tm), lambda i: (i // per_b, 0, i % per_b)))
    return w, (col, t_width), tuple(groups), groups_t, out_shape, out_specs


def _odd_plan(w_in, nb, seq):
    cw = C_HEADS * HEAD_DIM
    col_scale = np.ones((1, 4 * cw), np.float32)
    col_scale[:, :cw] = Q_SCALE * LOG2E
    w = (w_in * col_scale).astype(BF16)
    groups = ((0, cw, "rope", 0), (cw, cw, "rope_mean", 1), (3 * cw, cw, "silu", 2))
    groups_t = ((0, cw, "plain", 3, MOBA_BLOCK),)
    rows, tm = nb * seq, ROW_TILE
    per_b = seq // tm
    widths = [(cw, BF16), (cw, BF16), (cw, F32)]
    out_shape = [jax.ShapeDtypeStruct((rows, wd), dt) for wd, dt in widths]
    out_specs = [pl.BlockSpec((tm, wd), lambda i: (i, 0)) for wd, _ in widths]
    out_shape += [jax.ShapeDtypeStruct((nb, seq // MOBA_BLOCK, cw, MOBA_BLOCK), BF16),
                  jax.ShapeDtypeStruct((nb, seq // MOBA_BLOCK, 1, cw), F32)]
    out_specs += [pl.BlockSpec((1, tm // MOBA_BLOCK, cw, MOBA_BLOCK), lambda i: (i // per_b, i % per_b, 0, 0)),
                  pl.BlockSpec((1, tm // MOBA_BLOCK, 1, cw), lambda i: (i // per_b, i % per_b, 0, 0))]
    return w, (2 * cw, cw), groups, groups_t, out_shape, out_specs


def _rope_tables(seq):
    inv = ROPE_THETA ** (-jnp.arange(0, HEAD_DIM, 2, dtype=F32) / HEAD_DIM)
    ang = jnp.arange(seq, dtype=F32)[:, None] * inv[None, :]
    cos, sin = jnp.cos(ang), jnp.sin(ang)
    reps = PAIR // (HEAD_DIM // 2)
    sign = np.tile(np.concatenate([-np.ones(HEAD_DIM // 2), np.ones(HEAD_DIM // 2)]), PAIR // HEAD_DIM)
    return jnp.tile(cos, (1, reps)), jnp.tile(sin, (1, reps)) * jnp.asarray(sign, F32)[None, :]


def _overlap_matrix_t(seq, nch):
    nc = (seq - CMP_LEN) // CMP_STRIDE + 1
    nsb = seq // SLC_LEN
    cst = np.arange(nc) * CMP_STRIDE
    jj = np.arange(nsb)
    ov = ((cst[None, :] < (jj[:, None] + 1) * SLC_LEN) & (cst[None, :] + CMP_LEN > jj[:, None] * SLC_LEN))
    full = np.zeros((nsb, nch), np.float32)
    full[:, :nc] = ov
    return jnp.asarray(full, BF16), nc, nsb


def _cmp_weights(pe, w1, w2, transpose_out):
    pe2 = jnp.concatenate([pe, pe], axis=1)
    w1t = w1.reshape(CMP_LEN, HEAD_DIM, CMP_HIDDEN)
    zeros = jnp.zeros_like(w1t)
    w1p = jnp.concatenate([jnp.concatenate([w1t, zeros], axis=2), jnp.concatenate([zeros, w1t], axis=2)], axis=1)
    w2o = w2.T if transpose_out else jnp.concatenate([w2, w2], axis=1)
    return pe2, w1p.astype(BF16), w2o.astype(BF16)


def kernel(x, c, w_ada, b_ada, norm_g, w_in_even, a_sinks, cmp_pe_k, cmp_w1_k, cmp_w2_k, cmp_pe_v, cmp_w1_v, cmp_w2_v, w_out_even, w_in_odd, w_out_odd, final_g):
    nb, seq, d = x.shape
    depth = w_ada.shape[0]
    assert seq % ROW_TILE == 0 and seq % MOBA_BLOCK == 0 and NSA_TQ % SLC_LEN == 0
    assert (seq // SLC_LEN) % 8 == 0 and seq % (SWA_TQ * SWA_BLOCKS_PER_STEP) == 0 and ROW_TILE % NSA_TQ == 0
    cos_t, sin_t = _rope_tables(seq)
    ada = _ada(c, w_ada, b_ada)
    nch = seq // CMP_STRIDE
    ovl_t, nc, nsb = _overlap_matrix_t(seq, nch)
    fg = final_g.reshape(1, d)
    x2 = x.reshape(nb * seq, d)
    r3 = lambda a: a.reshape(nb, seq, a.shape[-1])
    prev = None
    for layer in range(depth):
        li = layer // 2
        ng = norm_g[layer].reshape(1, d)
        plan = _even_plan(w_in_even[li], nb, seq) if layer % 2 == 0 else _odd_plan(w_in_odd[li], nb, seq)
        w, t_cols, groups, groups_t, out_shape, out_specs = plan
        outs = _proj(x2, ada, layer, ng, cos_t, sin_t, w, t_cols, groups, groups_t, out_shape, out_specs, seq,
                     moba_mean=None if layer % 2 == 0 else 4, prev=prev)
        if prev is not None:
            x2, outs = outs[0], outs[1:]
        if layer % 2 == 0:
            q, kr, kc, vc, z, vt_a, vt_b, gates_t = outs
            q, kr, kc, vc, z = map(r3, (q, kr, kc, vc, z))
            kcmp, vcmp_t = _compress(kc, vc, *_cmp_weights(cmp_pe_k[li], cmp_w1_k[li], cmp_w2_k[li], False),
                                     *_cmp_weights(cmp_pe_v[li], cmp_w1_v[li], cmp_w2_v[li], True))
            oa = _swa(a_sinks[li], q, kr, vt_a, z, SWA_TQ)
            ob = _nsa(q, kcmp, vcmp_t, kr, vt_b, gates_t, z, ovl_t, nc, nsb, NSA_TQ)
            o_list = [oa.reshape(nb * seq, -1), ob.reshape(nb * seq, -1)]
            w_out = w_out_even[li].astype(BF16)
        else:
            q, k, z, vt, kmean = outs
            o = _moba(r3(q), r3(k), vt, kmean.reshape(nb, seq // MOBA_BLOCK, -1), r3(z))
            o_list = [o.reshape(nb * seq, -1)]
            w_out = w_out_odd[li].astype(BF16)
        prev = (o_list, w_out)
    x2 = _out_proj(x2, ada, depth - 1, prev[0], prev[1], fg, True, seq)
    return x2.reshape(nb, seq, d)
```

```python
import functools

import numpy as np
import jax
import jax.numpy as jnp
from jax import lax
from jax.experimental import pallas as pl
from jax.experimental.pallas import tpu as pltpu

D_MODEL = 1024
HEAD_DIM = 64
PAIR = 2 * HEAD_DIM
ROPE_THETA = 10000.0
RMS_EPS = 1e-6
A_HEADS = 8
A_WINDOW = 128
B_HEADS = 8
B_KV = 2
B_REP = B_HEADS // B_KV
CMP_LEN = 32
CMP_STRIDE = 16
CMP_HIDDEN = 256
SLC_LEN = 64
SLC_TOPN = 8
WIN_LEN = 512
FORCE_BONUS = 1e4
C_HEADS = 16
MOBA_BLOCK = 256
MOBA_TOPK = 3
Q_SCALE = HEAD_DIM ** -0.5
LOG2E = 1.4426950408889634

NEG = -1e30
SAFE_SUM = 2.0 ** 64
MXU_COLS = 256
ROW_TILE = 512
SWA_TQ = 128
NSA_TQ = 256
SWA_BLOCKS_PER_STEP = 4
MOBA_PAIRS_PER_STEP = 8
MXU_LAG = 3
SUM_ROWS = 16
GATE_ROWS = 16
VMEM_LIMIT = 56 * 1024 * 1024

BF16 = jnp.bfloat16
F32 = jnp.float32


def _dot_nt(a, b):
    return lax.dot_general(a, b, (((1,), (1,)), ((), ())), preferred_element_type=F32)


def _dot(a, b):
    return jnp.dot(a, b, preferred_element_type=F32)


def _cparams(sem):
    return pltpu.CompilerParams(dimension_semantics=sem, vmem_limit_bytes=VMEM_LIMIT)


def _tile_lanes(a, n):
    return jnp.concatenate([a] * n, axis=1) if n > 1 else a


def _ada_kernel(c_ref, w_ref, b_ref, o_ref):
    c = c_ref[...]
    ca = (c * jax.nn.sigmoid(c)).astype(BF16)
    d = c.shape[-1]
    for j in range(o_ref.shape[1]):
        o_ref[0, j] = _dot(ca, w_ref[0, :, j * d:(j + 1) * d].astype(BF16)) + b_ref[0, j]


def _ada(c, w_ada, b_ada):
    depth, d, _ = w_ada.shape
    nb = c.shape[0]
    b4 = b_ada.reshape(depth, 3, 1, d)
    out = pl.pallas_call(
        _ada_kernel,
        grid=(depth,),
        in_specs=[
            pl.BlockSpec((nb, d), lambda l: (0, 0)),
            pl.BlockSpec((1, d, 3 * d), lambda l: (l, 0, 0)),
            pl.BlockSpec((1, 3, 1, d), lambda l: (l, 0, 0, 0)),
        ],
        out_specs=pl.BlockSpec((1, 3, nb, d), lambda l: (l, 0, 0, 0)),
        out_shape=jax.ShapeDtypeStruct((depth, 3, nb, d), F32),
        compiler_params=_cparams(("arbitrary",)),
        name="ada",
    )(c, w_ada, b4)
    return out.reshape(depth, 3, nb, 1, d)


def _rope(a, cos, sin_signed):
    w = a.shape[-1]
    lane = lax.broadcasted_iota(jnp.int32, a.shape, 1)
    first_half = (lane % HEAD_DIM) < (HEAD_DIM // 2)
    partner = jnp.where(first_half, pltpu.roll(a, w - HEAD_DIM // 2, 1), pltpu.roll(a, HEAD_DIM // 2, 1))
    return a * cos + partner * sin_signed


def _proj_kernel(groups, groups_t, t_cols, moba_mean, n_prev_o, *refs):
    x_ref = refs[0]
    prev_refs, refs = refs[1:1 + (n_prev_o + 2 if n_prev_o else 0)], refs[1 + (n_prev_o + 2 if n_prev_o else 0):]
    shift_ref, scale_ref, g_ref, cos_ref, sin_ref, w_ref = refs[:6]
    out_refs = refs[6:]
    t_start, t_width = t_cols

    x = x_ref[...]
    if n_prev_o:
        gate_ref, wout_ref = prev_refs[0], prev_refs[-1]
        mix, k0 = None, 0
        for o_ref in prev_refs[1:-1]:
            kw = o_ref.shape[-1]
            part = _dot(o_ref[...], wout_ref[k0:k0 + kw, :])
            mix = part if mix is None else mix + part
            k0 += kw
        x = x + gate_ref[0, 0, 0] * mix
        out_refs[0][...] = x
        out_refs = out_refs[1:]
    y = x * lax.rsqrt(jnp.mean(x * x, axis=-1, keepdims=True) + RMS_EPS)
    h = (y * g_ref[...]) * (1.0 + scale_ref[0, 0, 0]) + shift_ref[0, 0, 0]
    hb = h.astype(BF16)
    tm = hb.shape[0]
    cos, sin = cos_ref[...], sin_ref[...]
    lo_half = lax.broadcasted_iota(jnp.int32, (tm, PAIR), 1) < HEAD_DIM
    pair_plan = {}
    for (w_start, width, kind, out_idx) in groups:
        for c in range(0, width, PAIR):
            pair_plan[(w_start + c) // PAIR] = (kind, out_idx, c)
    n_cols = w_ref.shape[1]
    for s0 in range(0, n_cols, MXU_COLS):
        sw = min(MXU_COLS, n_cols - s0)
        acc = _dot(hb, w_ref[:, s0:s0 + sw])
        for h0 in range(0, sw, PAIR):
            col = s0 + h0
            a = acc[:, h0:h0 + PAIR]
            if t_start <= col < t_start + t_width:
                f0 = col - t_start
                a_t = a.T
                for (r_start, n_rows, kind, out_idx, chunk) in groups_t:
                    lo, hi = max(r_start, f0), min(r_start + n_rows, f0 + PAIR)
                    if lo >= hi:
                        continue
                    o_ref = out_refs[out_idx]
                    part = a_t[lo - f0:hi - f0]
                    if kind == "sigmoid":
                        o_ref[0, lo - r_start:hi - r_start, :] = jax.nn.sigmoid(part)
                    else:
                        for jj in range(tm // chunk):
                            o_ref[0, jj, lo - r_start:hi - r_start, :] = (
                                part[:, jj * chunk:(jj + 1) * chunk].astype(o_ref.dtype))
                continue
            kind, out_idx, c = pair_plan[col // PAIR]
            o_ref = out_refs[out_idx]
            if kind in ("rope", "rope_mean", "rope_dup"):
                a = _rope(a, cos, sin)
            elif kind == "silu":
                a = a * jax.nn.sigmoid(a)
            if kind == "rope_dup":
                b = pltpu.roll(a, HEAD_DIM, 1)
                for hh, dup in enumerate((jnp.where(lo_half, a, b), jnp.where(lo_half, b, a))):
                    dst = (2 * (c // PAIR) + hh) * PAIR
                    if dst < o_ref.shape[1]:
                        o_ref[:, dst:dst + PAIR] = dup.astype(o_ref.dtype)
                continue
            o_ref[:, c:c + PAIR] = a.astype(o_ref.dtype)
            if kind == "rope_mean":
                km_ref = out_refs[moba_mean]
                for j in range(tm // MOBA_BLOCK):
                    blk = a[j * MOBA_BLOCK:(j + 1) * MOBA_BLOCK]
                    km_ref[0, j, :, c:c + PAIR] = jnp.sum(blk, axis=0, keepdims=True) * (1.0 / MOBA_BLOCK)


def _proj(x2, ada, layer, norm_g, cos_t, sin_t, w, t_cols, groups, groups_t, out_shape, out_specs, seq,
          moba_mean=None, prev=None):
    rows, d = x2.shape
    tm = ROW_TILE
    per_b = seq // tm
    args, in_specs = [x2], [pl.BlockSpec((tm, d), lambda i: (i, 0))]
    n_prev_o = 0
    if prev is not None:
        o_list, w_out = prev
        n_prev_o = len(o_list)
        args += [ada, *o_list, w_out]
        in_specs.append(pl.BlockSpec((1, 1, 1, 1, d), lambda i: (layer - 1, 2, i // per_b, 0, 0)))
        in_specs += [pl.BlockSpec((tm, o.shape[-1]), lambda i: (i, 0)) for o in o_list]
        in_specs.append(pl.BlockSpec(w_out.shape, lambda i: (0, 0)))
        out_shape = [jax.ShapeDtypeStruct((rows, d), F32)] + list(out_shape)
        out_specs = [pl.BlockSpec((tm, d), lambda i: (i, 0))] + list(out_specs)
    args += [ada, ada, norm_g, cos_t, sin_t, w]
    in_specs += [
        pl.BlockSpec((1, 1, 1, 1, d), lambda i: (layer, 0, i // per_b, 0, 0)),
        pl.BlockSpec((1, 1, 1, 1, d), lambda i: (layer, 1, i // per_b, 0, 0)),
        pl.BlockSpec((1, d), lambda i: (0, 0)),
        pl.BlockSpec((tm, PAIR), lambda i: (i % per_b, 0)),
        pl.BlockSpec((tm, PAIR), lambda i: (i % per_b, 0)),
        pl.BlockSpec(w.shape, lambda i: (0, 0)),
    ]
    return pl.pallas_call(
        functools.partial(_proj_kernel, groups, groups_t, t_cols, moba_mean, n_prev_o),
        grid=(rows // tm,),
        in_specs=in_specs,
        out_specs=out_specs,
        out_shape=out_shape,
        compiler_params=_cparams(("arbitrary",)),
        name="proj",
    )(*args)


def _out_kernel(n_o, final, *refs):
    x_ref, gate_ref = refs[0], refs[1]
    o_refs = refs[2:2 + n_o]
    w_ref = refs[2 + n_o]
    fg_ref = refs[3 + n_o]
    out_ref = refs[4 + n_o]
    y = None
    k0 = 0
    for o_ref in o_refs:
        kw = o_ref.shape[-1]
        part = _dot(o_ref[...], w_ref[k0:k0 + kw, :])
        y = part if y is None else y + part
        k0 += kw
    xn = x_ref[...] + gate_ref[0, 0, 0] * y
    if final:
        xn = (xn * lax.rsqrt(jnp.mean(xn * xn, axis=-1, keepdims=True) + RMS_EPS)) * fg_ref[...]
    out_ref[...] = xn


def _out_proj(x2, ada, layer, o_list, w_out, final_g, final, seq):
    rows, d = x2.shape
    tm = ROW_TILE
    per_b = seq // tm
    n_o = len(o_list)
    in_specs = [
        pl.BlockSpec((tm, d), lambda i: (i, 0)),
        pl.BlockSpec((1, 1, 1, 1, d), lambda i: (layer, 2, i // per_b, 0, 0)),
    ]
    in_specs += [pl.BlockSpec((tm, o.shape[-1]), lambda i: (i, 0)) for o in o_list]
    in_specs += [pl.BlockSpec(w_out.shape, lambda i: (0, 0)), pl.BlockSpec((1, d), lambda i: (0, 0))]
    return pl.pallas_call(
        functools.partial(_out_kernel, n_o, final),
        grid=(rows // tm,),
        in_specs=in_specs,
        out_specs=pl.BlockSpec((tm, d), lambda i: (i, 0)),
        out_shape=jax.ShapeDtypeStruct((rows, d), F32),
        compiler_params=_cparams(("arbitrary",)),
        name="out_proj",
    )(x2, ada, *o_list, w_out, final_g)


def _stack_heads(q_pairs):
    row = lax.broadcasted_iota(jnp.int32, (PAIR, 1), 0)
    m_lo = (row < HEAD_DIM).astype(F32)
    m_hi = (row >= HEAD_DIM).astype(F32)
    blocks = []
    for p in range(q_pairs.shape[-1] // PAIR):
        qp_t = q_pairs[:, p * PAIR:(p + 1) * PAIR].astype(F32).T
        blocks += [qp_t * m_lo, qp_t * m_hi]
    return jnp.concatenate(blocks, axis=1).astype(BF16)


def _flash_init(d_rows, n_cols):
    return jnp.full((1, n_cols), NEG, F32), jnp.zeros((1, n_cols), F32), jnp.zeros((d_rows, n_cols), F32)


def _flash(carry, s_t, v_t_chunks):
    m, l, acc = carry
    m_new = jnp.maximum(m, jnp.max(s_t, axis=0, keepdims=True))
    alpha = jnp.exp2(m - m_new)
    e = jnp.exp2(s_t - m_new).astype(BF16)
    l = alpha * l + _dot(jnp.ones((SUM_ROWS, e.shape[0]), BF16), e)[0:1]
    acc, k0 = alpha * acc, 0
    for vt in v_t_chunks:
        acc = acc + _dot(vt, e[k0:k0 + vt.shape[1]])
        k0 += vt.shape[1]
    return m_new, l, acc


def _topn_rows(score, valid, n_cand, topn):
    ridx = lax.broadcasted_iota(jnp.int32, score.shape, 0)
    rank = jnp.zeros(score.shape, F32)
    for jp in range(n_cand):
        row = score[jp:jp + 1, :]
        earlier = jnp.where(ridx > jp, 1.0, 0.0)
        rank = rank + jnp.where(row > score, 1.0, jnp.where(row == score, earlier, 0.0))
    return valid & (rank < topn)


def _swa_kernel(tq, nqb, sink_ref, q_ref, k_ref, vt_ref, z_ref, o_ref):
    step = pl.program_id(1)
    n_prev = -(-(A_WINDOW - 1) // tq)
    krow = lax.broadcasted_iota(jnp.int32, (tq, tq), 0)
    qcol = lax.broadcasted_iota(jnp.int32, (tq, tq), 1)
    causal = _tile_lanes(jnp.where(krow <= qcol, 0.0, NEG), A_HEADS)
    edge = _tile_lanes(jnp.where(n_prev * tq + qcol - krow < A_WINDOW, 0.0, NEG), A_HEADS)
    ones = jnp.ones((SUM_ROWS, tq), BF16)
    sink = jnp.concatenate([jnp.full((1, tq), sink_ref[hd] * LOG2E, F32) for hd in range(A_HEADS)], axis=1)

    blocks = [step * nqb + bb for bb in range(nqb)]
    qs = [_stack_heads(q_ref[0, bb * tq:(bb + 1) * tq, :]) for bb in range(nqb)]
    scores = []
    for bb, i in enumerate(blocks):
        row = []
        for j in range(n_prev + 1):
            off = pl.multiple_of(jnp.maximum(i - j, 0) * tq, tq)
            s_t = _dot(k_ref[0, pl.ds(off, tq), :], qs[bb])
            s_t = s_t + causal if j == 0 else (s_t + edge if j == n_prev else s_t)
            row.append(s_t if j == 0 else s_t + jnp.where(i - j >= 0, 0.0, NEG))
        scores.append(row)

    outs = []
    for bb, i in enumerate(blocks):
        m = sink
        for s_t in scores[bb]:
            m = jnp.maximum(m, jnp.max(s_t, axis=0, keepdims=True))
        den, acc = jnp.exp2(sink - m), None
        for j, s_t in enumerate(scores[bb]):
            v_ext = jnp.concatenate([vt_ref[0, jnp.maximum(i - j, 0)], ones], axis=0)
            res = _dot(v_ext, jnp.exp2(s_t - m).astype(BF16))
            den = den + res[HEAD_DIM:HEAD_DIM + 1]
            acc = res[:HEAD_DIM] if acc is None else acc + res[:HEAD_DIM]
        o_t = acc * (1.0 / den)
        heads = jnp.concatenate([o_t[:, hd * tq:(hd + 1) * tq] for hd in range(A_HEADS)], axis=0)
        outs.append(heads.T)
    o_ref[0] = (jnp.concatenate(outs, axis=0) * z_ref[0]).astype(o_ref.dtype)


def _swa(sinks, q, kr, vt, z, tq):
    nb, seq, _ = q.shape
    aw = A_HEADS * HEAD_DIM
    nqb = SWA_BLOCKS_PER_STEP
    rows = nqb * tq
    return pl.pallas_call(
        functools.partial(_swa_kernel, tq, nqb),
        grid=(nb, seq // rows),
        in_specs=[
            pl.BlockSpec(memory_space=pltpu.SMEM),
            pl.BlockSpec((1, rows, aw), lambda b, i: (b, i, 0)),
            pl.BlockSpec((1, seq, PAIR), lambda b, i: (b, 0, 0)),
            pl.BlockSpec((1, seq // tq, HEAD_DIM, tq), lambda b, i: (b, 0, 0, 0)),
            pl.BlockSpec((1, rows, aw), lambda b, i: (b, i, 0)),
        ],
        out_specs=pl.BlockSpec((1, rows, aw), lambda b, i: (b, i, 0)),
        out_shape=jax.ShapeDtypeStruct((nb, seq, aw), BF16),
        compiler_params=_cparams(("arbitrary", "arbitrary")),
        name="swa",
    )(sinks, q, kr, vt, z)


def _cmp_hidden(x_ref, pe_ref, w1_ref, nch):
    top, bot = None, None
    for l in range(CMP_STRIDE):
        rows = x_ref[0, pl.ds(l, nch, stride=CMP_STRIDE), :]
        t = _dot((rows + pe_ref[l:l + 1, :]).astype(BF16), w1_ref[l])
        b = _dot((rows + pe_ref[CMP_STRIDE + l:CMP_STRIDE + l + 1, :]).astype(BF16), w1_ref[CMP_STRIDE + l])
        top = t if top is None else top + t
        bot = b if bot is None else bot + b
    return jax.nn.gelu(top + pltpu.roll(bot, nch - 1, 0))


def _compress_kernel(kc_ref, vc_ref, pek_ref, w1k_ref, w2k_ref, pev_ref, w1v_ref, w2v_ref, ko_ref, vo_ref):
    nch = ko_ref.shape[2]
    hk = _cmp_hidden(kc_ref, pek_ref, w1k_ref, nch).astype(BF16)
    hv = _cmp_hidden(vc_ref, pev_ref, w1v_ref, nch).astype(BF16)
    for g in range(B_KV):
        gs = slice(g * CMP_HIDDEN, (g + 1) * CMP_HIDDEN)
        ko_ref[0, g] = _dot(hk[:, gs], w2k_ref[...]).astype(ko_ref.dtype)
        vo_ref[0, g] = _dot_nt(w2v_ref[...], hv[:, gs]).astype(vo_ref.dtype)


def _compress(kc, vc, pek, w1k, w2k, pev, w1v, w2v):
    nb, seq, _ = kc.shape
    nch = seq // CMP_STRIDE
    full = lambda a: pl.BlockSpec(a.shape, lambda b: (0,) * a.ndim)
    blk = pl.BlockSpec((1, seq, PAIR), lambda b: (b, 0, 0))
    return pl.pallas_call(
        _compress_kernel,
        grid=(nb,),
        in_specs=[blk, blk, full(pek), full(w1k), full(w2k), full(pev), full(w1v), full(w2v)],
        out_specs=[pl.BlockSpec((1, B_KV, nch, PAIR), lambda b: (b, 0, 0, 0)),
                   pl.BlockSpec((1, B_KV, HEAD_DIM, nch), lambda b: (b, 0, 0, 0))],
        out_shape=[jax.ShapeDtypeStruct((nb, B_KV, nch, PAIR), BF16),
                   jax.ShapeDtypeStruct((nb, B_KV, HEAD_DIM, nch), BF16)],
        compiler_params=_cparams(("arbitrary",)),
        name="compress",
    )(kc, vc, pek, w1k, w2k, pev, w1v, w2v)


def _nsa_kernel(tq, nc, nsb, q_ref, kc_ref, vct_ref, ks0_ref, ks1_ref, vs0_ref, vs1_ref, kw0_ref, kw1_ref,
                vw0_ref, vw1_ref, gt_ref, z_ref, ovl_ref, o_ref, bias_ref):
    i = pl.program_id(1)
    groups = range(B_KV)
    ks_refs, vs_refs = (ks0_ref, ks1_ref), (vs0_ref, vs1_ref)
    kw_refs, vw_refs = (kw0_ref, kw1_ref), (vw0_ref, vw1_ref)
    gw = B_REP * HEAD_DIM
    nr = B_REP * tq
    qs = [_stack_heads(q_ref[0, :, g * gw:(g + 1) * gw]) for g in groups]
    t_row = i * tq + lax.broadcasted_iota(jnp.int32, (1, tq), 1)

    nch = kc_ref.shape[2]
    cidx = lax.broadcasted_iota(jnp.int32, (nch, tq), 0)
    ok_c = (cidx < nc) & (t_row >= cidx * CMP_STRIDE + (CMP_LEN - 1))
    bias_c = _tile_lanes(jnp.where(ok_c, 0.0, NEG), B_REP)
    jr = lax.broadcasted_iota(jnp.int32, (nsb, tq), 0)
    tb = t_row // SLC_LEN
    valid = jr <= tb
    forced = (jr == 0) | (jr == tb) | (jr == tb - 1)
    n_prev = -(-(WIN_LEN - 1) // tq)
    blk_per_chunk = tq // SLC_LEN
    krow = lax.broadcasted_iota(jnp.int32, (tq, tq), 0)
    qcol = lax.broadcasted_iota(jnp.int32, (tq, tq), 1)
    causal = jnp.where(krow <= qcol, 0.0, NEG)
    edge = _tile_lanes(jnp.where(n_prev * tq + qcol - krow < WIN_LEN, 0.0, NEG), B_REP)
    causal_r = _tile_lanes(causal, B_REP)
    ones = jnp.ones((SUM_ROWS, tq), BF16)

    def win_chunk(j):
        return jnp.maximum(i - j, 0)

    def win_scores(g, j):
        off = pl.multiple_of(win_chunk(j) * tq, tq)
        s_t = _dot(kw_refs[g][0, pl.ds(off, tq), :], qs[g])
        return s_t + causal_r if j == 0 else (s_t + edge if j == n_prev else s_t)

    def slc_bias(g, c, diagonal):
        rows = [jnp.broadcast_to(bias_ref[g, pl.ds(c * blk_per_chunk + bb, 1), :], (SLC_LEN, tq))
                for bb in range(blk_per_chunk)]
        b = jnp.concatenate(rows, axis=0)
        return _tile_lanes(b + causal if diagonal else b, B_REP)

    def slc_qk(g, c):
        off = pl.multiple_of(c * tq, tq)
        return _dot(ks_refs[g][0, pl.ds(off, tq), :], qs[g])

    def sums_and_pv(v_t, e):
        res = _dot(jnp.concatenate([v_t, ones], axis=0), e)
        return res[HEAD_DIM:HEAD_DIM + 1], res[:HEAD_DIM]

    def masked_ref(m0, present):
        return m0 + jnp.where(present, 0.0, -NEG)

    s_cmp = [_dot(kc_ref[0, g], qs[g]) + bias_c for g in groups]
    s_win0 = [win_scores(g, 0) for g in groups]
    s_diag = [slc_qk(g, i) for g in groups]

    o_cmp = []
    for g in groups:
        s = s_cmp[g]
        m = jnp.max(s, axis=0, keepdims=True)
        m = jnp.where(m > 0.5 * NEG, m, 0.0)
        e = jnp.exp2(s - m)
        den = jnp.sum(e, axis=0, keepdims=True)
        p_t = (e * (1.0 / jnp.where(den > 0, den, 1.0))).astype(BF16)
        res = _dot(jnp.concatenate([vct_ref[0, g], ovl_ref[...]], axis=0), p_t)
        o_cmp.append(res[:HEAD_DIM])
        imp = res[HEAD_DIM:, 0:tq]
        for r in range(1, B_REP):
            imp = imp + res[HEAD_DIM:, r * tq:(r + 1) * tq]
        score = jnp.where(valid, jnp.where(forced, FORCE_BONUS, imp), -jnp.inf)
        bias_ref[g] = jnp.where(_topn_rows(score, valid, nsb, SLC_TOPN), 0.0, NEG)

    def write_out(slc, win):
        heads = []
        for g in groups:
            o_slc = slc[g][1] * (1.0 / slc[g][0])
            o_win = win[g][1] * (1.0 / win[g][0])
            gt = gt_ref[0, g * GATE_ROWS:(g + 1) * GATE_ROWS]
            for r in range(B_REP):
                cs = slice(r * tq, (r + 1) * tq)
                heads.append(gt[r:r + 1] * o_cmp[g][:, cs] + gt[B_REP + r:B_REP + r + 1] * o_slc[:, cs]
                             + gt[2 * B_REP + r:2 * B_REP + r + 1] * o_win[:, cs])
        o_t = jnp.concatenate(heads, axis=0)
        o_ref[0] = (o_t.T * z_ref[0]).astype(o_ref.dtype)

    win_m0 = [s_win0[g][0:1, :] for g in groups]
    win = [list(sums_and_pv(vw_refs[g][0, i], jnp.exp2(s_win0[g] - win_m0[g]).astype(BF16))) for g in groups]
    pending = []

    def finish_win(g, j, arg):
        l_c, pv = sums_and_pv(vw_refs[g][0, win_chunk(j)], jnp.exp2(arg).astype(BF16))
        win[g][0], win[g][1] = win[g][0] + l_c, win[g][1] + pv

    for j in range(1, n_prev + 1):
        for g in groups:
            pending.append((g, j, win_scores(g, j) - masked_ref(win_m0[g], i - j >= 0)))
            if len(pending) > MXU_LAG:
                finish_win(*pending.pop(0))
    while pending:
        finish_win(*pending.pop(0))

    s_diag = [s_diag[g] + slc_bias(g, i, True) for g in groups]
    slc_m0 = [jnp.max(s_diag[g], axis=0, keepdims=True) for g in groups]
    slc0 = tuple(sums_and_pv(vs_refs[g][0, i], jnp.exp2(s_diag[g] - slc_m0[g]).astype(BF16)) for g in groups)

    def fast_pair(u, sums):
        sums = [list(x) for x in sums]
        waiting = []

        def finish(g, c, arg):
            l_c, pv = sums_and_pv(vs_refs[g][0, c], jnp.exp2(arg).astype(BF16))
            sums[g][0], sums[g][1] = sums[g][0] + l_c, sums[g][1] + pv

        for c in (2 * u, 2 * u + 1):
            for g in groups:
                waiting.append((g, c, slc_qk(g, c) + slc_bias(g, c, False) - masked_ref(slc_m0[g], c < i)))
                if len(waiting) > MXU_LAG:
                    finish(*waiting.pop(0))
        while waiting:
            finish(*waiting.pop(0))
        return tuple(tuple(x) for x in sums)

    slc = lax.fori_loop(0, (i + 1) // 2, fast_pair, slc0)
    write_out(slc, win)
    worst = jnp.maximum(jnp.maximum(slc[0][0], slc[1][0]), jnp.maximum(win[0][0], win[1][0]))
    safe = jnp.max(worst) < SAFE_SUM

    @pl.when(jnp.logical_not(safe))
    def _():
        span = (n_prev + 1) * tq
        c0 = jnp.maximum(i - n_prev, 0)
        start = pl.multiple_of(c0 * tq, tq)
        diff = t_row - (start + lax.broadcasted_iota(jnp.int32, (span, tq), 0))
        bias_w = _tile_lanes(jnp.where(diff >= 0, jnp.where(diff < WIN_LEN, 0.0, NEG), NEG), B_REP)
        win_x = []
        for g in groups:
            sw = _dot(kw_refs[g][0, pl.ds(start, span), :], qs[g]) + bias_w
            _, l_w, acc_w = _flash(_flash_init(HEAD_DIM, nr), sw,
                                   [vw_refs[g][0, c0 + cc] for cc in range(n_prev + 1)])
            win_x.append((l_w, acc_w))

        def exact_pair(u, carries):
            out = []
            for g in groups:
                carry = carries[g]
                for c in (2 * u, 2 * u + 1):
                    s_t = slc_qk(g, c) + slc_bias(g, c, False) + jnp.where(c < i, 0.0, NEG)
                    carry = _flash(carry, s_t, [vs_refs[g][0, c]])
                out.append(carry)
            return tuple(out)

        first = tuple(_flash(_flash_init(HEAD_DIM, nr), s_diag[g], [vs_refs[g][0, i]]) for g in groups)
        exact = lax.fori_loop(0, (i + 1) // 2, exact_pair, first)
        write_out([(l, acc) for (_, l, acc) in exact], win_x)


def _nsa(q, kcmp, vcmp_t, kr, vt, gates_t, z, ovl_t, nc, nsb, tq):
    nb, seq, _ = q.shape
    bw = B_HEADS * HEAD_DIM
    nch = kcmp.shape[2]
    kblk = lambda col: pl.BlockSpec((1, seq, PAIR), lambda b, i: (b, 0, col))
    vblk = lambda row: pl.BlockSpec((1, seq // tq, HEAD_DIM, tq), lambda b, i: (b, 0, row, 0))
    return pl.pallas_call(
        functools.partial(_nsa_kernel, tq, nc, nsb),
        grid=(nb, seq // tq),
        in_specs=[
            pl.BlockSpec((1, tq, bw), lambda b, i: (b, i, 1)),
            pl.BlockSpec((1, B_KV, nch, PAIR), lambda b, i: (b, 0, 0, 0)),
            pl.BlockSpec((1, B_KV, HEAD_DIM, nch), lambda b, i: (b, 0, 0, 0)),
            kblk(1), kblk(2), vblk(0), vblk(1), kblk(3), kblk(4), vblk(2), vblk(3),
            pl.BlockSpec((1, B_KV * GATE_ROWS, tq), lambda b, i: (b, 0, i)),
            pl.BlockSpec((1, tq, bw), lambda b, i: (b, i, 1)),
            pl.BlockSpec(ovl_t.shape, lambda b, i: (0, 0)),
        ],
        out_specs=pl.BlockSpec((1, tq, bw), lambda b, i: (b, i, 0)),
        out_shape=jax.ShapeDtypeStruct((nb, seq, bw), BF16),
        scratch_shapes=[pltpu.VMEM((B_KV, nsb, tq), F32)],
        compiler_params=_cparams(("arbitrary", "arbitrary")),
        name="nsa",
    )(q, kcmp, vcmp_t, kr, kr, vt, vt, kr, kr, vt, vt, gates_t, z, ovl_t)


def _moba_kernel(nblk, npair, q_ref, k_ref, vt_ref, km_ref, z_ref, o_ref, bias_ref):
    i = pl.program_id(2)
    tq = MOBA_BLOCK
    nr = 2 * tq
    ncand = bias_ref.shape[1]
    pairs = range(npair)
    lanes = [slice(p * PAIR, (p + 1) * PAIR) for p in pairs]
    qs = [_stack_heads(q_ref[0, :, lanes[p]]) for p in pairs]

    gs = []
    for p in pairs:
        km = km_ref[0, :, lanes[p]]
        km = jnp.concatenate([km, jnp.zeros((ncand - nblk, PAIR), F32)], axis=0).astype(BF16)
        gs.append(_dot(km, qs[p]))

    krow = lax.broadcasted_iota(jnp.int32, (tq, tq), 0)
    qcol = lax.broadcasted_iota(jnp.int32, (tq, tq), 1)
    causal = _tile_lanes(jnp.where(krow <= qcol, 0.0, NEG), 2)
    ones = jnp.ones((SUM_ROWS, tq), BF16)

    def scores(p, blk, bias):
        off = pl.multiple_of(blk * tq, tq)
        return _dot(k_ref[0, pl.ds(off, tq), lanes[p]], qs[p]) + bias

    def past_scores(p, blk):
        return scores(p, blk, bias_ref[p, pl.ds(blk, 1), :])

    def weights_times_v(e, p, blk):
        res = []
        for hh in range(2):
            v_ext = jnp.concatenate([vt_ref[0, blk, pl.ds(p * PAIR + hh * HEAD_DIM, HEAD_DIM), :], ones], axis=0)
            res.append(_dot(v_ext, e[:, hh * tq:(hh + 1) * tq]))
        res = jnp.concatenate(res, axis=1)
        return res[HEAD_DIM:HEAD_DIM + 1], res[:HEAD_DIM]

    def flash(carry, s_t, p, blk):
        m, l, acc = carry
        m_new = jnp.maximum(m, jnp.max(s_t, axis=0, keepdims=True))
        alpha = jnp.exp2(m - m_new)
        l_c, pv = weights_times_v(jnp.exp2(s_t - m_new).astype(BF16), p, blk)
        return m_new, alpha * l + l_c, alpha * acc + pv

    def write_out(results):
        outs = []
        for p in pairs:
            l_f, acc_f = results[p]
            o_t = acc_f * (1.0 / l_f)
            outs += [o_t[:, :tq], o_t[:, tq:]]
        o_ref[0] = (jnp.concatenate(outs, axis=0).T * z_ref[0]).astype(o_ref.dtype)

    s_own = [scores(p, i, causal) for p in pairs]
    jr = lax.broadcasted_iota(jnp.int32, (ncand, nr), 0)
    past = jr < i
    for p in pairs:
        sel = _topn_rows(jnp.where(past, gs[p], -jnp.inf), past, nblk, MOBA_TOPK)
        bias_ref[p] = jnp.where(sel, 0.0, NEG)
    own = []
    for p in pairs:
        m0 = s_own[p][0:1, :]
        l0, acc0 = weights_times_v(jnp.exp2(s_own[p] - m0).astype(BF16), p, i)
        own.append((m0, l0, acc0))

    def fast_blocks(blks, sums):
        sums = [list(x) for x in sums]
        pending = []

        def finish(p, blk, s_t):
            l_c, pv = weights_times_v(jnp.exp2(s_t - own[p][0]).astype(BF16), p, blk)
            sums[p][0] = sums[p][0] + l_c
            sums[p][1] = sums[p][1] + pv

        for blk in blks:
            for p in pairs:
                pending.append((p, blk, past_scores(p, blk)))
                if len(pending) > MXU_LAG:
                    finish(*pending.pop(0))
        while pending:
            finish(*pending.pop(0))
        return tuple(tuple(x) for x in sums)

    fast = lax.fori_loop(0, (i + 1) // 2, lambda u, s: fast_blocks((2 * u, 2 * u + 1), s),
                         tuple((l0, acc0) for (_, l0, acc0) in own))
    write_out(fast)
    worst = fast[0][0]
    for p in pairs[1:]:
        worst = jnp.maximum(worst, fast[p][0])
    safe = jnp.max(worst) < SAFE_SUM

    @pl.when(jnp.logical_not(safe))
    def _():
        def exact_block(blk, carries):
            return tuple(flash(carries[p], past_scores(p, blk), p, blk) for p in pairs)

        start = (jnp.full((1, nr), NEG, F32), jnp.zeros((1, nr), F32), jnp.zeros((HEAD_DIM, nr), F32))
        first = tuple(flash(start, scores(p, i, causal), p, i) for p in pairs)
        exact = lax.fori_loop(0, i, exact_block, first)
        write_out([(l, acc) for (_, l, acc) in exact])


def _moba(q, k, vt, kmean, z):
    nb, seq, width = q.shape
    tq = MOBA_BLOCK
    nblk = seq // MOBA_BLOCK
    ncand = -(-nblk // 16) * 16
    npair = MOBA_PAIRS_PER_STEP
    gw = npair * PAIR
    qblk = pl.BlockSpec((1, tq, gw), lambda b, p, i: (b, i, p))
    return pl.pallas_call(
        functools.partial(_moba_kernel, nblk, npair),
        grid=(nb, width // gw, seq // tq),
        in_specs=[
            qblk,
            pl.BlockSpec((1, seq, gw), lambda b, p, i: (b, 0, p)),
            pl.BlockSpec((1, nblk, gw, tq), lambda b, p, i: (b, 0, p, 0)),
            pl.BlockSpec((1, nblk, gw), lambda b, p, i: (b, 0, p)),
            qblk,
        ],
        out_specs=qblk,
        out_shape=jax.ShapeDtypeStruct((nb, seq, width), BF16),
        scratch_shapes=[pltpu.VMEM((npair, ncand, 2 * tq), F32)],
        compiler_params=_cparams(("arbitrary", "arbitrary", "arbitrary")),
        name="moba",
    )(q, k, vt, kmean, z)


def _even_plan(w_in, nb, seq):
    aq, akv = A_HEADS * HEAD_DIM, HEAD_DIM
    bq, bkv, bg = B_HEADS * HEAD_DIM, B_KV * HEAD_DIM, 3 * B_HEADS
    sizes = (aq, akv, akv, aq, bq, bkv, bkv, bkv, bkv, bkv, bkv, bg, bq)
    offs = np.concatenate([[0], np.cumsum(sizes)])
    qa, ka, va, za, qb, kc, vc, ks, vs, kw, vw, gb, zb = [w_in[:, offs[n]:offs[n + 1]] for n in range(len(sizes))]
    qw = aq + bq
    krw = PAIR * (1 + 2 * B_KV)
    gb4 = gb.reshape(-1, B_KV, B_REP, 3).transpose(0, 1, 3, 2).reshape(-1, B_KV, 3 * B_REP)
    gb4 = jnp.pad(gb4, ((0, 0), (0, 0), (0, GATE_ROWS - 3 * B_REP))).reshape(-1, B_KV * GATE_ROWS)
    vrows = akv + 2 * bkv
    t_width = -(-(vrows + B_KV * GATE_ROWS) // PAIR) * PAIR
    t_pad = jnp.zeros((w_in.shape[0], t_width - vrows - B_KV * GATE_ROWS), w_in.dtype)
    kheads = akv + 2 * bkv
    k_width = -(-kheads // PAIR) * PAIR
    k_pad = jnp.zeros((w_in.shape[0], k_width - kheads), w_in.dtype)
    w = jnp.concatenate([qa * (Q_SCALE * LOG2E), qb * (Q_SCALE * LOG2E), ka, ks, kw, k_pad, kc, vc, za, zb,
                         va, vs, vw, gb4, t_pad], axis=1).astype(BF16)
    groups, col = [], 0
    for width, kind in ((qw, "rope"), (k_width, "rope_dup"), (bkv, "rope"), (bkv, "plain"), (qw, "silu")):
        groups.append((col, width, kind, len(groups)))
        col += width
    groups_t = ((0, akv, "plain", 5, SWA_TQ), (akv, 2 * bkv, "plain", 6, NSA_TQ),
                (vrows, B_KV * GATE_ROWS, "sigmoid", 7, None))
    rows, tm = nb * seq, ROW_TILE
    per_b = seq // tm
    widths = [(qw, BF16), (krw, BF16), (bkv, F32), (bkv, F32), (qw, F32)]
    out_shape = [jax.ShapeDtypeStruct((rows, wd), dt) for wd, dt in widths]
    out_specs = [pl.BlockSpec((tm, wd), lambda i: (i, 0)) for wd, _ in widths]
    for n_rows, chunk in ((akv, SWA_TQ), (2 * bkv, NSA_TQ)):
        out_shape.append(jax.ShapeDtypeStruct((nb, seq // chunk, n_rows, chunk), BF16))
        out_specs.append(pl.BlockSpec((1, tm // chunk, n_rows, chunk), lambda i: (i // per_b, i % per_b, 0, 0)))
    out_shape.append(jax.ShapeDtypeStruct((nb, B_KV * GATE_ROWS, seq), F32))
    out_specs.append(pl.BlockSpec((1, B_KV * GATE_ROWS, tm), lambda i: (i // per_b, 0, i % per_b)))
    return w, (col, t_width), tuple(groups), groups_t, out_shape, out_specs


def _odd_plan(w_in, nb, seq):
    cw = C_HEADS * HEAD_DIM
    col_scale = np.ones((1, 4 * cw), np.float32)
    col_scale[:, :cw] = Q_SCALE * LOG2E
    w = (w_in * col_scale).astype(BF16)
    groups = ((0, cw, "rope", 0), (cw, cw, "rope_mean", 1), (3 * cw, cw, "silu", 2))
    groups_t = ((0, cw, "plain", 3, MOBA_BLOCK),)
    rows, tm = nb * seq, ROW_TILE
    per_b = seq // tm
    widths = [(cw, BF16), (cw, BF16), (cw, F32)]
    out_shape = [jax.ShapeDtypeStruct((rows, wd), dt) for wd, dt in widths]
    out_specs = [pl.BlockSpec((tm, wd), lambda i: (i, 0)) for wd, _ in widths]
    out_shape += [jax.ShapeDtypeStruct((nb, seq // MOBA_BLOCK, cw, MOBA_BLOCK), BF16),
                  jax.ShapeDtypeStruct((nb, seq // MOBA_BLOCK, 1, cw), F32)]
    out_specs += [pl.BlockSpec((1, tm // MOBA_BLOCK, cw, MOBA_BLOCK), lambda i: (i // per_b, i % per_b, 0, 0)),
                  pl.BlockSpec((1, tm // MOBA_BLOCK, 1, cw), lambda i: (i // per_b, i % per_b, 0, 0))]
    return w, (2 * cw, cw), groups, groups_t, out_shape, out_specs


def _rope_tables(seq):
    inv = ROPE_THETA ** (-jnp.arange(0, HEAD_DIM, 2, dtype=F32) / HEAD_DIM)
    ang = jnp.arange(seq, dtype=F32)[:, None] * inv[None, :]
    cos, sin = jnp.cos(ang), jnp.sin(ang)
    reps = PAIR // (HEAD_DIM // 2)
    sign = np.tile(np.concatenate([-np.ones(HEAD_DIM // 2), np.ones(HEAD_DIM // 2)]), PAIR // HEAD_DIM)
    return jnp.tile(cos, (1, reps)), jnp.tile(sin, (1, reps)) * jnp.asarray(sign, F32)[None, :]


def _overlap_matrix_t(seq, nch):
    nc = (seq - CMP_LEN) // CMP_STRIDE + 1
    nsb = seq // SLC_LEN
    cst = np.arange(nc) * CMP_STRIDE
    jj = np.arange(nsb)
    ov = ((cst[None, :] < (jj[:, None] + 1) * SLC_LEN) & (cst[None, :] + CMP_LEN > jj[:, None] * SLC_LEN))
    full = np.zeros((nsb, nch), np.float32)
    full[:, :nc] = ov
    return jnp.asarray(full, BF16), nc, nsb


def _cmp_weights(pe, w1, w2, transpose_out):
    pe2 = jnp.concatenate([pe, pe], axis=1)
    w1t = w1.reshape(CMP_LEN, HEAD_DIM, CMP_HIDDEN)
    zeros = jnp.zeros_like(w1t)
    w1p = jnp.concatenate([jnp.concatenate([w1t, zeros], axis=2), jnp.concatenate([zeros, w1t], axis=2)], axis=1)
    w2o = w2.T if transpose_out else jnp.concatenate([w2, w2], axis=1)
    return pe2, w1p.astype(BF16), w2o.astype(BF16)


def kernel(x, c, w_ada, b_ada, norm_g, w_in_even, a_sinks, cmp_pe_k, cmp_w1_k, cmp_w2_k, cmp_pe_v, cmp_w1_v, cmp_w2_v, w_out_even, w_in_odd, w_out_odd, final_g):
    nb, seq, d = x.shape
    depth = w_ada.shape[0]
    assert seq % ROW_TILE == 0 and seq % MOBA_BLOCK == 0 and NSA_TQ % SLC_LEN == 0
    assert (seq // SLC_LEN) % 8 == 0 and seq % (SWA_TQ * SWA_BLOCKS_PER_STEP) == 0 and ROW_TILE % NSA_TQ == 0
    cos_t, sin_t = _rope_tables(seq)
    ada = _ada(c, w_ada, b_ada)
    nch = seq // CMP_STRIDE
    ovl_t, nc, nsb = _overlap_matrix_t(seq, nch)
    fg = final_g.reshape(1, d)
    x2 = x.reshape(nb * seq, d)
    r3 = lambda a: a.reshape(nb, seq, a.shape[-1])
    prev = None
    for layer in range(depth):
        li = layer // 2
        ng = norm_g[layer].reshape(1, d)
        plan = _even_plan(w_in_even[li], nb, seq) if layer % 2 == 0 else _odd_plan(w_in_odd[li], nb, seq)
        w, t_cols, groups, groups_t, out_shape, out_specs = plan
        outs = _proj(x2, ada, layer, ng, cos_t, sin_t, w, t_cols, groups, groups_t, out_shape, out_specs, seq,
                     moba_mean=None if layer % 2 == 0 else 4, prev=prev)
        if prev is not None:
            x2, outs = outs[0], outs[1:]
        if layer % 2 == 0:
            q, kr, kc, vc, z, vt_a, vt_b, gates_t = outs
            q, kr, kc, vc, z = map(r3, (q, kr, kc, vc, z))
            kcmp, vcmp_t = _compress(kc, vc, *_cmp_weights(cmp_pe_k[li], cmp_w1_k[li], cmp_w2_k[li], False),
                                     *_cmp_weights(cmp_pe_v[li], cmp_w1_v[li], cmp_w2_v[li], True))
            oa = _swa(a_sinks[li], q, kr, vt_a, z, SWA_TQ)
            ob = _nsa(q, kcmp, vcmp_t, kr, vt_b, gates_t, z, ovl_t, nc, nsb, NSA_TQ)
            o_list = [oa.reshape(nb * seq, -1), ob.reshape(nb * seq, -1)]
            w_out = w_out_even[li].astype(BF16)
        else:
            q, k, z, vt, kmean = outs
            o = _moba(r3(q), r3(k), vt, kmean.reshape(nb, seq // MOBA_BLOCK, -1), r3(z))
            o_list = [o.reshape(nb * seq, -1)]
            w_out = w_out_odd[li].astype(BF16)
        prev = (o_list, w_out)
    x2 = _out_proj(x2, ada, depth - 1, prev[0], prev[1], fg, True, seq)
    return x2.reshape(nb, seq, d)
```

```python
import functools

import numpy as np
import jax
import jax.numpy as jnp
from jax import lax
from jax.experimental import pallas as pl
from jax.experimental.pallas import tpu as pltpu

D_MODEL = 1024
HEAD_DIM = 64
PAIR = 2 * HEAD_DIM
ROPE_THETA = 10000.0
RMS_EPS = 1e-6
A_HEADS = 8
A_WINDOW = 128
B_HEADS = 8
B_KV = 2
B_REP = B_HEADS // B_KV
CMP_LEN = 32
CMP_STRIDE = 16
CMP_HIDDEN = 256
SLC_LEN = 64
SLC_TOPN = 8
WIN_LEN = 512
FORCE_BONUS = 1e4
C_HEADS = 16
MOBA_BLOCK = 256
MOBA_TOPK = 3
Q_SCALE = HEAD_DIM ** -0.5
LOG2E = 1.4426950408889634

NEG = -1e30
SAFE_SUM = 2.0 ** 64
MXU_COLS = 256
ROW_TILE = 1024
SWA_TQ = 128
NSA_TQ = 256
SWA_BLOCKS_PER_STEP = 4
MOBA_PAIRS_PER_STEP = 8
MXU_LAG = 3
SUM_ROWS = 16
GATE_ROWS = 16
VMEM_LIMIT = 60 * 1024 * 1024

BF16 = jnp.bfloat16
F32 = jnp.float32


def _dot_nt(a, b):
    return lax.dot_general(a, b, (((1,), (1,)), ((), ())), preferred_element_type=F32)


def _dot(a, b):
    return jnp.dot(a, b, preferred_element_type=F32)


def _cparams(sem):
    return pltpu.CompilerParams(dimension_semantics=sem, vmem_limit_bytes=VMEM_LIMIT)


def _tile_lanes(a, n):
    return jnp.concatenate([a] * n, axis=1) if n > 1 else a


def _ada_kernel(c_ref, w_ref, b_ref, o_ref):
    c = c_ref[...]
    ca = (c * jax.nn.sigmoid(c)).astype(BF16)
    d = c.shape[-1]
    for j in range(o_ref.shape[1]):
        o_ref[0, j] = _dot(ca, w_ref[0, :, j * d:(j + 1) * d].astype(BF16)) + b_ref[0, j]


def _ada(c, w_ada, b_ada):
    depth, d, _ = w_ada.shape
    nb = c.shape[0]
    b4 = b_ada.reshape(depth, 3, 1, d)
    out = pl.pallas_call(
        _ada_kernel,
        grid=(depth,),
        in_specs=[
            pl.BlockSpec((nb, d), lambda l: (0, 0)),
            pl.BlockSpec((1, d, 3 * d), lambda l: (l, 0, 0)),
            pl.BlockSpec((1, 3, 1, d), lambda l: (l, 0, 0, 0)),
        ],
        out_specs=pl.BlockSpec((1, 3, nb, d), lambda l: (l, 0, 0, 0)),
        out_shape=jax.ShapeDtypeStruct((depth, 3, nb, d), F32),
        compiler_params=_cparams(("arbitrary",)),
        name="ada",
    )(c, w_ada, b4)
    return out.reshape(depth, 3, nb, 1, d)


def _rope(a, cos, sin_signed):
    w = a.shape[-1]
    lane = lax.broadcasted_iota(jnp.int32, a.shape, 1)
    first_half = (lane % HEAD_DIM) < (HEAD_DIM // 2)
    partner = jnp.where(first_half, pltpu.roll(a, w - HEAD_DIM // 2, 1), pltpu.roll(a, HEAD_DIM // 2, 1))
    return a * cos + partner * sin_signed


def _proj_kernel(groups, groups_t, t_cols, moba_mean, n_prev_o, *refs):
    x_ref = refs[0]
    prev_refs, refs = refs[1:1 + (n_prev_o + 2 if n_prev_o else 0)], refs[1 + (n_prev_o + 2 if n_prev_o else 0):]
    shift_ref, scale_ref, g_ref, cos_ref, sin_ref, w_ref = refs[:6]
    out_refs = refs[6:]
    t_start, t_width = t_cols

    x = x_ref[...]
    if n_prev_o:
        gate_ref, wout_ref = prev_refs[0], prev_refs[-1]
        mix, k0 = None, 0
        for o_ref in prev_refs[1:-1]:
            kw = o_ref.shape[-1]
            part = _dot(o_ref[...], wout_ref[k0:k0 + kw, :])
            mix = part if mix is None else mix + part
            k0 += kw
        x = x + gate_ref[0, 0, 0] * mix
        out_refs[0][...] = x
        out_refs = out_refs[1:]
    y = x * lax.rsqrt(jnp.mean(x * x, axis=-1, keepdims=True) + RMS_EPS)
    h = (y * g_ref[...]) * (1.0 + scale_ref[0, 0, 0]) + shift_ref[0, 0, 0]
    hb = h.astype(BF16)
    tm = hb.shape[0]
    cos, sin = cos_ref[...], sin_ref[...]
    lo_half = lax.broadcasted_iota(jnp.int32, (tm, PAIR), 1) < HEAD_DIM
    pair_plan = {}
    for (w_start, width, kind, out_idx) in groups:
        for c in range(0, width, PAIR):
            pair_plan[(w_start + c) // PAIR] = (kind, out_idx, c)
    n_cols = w_ref.shape[1]
    for s0 in range(0, n_cols, MXU_COLS):
        sw = min(MXU_COLS, n_cols - s0)
        acc = _dot(hb, w_ref[:, s0:s0 + sw])
        for h0 in range(0, sw, PAIR):
            col = s0 + h0
            a = acc[:, h0:h0 + PAIR]
            if t_start <= col < t_start + t_width:
                f0 = col - t_start
                a_t = a.T
                for (r_start, n_rows, kind, out_idx, chunk) in groups_t:
                    lo, hi = max(r_start, f0), min(r_start + n_rows, f0 + PAIR)
                    if lo >= hi:
                        continue
                    o_ref = out_refs[out_idx]
                    part = a_t[lo - f0:hi - f0]
                    if kind == "sigmoid":
                        o_ref[0, lo - r_start:hi - r_start, :] = jax.nn.sigmoid(part)
                    else:
                        for jj in range(tm // chunk):
                            o_ref[0, jj, lo - r_start:hi - r_start, :] = (
                                part[:, jj * chunk:(jj + 1) * chunk].astype(o_ref.dtype))
                continue
            kind, out_idx, c = pair_plan[col // PAIR]
            o_ref = out_refs[out_idx]
            if kind in ("rope", "rope_mean", "rope_dup"):
                a = _rope(a, cos, sin)
            elif kind == "silu":
                a = a * jax.nn.sigmoid(a)
            if kind == "rope_dup":
                b = pltpu.roll(a, HEAD_DIM, 1)
                for hh, dup in enumerate((jnp.where(lo_half, a, b), jnp.where(lo_half, b, a))):
                    dst = (2 * (c // PAIR) + hh) * PAIR
                    if dst < o_ref.shape[1]:
                        o_ref[:, dst:dst + PAIR] = dup.astype(o_ref.dtype)
                continue
            o_ref[:, c:c + PAIR] = a.astype(o_ref.dtype)
            if kind == "rope_mean":
                km_ref = out_refs[moba_mean]
                for j in range(tm // MOBA_BLOCK):
                    blk = a[j * MOBA_BLOCK:(j + 1) * MOBA_BLOCK]
                    km_ref[0, j, :, c:c + PAIR] = jnp.sum(blk, axis=0, keepdims=True) * (1.0 / MOBA_BLOCK)


def _proj(x2, ada, layer, norm_g, cos_t, sin_t, w, t_cols, groups, groups_t, out_shape, out_specs, seq,
          moba_mean=None, prev=None):
    rows, d = x2.shape
    tm = ROW_TILE
    per_b = seq // tm
    args, in_specs = [x2], [pl.BlockSpec((tm, d), lambda i: (i, 0))]
    n_prev_o = 0
    if prev is not None:
        o_list, w_out = prev
        n_prev_o = len(o_list)
        args += [ada, *o_list, w_out]
        in_specs.append(pl.BlockSpec((1, 1, 1, 1, d), lambda i: (layer - 1, 2, i // per_b, 0, 0)))
        in_specs += [pl.BlockSpec((tm, o.shape[-1]), lambda i: (i, 0)) for o in o_list]
        in_specs.append(pl.BlockSpec(w_out.shape, lambda i: (0, 0)))
        out_shape = [jax.ShapeDtypeStruct((rows, d), F32)] + list(out_shape)
        out_specs = [pl.BlockSpec((tm, d), lambda i: (i, 0))] + list(out_specs)
    args += [ada, ada, norm_g, cos_t, sin_t, w]
    in_specs += [
        pl.BlockSpec((1, 1, 1, 1, d), lambda i: (layer, 0, i // per_b, 0, 0)),
        pl.BlockSpec((1, 1, 1, 1, d), lambda i: (layer, 1, i // per_b, 0, 0)),
        pl.BlockSpec((1, d), lambda i: (0, 0)),
        pl.BlockSpec((tm, PAIR), lambda i: (i % per_b, 0)),
        pl.BlockSpec((tm, PAIR), lambda i: (i % per_b, 0)),
        pl.BlockSpec(w.shape, lambda i: (0, 0)),
    ]
    return pl.pallas_call(
        functools.partial(_proj_kernel, groups, groups_t, t_cols, moba_mean, n_prev_o),
        grid=(rows // tm,),
        in_specs=in_specs,
        out_specs=out_specs,
        out_shape=out_shape,
        compiler_params=_cparams(("arbitrary",)),
        name="proj",
    )(*args)


def _out_kernel(n_o, final, *refs):
    x_ref, gate_ref = refs[0], refs[1]
    o_refs = refs[2:2 + n_o]
    w_ref = refs[2 + n_o]
    fg_ref = refs[3 + n_o]
    out_ref = refs[4 + n_o]
    y = None
    k0 = 0
    for o_ref in o_refs:
        kw = o_ref.shape[-1]
        part = _dot(o_ref[...], w_ref[k0:k0 + kw, :])
        y = part if y is None else y + part
        k0 += kw
    xn = x_ref[...] + gate_ref[0, 0, 0] * y
    if final:
        xn = (xn * lax.rsqrt(jnp.mean(xn * xn, axis=-1, keepdims=True) + RMS_EPS)) * fg_ref[...]
    out_ref[...] = xn


def _out_proj(x2, ada, layer, o_list, w_out, final_g, final, seq):
    rows, d = x2.shape
    tm = ROW_TILE
    per_b = seq // tm
    n_o = len(o_list)
    in_specs = [
        pl.BlockSpec((tm, d), lambda i: (i, 0)),
        pl.BlockSpec((1, 1, 1, 1, d), lambda i: (layer, 2, i // per_b, 0, 0)),
    ]
    in_specs += [pl.BlockSpec((tm, o.shape[-1]), lambda i: (i, 0)) for o in o_list]
    in_specs += [pl.BlockSpec(w_out.shape, lambda i: (0, 0)), pl.BlockSpec((1, d), lambda i: (0, 0))]
    return pl.pallas_call(
        functools.partial(_out_kernel, n_o, final),
        grid=(rows // tm,),
        in_specs=in_specs,
        out_specs=pl.BlockSpec((tm, d), lambda i: (i, 0)),
        out_shape=jax.ShapeDtypeStruct((rows, d), F32),
        compiler_params=_cparams(("arbitrary",)),
        name="out_proj",
    )(x2, ada, *o_list, w_out, final_g)


def _stack_heads(q_pairs):
    row = lax.broadcasted_iota(jnp.int32, (PAIR, 1), 0)
    m_lo = (row < HEAD_DIM).astype(F32)
    m_hi = (row >= HEAD_DIM).astype(F32)
    blocks = []
    for p in range(q_pairs.shape[-1] // PAIR):
        qp_t = q_pairs[:, p * PAIR:(p + 1) * PAIR].astype(F32).T
        blocks += [qp_t * m_lo, qp_t * m_hi]
    return jnp.concatenate(blocks, axis=1).astype(BF16)


def _flash_init(d_rows, n_cols):
    return jnp.full((1, n_cols), NEG, F32), jnp.zeros((1, n_cols), F32), jnp.zeros((d_rows, n_cols), F32)


def _flash(carry, s_t, v_t_chunks):
    m, l, acc = carry
    m_new = jnp.maximum(m, jnp.max(s_t, axis=0, keepdims=True))
    alpha = jnp.exp2(m - m_new)
    e = jnp.exp2(s_t - m_new).astype(BF16)
    l = alpha * l + _dot(jnp.ones((SUM_ROWS, e.shape[0]), BF16), e)[0:1]
    acc, k0 = alpha * acc, 0
    for vt in v_t_chunks:
        acc = acc + _dot(vt, e[k0:k0 + vt.shape[1]])
        k0 += vt.shape[1]
    return m_new, l, acc


def _topn_rows(score, valid, n_cand, topn):
    ridx = lax.broadcasted_iota(jnp.int32, score.shape, 0)
    rank = jnp.zeros(score.shape, F32)
    for jp in range(n_cand):
        row = score[jp:jp + 1, :]
        earlier = jnp.where(ridx > jp, 1.0, 0.0)
        rank = rank + jnp.where(row > score, 1.0, jnp.where(row == score, earlier, 0.0))
    return valid & (rank < topn)


def _swa_kernel(tq, nqb, sink_ref, q_ref, k_ref, vt_ref, z_ref, o_ref):
    step = pl.program_id(1)
    n_prev = -(-(A_WINDOW - 1) // tq)
    krow = lax.broadcasted_iota(jnp.int32, (tq, tq), 0)
    qcol = lax.broadcasted_iota(jnp.int32, (tq, tq), 1)
    causal = _tile_lanes(jnp.where(krow <= qcol, 0.0, NEG), A_HEADS)
    edge = _tile_lanes(jnp.where(n_prev * tq + qcol - krow < A_WINDOW, 0.0, NEG), A_HEADS)
    ones = jnp.ones((SUM_ROWS, tq), BF16)
    sink = jnp.concatenate([jnp.full((1, tq), sink_ref[hd] * LOG2E, F32) for hd in range(A_HEADS)], axis=1)

    blocks = [step * nqb + bb for bb in range(nqb)]
    qs = [_stack_heads(q_ref[0, bb * tq:(bb + 1) * tq, :]) for bb in range(nqb)]
    scores = []
    for bb, i in enumerate(blocks):
        row = []
        for j in range(n_prev + 1):
            off = pl.multiple_of(jnp.maximum(i - j, 0) * tq, tq)
            s_t = _dot(k_ref[0, pl.ds(off, tq), :], qs[bb])
            s_t = s_t + causal if j == 0 else (s_t + edge if j == n_prev else s_t)
            row.append(s_t if j == 0 else s_t + jnp.where(i - j >= 0, 0.0, NEG))
        scores.append(row)

    outs = []
    for bb, i in enumerate(blocks):
        m = sink
        for s_t in scores[bb]:
            m = jnp.maximum(m, jnp.max(s_t, axis=0, keepdims=True))
        den, acc = jnp.exp2(sink - m), None
        for j, s_t in enumerate(scores[bb]):
            v_ext = jnp.concatenate([vt_ref[0, jnp.maximum(i - j, 0)], ones], axis=0)
            res = _dot(v_ext, jnp.exp2(s_t - m).astype(BF16))
            den = den + res[HEAD_DIM:HEAD_DIM + 1]
            acc = res[:HEAD_DIM] if acc is None else acc + res[:HEAD_DIM]
        o_t = acc * (1.0 / den)
        heads = jnp.concatenate([o_t[:, hd * tq:(hd + 1) * tq] for hd in range(A_HEADS)], axis=0)
        outs.append(heads.T)
    o_ref[0] = (jnp.concatenate(outs, axis=0) * z_ref[0]).astype(o_ref.dtype)


def _swa(sinks, q, kr, vt, z, tq):
    nb, seq, _ = q.shape
    aw = A_HEADS * HEAD_DIM
    nqb = SWA_BLOCKS_PER_STEP
    rows = nqb * tq
    return pl.pallas_call(
        functools.partial(_swa_kernel, tq, nqb),
        grid=(nb, seq // rows),
        in_specs=[
            pl.BlockSpec(memory_space=pltpu.SMEM),
            pl.BlockSpec((1, rows, aw), lambda b, i: (b, i, 0)),
            pl.BlockSpec((1, seq, PAIR), lambda b, i: (b, 0, 0)),
            pl.BlockSpec((1, seq // tq, HEAD_DIM, tq), lambda b, i: (b, 0, 0, 0)),
            pl.BlockSpec((1, rows, aw), lambda b, i: (b, i, 0)),
        ],
        out_specs=pl.BlockSpec((1, rows, aw), lambda b, i: (b, i, 0)),
        out_shape=jax.ShapeDtypeStruct((nb, seq, aw), BF16),
        compiler_params=_cparams(("arbitrary", "arbitrary")),
        name="swa",
    )(sinks, q, kr, vt, z)


def _cmp_hidden(x_ref, pe_ref, w1_ref, nch):
    top, bot = None, None
    for l in range(CMP_STRIDE):
        rows = x_ref[0, pl.ds(l, nch, stride=CMP_STRIDE), :]
        t = _dot((rows + pe_ref[l:l + 1, :]).astype(BF16), w1_ref[l])
        b = _dot((rows + pe_ref[CMP_STRIDE + l:CMP_STRIDE + l + 1, :]).astype(BF16), w1_ref[CMP_STRIDE + l])
        top = t if top is None else top + t
        bot = b if bot is None else bot + b
    return jax.nn.gelu(top + pltpu.roll(bot, nch - 1, 0))


def _compress_kernel(kc_ref, vc_ref, pek_ref, w1k_ref, w2k_ref, pev_ref, w1v_ref, w2v_ref, ko_ref, vo_ref):
    nch = ko_ref.shape[2]
    hk = _cmp_hidden(kc_ref, pek_ref, w1k_ref, nch).astype(BF16)
    hv = _cmp_hidden(vc_ref, pev_ref, w1v_ref, nch).astype(BF16)
    for g in range(B_KV):
        gs = slice(g * CMP_HIDDEN, (g + 1) * CMP_HIDDEN)
        ko_ref[0, g] = _dot(hk[:, gs], w2k_ref[...]).astype(ko_ref.dtype)
        vo_ref[0, g] = _dot_nt(w2v_ref[...], hv[:, gs]).astype(vo_ref.dtype)


def _compress(kc, vc, pek, w1k, w2k, pev, w1v, w2v):
    nb, seq, _ = kc.shape
    nch = seq // CMP_STRIDE
    full = lambda a: pl.BlockSpec(a.shape, lambda b: (0,) * a.ndim)
    blk = pl.BlockSpec((1, seq, PAIR), lambda b: (b, 0, 0))
    return pl.pallas_call(
        _compress_kernel,
        grid=(nb,),
        in_specs=[blk, blk, full(pek), full(w1k), full(w2k), full(pev), full(w1v), full(w2v)],
        out_specs=[pl.BlockSpec((1, B_KV, nch, PAIR), lambda b: (b, 0, 0, 0)),
                   pl.BlockSpec((1, B_KV, HEAD_DIM, nch), lambda b: (b, 0, 0, 0))],
        out_shape=[jax.ShapeDtypeStruct((nb, B_KV, nch, PAIR), BF16),
                   jax.ShapeDtypeStruct((nb, B_KV, HEAD_DIM, nch), BF16)],
        compiler_params=_cparams(("arbitrary",)),
        name="compress",
    )(kc, vc, pek, w1k, w2k, pev, w1v, w2v)


def _nsa_kernel(tq, nc, nsb, q_ref, kc_ref, vct_ref, ks0_ref, ks1_ref, vs0_ref, vs1_ref, kw0_ref, kw1_ref,
                vw0_ref, vw1_ref, gt_ref, z_ref, ovl_ref, o_ref, bias_ref):
    i = pl.program_id(1)
    groups = range(B_KV)
    ks_refs, vs_refs = (ks0_ref, ks1_ref), (vs0_ref, vs1_ref)
    kw_refs, vw_refs = (kw0_ref, kw1_ref), (vw0_ref, vw1_ref)
    gw = B_REP * HEAD_DIM
    nr = B_REP * tq
    qs = [_stack_heads(q_ref[0, :, g * gw:(g + 1) * gw]) for g in groups]
    t_row = i * tq + lax.broadcasted_iota(jnp.int32, (1, tq), 1)

    nch = kc_ref.shape[2]
    cidx = lax.broadcasted_iota(jnp.int32, (nch, tq), 0)
    ok_c = (cidx < nc) & (t_row >= cidx * CMP_STRIDE + (CMP_LEN - 1))
    bias_c = _tile_lanes(jnp.where(ok_c, 0.0, NEG), B_REP)
    jr = lax.broadcasted_iota(jnp.int32, (nsb, tq), 0)
    tb = t_row // SLC_LEN
    valid = jr <= tb
    forced = (jr == 0) | (jr == tb) | (jr == tb - 1)
    n_prev = -(-(WIN_LEN - 1) // tq)
    blk_per_chunk = tq // SLC_LEN
    krow = lax.broadcasted_iota(jnp.int32, (tq, tq), 0)
    qcol = lax.broadcasted_iota(jnp.int32, (tq, tq), 1)
    causal = jnp.where(krow <= qcol, 0.0, NEG)
    edge = _tile_lanes(jnp.where(n_prev * tq + qcol - krow < WIN_LEN, 0.0, NEG), B_REP)
    causal_r = _tile_lanes(causal, B_REP)
    ones = jnp.ones((SUM_ROWS, tq), BF16)

    def win_chunk(j):
        return jnp.maximum(i - j, 0)

    def win_scores(g, j):
        off = pl.multiple_of(win_chunk(j) * tq, tq)
        s_t = _dot(kw_refs[g][0, pl.ds(off, tq), :], qs[g])
        return s_t + causal_r if j == 0 else (s_t + edge if j == n_prev else s_t)

    def slc_bias(g, c, diagonal):
        rows = [jnp.broadcast_to(bias_ref[g, pl.ds(c * blk_per_chunk + bb, 1), :], (SLC_LEN, tq))
                for bb in range(blk_per_chunk)]
        b = jnp.concatenate(rows, axis=0)
        return _tile_lanes(b + causal if diagonal else b, B_REP)

    def slc_qk(g, c):
        off = pl.multiple_of(c * tq, tq)
        return _dot(ks_refs[g][0, pl.ds(off, tq), :], qs[g])

    def sums_and_pv(v_t, e):
        res = _dot(jnp.concatenate([v_t, ones], axis=0), e)
        return res[HEAD_DIM:HEAD_DIM + 1], res[:HEAD_DIM]

    def masked_ref(m0, present):
        return m0 + jnp.where(present, 0.0, -NEG)

    s_cmp = [_dot(kc_ref[0, g], qs[g]) + bias_c for g in groups]
    s_win0 = [win_scores(g, 0) for g in groups]
    s_diag = [slc_qk(g, i) for g in groups]

    o_cmp = []
    for g in groups:
        s = s_cmp[g]
        m = jnp.max(s, axis=0, keepdims=True)
        m = jnp.where(m > 0.5 * NEG, m, 0.0)
        e = jnp.exp2(s - m)
        den = jnp.sum(e, axis=0, keepdims=True)
        p_t = (e * (1.0 / jnp.where(den > 0, den, 1.0))).astype(BF16)
        res = _dot(jnp.concatenate([vct_ref[0, g], ovl_ref[...]], axis=0), p_t)
        o_cmp.append(res[:HEAD_DIM])
        imp = res[HEAD_DIM:, 0:tq]
        for r in range(1, B_REP):
            imp = imp + res[HEAD_DIM:, r * tq:(r + 1) * tq]
        score = jnp.where(valid, jnp.where(forced, FORCE_BONUS, imp), -jnp.inf)
        bias_ref[g] = jnp.where(_topn_rows(score, valid, nsb, SLC_TOPN), 0.0, NEG)

    def write_out(slc, win):
        heads = []
        for g in groups:
            o_slc = slc[g][1] * (1.0 / slc[g][0])
            o_win = win[g][1] * (1.0 / win[g][0])
            gt = gt_ref[0, g * GATE_ROWS:(g + 1) * GATE_ROWS]
            for r in range(B_REP):
                cs = slice(r * tq, (r + 1) * tq)
                heads.append(gt[r:r + 1] * o_cmp[g][:, cs] + gt[B_REP + r:B_REP + r + 1] * o_slc[:, cs]
                             + gt[2 * B_REP + r:2 * B_REP + r + 1] * o_win[:, cs])
        o_t = jnp.concatenate(heads, axis=0)
        o_ref[0] = (o_t.T * z_ref[0]).astype(o_ref.dtype)

    win_m0 = [s_win0[g][0:1, :] for g in groups]
    win = [list(sums_and_pv(vw_refs[g][0, i], jnp.exp2(s_win0[g] - win_m0[g]).astype(BF16))) for g in groups]
    pending = []

    def finish_win(g, j, arg):
        l_c, pv = sums_and_pv(vw_refs[g][0, win_chunk(j)], jnp.exp2(arg).astype(BF16))
        win[g][0], win[g][1] = win[g][0] + l_c, win[g][1] + pv

    for j in range(1, n_prev + 1):
        for g in groups:
            pending.append((g, j, win_scores(g, j) - masked_ref(win_m0[g], i - j >= 0)))
            if len(pending) > MXU_LAG:
                finish_win(*pending.pop(0))
    while pending:
        finish_win(*pending.pop(0))

    s_diag = [s_diag[g] + slc_bias(g, i, True) for g in groups]
    own_blk = _tile_lanes(lax.broadcasted_iota(jnp.int32, (1, tq), 1) // SLC_LEN, B_REP)
    slc_m0 = []
    for g in groups:
        ref = s_diag[g][0:1, :]
        for bb in range(1, blk_per_chunk):
            ref = jnp.where(own_blk == bb, s_diag[g][bb * SLC_LEN:bb * SLC_LEN + 1, :], ref)
        slc_m0.append(ref)
    slc0 = tuple(sums_and_pv(vs_refs[g][0, i], jnp.exp2(s_diag[g] - slc_m0[g]).astype(BF16)) for g in groups)

    def fast_pair(u, sums):
        sums = [list(x) for x in sums]
        waiting = []

        def finish(g, c, arg):
            l_c, pv = sums_and_pv(vs_refs[g][0, c], jnp.exp2(arg).astype(BF16))
            sums[g][0], sums[g][1] = sums[g][0] + l_c, sums[g][1] + pv

        for c in (2 * u, 2 * u + 1):
            for g in groups:
                waiting.append((g, c, slc_qk(g, c) + slc_bias(g, c, False) - masked_ref(slc_m0[g], c < i)))
                if len(waiting) > MXU_LAG:
                    finish(*waiting.pop(0))
        while waiting:
            finish(*waiting.pop(0))
        return tuple(tuple(x) for x in sums)

    slc = lax.fori_loop(0, (i + 1) // 2, fast_pair, slc0)
    write_out(slc, win)
    worst = jnp.maximum(jnp.maximum(slc[0][0], slc[1][0]), jnp.maximum(win[0][0], win[1][0]))
    safe = jnp.max(worst) < SAFE_SUM

    @pl.when(jnp.logical_not(safe))
    def _():
        span = (n_prev + 1) * tq
        c0 = jnp.maximum(i - n_prev, 0)
        start = pl.multiple_of(c0 * tq, tq)
        diff = t_row - (start + lax.broadcasted_iota(jnp.int32, (span, tq), 0))
        bias_w = _tile_lanes(jnp.where(diff >= 0, jnp.where(diff < WIN_LEN, 0.0, NEG), NEG), B_REP)
        win_x = []
        for g in groups:
            sw = _dot(kw_refs[g][0, pl.ds(start, span), :], qs[g]) + bias_w
            _, l_w, acc_w = _flash(_flash_init(HEAD_DIM, nr), sw,
                                   [vw_refs[g][0, c0 + cc] for cc in range(n_prev + 1)])
            win_x.append((l_w, acc_w))

        def exact_pair(u, carries):
            out = []
            for g in groups:
                carry = carries[g]
                for c in (2 * u, 2 * u + 1):
                    s_t = slc_qk(g, c) + slc_bias(g, c, False) + jnp.where(c < i, 0.0, NEG)
                    carry = _flash(carry, s_t, [vs_refs[g][0, c]])
                out.append(carry)
            return tuple(out)

        first = tuple(_flash(_flash_init(HEAD_DIM, nr), s_diag[g], [vs_refs[g][0, i]]) for g in groups)
        exact = lax.fori_loop(0, (i + 1) // 2, exact_pair, first)
        write_out([(l, acc) for (_, l, acc) in exact], win_x)


def _nsa(q, kcmp, vcmp_t, kr, vt, gates_t, z, ovl_t, nc, nsb, tq):
    nb, seq, _ = q.shape
    bw = B_HEADS * HEAD_DIM
    nch = kcmp.shape[2]
    kblk = lambda col: pl.BlockSpec((1, seq, PAIR), lambda b, i: (b, 0, col))
    vblk = lambda row: pl.BlockSpec((1, seq // tq, HEAD_DIM, tq), lambda b, i: (b, 0, row, 0))
    return pl.pallas_call(
        functools.partial(_nsa_kernel, tq, nc, nsb),
        grid=(nb, seq // tq),
        in_specs=[
            pl.BlockSpec((1, tq, bw), lambda b, i: (b, i, 1)),
            pl.BlockSpec((1, B_KV, nch, PAIR), lambda b, i: (b, 0, 0, 0)),
            pl.BlockSpec((1, B_KV, HEAD_DIM, nch), lambda b, i: (b, 0, 0, 0)),
            kblk(1), kblk(2), vblk(0), vblk(1), kblk(3), kblk(4), vblk(2), vblk(3),
            pl.BlockSpec((1, B_KV * GATE_ROWS, tq), lambda b, i: (b, 0, i)),
            pl.BlockSpec((1, tq, bw), lambda b, i: (b, i, 1)),
            pl.BlockSpec(ovl_t.shape, lambda b, i: (0, 0)),
        ],
        out_specs=pl.BlockSpec((1, tq, bw), lambda b, i: (b, i, 0)),
        out_shape=jax.ShapeDtypeStruct((nb, seq, bw), BF16),
        scratch_shapes=[pltpu.VMEM((B_KV, nsb, tq), F32)],
        compiler_params=_cparams(("arbitrary", "arbitrary")),
        name="nsa",
    )(q, kcmp, vcmp_t, kr, kr, vt, vt, kr, kr, vt, vt, gates_t, z, ovl_t)


def _moba_kernel(nblk, npair, q_ref, k_ref, vt_ref, km_ref, z_ref, o_ref, bias_ref):
    i = pl.program_id(2)
    tq = MOBA_BLOCK
    nr = 2 * tq
    ncand = bias_ref.shape[1]
    pairs = range(npair)
    lanes = [slice(p * PAIR, (p + 1) * PAIR) for p in pairs]
    qs = [_stack_heads(q_ref[0, :, lanes[p]]) for p in pairs]

    gs = []
    for p in pairs:
        km = km_ref[0, :, lanes[p]]
        km = jnp.concatenate([km, jnp.zeros((ncand - nblk, PAIR), F32)], axis=0).astype(BF16)
        gs.append(_dot(km, qs[p]))

    krow = lax.broadcasted_iota(jnp.int32, (tq, tq), 0)
    qcol = lax.broadcasted_iota(jnp.int32, (tq, tq), 1)
    causal = _tile_lanes(jnp.where(krow <= qcol, 0.0, NEG), 2)
    ones = jnp.ones((SUM_ROWS, tq), BF16)

    def scores(p, blk, bias):
        off = pl.multiple_of(blk * tq, tq)
        return _dot(k_ref[0, pl.ds(off, tq), lanes[p]], qs[p]) + bias

    def past_scores(p, blk):
        return scores(p, blk, bias_ref[p, pl.ds(blk, 1), :])

    def weights_times_v(e, p, blk):
        res = []
        for hh in range(2):
            v_ext = jnp.concatenate([vt_ref[0, blk, pl.ds(p * PAIR + hh * HEAD_DIM, HEAD_DIM), :], ones], axis=0)
            res.append(_dot(v_ext, e[:, hh * tq:(hh + 1) * tq]))
        res = jnp.concatenate(res, axis=1)
        return res[HEAD_DIM:HEAD_DIM + 1], res[:HEAD_DIM]

    def flash(carry, s_t, p, blk):
        m, l, acc = carry
        m_new = jnp.maximum(m, jnp.max(s_t, axis=0, keepdims=True))
        alpha = jnp.exp2(m - m_new)
        l_c, pv = weights_times_v(jnp.exp2(s_t - m_new).astype(BF16), p, blk)
        return m_new, alpha * l + l_c, alpha * acc + pv

    def write_out(results):
        outs = []
        for p in pairs:
            l_f, acc_f = results[p]
            o_t = acc_f * (1.0 / l_f)
            outs += [o_t[:, :tq], o_t[:, tq:]]
        o_ref[0] = (jnp.concatenate(outs, axis=0).T * z_ref[0]).astype(o_ref.dtype)

    s_own = [scores(p, i, causal) for p in pairs]
    jr = lax.broadcasted_iota(jnp.int32, (ncand, nr), 0)
    past = jr < i
    for p in pairs:
        sel = _topn_rows(jnp.where(past, gs[p], -jnp.inf), past, nblk, MOBA_TOPK)
        bias_ref[p] = jnp.where(sel, 0.0, NEG)
    own = []
    for p in pairs:
        m0 = s_own[p][0:1, :]
        l0, acc0 = weights_times_v(jnp.exp2(s_own[p] - m0).astype(BF16), p, i)
        own.append((m0, l0, acc0))

    def fast_blocks(blks, sums):
        sums = [list(x) for x in sums]
        pending = []

        def finish(p, blk, s_t):
            l_c, pv = weights_times_v(jnp.exp2(s_t - own[p][0]).astype(BF16), p, blk)
            sums[p][0] = sums[p][0] + l_c
            sums[p][1] = sums[p][1] + pv

        for blk in blks:
            for p in pairs:
                pending.append((p, blk, past_scores(p, blk)))
                if len(pending) > MXU_LAG:
                    finish(*pending.pop(0))
        while pending:
            finish(*pending.pop(0))
        return tuple(tuple(x) for x in sums)

    fast = lax.fori_loop(0, (i + 1) // 2, lambda u, s: fast_blocks((2 * u, 2 * u + 1), s),
                         tuple((l0, acc0) for (_, l0, acc0) in own))
    write_out(fast)
    worst = fast[0][0]
    for p in pairs[1:]:
        worst = jnp.maximum(worst, fast[p][0])
    safe = jnp.max(worst) < SAFE_SUM

    @pl.when(jnp.logical_not(safe))
    def _():
        def exact_block(blk, carries):
            return tuple(flash(carries[p], past_scores(p, blk), p, blk) for p in pairs)

        start = (jnp.full((1, nr), NEG, F32), jnp.zeros((1, nr), F32), jnp.zeros((HEAD_DIM, nr), F32))
        first = tuple(flash(start, scores(p, i, causal), p, i) for p in pairs)
        exact = lax.fori_loop(0, i, exact_block, first)
        write_out([(l, acc) for (_, l, acc) in exact])


def _moba(q, k, vt, kmean, z):
    nb, seq, width = q.shape
    tq = MOBA_BLOCK
    nblk = seq // MOBA_BLOCK
    ncand = -(-nblk // 16) * 16
    npair = MOBA_PAIRS_PER_STEP
    gw = npair * PAIR
    qblk = pl.BlockSpec((1, tq, gw), lambda b, p, i: (b, i, p))
    return pl.pallas_call(
        functools.partial(_moba_kernel, nblk, npair),
        grid=(nb, width // gw, seq // tq),
        in_specs=[
            qblk,
            pl.BlockSpec((1, seq, gw), lambda b, p, i: (b, 0, p)),
            pl.BlockSpec((1, nblk, gw, tq), lambda b, p, i: (b, 0, p, 0)),
            pl.BlockSpec((1, nblk, gw), lambda b, p, i: (b, 0, p)),
            qblk,
        ],
        out_specs=qblk,
        out_shape=jax.ShapeDtypeStruct((nb, seq, width), BF16),
        scratch_shapes=[pltpu.VMEM((npair, ncand, 2 * tq), F32)],
        compiler_params=_cparams(("arbitrary", "arbitrary", "arbitrary")),
        name="moba",
    )(q, k, vt, kmean, z)


def _even_plan(w_in, nb, seq):
    aq, akv = A_HEADS * HEAD_DIM, HEAD_DIM
    bq, bkv, bg = B_HEADS * HEAD_DIM, B_KV * HEAD_DIM, 3 * B_HEADS
    sizes = (aq, akv, akv, aq, bq, bkv, bkv, bkv, bkv, bkv, bkv, bg, bq)
    offs = np.concatenate([[0], np.cumsum(sizes)])
    qa, ka, va, za, qb, kc, vc, ks, vs, kw, vw, gb, zb = [w_in[:, offs[n]:offs[n + 1]] for n in range(len(sizes))]
    qw = aq + bq
    krw = PAIR * (1 + 2 * B_KV)
    gb4 = gb.reshape(-1, B_KV, B_REP, 3).transpose(0, 1, 3, 2).reshape(-1, B_KV, 3 * B_REP)
    gb4 = jnp.pad(gb4, ((0, 0), (0, 0), (0, GATE_ROWS - 3 * B_REP))).reshape(-1, B_KV * GATE_ROWS)
    vrows = akv + 2 * bkv
    t_width = -(-(vrows + B_KV * GATE_ROWS) // PAIR) * PAIR
    t_pad = jnp.zeros((w_in.shape[0], t_width - vrows - B_KV * GATE_ROWS), w_in.dtype)
    kheads = akv + 2 * bkv
    k_width = -(-kheads // PAIR) * PAIR
    k_pad = jnp.zeros((w_in.shape[0], k_width - kheads), w_in.dtype)
    w = jnp.concatenate([qa * (Q_SCALE * LOG2E), qb * (Q_SCALE * LOG2E), ka, ks, kw, k_pad, kc, vc, za, zb,
                         va, vs, vw, gb4, t_pad], axis=1).astype(BF16)
    groups, col = [], 0
    for width, kind in ((qw, "rope"), (k_width, "rope_dup"), (bkv, "rope"), (bkv, "plain"), (qw, "silu")):
        groups.append((col, width, kind, len(groups)))
        col += width
    groups_t = ((0, akv, "plain", 5, SWA_TQ), (akv, 2 * bkv, "plain", 6, NSA_TQ),
                (vrows, B_KV * GATE_ROWS, "sigmoid", 7, None))
    rows, tm = nb * seq, ROW_TILE
    per_b = seq // tm
    widths = [(qw, BF16), (krw, BF16), (bkv, F32), (bkv, F32), (qw, F32)]
    out_shape = [jax.ShapeDtypeStruct((rows, wd), dt) for wd, dt in widths]
    out_specs = [pl.BlockSpec((tm, wd), lambda i: (i, 0)) for wd, _ in widths]
    for n_rows, chunk in ((akv, SWA_TQ), (2 * bkv, NSA_TQ)):
        out_shape.append(jax.ShapeDtypeStruct((nb, seq // chunk, n_rows, chunk), BF16))
        out_specs.append(pl.BlockSpec((1, tm // chunk, n_rows, chunk), lambda i: (i // per_b, i % per_b, 0, 0)))
    out_shape.append(jax.ShapeDtypeStruct((nb, B_KV * GATE_ROWS, seq), F32))
    out_specs.append(pl.BlockSpec((1, B_KV * GATE_ROWS, tm), lambda i: (i // per_b, 0, i % per_b)))
    return w, (col, t_width), tuple(groups), groups_t, out_shape, out_specs


def _odd_plan(w_in, nb, seq):
    cw = C_HEADS * HEAD_DIM
    col_scale = np.ones((1, 4 * cw), np.float32)
    col_scale[:, :cw] = Q_SCALE * LOG2E
    w = (w_in * col_scale).astype(BF16)
    groups = ((0, cw, "rope", 0), (cw, cw, "rope_mean", 1), (3 * cw, cw, "silu", 2))
    groups_t = ((0, cw, "plain", 3, MOBA_BLOCK),)
    rows, tm = nb * seq, ROW_TILE
    per_b = seq // tm
    widths = [(cw, BF16), (cw, BF16), (cw, F32)]
    out_shape = [jax.ShapeDtypeStruct((rows, wd), dt) for wd, dt in widths]
    out_specs = [pl.BlockSpec((tm, wd), lambda i: (i, 0)) for wd, _ in widths]
    out_shape += [jax.ShapeDtypeStruct((nb, seq // MOBA_BLOCK, cw, MOBA_BLOCK), BF16),
                  jax.ShapeDtypeStruct((nb, seq // MOBA_BLOCK, 1, cw), F32)]
    out_specs += [pl.BlockSpec((1, tm // MOBA_BLOCK, cw, MOBA_BLOCK), lambda i: (i // per_b, i % per_b, 0, 0)),
                  pl.BlockSpec((1, tm // MOBA_BLOCK, 1, cw), lambda i: (i // per_b, i % per_b, 0, 0))]
    return w, (2 * cw, cw), groups, groups_t, out_shape, out_specs


def _rope_tables(seq):
    inv = ROPE_THETA ** (-jnp.arange(0, HEAD_DIM, 2, dtype=F32) / HEAD_DIM)
    ang = jnp.arange(seq, dtype=F32)[:, None] * inv[None, :]
    cos, sin = jnp.cos(ang), jnp.sin(ang)
    reps = PAIR // (HEAD_DIM // 2)
    sign = np.tile(np.concatenate([-np.ones(HEAD_DIM // 2), np.ones(HEAD_DIM // 2)]), PAIR // HEAD_DIM)
    return jnp.tile(cos, (1, reps)), jnp.tile(sin, (1, reps)) * jnp.asarray(sign, F32)[None, :]


def _overlap_matrix_t(seq, nch):
    nc = (seq - CMP_LEN) // CMP_STRIDE + 1
    nsb = seq // SLC_LEN
    cst = np.arange(nc) * CMP_STRIDE
    jj = np.arange(nsb)
    ov = ((cst[None, :] < (jj[:, None] + 1) * SLC_LEN) & (cst[None, :] + CMP_LEN > jj[:, None] * SLC_LEN))
    full = np.zeros((nsb, nch), np.float32)
    full[:, :nc] = ov
    return jnp.asarray(full, BF16), nc, nsb


def _cmp_weights(pe, w1, w2, transpose_out):
    pe2 = jnp.concatenate([pe, pe], axis=1)
    w1t = w1.reshape(CMP_LEN, HEAD_DIM, CMP_HIDDEN)
    zeros = jnp.zeros_like(w1t)
    w1p = jnp.concatenate([jnp.concatenate([w1t, zeros], axis=2), jnp.concatenate([zeros, w1t], axis=2)], axis=1)
    w2o = w2.T if transpose_out else jnp.concatenate([w2, w2], axis=1)
    return pe2, w1p.astype(BF16), w2o.astype(BF16)


def kernel(x, c, w_ada, b_ada, norm_g, w_in_even, a_sinks, cmp_pe_k, cmp_w1_k, cmp_w2_k, cmp_pe_v, cmp_w1_v, cmp_w2_v, w_out_even, w_in_odd, w_out_odd, final_g):
    nb, seq, d = x.shape
    depth = w_ada.shape[0]
    assert seq % ROW_TILE == 0 and seq % MOBA_BLOCK == 0 and NSA_TQ % SLC_LEN == 0
    assert (seq // SLC_LEN) % 8 == 0 and seq % (SWA_TQ * SWA_BLOCKS_PER_STEP) == 0 and ROW_TILE % NSA_TQ == 0
    cos_t, sin_t = _rope_tables(seq)
    ada = _ada(c, w_ada, b_ada)
    nch = seq // CMP_STRIDE
    ovl_t, nc, nsb = _overlap_matrix_t(seq, nch)
    fg = final_g.reshape(1, d)
    x2 = x.reshape(nb * seq, d)
    r3 = lambda a: a.reshape(nb, seq, a.shape[-1])
    prev = None
    for layer in range(depth):
        li = layer // 2
        ng = norm_g[layer].reshape(1, d)
        plan = _even_plan(w_in_even[li], nb, seq) if layer % 2 == 0 else _odd_plan(w_in_odd[li], nb, seq)
        w, t_cols, groups, groups_t, out_shape, out_specs = plan
        outs = _proj(x2, ada, layer, ng, cos_t, sin_t, w, t_cols, groups, groups_t, out_shape, out_specs, seq,
                     moba_mean=None if layer % 2 == 0 else 4, prev=prev)
        if prev is not None:
            x2, outs = outs[0], outs[1:]
        if layer % 2 == 0:
            q, kr, kc, vc, z, vt_a, vt_b, gates_t = outs
            q, kr, kc, vc, z = map(r3, (q, kr, kc, vc, z))
            kcmp, vcmp_t = _compress(kc, vc, *_cmp_weights(cmp_pe_k[li], cmp_w1_k[li], cmp_w2_k[li], False),
                                     *_cmp_weights(cmp_pe_v[li], cmp_w1_v[li], cmp_w2_v[li], True))
            oa = _swa(a_sinks[li], q, kr, vt_a, z, SWA_TQ)
            ob = _nsa(q, kcmp, vcmp_t, kr, vt_b, gates_t, z, ovl_t, nc, nsb, NSA_TQ)
            o_list = [oa.reshape(nb * seq, -1), ob.reshape(nb * seq, -1)]
            w_out = w_out_even[li].astype(BF16)
        else:
            q, k, z, vt, kmean = outs
            o = _moba(r3(q), r3(k), vt, kmean.reshape(nb, seq // MOBA_BLOCK, -1), r3(z))
            o_list = [o.reshape(nb * seq, -1)]
            w_out = w_out_odd[li].astype(BF16)
        prev = (o_list, w_out)
    x2 = _out_proj(x2, ada, depth - 1, prev[0], prev[1], fg, True, seq)
    return x2.reshape(nb, seq, d)
```

```python
import functools

import numpy as np
import jax
import jax.numpy as jnp
from jax import lax
from jax.experimental import pallas as pl
from jax.experimental.pallas import tpu as pltpu

D_MODEL = 1024
HEAD_DIM = 64
PAIR = 2 * HEAD_DIM
ROPE_THETA = 10000.0
RMS_EPS = 1e-6
A_HEADS = 8
A_WINDOW = 128
B_HEADS = 8
B_KV = 2
B_REP = B_HEADS // B_KV
CMP_LEN = 32
CMP_STRIDE = 16
CMP_HIDDEN = 256
SLC_LEN = 64
SLC_TOPN = 8
WIN_LEN = 512
FORCE_BONUS = 1e4
C_HEADS = 16
MOBA_BLOCK = 256
MOBA_TOPK = 3
Q_SCALE = HEAD_DIM ** -0.5
LOG2E = 1.4426950408889634

NEG = -1e30
SAFE_SUM = 2.0 ** 64
MXU_COLS = 256
ROW_TILE = 1024
SWA_TQ = 128
NSA_TQ = 256
SWA_BLOCKS_PER_STEP = 4
MOBA_PAIRS_PER_STEP = 8
MXU_LAG = 3
SUM_ROWS = 16
GATE_ROWS = 16
VMEM_LIMIT = 60 * 1024 * 1024

BF16 = jnp.bfloat16
F32 = jnp.float32


def _dot_nt(a, b):
    return lax.dot_general(a, b, (((1,), (1,)), ((), ())), preferred_element_type=F32)


def _dot(a, b):
    return jnp.dot(a, b, preferred_element_type=F32)


def _cparams(sem):
    return pltpu.CompilerParams(dimension_semantics=sem, vmem_limit_bytes=VMEM_LIMIT)


def _tile_lanes(a, n):
    return jnp.concatenate([a] * n, axis=1) if n > 1 else a


def _ada_kernel(c_ref, *refs):
    w_refs, b_ref, o_ref = refs[:-2], refs[-2], refs[-1]
    c = c_ref[...]
    ca = (c * jax.nn.sigmoid(c)).astype(BF16)
    for j, w_ref in enumerate(w_refs):
        o_ref[0, j] = _dot(ca, w_ref[0].astype(BF16)) + b_ref[0, j]


def _ada(c, w_ada, b_ada):
    depth, d, _ = w_ada.shape
    nb = c.shape[0]
    b4 = b_ada.reshape(depth, 3, 1, d)
    w_specs = [pl.BlockSpec((1, d, d), functools.partial(lambda l, j: (l, 0, j), j=j)) for j in range(3)]
    out = pl.pallas_call(
        _ada_kernel,
        grid=(depth,),
        in_specs=[pl.BlockSpec((nb, d), lambda l: (0, 0)), *w_specs,
                  pl.BlockSpec((1, 3, 1, d), lambda l: (l, 0, 0, 0))],
        out_specs=pl.BlockSpec((1, 3, nb, d), lambda l: (l, 0, 0, 0)),
        out_shape=jax.ShapeDtypeStruct((depth, 3, nb, d), F32),
        compiler_params=_cparams(("arbitrary",)),
        name="ada",
    )(c, w_ada, w_ada, w_ada, b4)
    return out.reshape(depth, 3, nb, 1, d)


def _rope(a, cos, sin_signed):
    w = a.shape[-1]
    lane = lax.broadcasted_iota(jnp.int32, a.shape, 1)
    first_half = (lane % HEAD_DIM) < (HEAD_DIM // 2)
    partner = jnp.where(first_half, pltpu.roll(a, w - HEAD_DIM // 2, 1), pltpu.roll(a, HEAD_DIM // 2, 1))
    return a * cos + partner * sin_signed


def _proj_kernel(groups, groups_t, t_cols, moba_mean, n_prev_o, *refs):
    x_ref = refs[0]
    prev_refs, refs = refs[1:1 + (n_prev_o + 2 if n_prev_o else 0)], refs[1 + (n_prev_o + 2 if n_prev_o else 0):]
    shift_ref, scale_ref, g_ref, cos_ref, sin_ref, w_ref = refs[:6]
    out_refs = refs[6:]
    t_start, t_width = t_cols

    x = x_ref[...]
    if n_prev_o:
        gate_ref, wout_ref = prev_refs[0], prev_refs[-1]
        mix, k0 = None, 0
        for o_ref in prev_refs[1:-1]:
            kw = o_ref.shape[-1]
            part = _dot(o_ref[...], wout_ref[k0:k0 + kw, :])
            mix = part if mix is None else mix + part
            k0 += kw
        x = x + gate_ref[0, 0, 0] * mix
        out_refs[0][...] = x
        out_refs = out_refs[1:]
    y = x * lax.rsqrt(jnp.mean(x * x, axis=-1, keepdims=True) + RMS_EPS)
    h = (y * g_ref[...]) * (1.0 + scale_ref[0, 0, 0]) + shift_ref[0, 0, 0]
    hb = h.astype(BF16)
    tm = hb.shape[0]
    cos, sin = cos_ref[...], sin_ref[...]
    lo_half = lax.broadcasted_iota(jnp.int32, (tm, PAIR), 1) < HEAD_DIM
    pair_plan = {}
    for (w_start, width, kind, out_idx) in groups:
        for c in range(0, width, PAIR):
            pair_plan[(w_start + c) // PAIR] = (kind, out_idx, c)
    n_cols = w_ref.shape[1]
    for s0 in range(0, n_cols, MXU_COLS):
        sw = min(MXU_COLS, n_cols - s0)
        acc = _dot(hb, w_ref[:, s0:s0 + sw])
        for h0 in range(0, sw, PAIR):
            col = s0 + h0
            a = acc[:, h0:h0 + PAIR]
            if t_start <= col < t_start + t_width:
                f0 = col - t_start
                a_t = a.T
                for (r_start, n_rows, kind, out_idx, chunk) in groups_t:
                    lo, hi = max(r_start, f0), min(r_start + n_rows, f0 + PAIR)
                    if lo >= hi:
                        continue
                    o_ref = out_refs[out_idx]
                    part = a_t[lo - f0:hi - f0]
                    if kind == "sigmoid":
                        o_ref[0, lo - r_start:hi - r_start, :] = jax.nn.sigmoid(part)
                    else:
                        for jj in range(tm // chunk):
                            o_ref[0, jj, lo - r_start:hi - r_start, :] = (
                                part[:, jj * chunk:(jj + 1) * chunk].astype(o_ref.dtype))
                continue
            kind, out_idx, c = pair_plan[col // PAIR]
            o_ref = out_refs[out_idx]
            if kind in ("rope", "rope_mean", "rope_dup"):
                a = _rope(a, cos, sin)
            elif kind == "silu":
                a = a * jax.nn.sigmoid(a)
            if kind == "rope_dup":
                b = pltpu.roll(a, HEAD_DIM, 1)
                for hh, dup in enumerate((jnp.where(lo_half, a, b), jnp.where(lo_half, b, a))):
                    dst = (2 * (c // PAIR) + hh) * PAIR
                    if dst < o_ref.shape[1]:
                        o_ref[:, dst:dst + PAIR] = dup.astype(o_ref.dtype)
                continue
            o_ref[:, c:c + PAIR] = a.astype(o_ref.dtype)
            if kind == "rope_mean":
                km_ref = out_refs[moba_mean]
                for j in range(tm // MOBA_BLOCK):
                    blk = a[j * MOBA_BLOCK:(j + 1) * MOBA_BLOCK]
                    km_ref[0, j, :, c:c + PAIR] = jnp.sum(blk, axis=0, keepdims=True) * (1.0 / MOBA_BLOCK)


def _proj(x2, ada, layer, norm_g, cos_t, sin_t, w, t_cols, groups, groups_t, out_shape, out_specs, seq,
          moba_mean=None, prev=None):
    rows, d = x2.shape
    tm = ROW_TILE
    per_b = seq // tm
    args, in_specs = [x2], [pl.BlockSpec((tm, d), lambda i: (i, 0))]
    n_prev_o = 0
    if prev is not None:
        o_list, w_out = prev
        n_prev_o = len(o_list)
        args += [ada, *o_list, w_out]
        in_specs.append(pl.BlockSpec((1, 1, 1, 1, d), lambda i: (layer - 1, 2, i // per_b, 0, 0)))
        in_specs += [pl.BlockSpec((tm, o.shape[-1]), lambda i: (i, 0)) for o in o_list]
        in_specs.append(pl.BlockSpec(w_out.shape, lambda i: (0, 0)))
        out_shape = [jax.ShapeDtypeStruct((rows, d), F32)] + list(out_shape)
        out_specs = [pl.BlockSpec((tm, d), lambda i: (i, 0))] + list(out_specs)
    args += [ada, ada, norm_g, cos_t, sin_t, w]
    in_specs += [
        pl.BlockSpec((1, 1, 1, 1, d), lambda i: (layer, 0, i // per_b, 0, 0)),
        pl.BlockSpec((1, 1, 1, 1, d), lambda i: (layer, 1, i // per_b, 0, 0)),
        pl.BlockSpec((1, d), lambda i: (0, 0)),
        pl.BlockSpec((tm, PAIR), lambda i: (i % per_b, 0)),
        pl.BlockSpec((tm, PAIR), lambda i: (i % per_b, 0)),
        pl.BlockSpec(w.shape, lambda i: (0, 0)),
    ]
    return pl.pallas_call(
        functools.partial(_proj_kernel, groups, groups_t, t_cols, moba_mean, n_prev_o),
        grid=(rows // tm,),
        in_specs=in_specs,
        out_specs=out_specs,
        out_shape=out_shape,
        compiler_params=_cparams(("arbitrary",)),
        name="proj",
    )(*args)


def _out_kernel(n_o, final, *refs):
    x_ref, gate_ref = refs[0], refs[1]
    o_refs = refs[2:2 + n_o]
    w_ref = refs[2 + n_o]
    fg_ref = refs[3 + n_o]
    out_ref = refs[4 + n_o]
    y = None
    k0 = 0
    for o_ref in o_refs:
        kw = o_ref.shape[-1]
        part = _dot(o_ref[...], w_ref[k0:k0 + kw, :])
        y = part if y is None else y + part
        k0 += kw
    xn = x_ref[...] + gate_ref[0, 0, 0] * y
    if final:
        xn = (xn * lax.rsqrt(jnp.mean(xn * xn, axis=-1, keepdims=True) + RMS_EPS)) * fg_ref[...]
    out_ref[...] = xn


def _out_proj(x2, ada, layer, o_list, w_out, final_g, final, seq):
    rows, d = x2.shape
    tm = ROW_TILE
    per_b = seq // tm
    n_o = len(o_list)
    in_specs = [
        pl.BlockSpec((tm, d), lambda i: (i, 0)),
        pl.BlockSpec((1, 1, 1, 1, d), lambda i: (layer, 2, i // per_b, 0, 0)),
    ]
    in_specs += [pl.BlockSpec((tm, o.shape[-1]), lambda i: (i, 0)) for o in o_list]
    in_specs += [pl.BlockSpec(w_out.shape, lambda i: (0, 0)), pl.BlockSpec((1, d), lambda i: (0, 0))]
    return pl.pallas_call(
        functools.partial(_out_kernel, n_o, final),
        grid=(rows // tm,),
        in_specs=in_specs,
        out_specs=pl.BlockSpec((tm, d), lambda i: (i, 0)),
        out_shape=jax.ShapeDtypeStruct((rows, d), F32),
        compiler_params=_cparams(("arbitrary",)),
        name="out_proj",
    )(x2, ada, *o_list, w_out, final_g)


def _stack_heads(q_pairs):
    row = lax.broadcasted_iota(jnp.int32, (PAIR, 1), 0)
    m_lo = (row < HEAD_DIM).astype(F32)
    m_hi = (row >= HEAD_DIM).astype(F32)
    blocks = []
    for p in range(q_pairs.shape[-1] // PAIR):
        qp_t = q_pairs[:, p * PAIR:(p + 1) * PAIR].astype(F32).T
        blocks += [qp_t * m_lo, qp_t * m_hi]
    return jnp.concatenate(blocks, axis=1).astype(BF16)


def _flash_init(d_rows, n_cols):
    return jnp.full((1, n_cols), NEG, F32), jnp.zeros((1, n_cols), F32), jnp.zeros((d_rows, n_cols), F32)


def _flash(carry, s_t, v_t_chunks):
    m, l, acc = carry
    m_new = jnp.maximum(m, jnp.max(s_t, axis=0, keepdims=True))
    alpha = jnp.exp2(m - m_new)
    e = jnp.exp2(s_t - m_new).astype(BF16)
    l = alpha * l + _dot(jnp.ones((SUM_ROWS, e.shape[0]), BF16), e)[0:1]
    acc, k0 = alpha * acc, 0
    for vt in v_t_chunks:
        acc = acc + _dot(vt, e[k0:k0 + vt.shape[1]])
        k0 += vt.shape[1]
    return m_new, l, acc


def _topn_rows(score, valid, n_cand, topn):
    ridx = lax.broadcasted_iota(jnp.int32, score.shape, 0)
    rank = jnp.zeros(score.shape, F32)
    for jp in range(n_cand):
        row = score[jp:jp + 1, :]
        earlier = jnp.where(ridx > jp, 1.0, 0.0)
        rank = rank + jnp.where(row > score, 1.0, jnp.where(row == score, earlier, 0.0))
    return valid & (rank < topn)


def _swa_kernel(tq, nqb, sink_ref, q_ref, k_ref, vt_ref, z_ref, o_ref):
    step = pl.program_id(1)
    n_prev = -(-(A_WINDOW - 1) // tq)
    krow = lax.broadcasted_iota(jnp.int32, (tq, tq), 0)
    qcol = lax.broadcasted_iota(jnp.int32, (tq, tq), 1)
    causal = _tile_lanes(jnp.where(krow <= qcol, 0.0, NEG), A_HEADS)
    edge = _tile_lanes(jnp.where(n_prev * tq + qcol - krow < A_WINDOW, 0.0, NEG), A_HEADS)
    ones = jnp.ones((SUM_ROWS, tq), BF16)
    sink = jnp.concatenate([jnp.full((1, tq), sink_ref[hd] * LOG2E, F32) for hd in range(A_HEADS)], axis=1)

    blocks = [step * nqb + bb for bb in range(nqb)]
    qs = [_stack_heads(q_ref[0, bb * tq:(bb + 1) * tq, :]) for bb in range(nqb)]
    scores = []
    for bb, i in enumerate(blocks):
        row = []
        for j in range(n_prev + 1):
            off = pl.multiple_of(jnp.maximum(i - j, 0) * tq, tq)
            s_t = _dot(k_ref[0, pl.ds(off, tq), :], qs[bb])
            s_t = s_t + causal if j == 0 else (s_t + edge if j == n_prev else s_t)
            row.append(s_t if j == 0 else s_t + jnp.where(i - j >= 0, 0.0, NEG))
        scores.append(row)

    outs = []
    for bb, i in enumerate(blocks):
        m = sink
        for s_t in scores[bb]:
            m = jnp.maximum(m, jnp.max(s_t, axis=0, keepdims=True))
        den, acc = jnp.exp2(sink - m), None
        for j, s_t in enumerate(scores[bb]):
            v_ext = jnp.concatenate([vt_ref[0, jnp.maximum(i - j, 0)], ones], axis=0)
            res = _dot(v_ext, jnp.exp2(s_t - m).astype(BF16))
            den = den + res[HEAD_DIM:HEAD_DIM + 1]
            acc = res[:HEAD_DIM] if acc is None else acc + res[:HEAD_DIM]
        o_t = acc * (1.0 / den)
        heads = jnp.concatenate([o_t[:, hd * tq:(hd + 1) * tq] for hd in range(A_HEADS)], axis=0)
        outs.append(heads.T)
    o_ref[0] = (jnp.concatenate(outs, axis=0) * z_ref[0]).astype(o_ref.dtype)


def _swa(sinks, q, kr, vt, z, tq):
    nb, seq, _ = q.shape
    aw = A_HEADS * HEAD_DIM
    nqb = SWA_BLOCKS_PER_STEP
    rows = nqb * tq
    return pl.pallas_call(
        functools.partial(_swa_kernel, tq, nqb),
        grid=(nb, seq // rows),
        in_specs=[
            pl.BlockSpec(memory_space=pltpu.SMEM),
            pl.BlockSpec((1, rows, aw), lambda b, i: (b, i, 0)),
            pl.BlockSpec((1, seq, PAIR), lambda b, i: (b, 0, 0)),
            pl.BlockSpec((1, seq // tq, HEAD_DIM, tq), lambda b, i: (b, 0, 0, 0)),
            pl.BlockSpec((1, rows, aw), lambda b, i: (b, i, 0)),
        ],
        out_specs=pl.BlockSpec((1, rows, aw), lambda b, i: (b, i, 0)),
        out_shape=jax.ShapeDtypeStruct((nb, seq, aw), BF16),
        compiler_params=_cparams(("arbitrary", "arbitrary")),
        name="swa",
    )(sinks, q, kr, vt, z)


def _cmp_hidden(x_ref, pe_ref, w1_ref, nch):
    top, bot = None, None
    for l in range(CMP_STRIDE):
        rows = x_ref[0, pl.ds(l, nch, stride=CMP_STRIDE), :]
        t = _dot((rows + pe_ref[l:l + 1, :]).astype(BF16), w1_ref[l])
        b = _dot((rows + pe_ref[CMP_STRIDE + l:CMP_STRIDE + l + 1, :]).astype(BF16), w1_ref[CMP_STRIDE + l])
        top = t if top is None else top + t
        bot = b if bot is None else bot + b
    return jax.nn.gelu(top + pltpu.roll(bot, nch - 1, 0))


def _compress_kernel(kc_ref, vc_ref, pek_ref, w1k_ref, w2k_ref, pev_ref, w1v_ref, w2v_ref, ko_ref, vo_ref):
    nch = ko_ref.shape[2]
    hk = _cmp_hidden(kc_ref, pek_ref, w1k_ref, nch).astype(BF16)
    hv = _cmp_hidden(vc_ref, pev_ref, w1v_ref, nch).astype(BF16)
    for g in range(B_KV):
        gs = slice(g * CMP_HIDDEN, (g + 1) * CMP_HIDDEN)
        ko_ref[0, g] = _dot(hk[:, gs], w2k_ref[...]).astype(ko_ref.dtype)
        vo_ref[0, g] = _dot_nt(w2v_ref[...], hv[:, gs]).astype(vo_ref.dtype)


def _compress(kc, vc, pek, w1k, w2k, pev, w1v, w2v):
    nb, seq, _ = kc.shape
    nch = seq // CMP_STRIDE
    full = lambda a: pl.BlockSpec(a.shape, lambda b: (0,) * a.ndim)
    blk = pl.BlockSpec((1, seq, PAIR), lambda b: (b, 0, 0))
    return pl.pallas_call(
        _compress_kernel,
        grid=(nb,),
        in_specs=[blk, blk, full(pek), full(w1k), full(w2k), full(pev), full(w1v), full(w2v)],
        out_specs=[pl.BlockSpec((1, B_KV, nch, PAIR), lambda b: (b, 0, 0, 0)),
                   pl.BlockSpec((1, B_KV, HEAD_DIM, nch), lambda b: (b, 0, 0, 0))],
        out_shape=[jax.ShapeDtypeStruct((nb, B_KV, nch, PAIR), BF16),
                   jax.ShapeDtypeStruct((nb, B_KV, HEAD_DIM, nch), BF16)],
        compiler_params=_cparams(("arbitrary",)),
        name="compress",
    )(kc, vc, pek, w1k, w2k, pev, w1v, w2v)


def _nsa_kernel(tq, nc, nsb, q_ref, kc_ref, vct_ref, ks0_ref, ks1_ref, vs0_ref, vs1_ref, kw0_ref, kw1_ref,
                vw0_ref, vw1_ref, gt_ref, z_ref, ovl_ref, o_ref, bias_ref):
    i = pl.program_id(1)
    groups = range(B_KV)
    ks_refs, vs_refs = (ks0_ref, ks1_ref), (vs0_ref, vs1_ref)
    kw_refs, vw_refs = (kw0_ref, kw1_ref), (vw0_ref, vw1_ref)
    gw = B_REP * HEAD_DIM
    nr = B_REP * tq
    qs = [_stack_heads(q_ref[0, :, g * gw:(g + 1) * gw]) for g in groups]
    t_row = i * tq + lax.broadcasted_iota(jnp.int32, (1, tq), 1)

    nch = kc_ref.shape[2]
    cidx = lax.broadcasted_iota(jnp.int32, (nch, tq), 0)
    ok_c = (cidx < nc) & (t_row >= cidx * CMP_STRIDE + (CMP_LEN - 1))
    bias_c = _tile_lanes(jnp.where(ok_c, 0.0, NEG), B_REP)
    jr = lax.broadcasted_iota(jnp.int32, (nsb, tq), 0)
    tb = t_row // SLC_LEN
    valid = jr <= tb
    forced = (jr == 0) | (jr == tb) | (jr == tb - 1)
    n_prev = -(-(WIN_LEN - 1) // tq)
    blk_per_chunk = tq // SLC_LEN
    krow = lax.broadcasted_iota(jnp.int32, (tq, tq), 0)
    qcol = lax.broadcasted_iota(jnp.int32, (tq, tq), 1)
    causal = jnp.where(krow <= qcol, 0.0, NEG)
    edge = _tile_lanes(jnp.where(n_prev * tq + qcol - krow < WIN_LEN, 0.0, NEG), B_REP)
    causal_r = _tile_lanes(causal, B_REP)
    ones = jnp.ones((SUM_ROWS, tq), BF16)

    def win_chunk(j):
        return jnp.maximum(i - j, 0)

    def win_scores(g, j):
        off = pl.multiple_of(win_chunk(j) * tq, tq)
        s_t = _dot(kw_refs[g][0, pl.ds(off, tq), :], qs[g])
        return s_t + causal_r if j == 0 else (s_t + edge if j == n_prev else s_t)

    def slc_bias(g, c, diagonal):
        rows = [jnp.broadcast_to(bias_ref[g, pl.ds(c * blk_per_chunk + bb, 1), :], (SLC_LEN, tq))
                for bb in range(blk_per_chunk)]
        b = jnp.concatenate(rows, axis=0)
        return _tile_lanes(b + causal if diagonal else b, B_REP)

    def slc_qk(g, c):
        off = pl.multiple_of(c * tq, tq)
        return _dot(ks_refs[g][0, pl.ds(off, tq), :], qs[g])

    def sums_and_pv(v_t, e):
        res = _dot(jnp.concatenate([v_t, ones], axis=0), e)
        return res[HEAD_DIM:HEAD_DIM + 1], res[:HEAD_DIM]

    def masked_ref(m0, present):
        return m0 + jnp.where(present, 0.0, -NEG)

    s_cmp = [_dot(kc_ref[0, g], qs[g]) + bias_c for g in groups]
    s_win0 = [win_scores(g, 0) for g in groups]
    s_diag = [slc_qk(g, i) for g in groups]

    o_cmp = []
    for g in groups:
        s = s_cmp[g]
        m = jnp.max(s, axis=0, keepdims=True)
        m = jnp.where(m > 0.5 * NEG, m, 0.0)
        e = jnp.exp2(s - m)
        den = jnp.sum(e, axis=0, keepdims=True)
        p_t = (e * (1.0 / jnp.where(den > 0, den, 1.0))).astype(BF16)
        res = _dot(jnp.concatenate([vct_ref[0, g], ovl_ref[...]], axis=0), p_t)
        o_cmp.append(res[:HEAD_DIM])
        imp = res[HEAD_DIM:, 0:tq]
        for r in range(1, B_REP):
            imp = imp + res[HEAD_DIM:, r * tq:(r + 1) * tq]
        score = jnp.where(valid, jnp.where(forced, FORCE_BONUS, imp), -jnp.inf)
        bias_ref[g] = jnp.where(_topn_rows(score, valid, nsb, SLC_TOPN), 0.0, NEG)

    def write_out(slc, win):
        heads = []
        for g in groups:
            o_slc = slc[g][1] * (1.0 / slc[g][0])
            o_win = win[g][1] * (1.0 / win[g][0])
            gt = gt_ref[0, g * GATE_ROWS:(g + 1) * GATE_ROWS]
            for r in range(B_REP):
                cs = slice(r * tq, (r + 1) * tq)
                heads.append(gt[r:r + 1] * o_cmp[g][:, cs] + gt[B_REP + r:B_REP + r + 1] * o_slc[:, cs]
                             + gt[2 * B_REP + r:2 * B_REP + r + 1] * o_win[:, cs])
        o_t = jnp.concatenate(heads, axis=0)
        o_ref[0] = (o_t.T * z_ref[0]).astype(o_ref.dtype)

    win_m0 = [s_win0[g][0:1, :] for g in groups]
    win = [list(sums_and_pv(vw_refs[g][0, i], jnp.exp2(s_win0[g] - win_m0[g]).astype(BF16))) for g in groups]
    pending = []

    def finish_win(g, j, arg):
        l_c, pv = sums_and_pv(vw_refs[g][0, win_chunk(j)], jnp.exp2(arg).astype(BF16))
        win[g][0], win[g][1] = win[g][0] + l_c, win[g][1] + pv

    for j in range(1, n_prev + 1):
        for g in groups:
            pending.append((g, j, win_scores(g, j) - masked_ref(win_m0[g], i - j >= 0)))
            if len(pending) > MXU_LAG:
                finish_win(*pending.pop(0))
    while pending:
        finish_win(*pending.pop(0))

    s_diag = [s_diag[g] + slc_bias(g, i, True) for g in groups]
    own_blk = _tile_lanes(lax.broadcasted_iota(jnp.int32, (1, tq), 1) // SLC_LEN, B_REP)
    slc_m0 = []
    for g in groups:
        ref = s_diag[g][0:1, :]
        for bb in range(1, blk_per_chunk):
            ref = jnp.where(own_blk == bb, s_diag[g][bb * SLC_LEN:bb * SLC_LEN + 1, :], ref)
        slc_m0.append(ref)
    slc0 = tuple(sums_and_pv(vs_refs[g][0, i], jnp.exp2(s_diag[g] - slc_m0[g]).astype(BF16)) for g in groups)

    def fast_pair(u, sums):
        sums = [list(x) for x in sums]
        waiting = []

        def finish(g, c, arg):
            l_c, pv = sums_and_pv(vs_refs[g][0, c], jnp.exp2(arg).astype(BF16))
            sums[g][0], sums[g][1] = sums[g][0] + l_c, sums[g][1] + pv

        for c in (2 * u, 2 * u + 1):
            for g in groups:
                waiting.append((g, c, slc_qk(g, c) + slc_bias(g, c, False) - masked_ref(slc_m0[g], c < i)))
                if len(waiting) > MXU_LAG:
                    finish(*waiting.pop(0))
        while waiting:
            finish(*waiting.pop(0))
        return tuple(tuple(x) for x in sums)

    slc = lax.fori_loop(0, (i + 1) // 2, fast_pair, slc0)
    write_out(slc, win)
    worst = jnp.maximum(jnp.maximum(slc[0][0], slc[1][0]), jnp.maximum(win[0][0], win[1][0]))
    safe = jnp.max(worst) < SAFE_SUM

    @pl.when(jnp.logical_not(safe))
    def _():
        span = (n_prev + 1) * tq
        c0 = jnp.maximum(i - n_prev, 0)
        start = pl.multiple_of(c0 * tq, tq)
        diff = t_row - (start + lax.broadcasted_iota(jnp.int32, (span, tq), 0))
        bias_w = _tile_lanes(jnp.where(diff >= 0, jnp.where(diff < WIN_LEN, 0.0, NEG), NEG), B_REP)
        win_x = []
        for g in groups:
            sw = _dot(kw_refs[g][0, pl.ds(start, span), :], qs[g]) + bias_w
            _, l_w, acc_w = _flash(_flash_init(HEAD_DIM, nr), sw,
                                   [vw_refs[g][0, c0 + cc] for cc in range(n_prev + 1)])
            win_x.append((l_w, acc_w))

        def exact_pair(u, carries):
            out = []
            for g in groups:
                carry = carries[g]
                for c in (2 * u, 2 * u + 1):
                    s_t = slc_qk(g, c) + slc_bias(g, c, False) + jnp.where(c < i, 0.0, NEG)
                    carry = _flash(carry, s_t, [vs_refs[g][0, c]])
                out.append(carry)
            return tuple(out)

        first = tuple(_flash(_flash_init(HEAD_DIM, nr), s_diag[g], [vs_refs[g][0, i]]) for g in groups)
        exact = lax.fori_loop(0, (i + 1) // 2, exact_pair, first)
        write_out([(l, acc) for (_, l, acc) in exact], win_x)


def _nsa(q, kcmp, vcmp_t, kr, vt, gates_t, z, ovl_t, nc, nsb, tq):
    nb, seq, _ = q.shape
    bw = B_HEADS * HEAD_DIM
    nch = kcmp.shape[2]
    kblk = lambda col: pl.BlockSpec((1, seq, PAIR), lambda b, i: (b, 0, col))
    vblk = lambda row: pl.BlockSpec((1, seq // tq, HEAD_DIM, tq), lambda b, i: (b, 0, row, 0))
    return pl.pallas_call(
        functools.partial(_nsa_kernel, tq, nc, nsb),
        grid=(nb, seq // tq),
        in_specs=[
            pl.BlockSpec((1, tq, bw), lambda b, i: (b, i, 1)),
            pl.BlockSpec((1, B_KV, nch, PAIR), lambda b, i: (b, 0, 0, 0)),
            pl.BlockSpec((1, B_KV, HEAD_DIM, nch), lambda b, i: (b, 0, 0, 0)),
            kblk(1), kblk(2), vblk(0), vblk(1), kblk(3), kblk(4), vblk(2), vblk(3),
            pl.BlockSpec((1, B_KV * GATE_ROWS, tq), lambda b, i: (b, 0, i)),
            pl.BlockSpec((1, tq, bw), lambda b, i: (b, i, 1)),
            pl.BlockSpec(ovl_t.shape, lambda b, i: (0, 0)),
        ],
        out_specs=pl.BlockSpec((1, tq, bw), lambda b, i: (b, i, 0)),
        out_shape=jax.ShapeDtypeStruct((nb, seq, bw), BF16),
        scratch_shapes=[pltpu.VMEM((B_KV, nsb, tq), F32)],
        compiler_params=_cparams(("arbitrary", "arbitrary")),
        name="nsa",
    )(q, kcmp, vcmp_t, kr, kr, vt, vt, kr, kr, vt, vt, gates_t, z, ovl_t)


def _moba_kernel(nblk, npair, q_ref, k_ref, vt_ref, km_ref, z_ref, o_ref, bias_ref):
    i = pl.program_id(2)
    tq = MOBA_BLOCK
    nr = 2 * tq
    ncand = bias_ref.shape[1]
    pairs = range(npair)
    lanes = [slice(p * PAIR, (p + 1) * PAIR) for p in pairs]
    qs = [_stack_heads(q_ref[0, :, lanes[p]]) for p in pairs]

    gs = []
    for p in pairs:
        km = km_ref[0, :, lanes[p]]
        pad = -nblk % SUM_ROWS
        km = jnp.concatenate([km, jnp.zeros((pad, PAIR), F32)], axis=0).astype(BF16)
        gs.append(_dot(km, qs[p])[:ncand])

    krow = lax.broadcasted_iota(jnp.int32, (tq, tq), 0)
    qcol = lax.broadcasted_iota(jnp.int32, (tq, tq), 1)
    causal = _tile_lanes(jnp.where(krow <= qcol, 0.0, NEG), 2)
    ones = jnp.ones((SUM_ROWS, tq), BF16)

    def scores(p, blk, bias):
        off = pl.multiple_of(blk * tq, tq)
        return _dot(k_ref[0, pl.ds(off, tq), lanes[p]], qs[p]) + bias

    def past_scores(p, blk):
        return scores(p, blk, bias_ref[p, pl.ds(blk, 1), :])

    def weights_times_v(e, p, blk):
        res = []
        for hh in range(2):
            v_ext = jnp.concatenate([vt_ref[0, blk, pl.ds(p * PAIR + hh * HEAD_DIM, HEAD_DIM), :], ones], axis=0)
            res.append(_dot(v_ext, e[:, hh * tq:(hh + 1) * tq]))
        res = jnp.concatenate(res, axis=1)
        return res[HEAD_DIM:HEAD_DIM + 1], res[:HEAD_DIM]

    def flash(carry, s_t, p, blk):
        m, l, acc = carry
        m_new = jnp.maximum(m, jnp.max(s_t, axis=0, keepdims=True))
        alpha = jnp.exp2(m - m_new)
        l_c, pv = weights_times_v(jnp.exp2(s_t - m_new).astype(BF16), p, blk)
        return m_new, alpha * l + l_c, alpha * acc + pv

    def write_out(results):
        outs = []
        for p in pairs:
            l_f, acc_f = results[p]
            o_t = acc_f * (1.0 / l_f)
            outs += [o_t[:, :tq], o_t[:, tq:]]
        o_ref[0] = (jnp.concatenate(outs, axis=0).T * z_ref[0]).astype(o_ref.dtype)

    s_own = [scores(p, i, causal) for p in pairs]
    jr = lax.broadcasted_iota(jnp.int32, (ncand, nr), 0)
    past = jr < i
    for p in pairs:
        sel = _topn_rows(jnp.where(past, gs[p], -jnp.inf), past, nblk, MOBA_TOPK)
        bias_ref[p] = jnp.where(sel, 0.0, NEG)
    own = []
    for p in pairs:
        m0 = s_own[p][0:1, :]
        l0, acc0 = weights_times_v(jnp.exp2(s_own[p] - m0).astype(BF16), p, i)
        own.append((m0, l0, acc0))

    def fast_blocks(blks, sums):
        sums = [list(x) for x in sums]
        pending = []

        def finish(p, blk, s_t):
            l_c, pv = weights_times_v(jnp.exp2(s_t - own[p][0]).astype(BF16), p, blk)
            sums[p][0] = sums[p][0] + l_c
            sums[p][1] = sums[p][1] + pv

        for blk in blks:
            for p in pairs:
                pending.append((p, blk, past_scores(p, blk)))
                if len(pending) > MXU_LAG:
                    finish(*pending.pop(0))
        while pending:
            finish(*pending.pop(0))
        return tuple(tuple(x) for x in sums)

    fast = lax.fori_loop(0, (i + 1) // 2, lambda u, s: fast_blocks((2 * u, 2 * u + 1), s),
                         tuple((l0, acc0) for (_, l0, acc0) in own))
    write_out(fast)
    worst = fast[0][0]
    for p in pairs[1:]:
        worst = jnp.maximum(worst, fast[p][0])
    safe = jnp.max(worst) < SAFE_SUM

    @pl.when(jnp.logical_not(safe))
    def _():
        def exact_block(blk, carries):
            return tuple(flash(carries[p], past_scores(p, blk), p, blk) for p in pairs)

        start = (jnp.full((1, nr), NEG, F32), jnp.zeros((1, nr), F32), jnp.zeros((HEAD_DIM, nr), F32))
        first = tuple(flash(start, scores(p, i, causal), p, i) for p in pairs)
        exact = lax.fori_loop(0, i, exact_block, first)
        write_out([(l, acc) for (_, l, acc) in exact])


def _moba(q, k, vt, kmean, z):
    nb, seq, width = q.shape
    tq = MOBA_BLOCK
    nblk = seq // MOBA_BLOCK
    ncand = -(-nblk // 8) * 8
    npair = MOBA_PAIRS_PER_STEP
    gw = npair * PAIR
    qblk = pl.BlockSpec((1, tq, gw), lambda b, p, i: (b, i, p))
    return pl.pallas_call(
        functools.partial(_moba_kernel, nblk, npair),
        grid=(nb, width // gw, seq // tq),
        in_specs=[
            qblk,
            pl.BlockSpec((1, seq, gw), lambda b, p, i: (b, 0, p)),
            pl.BlockSpec((1, nblk, gw, tq), lambda b, p, i: (b, 0, p, 0)),
            pl.BlockSpec((1, nblk, gw), lambda b, p, i: (b, 0, p)),
            qblk,
        ],
        out_specs=qblk,
        out_shape=jax.ShapeDtypeStruct((nb, seq, width), BF16),
        scratch_shapes=[pltpu.VMEM((npair, ncand, 2 * tq), F32)],
        compiler_params=_cparams(("arbitrary", "arbitrary", "arbitrary")),
        name="moba",
    )(q, k, vt, kmean, z)


def _even_plan(w_in, nb, seq):
    aq, akv = A_HEADS * HEAD_DIM, HEAD_DIM
    bq, bkv, bg = B_HEADS * HEAD_DIM, B_KV * HEAD_DIM, 3 * B_HEADS
    sizes = (aq, akv, akv, aq, bq, bkv, bkv, bkv, bkv, bkv, bkv, bg, bq)
    offs = np.concatenate([[0], np.cumsum(sizes)])
    qa, ka, va, za, qb, kc, vc, ks, vs, kw, vw, gb, zb = [w_in[:, offs[n]:offs[n + 1]] for n in range(len(sizes))]
    qw = aq + bq
    krw = PAIR * (1 + 2 * B_KV)
    gb4 = gb.reshape(-1, B_KV, B_REP, 3).transpose(0, 1, 3, 2).reshape(-1, B_KV, 3 * B_REP)
    gb4 = jnp.pad(gb4, ((0, 0), (0, 0), (0, GATE_ROWS - 3 * B_REP))).reshape(-1, B_KV * GATE_ROWS)
    vrows = akv + 2 * bkv
    t_width = -(-(vrows + B_KV * GATE_ROWS) // PAIR) * PAIR
    t_pad = jnp.zeros((w_in.shape[0], t_width - vrows - B_KV * GATE_ROWS), w_in.dtype)
    kheads = akv + 2 * bkv
    k_width = -(-kheads // PAIR) * PAIR
    k_pad = jnp.zeros((w_in.shape[0], k_width - kheads), w_in.dtype)
    w = jnp.concatenate([qa * (Q_SCALE * LOG2E), qb * (Q_SCALE * LOG2E), ka, ks, kw, k_pad, kc, vc, za, zb,
                         va, vs, vw, gb4, t_pad], axis=1).astype(BF16)
    groups, col = [], 0
    for width, kind in ((qw, "rope"), (k_width, "rope_dup"), (bkv, "rope"), (bkv, "plain"), (qw, "silu")):
        groups.append((col, width, kind, len(groups)))
        col += width
    groups_t = ((0, akv, "plain", 5, SWA_TQ), (akv, 2 * bkv, "plain", 6, NSA_TQ),
                (vrows, B_KV * GATE_ROWS, "sigmoid", 7, None))
    rows, tm = nb * seq, ROW_TILE
    per_b = seq // tm
    widths = [(qw, BF16), (krw, BF16), (bkv, F32), (bkv, F32), (qw, F32)]
    out_shape = [jax.ShapeDtypeStruct((rows, wd), dt) for wd, dt in widths]
    out_specs = [pl.BlockSpec((tm, wd), lambda i: (i, 0)) for wd, _ in widths]
    for n_rows, chunk in ((akv, SWA_TQ), (2 * bkv, NSA_TQ)):
        out_shape.append(jax.ShapeDtypeStruct((nb, seq // chunk, n_rows, chunk), BF16))
        out_specs.append(pl.BlockSpec((1, tm // chunk, n_rows, chunk), lambda i: (i // per_b, i % per_b, 0, 0)))
    out_shape.append(jax.ShapeDtypeStruct((nb, B_KV * GATE_ROWS, seq), F32))
    out_specs.append(pl.BlockSpec((1, B_KV * GATE_ROWS, tm), lambda i: (i // per_b, 0, i % per_b)))
    return w, (col, t_width), tuple(groups), groups_t, out_shape, out_specs


def _odd_plan(w_in, nb, seq):
    cw = C_HEADS * HEAD_DIM
    col_scale = np.ones((1, 4 * cw), np.float32)
    col_scale[:, :cw] = Q_SCALE * LOG2E
    w = (w_in * col_scale).astype(BF16)
    groups = ((0, cw, "rope", 0), (cw, cw, "rope_mean", 1), (3 * cw, cw, "silu", 2))
    groups_t = ((0, cw, "plain", 3, MOBA_BLOCK),)
    rows, tm = nb * seq, ROW_TILE
    per_b = seq // tm
    widths = [(cw, BF16), (cw, BF16), (cw, F32)]
    out_shape = [jax.ShapeDtypeStruct((rows, wd), dt) for wd, dt in widths]
    out_specs = [pl.BlockSpec((tm, wd), lambda i: (i, 0)) for wd, _ in widths]
    out_shape += [jax.ShapeDtypeStruct((nb, seq // MOBA_BLOCK, cw, MOBA_BLOCK), BF16),
                  jax.ShapeDtypeStruct((nb, seq // MOBA_BLOCK, 1, cw), F32)]
    out_specs += [pl.BlockSpec((1, tm // MOBA_BLOCK, cw, MOBA_BLOCK), lambda i: (i // per_b, i % per_b, 0, 0)),
                  pl.BlockSpec((1, tm // MOBA_BLOCK, 1, cw), lambda i: (i // per_b, i % per_b, 0, 0))]
    return w, (2 * cw, cw), groups, groups_t, out_shape, out_specs


def _rope_tables(seq):
    inv = ROPE_THETA ** (-jnp.arange(0, HEAD_DIM, 2, dtype=F32) / HEAD_DIM)
    ang = jnp.arange(seq, dtype=F32)[:, None] * inv[None, :]
    cos, sin = jnp.cos(ang), jnp.sin(ang)
    reps = PAIR // (HEAD_DIM // 2)
    sign = np.tile(np.concatenate([-np.ones(HEAD_DIM // 2), np.ones(HEAD_DIM // 2)]), PAIR // HEAD_DIM)
    return jnp.tile(cos, (1, reps)), jnp.tile(sin, (1, reps)) * jnp.asarray(sign, F32)[None, :]


def _overlap_matrix_t(seq, nch):
    nc = (seq - CMP_LEN) // CMP_STRIDE + 1
    nsb = seq // SLC_LEN
    cst = np.arange(nc) * CMP_STRIDE
    jj = np.arange(nsb)
    ov = ((cst[None, :] < (jj[:, None] + 1) * SLC_LEN) & (cst[None, :] + CMP_LEN > jj[:, None] * SLC_LEN))
    full = np.zeros((nsb, nch), np.float32)
    full[:, :nc] = ov
    return jnp.asarray(full, BF16), nc, nsb


def _cmp_weights(pe, w1, w2, transpose_out):
    pe2 = jnp.concatenate([pe, pe], axis=1)
    w1t = w1.reshape(CMP_LEN, HEAD_DIM, CMP_HIDDEN)
    zeros = jnp.zeros_like(w1t)
    w1p = jnp.concatenate([jnp.concatenate([w1t, zeros], axis=2), jnp.concatenate([zeros, w1t], axis=2)], axis=1)
    w2o = w2.T if transpose_out else jnp.concatenate([w2, w2], axis=1)
    return pe2, w1p.astype(BF16), w2o.astype(BF16)


def kernel(x, c, w_ada, b_ada, norm_g, w_in_even, a_sinks, cmp_pe_k, cmp_w1_k, cmp_w2_k, cmp_pe_v, cmp_w1_v, cmp_w2_v, w_out_even, w_in_odd, w_out_odd, final_g):
    nb, seq, d = x.shape
    depth = w_ada.shape[0]
    assert seq % ROW_TILE == 0 and seq % MOBA_BLOCK == 0 and NSA_TQ % SLC_LEN == 0
    assert (seq // SLC_LEN) % 8 == 0 and seq % (SWA_TQ * SWA_BLOCKS_PER_STEP) == 0 and ROW_TILE % NSA_TQ == 0
    cos_t, sin_t = _rope_tables(seq)
    ada = _ada(c, w_ada, b_ada)
    nch = seq // CMP_STRIDE
    ovl_t, nc, nsb = _overlap_matrix_t(seq, nch)
    fg = final_g.reshape(1, d)
    x2 = x.reshape(nb * seq, d)
    r3 = lambda a: a.reshape(nb, seq, a.shape[-1])
    prev = None
    for layer in range(depth):
        li = layer // 2
        ng = norm_g[layer].reshape(1, d)
        plan = _even_plan(w_in_even[li], nb, seq) if layer % 2 == 0 else _odd_plan(w_in_odd[li], nb, seq)
        w, t_cols, groups, groups_t, out_shape, out_specs = plan
        outs = _proj(x2, ada, layer, ng, cos_t, sin_t, w, t_cols, groups, groups_t, out_shape, out_specs, seq,
                     moba_mean=None if layer % 2 == 0 else 4, prev=prev)
        if prev is not None:
            x2, outs = outs[0], outs[1:]
        if layer % 2 == 0:
            q, kr, kc, vc, z, vt_a, vt_b, gates_t = outs
            q, kr, kc, vc, z = map(r3, (q, kr, kc, vc, z))
            kcmp, vcmp_t = _compress(kc, vc, *_cmp_weights(cmp_pe_k[li], cmp_w1_k[li], cmp_w2_k[li], False),
                                     *_cmp_weights(cmp_pe_v[li], cmp_w1_v[li], cmp_w2_v[li], True))
            oa = _swa(a_sinks[li], q, kr, vt_a, z, SWA_TQ)
            ob = _nsa(q, kcmp, vcmp_t, kr, vt_b, gates_t, z, ovl_t, nc, nsb, NSA_TQ)
            o_list = [oa.reshape(nb * seq, -1), ob.reshape(nb * seq, -1)]
            w_out = w_out_even[li].astype(BF16)
        else:
            q, k, z, vt, kmean = outs
            o = _moba(r3(q), r3(k), vt, kmean.reshape(nb, seq // MOBA_BLOCK, -1), r3(z))
            o_list = [o.reshape(nb * seq, -1)]
            w_out = w_out_odd[li].astype(BF16)
        prev = (o_list, w_out)
    x2 = _out_proj(x2, ada, depth - 1, prev[0], prev[1], fg, True, seq)
    return x2.reshape(nb, seq, d)
```

```python
import functools

import numpy as np
import jax
import jax.numpy as jnp
from jax import lax
from jax.experimental import pallas as pl
from jax.experimental.pallas import tpu as pltpu

D_MODEL = 1024
HEAD_DIM = 64
PAIR = 2 * HEAD_DIM
ROPE_THETA = 10000.0
RMS_EPS = 1e-6
A_HEADS = 8
A_WINDOW = 128
B_HEADS = 8
B_KV = 2
B_REP = B_HEADS // B_KV
CMP_LEN = 32
CMP_STRIDE = 16
CMP_HIDDEN = 256
SLC_LEN = 64
SLC_TOPN = 8
WIN_LEN = 512
FORCE_BONUS = 1e4
C_HEADS = 16
MOBA_BLOCK = 256
MOBA_TOPK = 3
Q_SCALE = HEAD_DIM ** -0.5
LOG2E = 1.4426950408889634

NEG = -1e30
SAFE_SUM = 2.0 ** 64
MXU_COLS = 256
ROW_TILE = 1024
SWA_TQ = 128
NSA_TQ = 256
SWA_BLOCKS_PER_STEP = 8
MOBA_PAIRS_PER_STEP = 8
MXU_LAG = 3
SUM_ROWS = 16
GATE_ROWS = 16
VMEM_LIMIT = 60 * 1024 * 1024

BF16 = jnp.bfloat16
F32 = jnp.float32


def _dot_nt(a, b):
    return lax.dot_general(a, b, (((1,), (1,)), ((), ())), preferred_element_type=F32)


def _dot(a, b):
    return jnp.dot(a, b, preferred_element_type=F32)


def _cparams(sem):
    return pltpu.CompilerParams(dimension_semantics=sem, vmem_limit_bytes=VMEM_LIMIT)


def _tile_lanes(a, n):
    return jnp.concatenate([a] * n, axis=1) if n > 1 else a


def _ada_kernel(c_ref, *refs):
    w_refs, b_ref, o_ref = refs[:-2], refs[-2], refs[-1]
    c = c_ref[...]
    ca = (c * jax.nn.sigmoid(c)).astype(BF16)
    for j, w_ref in enumerate(w_refs):
        o_ref[0, j] = _dot(ca, w_ref[0].astype(BF16)) + b_ref[0, j]


def _ada(c, w_ada, b_ada):
    depth, d, _ = w_ada.shape
    nb = c.shape[0]
    b4 = b_ada.reshape(depth, 3, 1, d)
    w_specs = [pl.BlockSpec((1, d, d), functools.partial(lambda l, j: (l, 0, j), j=j)) for j in range(3)]
    out = pl.pallas_call(
        _ada_kernel,
        grid=(depth,),
        in_specs=[pl.BlockSpec((nb, d), lambda l: (0, 0)), *w_specs,
                  pl.BlockSpec((1, 3, 1, d), lambda l: (l, 0, 0, 0))],
        out_specs=pl.BlockSpec((1, 3, nb, d), lambda l: (l, 0, 0, 0)),
        out_shape=jax.ShapeDtypeStruct((depth, 3, nb, d), F32),
        compiler_params=_cparams(("arbitrary",)),
        name="ada",
    )(c, w_ada, w_ada, w_ada, b4)
    return out.reshape(depth, 3, nb, 1, d)


def _rope(a, cos, sin_signed):
    w = a.shape[-1]
    lane = lax.broadcasted_iota(jnp.int32, a.shape, 1)
    first_half = (lane % HEAD_DIM) < (HEAD_DIM // 2)
    partner = jnp.where(first_half, pltpu.roll(a, w - HEAD_DIM // 2, 1), pltpu.roll(a, HEAD_DIM // 2, 1))
    return a * cos + partner * sin_signed


def _proj_kernel(groups, groups_t, t_cols, moba_mean, n_prev_o, *refs):
    x_ref = refs[0]
    prev_refs, refs = refs[1:1 + (n_prev_o + 2 if n_prev_o else 0)], refs[1 + (n_prev_o + 2 if n_prev_o else 0):]
    shift_ref, scale_ref, g_ref, cos_ref, sin_ref, w_ref = refs[:6]
    out_refs = refs[6:]
    t_start, t_width = t_cols

    x = x_ref[...]
    if n_prev_o:
        gate_ref, wout_ref = prev_refs[0], prev_refs[-1]
        mix, k0 = None, 0
        for o_ref in prev_refs[1:-1]:
            kw = o_ref.shape[-1]
            part = _dot(o_ref[...], wout_ref[k0:k0 + kw, :])
            mix = part if mix is None else mix + part
            k0 += kw
        x = x + gate_ref[0, 0, 0] * mix
        out_refs[0][...] = x
        out_refs = out_refs[1:]
    y = x * lax.rsqrt(jnp.mean(x * x, axis=-1, keepdims=True) + RMS_EPS)
    h = (y * g_ref[...]) * (1.0 + scale_ref[0, 0, 0]) + shift_ref[0, 0, 0]
    hb = h.astype(BF16)
    tm = hb.shape[0]
    cos, sin = cos_ref[...], sin_ref[...]
    lo_half = lax.broadcasted_iota(jnp.int32, (tm, PAIR), 1) < HEAD_DIM
    pair_plan = {}
    for (w_start, width, kind, out_idx) in groups:
        for c in range(0, width, PAIR):
            pair_plan[(w_start + c) // PAIR] = (kind, out_idx, c)
    n_cols = w_ref.shape[1]
    for s0 in range(0, n_cols, MXU_COLS):
        sw = min(MXU_COLS, n_cols - s0)
        acc = _dot(hb, w_ref[:, s0:s0 + sw])
        for h0 in range(0, sw, PAIR):
            col = s0 + h0
            a = acc[:, h0:h0 + PAIR]
            if t_start <= col < t_start + t_width:
                f0 = col - t_start
                a_t = a.T
                for (r_start, n_rows, kind, out_idx, chunk) in groups_t:
                    lo, hi = max(r_start, f0), min(r_start + n_rows, f0 + PAIR)
                    if lo >= hi:
                        continue
                    o_ref = out_refs[out_idx]
                    part = a_t[lo - f0:hi - f0]
                    if kind == "sigmoid":
                        o_ref[0, lo - r_start:hi - r_start, :] = jax.nn.sigmoid(part)
                    else:
                        for jj in range(tm // chunk):
                            o_ref[0, jj, lo - r_start:hi - r_start, :] = (
                                part[:, jj * chunk:(jj + 1) * chunk].astype(o_ref.dtype))
                continue
            kind, out_idx, c = pair_plan[col // PAIR]
            o_ref = out_refs[out_idx]
            if kind in ("rope", "rope_mean", "rope_dup"):
                a = _rope(a, cos, sin)
            elif kind == "silu":
                a = a * jax.nn.sigmoid(a)
            if kind == "rope_dup":
                b = pltpu.roll(a, HEAD_DIM, 1)
                for hh, dup in enumerate((jnp.where(lo_half, a, b), jnp.where(lo_half, b, a))):
                    dst = (2 * (c // PAIR) + hh) * PAIR
                    if dst < o_ref.shape[1]:
                        o_ref[:, dst:dst + PAIR] = dup.astype(o_ref.dtype)
                continue
            o_ref[:, c:c + PAIR] = a.astype(o_ref.dtype)
            if kind == "rope_mean":
                km_ref = out_refs[moba_mean]
                for j in range(tm // MOBA_BLOCK):
                    blk = a[j * MOBA_BLOCK:(j + 1) * MOBA_BLOCK]
                    km_ref[0, j, :, c:c + PAIR] = jnp.sum(blk, axis=0, keepdims=True) * (1.0 / MOBA_BLOCK)


def _proj(x2, ada, layer, norm_g, cos_t, sin_t, w, t_cols, groups, groups_t, out_shape, out_specs, seq,
          moba_mean=None, prev=None):
    rows, d = x2.shape
    tm = ROW_TILE
    per_b = seq // tm
    args, in_specs = [x2], [pl.BlockSpec((tm, d), lambda i: (i, 0))]
    n_prev_o = 0
    if prev is not None:
        o_list, w_out = prev
        n_prev_o = len(o_list)
        args += [ada, *o_list, w_out]
        in_specs.append(pl.BlockSpec((1, 1, 1, 1, d), lambda i: (layer - 1, 2, i // per_b, 0, 0)))
        in_specs += [pl.BlockSpec((tm, o.shape[-1]), lambda i: (i, 0)) for o in o_list]
        in_specs.append(pl.BlockSpec(w_out.shape, lambda i: (0, 0)))
        out_shape = [jax.ShapeDtypeStruct((rows, d), F32)] + list(out_shape)
        out_specs = [pl.BlockSpec((tm, d), lambda i: (i, 0))] + list(out_specs)
    args += [ada, ada, norm_g, cos_t, sin_t, w]
    in_specs += [
        pl.BlockSpec((1, 1, 1, 1, d), lambda i: (layer, 0, i // per_b, 0, 0)),
        pl.BlockSpec((1, 1, 1, 1, d), lambda i: (layer, 1, i // per_b, 0, 0)),
        pl.BlockSpec((1, d), lambda i: (0, 0)),
        pl.BlockSpec((tm, PAIR), lambda i: (i % per_b, 0)),
        pl.BlockSpec((tm, PAIR), lambda i: (i % per_b, 0)),
        pl.BlockSpec(w.shape, lambda i: (0, 0)),
    ]
    return pl.pallas_call(
        functools.partial(_proj_kernel, groups, groups_t, t_cols, moba_mean, n_prev_o),
        grid=(rows // tm,),
        in_specs=in_specs,
        out_specs=out_specs,
        out_shape=out_shape,
        compiler_params=_cparams(("arbitrary",)),
        name="proj",
    )(*args)


def _out_kernel(n_o, final, *refs):
    x_ref, gate_ref = refs[0], refs[1]
    o_refs = refs[2:2 + n_o]
    w_ref = refs[2 + n_o]
    fg_ref = refs[3 + n_o]
    out_ref = refs[4 + n_o]
    y = None
    k0 = 0
    for o_ref in o_refs:
        kw = o_ref.shape[-1]
        part = _dot(o_ref[...], w_ref[k0:k0 + kw, :])
        y = part if y is None else y + part
        k0 += kw
    xn = x_ref[...] + gate_ref[0, 0, 0] * y
    if final:
        xn = (xn * lax.rsqrt(jnp.mean(xn * xn, axis=-1, keepdims=True) + RMS_EPS)) * fg_ref[...]
    out_ref[...] = xn


def _out_proj(x2, ada, layer, o_list, w_out, final_g, final, seq):
    rows, d = x2.shape
    tm = ROW_TILE
    per_b = seq // tm
    n_o = len(o_list)
    in_specs = [
        pl.BlockSpec((tm, d), lambda i: (i, 0)),
        pl.BlockSpec((1, 1, 1, 1, d), lambda i: (layer, 2, i // per_b, 0, 0)),
    ]
    in_specs += [pl.BlockSpec((tm, o.shape[-1]), lambda i: (i, 0)) for o in o_list]
    in_specs += [pl.BlockSpec(w_out.shape, lambda i: (0, 0)), pl.BlockSpec((1, d), lambda i: (0, 0))]
    return pl.pallas_call(
        functools.partial(_out_kernel, n_o, final),
        grid=(rows // tm,),
        in_specs=in_specs,
        out_specs=pl.BlockSpec((tm, d), lambda i: (i, 0)),
        out_shape=jax.ShapeDtypeStruct((rows, d), F32),
        compiler_params=_cparams(("arbitrary",)),
        name="out_proj",
    )(x2, ada, *o_list, w_out, final_g)


def _stack_heads(q_pairs):
    row = lax.broadcasted_iota(jnp.int32, (PAIR, 1), 0)
    m_lo = (row < HEAD_DIM).astype(F32)
    m_hi = (row >= HEAD_DIM).astype(F32)
    blocks = []
    for p in range(q_pairs.shape[-1] // PAIR):
        qp_t = q_pairs[:, p * PAIR:(p + 1) * PAIR].astype(F32).T
        blocks += [qp_t * m_lo, qp_t * m_hi]
    return jnp.concatenate(blocks, axis=1).astype(BF16)


def _flash_init(d_rows, n_cols):
    return jnp.full((1, n_cols), NEG, F32), jnp.zeros((1, n_cols), F32), jnp.zeros((d_rows, n_cols), F32)


def _flash(carry, s_t, v_t_chunks):
    m, l, acc = carry
    m_new = jnp.maximum(m, jnp.max(s_t, axis=0, keepdims=True))
    alpha = jnp.exp2(m - m_new)
    e = jnp.exp2(s_t - m_new).astype(BF16)
    l = alpha * l + _dot(jnp.ones((SUM_ROWS, e.shape[0]), BF16), e)[0:1]
    acc, k0 = alpha * acc, 0
    for vt in v_t_chunks:
        acc = acc + _dot(vt, e[k0:k0 + vt.shape[1]])
        k0 += vt.shape[1]
    return m_new, l, acc


def _topn_rows(score, valid, n_cand, topn):
    ridx = lax.broadcasted_iota(jnp.int32, score.shape, 0)
    rank = jnp.zeros(score.shape, F32)
    for jp in range(n_cand):
        row = score[jp:jp + 1, :]
        earlier = jnp.where(ridx > jp, 1.0, 0.0)
        rank = rank + jnp.where(row > score, 1.0, jnp.where(row == score, earlier, 0.0))
    return valid & (rank < topn)


def _swa_kernel(tq, nqb, sink_ref, q_ref, k_ref, vt_ref, z_ref, o_ref):
    step = pl.program_id(1)
    n_prev = -(-(A_WINDOW - 1) // tq)
    krow = lax.broadcasted_iota(jnp.int32, (tq, tq), 0)
    qcol = lax.broadcasted_iota(jnp.int32, (tq, tq), 1)
    causal = _tile_lanes(jnp.where(krow <= qcol, 0.0, NEG), A_HEADS)
    edge = _tile_lanes(jnp.where(n_prev * tq + qcol - krow < A_WINDOW, 0.0, NEG), A_HEADS)
    ones = jnp.ones((SUM_ROWS, tq), BF16)
    sink = jnp.concatenate([jnp.full((1, tq), sink_ref[hd] * LOG2E, F32) for hd in range(A_HEADS)], axis=1)

    blocks = [step * nqb + bb for bb in range(nqb)]
    qs = [_stack_heads(q_ref[0, bb * tq:(bb + 1) * tq, :]) for bb in range(nqb)]
    scores = []
    for bb, i in enumerate(blocks):
        row = []
        for j in range(n_prev + 1):
            off = pl.multiple_of(jnp.maximum(i - j, 0) * tq, tq)
            s_t = _dot(k_ref[0, pl.ds(off, tq), :], qs[bb])
            s_t = s_t + causal if j == 0 else (s_t + edge if j == n_prev else s_t)
            row.append(s_t if j == 0 else s_t + jnp.where(i - j >= 0, 0.0, NEG))
        scores.append(row)

    outs = []
    for bb, i in enumerate(blocks):
        m = sink
        for s_t in scores[bb]:
            m = jnp.maximum(m, jnp.max(s_t, axis=0, keepdims=True))
        den, acc = jnp.exp2(sink - m), None
        for j, s_t in enumerate(scores[bb]):
            v_ext = jnp.concatenate([vt_ref[0, jnp.maximum(i - j, 0)], ones], axis=0)
            res = _dot(v_ext, jnp.exp2(s_t - m).astype(BF16))
            den = den + res[HEAD_DIM:HEAD_DIM + 1]
            acc = res[:HEAD_DIM] if acc is None else acc + res[:HEAD_DIM]
        o_t = acc * (1.0 / den)
        heads = jnp.concatenate([o_t[:, hd * tq:(hd + 1) * tq] for hd in range(A_HEADS)], axis=0)
        outs.append(heads.T)
    o_ref[0] = (jnp.concatenate(outs, axis=0) * z_ref[0]).astype(o_ref.dtype)


def _swa(sinks, q, kr, vt, z, tq):
    nb, seq, _ = q.shape
    aw = A_HEADS * HEAD_DIM
    nqb = SWA_BLOCKS_PER_STEP
    rows = nqb * tq
    return pl.pallas_call(
        functools.partial(_swa_kernel, tq, nqb),
        grid=(nb, seq // rows),
        in_specs=[
            pl.BlockSpec(memory_space=pltpu.SMEM),
            pl.BlockSpec((1, rows, aw), lambda b, i: (b, i, 0)),
            pl.BlockSpec((1, seq, PAIR), lambda b, i: (b, 0, 0)),
            pl.BlockSpec((1, seq // tq, HEAD_DIM, tq), lambda b, i: (b, 0, 0, 0)),
            pl.BlockSpec((1, rows, aw), lambda b, i: (b, i, 0)),
        ],
        out_specs=pl.BlockSpec((1, rows, aw), lambda b, i: (b, i, 0)),
        out_shape=jax.ShapeDtypeStruct((nb, seq, aw), BF16),
        compiler_params=_cparams(("arbitrary", "arbitrary")),
        name="swa",
    )(sinks, q, kr, vt, z)


def _cmp_hidden(x_ref, pe_ref, w1_ref, nch):
    top, bot = None, None
    for l in range(CMP_STRIDE):
        rows = x_ref[0, pl.ds(l, nch, stride=CMP_STRIDE), :]
        t = _dot((rows + pe_ref[l:l + 1, :]).astype(BF16), w1_ref[l])
        b = _dot((rows + pe_ref[CMP_STRIDE + l:CMP_STRIDE + l + 1, :]).astype(BF16), w1_ref[CMP_STRIDE + l])
        top = t if top is None else top + t
        bot = b if bot is None else bot + b
    return jax.nn.gelu(top + pltpu.roll(bot, nch - 1, 0))


def _compress_kernel(kc_ref, vc_ref, pek_ref, w1k_ref, w2k_ref, pev_ref, w1v_ref, w2v_ref, ko_ref, vo_ref):
    nch = ko_ref.shape[2]
    hk = _cmp_hidden(kc_ref, pek_ref, w1k_ref, nch).astype(BF16)
    hv = _cmp_hidden(vc_ref, pev_ref, w1v_ref, nch).astype(BF16)
    for g in range(B_KV):
        gs = slice(g * CMP_HIDDEN, (g + 1) * CMP_HIDDEN)
        ko_ref[0, g] = _dot(hk[:, gs], w2k_ref[...]).astype(ko_ref.dtype)
        vo_ref[0, g] = _dot_nt(w2v_ref[...], hv[:, gs]).astype(vo_ref.dtype)


def _compress(kc, vc, pek, w1k, w2k, pev, w1v, w2v):
    nb, seq, _ = kc.shape
    nch = seq // CMP_STRIDE
    full = lambda a: pl.BlockSpec(a.shape, lambda b: (0,) * a.ndim)
    blk = pl.BlockSpec((1, seq, PAIR), lambda b: (b, 0, 0))
    return pl.pallas_call(
        _compress_kernel,
        grid=(nb,),
        in_specs=[blk, blk, full(pek), full(w1k), full(w2k), full(pev), full(w1v), full(w2v)],
        out_specs=[pl.BlockSpec((1, B_KV, nch, PAIR), lambda b: (b, 0, 0, 0)),
                   pl.BlockSpec((1, B_KV, HEAD_DIM, nch), lambda b: (b, 0, 0, 0))],
        out_shape=[jax.ShapeDtypeStruct((nb, B_KV, nch, PAIR), BF16),
                   jax.ShapeDtypeStruct((nb, B_KV, HEAD_DIM, nch), BF16)],
        compiler_params=_cparams(("arbitrary",)),
        name="compress",
    )(kc, vc, pek, w1k, w2k, pev, w1v, w2v)


def _nsa_kernel(tq, nc, nsb, q_ref, kc_ref, vct_ref, ks0_ref, ks1_ref, vs0_ref, vs1_ref, kw0_ref, kw1_ref,
                vw0_ref, vw1_ref, gt_ref, z_ref, ovl_ref, o_ref, bias_ref):
    i = pl.program_id(1)
    groups = range(B_KV)
    ks_refs, vs_refs = (ks0_ref, ks1_ref), (vs0_ref, vs1_ref)
    kw_refs, vw_refs = (kw0_ref, kw1_ref), (vw0_ref, vw1_ref)
    gw = B_REP * HEAD_DIM
    nr = B_REP * tq
    qs = [_stack_heads(q_ref[0, :, g * gw:(g + 1) * gw]) for g in groups]
    t_row = i * tq + lax.broadcasted_iota(jnp.int32, (1, tq), 1)

    nch = kc_ref.shape[2]
    cidx = lax.broadcasted_iota(jnp.int32, (nch, tq), 0)
    ok_c = (cidx < nc) & (t_row >= cidx * CMP_STRIDE + (CMP_LEN - 1))
    bias_c = _tile_lanes(jnp.where(ok_c, 0.0, NEG), B_REP)
    jr = lax.broadcasted_iota(jnp.int32, (nsb, tq), 0)
    tb = t_row // SLC_LEN
    valid = jr <= tb
    forced = (jr == 0) | (jr == tb) | (jr == tb - 1)
    n_prev = -(-(WIN_LEN - 1) // tq)
    blk_per_chunk = tq // SLC_LEN
    krow = lax.broadcasted_iota(jnp.int32, (tq, tq), 0)
    qcol = lax.broadcasted_iota(jnp.int32, (tq, tq), 1)
    causal = jnp.where(krow <= qcol, 0.0, NEG)
    edge = _tile_lanes(jnp.where(n_prev * tq + qcol - krow < WIN_LEN, 0.0, NEG), B_REP)
    causal_r = _tile_lanes(causal, B_REP)
    ones = jnp.ones((SUM_ROWS, tq), BF16)

    def win_chunk(j):
        return jnp.maximum(i - j, 0)

    def win_scores(g, j):
        off = pl.multiple_of(win_chunk(j) * tq, tq)
        s_t = _dot(kw_refs[g][0, pl.ds(off, tq), :], qs[g])
        return s_t + causal_r if j == 0 else (s_t + edge if j == n_prev else s_t)

    def slc_bias(g, c, diagonal):
        rows = [jnp.broadcast_to(bias_ref[g, pl.ds(c * blk_per_chunk + bb, 1), :], (SLC_LEN, tq))
                for bb in range(blk_per_chunk)]
        b = jnp.concatenate(rows, axis=0)
        return _tile_lanes(b + causal if diagonal else b, B_REP)

    def slc_qk(g, c):
        off = pl.multiple_of(c * tq, tq)
        return _dot(ks_refs[g][0, pl.ds(off, tq), :], qs[g])

    def sums_and_pv(v_t, e):
        res = _dot(jnp.concatenate([v_t, ones], axis=0), e)
        return res[HEAD_DIM:HEAD_DIM + 1], res[:HEAD_DIM]

    def masked_ref(m0, present):
        return m0 + jnp.where(present, 0.0, -NEG)

    s_cmp = [_dot(kc_ref[0, g], qs[g]) + bias_c for g in groups]
    s_win0 = [win_scores(g, 0) for g in groups]
    s_diag = [slc_qk(g, i) for g in groups]

    o_cmp = []
    for g in groups:
        s = s_cmp[g]
        m = jnp.max(s, axis=0, keepdims=True)
        m = jnp.where(m > 0.5 * NEG, m, 0.0)
        e = jnp.exp2(s - m)
        den = jnp.sum(e, axis=0, keepdims=True)
        p_t = (e * (1.0 / jnp.where(den > 0, den, 1.0))).astype(BF16)
        res = _dot(jnp.concatenate([vct_ref[0, g], ovl_ref[...]], axis=0), p_t)
        o_cmp.append(res[:HEAD_DIM])
        imp = res[HEAD_DIM:, 0:tq]
        for r in range(1, B_REP):
            imp = imp + res[HEAD_DIM:, r * tq:(r + 1) * tq]
        score = jnp.where(valid, jnp.where(forced, FORCE_BONUS, imp), -jnp.inf)
        bias_ref[g] = jnp.where(_topn_rows(score, valid, nsb, SLC_TOPN), 0.0, NEG)

    def write_out(slc, win):
        heads = []
        for g in groups:
            o_slc = slc[g][1] * (1.0 / slc[g][0])
            o_win = win[g][1] * (1.0 / win[g][0])
            gt = gt_ref[0, g * GATE_ROWS:(g + 1) * GATE_ROWS]
            for r in range(B_REP):
                cs = slice(r * tq, (r + 1) * tq)
                heads.append(gt[r:r + 1] * o_cmp[g][:, cs] + gt[B_REP + r:B_REP + r + 1] * o_slc[:, cs]
                             + gt[2 * B_REP + r:2 * B_REP + r + 1] * o_win[:, cs])
        o_t = jnp.concatenate(heads, axis=0)
        o_ref[0] = (o_t.T * z_ref[0]).astype(o_ref.dtype)

    win_m0 = [s_win0[g][0:1, :] for g in groups]
    win = [list(sums_and_pv(vw_refs[g][0, i], jnp.exp2(s_win0[g] - win_m0[g]).astype(BF16))) for g in groups]
    pending = []

    def finish_win(g, j, arg):
        l_c, pv = sums_and_pv(vw_refs[g][0, win_chunk(j)], jnp.exp2(arg).astype(BF16))
        win[g][0], win[g][1] = win[g][0] + l_c, win[g][1] + pv

    for j in range(1, n_prev + 1):
        for g in groups:
            pending.append((g, j, win_scores(g, j) - masked_ref(win_m0[g], i - j >= 0)))
            if len(pending) > MXU_LAG:
                finish_win(*pending.pop(0))
    while pending:
        finish_win(*pending.pop(0))

    own_blk = _tile_lanes(lax.broadcasted_iota(jnp.int32, (1, tq), 1) // SLC_LEN, B_REP)
    slc_m0 = []
    for g in groups:
        ref = s_diag[g][0:1, :]
        for bb in range(1, blk_per_chunk):
            ref = jnp.where(own_blk == bb, s_diag[g][bb * SLC_LEN:bb * SLC_LEN + 1, :], ref)
        slc_m0.append(ref)

    def slc_arg(g, c, raw, ref_row, diagonal):
        parts = []
        for bb in range(blk_per_chunk):
            rows = slice(bb * SLC_LEN, (bb + 1) * SLC_LEN)
            shift = _tile_lanes(bias_ref[g, pl.ds(c * blk_per_chunk + bb, 1), :], B_REP) - ref_row
            parts.append(raw[rows] + shift + causal_r[rows] if diagonal else raw[rows] + shift)
        return jnp.concatenate(parts, axis=0)

    slc0 = tuple(sums_and_pv(vs_refs[g][0, i],
                             jnp.exp2(slc_arg(g, i, s_diag[g], slc_m0[g], True)).astype(BF16)) for g in groups)

    def fast_pair(u, sums):
        sums = [list(x) for x in sums]
        waiting = []

        def finish(g, c, arg):
            l_c, pv = sums_and_pv(vs_refs[g][0, c], jnp.exp2(arg).astype(BF16))
            sums[g][0], sums[g][1] = sums[g][0] + l_c, sums[g][1] + pv

        for c in (2 * u, 2 * u + 1):
            for g in groups:
                waiting.append((g, c, slc_arg(g, c, slc_qk(g, c), masked_ref(slc_m0[g], c < i), False)))
                if len(waiting) > MXU_LAG:
                    finish(*waiting.pop(0))
        while waiting:
            finish(*waiting.pop(0))
        return tuple(tuple(x) for x in sums)

    slc = lax.fori_loop(0, (i + 1) // 2, fast_pair, slc0)
    write_out(slc, win)
    worst = jnp.maximum(jnp.maximum(slc[0][0], slc[1][0]), jnp.maximum(win[0][0], win[1][0]))
    safe = jnp.max(worst) < SAFE_SUM

    @pl.when(jnp.logical_not(safe))
    def _():
        span = (n_prev + 1) * tq
        c0 = jnp.maximum(i - n_prev, 0)
        start = pl.multiple_of(c0 * tq, tq)
        diff = t_row - (start + lax.broadcasted_iota(jnp.int32, (span, tq), 0))
        bias_w = _tile_lanes(jnp.where(diff >= 0, jnp.where(diff < WIN_LEN, 0.0, NEG), NEG), B_REP)
        win_x = []
        for g in groups:
            sw = _dot(kw_refs[g][0, pl.ds(start, span), :], qs[g]) + bias_w
            _, l_w, acc_w = _flash(_flash_init(HEAD_DIM, nr), sw,
                                   [vw_refs[g][0, c0 + cc] for cc in range(n_prev + 1)])
            win_x.append((l_w, acc_w))

        def exact_pair(u, carries):
            out = []
            for g in groups:
                carry = carries[g]
                for c in (2 * u, 2 * u + 1):
                    s_t = slc_qk(g, c) + slc_bias(g, c, False) + jnp.where(c < i, 0.0, NEG)
                    carry = _flash(carry, s_t, [vs_refs[g][0, c]])
                out.append(carry)
            return tuple(out)

        first = tuple(_flash(_flash_init(HEAD_DIM, nr), slc_qk(g, i) + slc_bias(g, i, True), [vs_refs[g][0, i]])
                      for g in groups)
        exact = lax.fori_loop(0, (i + 1) // 2, exact_pair, first)
        write_out([(l, acc) for (_, l, acc) in exact], win_x)


def _nsa(q, kcmp, vcmp_t, kr, vt, gates_t, z, ovl_t, nc, nsb, tq):
    nb, seq, _ = q.shape
    bw = B_HEADS * HEAD_DIM
    nch = kcmp.shape[2]
    kblk = lambda col: pl.BlockSpec((1, seq, PAIR), lambda b, i: (b, 0, col))
    vblk = lambda row: pl.BlockSpec((1, seq // tq, HEAD_DIM, tq), lambda b, i: (b, 0, row, 0))
    return pl.pallas_call(
        functools.partial(_nsa_kernel, tq, nc, nsb),
        grid=(nb, seq // tq),
        in_specs=[
            pl.BlockSpec((1, tq, bw), lambda b, i: (b, i, 1)),
            pl.BlockSpec((1, B_KV, nch, PAIR), lambda b, i: (b, 0, 0, 0)),
            pl.BlockSpec((1, B_KV, HEAD_DIM, nch), lambda b, i: (b, 0, 0, 0)),
            kblk(1), kblk(2), vblk(0), vblk(1), kblk(3), kblk(4), vblk(2), vblk(3),
            pl.BlockSpec((1, B_KV * GATE_ROWS, tq), lambda b, i: (b, 0, i)),
            pl.BlockSpec((1, tq, bw), lambda b, i: (b, i, 1)),
            pl.BlockSpec(ovl_t.shape, lambda b, i: (0, 0)),
        ],
        out_specs=pl.BlockSpec((1, tq, bw), lambda b, i: (b, i, 0)),
        out_shape=jax.ShapeDtypeStruct((nb, seq, bw), BF16),
        scratch_shapes=[pltpu.VMEM((B_KV, nsb, tq), F32)],
        compiler_params=_cparams(("arbitrary", "arbitrary")),
        name="nsa",
    )(q, kcmp, vcmp_t, kr, kr, vt, vt, kr, kr, vt, vt, gates_t, z, ovl_t)


def _moba_kernel(nblk, npair, q_ref, k_ref, vt_ref, km_ref, z_ref, o_ref, bias_ref):
    i = pl.program_id(2)
    tq = MOBA_BLOCK
    nr = 2 * tq
    ncand = bias_ref.shape[1]
    pairs = range(npair)
    lanes = [slice(p * PAIR, (p + 1) * PAIR) for p in pairs]
    qs = [_stack_heads(q_ref[0, :, lanes[p]]) for p in pairs]

    gs = []
    for p in pairs:
        km = km_ref[0, :, lanes[p]]
        pad = -nblk % SUM_ROWS
        km = jnp.concatenate([km, jnp.zeros((pad, PAIR), F32)], axis=0).astype(BF16)
        gs.append(_dot(km, qs[p])[:ncand])

    krow = lax.broadcasted_iota(jnp.int32, (tq, tq), 0)
    qcol = lax.broadcasted_iota(jnp.int32, (tq, tq), 1)
    causal = _tile_lanes(jnp.where(krow <= qcol, 0.0, NEG), 2)
    ones = jnp.ones((SUM_ROWS, tq), BF16)

    def scores(p, blk, bias):
        off = pl.multiple_of(blk * tq, tq)
        return _dot(k_ref[0, pl.ds(off, tq), lanes[p]], qs[p]) + bias

    def past_scores(p, blk):
        return scores(p, blk, bias_ref[p, pl.ds(blk, 1), :])

    def weights_times_v(e, p, blk):
        res = []
        for hh in range(2):
            v_ext = jnp.concatenate([vt_ref[0, blk, pl.ds(p * PAIR + hh * HEAD_DIM, HEAD_DIM), :], ones], axis=0)
            res.append(_dot(v_ext, e[:, hh * tq:(hh + 1) * tq]))
        res = jnp.concatenate(res, axis=1)
        return res[HEAD_DIM:HEAD_DIM + 1], res[:HEAD_DIM]

    def flash(carry, s_t, p, blk):
        m, l, acc = carry
        m_new = jnp.maximum(m, jnp.max(s_t, axis=0, keepdims=True))
        alpha = jnp.exp2(m - m_new)
        l_c, pv = weights_times_v(jnp.exp2(s_t - m_new).astype(BF16), p, blk)
        return m_new, alpha * l + l_c, alpha * acc + pv

    def write_out(results):
        outs = []
        for p in pairs:
            l_f, acc_f = results[p]
            o_t = acc_f * (1.0 / l_f)
            outs += [o_t[:, :tq], o_t[:, tq:]]
        o_ref[0] = (jnp.concatenate(outs, axis=0).T * z_ref[0]).astype(o_ref.dtype)

    s_own = [scores(p, i, causal) for p in pairs]
    jr = lax.broadcasted_iota(jnp.int32, (ncand, nr), 0)
    past = jr < i
    for p in pairs:
        sel = _topn_rows(jnp.where(past, gs[p], -jnp.inf), past, nblk, MOBA_TOPK)
        bias_ref[p] = jnp.where(sel, 0.0, NEG)
    own = []
    for p in pairs:
        m0 = s_own[p][0:1, :]
        l0, acc0 = weights_times_v(jnp.exp2(s_own[p] - m0).astype(BF16), p, i)
        own.append((m0, l0, acc0))

    def fast_blocks(blks, sums):
        sums = [list(x) for x in sums]
        pending = []

        def finish(p, blk, s_t):
            l_c, pv = weights_times_v(jnp.exp2(s_t).astype(BF16), p, blk)
            sums[p][0] = sums[p][0] + l_c
            sums[p][1] = sums[p][1] + pv

        for blk in blks:
            for p in pairs:
                pending.append((p, blk, scores(p, blk, bias_ref[p, pl.ds(blk, 1), :] - own[p][0])))
                if len(pending) > MXU_LAG:
                    finish(*pending.pop(0))
        while pending:
            finish(*pending.pop(0))
        return tuple(tuple(x) for x in sums)

    fast = lax.fori_loop(0, (i + 1) // 2, lambda u, s: fast_blocks((2 * u, 2 * u + 1), s),
                         tuple((l0, acc0) for (_, l0, acc0) in own))
    write_out(fast)
    worst = fast[0][0]
    for p in pairs[1:]:
        worst = jnp.maximum(worst, fast[p][0])
    safe = jnp.max(worst) < SAFE_SUM

    @pl.when(jnp.logical_not(safe))
    def _():
        def exact_block(blk, carries):
            return tuple(flash(carries[p], past_scores(p, blk), p, blk) for p in pairs)

        start = (jnp.full((1, nr), NEG, F32), jnp.zeros((1, nr), F32), jnp.zeros((HEAD_DIM, nr), F32))
        first = tuple(flash(start, scores(p, i, causal), p, i) for p in pairs)
        exact = lax.fori_loop(0, i, exact_block, first)
        write_out([(l, acc) for (_, l, acc) in exact])


def _moba(q, k, vt, kmean, z):
    nb, seq, width = q.shape
    tq = MOBA_BLOCK
    nblk = seq // MOBA_BLOCK
    ncand = -(-nblk // 8) * 8
    npair = MOBA_PAIRS_PER_STEP
    gw = npair * PAIR
    qblk = pl.BlockSpec((1, tq, gw), lambda b, p, i: (b, i, p))
    return pl.pallas_call(
        functools.partial(_moba_kernel, nblk, npair),
        grid=(nb, width // gw, seq // tq),
        in_specs=[
            qblk,
            pl.BlockSpec((1, seq, gw), lambda b, p, i: (b, 0, p)),
            pl.BlockSpec((1, nblk, gw, tq), lambda b, p, i: (b, 0, p, 0)),
            pl.BlockSpec((1, nblk, gw), lambda b, p, i: (b, 0, p)),
            qblk,
        ],
        out_specs=qblk,
        out_shape=jax.ShapeDtypeStruct((nb, seq, width), BF16),
        scratch_shapes=[pltpu.VMEM((npair, ncand, 2 * tq), F32)],
        compiler_params=_cparams(("arbitrary", "arbitrary", "arbitrary")),
        name="moba",
    )(q, k, vt, kmean, z)


def _even_plan(w_in, nb, seq):
    aq, akv = A_HEADS * HEAD_DIM, HEAD_DIM
    bq, bkv, bg = B_HEADS * HEAD_DIM, B_KV * HEAD_DIM, 3 * B_HEADS
    sizes = (aq, akv, akv, aq, bq, bkv, bkv, bkv, bkv, bkv, bkv, bg, bq)
    offs = np.concatenate([[0], np.cumsum(sizes)])
    qa, ka, va, za, qb, kc, vc, ks, vs, kw, vw, gb, zb = [w_in[:, offs[n]:offs[n + 1]] for n in range(len(sizes))]
    qw = aq + bq
    krw = PAIR * (1 + 2 * B_KV)
    gb4 = gb.reshape(-1, B_KV, B_REP, 3).transpose(0, 1, 3, 2).reshape(-1, B_KV, 3 * B_REP)
    gb4 = jnp.pad(gb4, ((0, 0), (0, 0), (0, GATE_ROWS - 3 * B_REP))).reshape(-1, B_KV * GATE_ROWS)
    vrows = akv + 2 * bkv
    t_width = -(-(vrows + B_KV * GATE_ROWS) // PAIR) * PAIR
    t_pad = jnp.zeros((w_in.shape[0], t_width - vrows - B_KV * GATE_ROWS), w_in.dtype)
    kheads = akv + 2 * bkv
    k_width = -(-kheads // PAIR) * PAIR
    k_pad = jnp.zeros((w_in.shape[0], k_width - kheads), w_in.dtype)
    w = jnp.concatenate([qa * (Q_SCALE * LOG2E), qb * (Q_SCALE * LOG2E), ka, ks, kw, k_pad, kc, vc, za, zb,
                         va, vs, vw, gb4, t_pad], axis=1).astype(BF16)
    groups, col = [], 0
    for width, kind in ((qw, "rope"), (k_width, "rope_dup"), (bkv, "rope"), (bkv, "plain"), (qw, "silu")):
        groups.append((col, width, kind, len(groups)))
        col += width
    groups_t = ((0, akv, "plain", 5, SWA_TQ), (akv, 2 * bkv, "plain", 6, NSA_TQ),
                (vrows, B_KV * GATE_ROWS, "sigmoid", 7, None))
    rows, tm = nb * seq, ROW_TILE
    per_b = seq // tm
    widths = [(qw, BF16), (krw, BF16), (bkv, F32), (bkv, F32), (qw, F32)]
    out_shape = [jax.ShapeDtypeStruct((rows, wd), dt) for wd, dt in widths]
    out_specs = [pl.BlockSpec((tm, wd), lambda i: (i, 0)) for wd, _ in widths]
    for n_rows, chunk in ((akv, SWA_TQ), (2 * bkv, NSA_TQ)):
        out_shape.append(jax.ShapeDtypeStruct((nb, seq // chunk, n_rows, chunk), BF16))
        out_specs.append(pl.BlockSpec((1, tm // chunk, n_rows, chunk), lambda i: (i // per_b, i % per_b, 0, 0)))
    out_shape.append(jax.ShapeDtypeStruct((nb, B_KV * GATE_ROWS, seq), F32))
    out_specs.append(pl.BlockSpec((1, B_KV * GATE_ROWS, tm), lambda i: (i // per_b, 0, i % per_b)))
    return w, (col, t_width), tuple(groups), groups_t, out_shape, out_specs


def _odd_plan(w_in, nb, seq):
    cw = C_HEADS * HEAD_DIM
    col_scale = np.ones((1, 4 * cw), np.float32)
    col_scale[:, :cw] = Q_SCALE * LOG2E
    w = (w_in * col_scale).astype(BF16)
    groups = ((0, cw, "rope", 0), (cw, cw, "rope_mean", 1), (3 * cw, cw, "silu", 2))
    groups_t = ((0, cw, "plain", 3, MOBA_BLOCK),)
    rows, tm = nb * seq, ROW_TILE
    per_b = seq // tm
    widths = [(cw, BF16), (cw, BF16), (cw, F32)]
    out_shape = [jax.ShapeDtypeStruct((rows, wd), dt) for wd, dt in widths]
    out_specs = [pl.BlockSpec((tm, wd), lambda i: (i, 0)) for wd, _ in widths]
    out_shape += [jax.ShapeDtypeStruct((nb, seq // MOBA_BLOCK, cw, MOBA_BLOCK), BF16),
                  jax.ShapeDtypeStruct((nb, seq // MOBA_BLOCK, 1, cw), F32)]
    out_specs += [pl.BlockSpec((1, tm // MOBA_BLOCK, cw, MOBA_BLOCK), lambda i: (i // per_b, i % per_b, 0, 0)),
                  pl.BlockSpec((1, tm // MOBA_BLOCK, 1, cw), lambda i: (i // per_b, i % per_b, 0, 0))]
    return w, (2 * cw, cw), groups, groups_t, out_shape, out_specs


def _rope_tables(seq):
    inv = ROPE_THETA ** (-jnp.arange(0, HEAD_DIM, 2, dtype=F32) / HEAD_DIM)
    ang = jnp.arange(seq, dtype=F32)[:, None] * inv[None, :]
    cos, sin = jnp.cos(ang), jnp.sin(ang)
    reps = PAIR // (HEAD_DIM // 2)
    sign = np.tile(np.concatenate([-np.ones(HEAD_DIM // 2), np.ones(HEAD_DIM // 2)]), PAIR // HEAD_DIM)
    return jnp.tile(cos, (1, reps)), jnp.tile(sin, (1, reps)) * jnp.asarray(sign, F32)[None, :]


def _overlap_matrix_t(seq, nch):
    nc = (seq - CMP_LEN) // CMP_STRIDE + 1
    nsb = seq // SLC_LEN
    cst = np.arange(nc) * CMP_STRIDE
    jj = np.arange(nsb)
    ov = ((cst[None, :] < (jj[:, None] + 1) * SLC_LEN) & (cst[None, :] + CMP_LEN > jj[:, None] * SLC_LEN))
    full = np.zeros((nsb, nch), np.float32)
    full[:, :nc] = ov
    return jnp.asarray(full, BF16), nc, nsb


def _cmp_weights(pe, w1, w2, transpose_out):
    pe2 = jnp.concatenate([pe, pe], axis=1)
    w1t = w1.reshape(CMP_LEN, HEAD_DIM, CMP_HIDDEN)
    zeros = jnp.zeros_like(w1t)
    w1p = jnp.concatenate([jnp.concatenate([w1t, zeros], axis=2), jnp.concatenate([zeros, w1t], axis=2)], axis=1)
    w2o = w2.T if transpose_out else jnp.concatenate([w2, w2], axis=1)
    return pe2, w1p.astype(BF16), w2o.astype(BF16)


def kernel(x, c, w_ada, b_ada, norm_g, w_in_even, a_sinks, cmp_pe_k, cmp_w1_k, cmp_w2_k, cmp_pe_v, cmp_w1_v, cmp_w2_v, w_out_even, w_in_odd, w_out_odd, final_g):
    nb, seq, d = x.shape
    depth = w_ada.shape[0]
    assert seq % ROW_TILE == 0 and seq % MOBA_BLOCK == 0 and NSA_TQ % SLC_LEN == 0
    assert (seq // SLC_LEN) % 8 == 0 and seq % (SWA_TQ * SWA_BLOCKS_PER_STEP) == 0 and ROW_TILE % NSA_TQ == 0
    cos_t, sin_t = _rope_tables(seq)
    ada = _ada(c, w_ada, b_ada)
    nch = seq // CMP_STRIDE
    ovl_t, nc, nsb = _overlap_matrix_t(seq, nch)
    fg = final_g.reshape(1, d)
    x2 = x.reshape(nb * seq, d)
    r3 = lambda a: a.reshape(nb, seq, a.shape[-1])
    prev = None
    for layer in range(depth):
        li = layer // 2
        ng = norm_g[layer].reshape(1, d)
        plan = _even_plan(w_in_even[li], nb, seq) if layer % 2 == 0 else _odd_plan(w_in_odd[li], nb, seq)
        w, t_cols, groups, groups_t, out_shape, out_specs = plan
        outs = _proj(x2, ada, layer, ng, cos_t, sin_t, w, t_cols, groups, groups_t, out_shape, out_specs, seq,
                     moba_mean=None if layer % 2 == 0 else 4, prev=prev)
        if prev is not None:
            x2, outs = outs[0], outs[1:]
        if layer % 2 == 0:
            q, kr, kc, vc, z, vt_a, vt_b, gates_t = outs
            q, kr, kc, vc, z = map(r3, (q, kr, kc, vc, z))
            kcmp, vcmp_t = _compress(kc, vc, *_cmp_weights(cmp_pe_k[li], cmp_w1_k[li], cmp_w2_k[li], False),
                                     *_cmp_weights(cmp_pe_v[li], cmp_w1_v[li], cmp_w2_v[li], True))
            oa = _swa(a_sinks[li], q, kr, vt_a, z, SWA_TQ)
            ob = _nsa(q, kcmp, vcmp_t, kr, vt_b, gates_t, z, ovl_t, nc, nsb, NSA_TQ)
            o_list = [oa.reshape(nb * seq, -1), ob.reshape(nb * seq, -1)]
            w_out = w_out_even[li].astype(BF16)
        else:
            q, k, z, vt, kmean = outs
            o = _moba(r3(q), r3(k), vt, kmean.reshape(nb, seq // MOBA_BLOCK, -1), r3(z))
            o_list = [o.reshape(nb * seq, -1)]
            w_out = w_out_odd[li].astype(BF16)
        prev = (o_list, w_out)
    x2 = _out_proj(x2, ada, depth - 1, prev[0], prev[1], fg, True, seq)
    return x2.reshape(nb, seq, d)
```

```python
import functools

import numpy as np
import jax
import jax.numpy as jnp
from jax import lax
from jax.experimental import pallas as pl
from jax.experimental.pallas import tpu as pltpu

D_MODEL = 1024
HEAD_DIM = 64
PAIR = 2 * HEAD_DIM
ROPE_THETA = 10000.0
RMS_EPS = 1e-6
A_HEADS = 8
A_WINDOW = 128
B_HEADS = 8
B_KV = 2
B_REP = B_HEADS // B_KV
CMP_LEN = 32
CMP_STRIDE = 16
CMP_HIDDEN = 256
SLC_LEN = 64
SLC_TOPN = 8
WIN_LEN = 512
FORCE_BONUS = 1e4
C_HEADS = 16
MOBA_BLOCK = 256
MOBA_TOPK = 3
Q_SCALE = HEAD_DIM ** -0.5
LOG2E = 1.4426950408889634

NEG = -1e30
SAFE_SUM = 2.0 ** 64
MXU_COLS = 256
ROW_TILE = 1024
SWA_TQ = 128
NSA_TQ = 256
SWA_BLOCKS_PER_STEP = 8
MOBA_PAIRS_PER_STEP = 8
MXU_LAG = 3
SUM_ROWS = 16
GATE_ROWS = 16
VMEM_LIMIT = 60 * 1024 * 1024

BF16 = jnp.bfloat16
F32 = jnp.float32


def _dot_nt(a, b):
    return lax.dot_general(a, b, (((1,), (1,)), ((), ())), preferred_element_type=F32)


def _dot(a, b):
    return jnp.dot(a, b, preferred_element_type=F32)


def _cparams(sem):
    return pltpu.CompilerParams(dimension_semantics=sem, vmem_limit_bytes=VMEM_LIMIT)


def _tile_lanes(a, n):
    return jnp.concatenate([a] * n, axis=1) if n > 1 else a


def _ada_kernel(c_ref, *refs):
    w_refs, b_ref, o_ref = refs[:-2], refs[-2], refs[-1]
    c = c_ref[...]
    ca = (c * jax.nn.sigmoid(c)).astype(BF16)
    for j, w_ref in enumerate(w_refs):
        o_ref[0, j] = _dot(ca, w_ref[0].astype(BF16)) + b_ref[0, j]


def _ada(c, w_ada, b_ada):
    depth, d, _ = w_ada.shape
    nb = c.shape[0]
    b4 = b_ada.reshape(depth, 3, 1, d)
    w_specs = [pl.BlockSpec((1, d, d), functools.partial(lambda l, j: (l, 0, j), j=j)) for j in range(3)]
    out = pl.pallas_call(
        _ada_kernel,
        grid=(depth,),
        in_specs=[pl.BlockSpec((nb, d), lambda l: (0, 0)), *w_specs,
                  pl.BlockSpec((1, 3, 1, d), lambda l: (l, 0, 0, 0))],
        out_specs=pl.BlockSpec((1, 3, nb, d), lambda l: (l, 0, 0, 0)),
        out_shape=jax.ShapeDtypeStruct((depth, 3, nb, d), F32),
        compiler_params=_cparams(("arbitrary",)),
        name="ada",
    )(c, w_ada, w_ada, w_ada, b4)
    return out.reshape(depth, 3, nb, 1, d)


def _rope(a, cos, sin_signed):
    w = a.shape[-1]
    lane = lax.broadcasted_iota(jnp.int32, a.shape, 1)
    first_half = (lane % HEAD_DIM) < (HEAD_DIM // 2)
    partner = jnp.where(first_half, pltpu.roll(a, w - HEAD_DIM // 2, 1), pltpu.roll(a, HEAD_DIM // 2, 1))
    return a * cos + partner * sin_signed


def _proj_kernel(groups, groups_t, t_cols, moba_mean, n_prev_o, *refs):
    x_ref = refs[0]
    prev_refs, refs = refs[1:1 + (n_prev_o + 2 if n_prev_o else 0)], refs[1 + (n_prev_o + 2 if n_prev_o else 0):]
    shift_ref, scale_ref, g_ref, cos_ref, sin_ref, w_ref = refs[:6]
    out_refs = refs[6:]
    t_start, t_width = t_cols

    x = x_ref[...]
    if n_prev_o:
        gate_ref, wout_ref = prev_refs[0], prev_refs[-1]
        mix, k0 = None, 0
        for o_ref in prev_refs[1:-1]:
            kw = o_ref.shape[-1]
            part = _dot(o_ref[...], wout_ref[k0:k0 + kw, :])
            mix = part if mix is None else mix + part
            k0 += kw
        x = x + gate_ref[0, 0, 0] * mix
        out_refs[0][...] = x
        out_refs = out_refs[1:]
    y = x * lax.rsqrt(jnp.mean(x * x, axis=-1, keepdims=True) + RMS_EPS)
    h = (y * g_ref[...]) * (1.0 + scale_ref[0, 0, 0]) + shift_ref[0, 0, 0]
    hb = h.astype(BF16)
    tm = hb.shape[0]
    cos, sin = cos_ref[...], sin_ref[...]
    lo_half = lax.broadcasted_iota(jnp.int32, (tm, PAIR), 1) < HEAD_DIM
    pair_plan = {}
    for (w_start, width, kind, out_idx) in groups:
        for c in range(0, width, PAIR):
            pair_plan[(w_start + c) // PAIR] = (kind, out_idx, c)
    n_cols = w_ref.shape[1]
    for s0 in range(0, n_cols, MXU_COLS):
        sw = min(MXU_COLS, n_cols - s0)
        acc = _dot(hb, w_ref[:, s0:s0 + sw])
        for h0 in range(0, sw, PAIR):
            col = s0 + h0
            a = acc[:, h0:h0 + PAIR]
            if t_start <= col < t_start + t_width:
                f0 = col - t_start
                a_t = a.T
                for (r_start, n_rows, kind, out_idx, chunk) in groups_t:
                    lo, hi = max(r_start, f0), min(r_start + n_rows, f0 + PAIR)
                    if lo >= hi:
                        continue
                    o_ref = out_refs[out_idx]
                    part = a_t[lo - f0:hi - f0]
                    if kind == "sigmoid":
                        o_ref[0, lo - r_start:hi - r_start, :] = jax.nn.sigmoid(part)
                    else:
                        for jj in range(tm // chunk):
                            o_ref[0, jj, lo - r_start:hi - r_start, :] = (
                                part[:, jj * chunk:(jj + 1) * chunk].astype(o_ref.dtype))
                continue
            kind, out_idx, c = pair_plan[col // PAIR]
            o_ref = out_refs[out_idx]
            if kind in ("rope", "rope_mean", "rope_dup"):
                a = _rope(a, cos, sin)
            elif kind == "silu":
                a = a * jax.nn.sigmoid(a)
            if kind == "rope_dup":
                b = pltpu.roll(a, HEAD_DIM, 1)
                for hh, dup in enumerate((jnp.where(lo_half, a, b), jnp.where(lo_half, b, a))):
                    dst = (2 * (c // PAIR) + hh) * PAIR
                    if dst < o_ref.shape[1]:
                        o_ref[:, dst:dst + PAIR] = dup.astype(o_ref.dtype)
                continue
            o_ref[:, c:c + PAIR] = a.astype(o_ref.dtype)
            if kind == "rope_mean":
                km_ref = out_refs[moba_mean]
                for j in range(tm // MOBA_BLOCK):
                    blk = a[j * MOBA_BLOCK:(j + 1) * MOBA_BLOCK]
                    km_ref[0, j, :, c:c + PAIR] = jnp.sum(blk, axis=0, keepdims=True) * (1.0 / MOBA_BLOCK)


def _proj(x2, ada, layer, norm_g, cos_t, sin_t, w, t_cols, groups, groups_t, out_shape, out_specs, seq,
          moba_mean=None, prev=None):
    rows, d = x2.shape
    tm = ROW_TILE
    per_b = seq // tm
    args, in_specs = [x2], [pl.BlockSpec((tm, d), lambda i: (i, 0))]
    n_prev_o = 0
    if prev is not None:
        o_list, w_out = prev
        n_prev_o = len(o_list)
        args += [ada, *o_list, w_out]
        in_specs.append(pl.BlockSpec((1, 1, 1, 1, d), lambda i: (layer - 1, 2, i // per_b, 0, 0)))
        in_specs += [pl.BlockSpec((tm, o.shape[-1]), lambda i: (i, 0)) for o in o_list]
        in_specs.append(pl.BlockSpec(w_out.shape, lambda i: (0, 0)))
        out_shape = [jax.ShapeDtypeStruct((rows, d), F32)] + list(out_shape)
        out_specs = [pl.BlockSpec((tm, d), lambda i: (i, 0))] + list(out_specs)
    args += [ada, ada, norm_g, cos_t, sin_t, w]
    in_specs += [
        pl.BlockSpec((1, 1, 1, 1, d), lambda i: (layer, 0, i // per_b, 0, 0)),
        pl.BlockSpec((1, 1, 1, 1, d), lambda i: (layer, 1, i // per_b, 0, 0)),
        pl.BlockSpec((1, d), lambda i: (0, 0)),
        pl.BlockSpec((tm, PAIR), lambda i: (i % per_b, 0)),
        pl.BlockSpec((tm, PAIR), lambda i: (i % per_b, 0)),
        pl.BlockSpec(w.shape, lambda i: (0, 0)),
    ]
    return pl.pallas_call(
        functools.partial(_proj_kernel, groups, groups_t, t_cols, moba_mean, n_prev_o),
        grid=(rows // tm,),
        in_specs=in_specs,
        out_specs=out_specs,
        out_shape=out_shape,
        compiler_params=_cparams(("arbitrary",)),
        name="proj",
    )(*args)


def _out_kernel(n_o, final, *refs):
    x_ref, gate_ref = refs[0], refs[1]
    o_refs = refs[2:2 + n_o]
    w_ref = refs[2 + n_o]
    fg_ref = refs[3 + n_o]
    out_ref = refs[4 + n_o]
    y = None
    k0 = 0
    for o_ref in o_refs:
        kw = o_ref.shape[-1]
        part = _dot(o_ref[...], w_ref[k0:k0 + kw, :])
        y = part if y is None else y + part
        k0 += kw
    xn = x_ref[...] + gate_ref[0, 0, 0] * y
    if final:
        xn = (xn * lax.rsqrt(jnp.mean(xn * xn, axis=-1, keepdims=True) + RMS_EPS)) * fg_ref[...]
    out_ref[...] = xn


def _out_proj(x2, ada, layer, o_list, w_out, final_g, final, seq):
    rows, d = x2.shape
    tm = ROW_TILE
    per_b = seq // tm
    n_o = len(o_list)
    in_specs = [
        pl.BlockSpec((tm, d), lambda i: (i, 0)),
        pl.BlockSpec((1, 1, 1, 1, d), lambda i: (layer, 2, i // per_b, 0, 0)),
    ]
    in_specs += [pl.BlockSpec((tm, o.shape[-1]), lambda i: (i, 0)) for o in o_list]
    in_specs += [pl.BlockSpec(w_out.shape, lambda i: (0, 0)), pl.BlockSpec((1, d), lambda i: (0, 0))]
    return pl.pallas_call(
        functools.partial(_out_kernel, n_o, final),
        grid=(rows // tm,),
        in_specs=in_specs,
        out_specs=pl.BlockSpec((tm, d), lambda i: (i, 0)),
        out_shape=jax.ShapeDtypeStruct((rows, d), F32),
        compiler_params=_cparams(("arbitrary",)),
        name="out_proj",
    )(x2, ada, *o_list, w_out, final_g)


def _stack_heads(q_pairs):
    row = lax.broadcasted_iota(jnp.int32, (PAIR, 1), 0)
    m_lo = (row < HEAD_DIM).astype(F32)
    m_hi = (row >= HEAD_DIM).astype(F32)
    blocks = []
    for p in range(q_pairs.shape[-1] // PAIR):
        qp_t = q_pairs[:, p * PAIR:(p + 1) * PAIR].astype(F32).T
        blocks += [qp_t * m_lo, qp_t * m_hi]
    return jnp.concatenate(blocks, axis=1).astype(BF16)


def _flash_init(d_rows, n_cols):
    return jnp.full((1, n_cols), NEG, F32), jnp.zeros((1, n_cols), F32), jnp.zeros((d_rows, n_cols), F32)


def _flash(carry, s_t, v_t_chunks):
    m, l, acc = carry
    m_new = jnp.maximum(m, jnp.max(s_t, axis=0, keepdims=True))
    alpha = jnp.exp2(m - m_new)
    e = jnp.exp2(s_t - m_new).astype(BF16)
    l = alpha * l + _dot(jnp.ones((SUM_ROWS, e.shape[0]), BF16), e)[0:1]
    acc, k0 = alpha * acc, 0
    for vt in v_t_chunks:
        acc = acc + _dot(vt, e[k0:k0 + vt.shape[1]])
        k0 += vt.shape[1]
    return m_new, l, acc


def _topn_rows(score, valid, n_cand, topn):
    ridx = lax.broadcasted_iota(jnp.int32, score.shape, 0)
    rank = jnp.zeros(score.shape, F32)
    for jp in range(n_cand):
        row = score[jp:jp + 1, :]
        earlier = jnp.where(ridx > jp, 1.0, 0.0)
        rank = rank + jnp.where(row > score, 1.0, jnp.where(row == score, earlier, 0.0))
    return valid & (rank < topn)


def _swa_kernel(tq, nqb, sink_ref, q_ref, k_ref, vt_ref, z_ref, o_ref):
    step = pl.program_id(1)
    n_prev = -(-(A_WINDOW - 1) // tq)
    krow = lax.broadcasted_iota(jnp.int32, (tq, tq), 0)
    qcol = lax.broadcasted_iota(jnp.int32, (tq, tq), 1)
    causal = _tile_lanes(jnp.where(krow <= qcol, 0.0, NEG), A_HEADS)
    edge = _tile_lanes(jnp.where(n_prev * tq + qcol - krow < A_WINDOW, 0.0, NEG), A_HEADS)
    ones = jnp.ones((SUM_ROWS, tq), BF16)
    sink = jnp.concatenate([jnp.full((1, tq), sink_ref[hd] * LOG2E, F32) for hd in range(A_HEADS)], axis=1)

    blocks = [step * nqb + bb for bb in range(nqb)]
    qs = [_stack_heads(q_ref[0, bb * tq:(bb + 1) * tq, :]) for bb in range(nqb)]
    scores = []
    for bb, i in enumerate(blocks):
        row = []
        for j in range(n_prev + 1):
            off = pl.multiple_of(jnp.maximum(i - j, 0) * tq, tq)
            s_t = _dot(k_ref[0, pl.ds(off, tq), :], qs[bb])
            s_t = s_t + causal if j == 0 else (s_t + edge if j == n_prev else s_t)
            row.append(s_t if j == 0 else s_t + jnp.where(i - j >= 0, 0.0, NEG))
        scores.append(row)

    outs = []
    for bb, i in enumerate(blocks):
        m = sink
        for s_t in scores[bb]:
            m = jnp.maximum(m, jnp.max(s_t, axis=0, keepdims=True))
        den, acc = jnp.exp2(sink - m), None
        for j, s_t in enumerate(scores[bb]):
            v_ext = jnp.concatenate([vt_ref[0, jnp.maximum(i - j, 0)], ones], axis=0)
            res = _dot(v_ext, jnp.exp2(s_t - m).astype(BF16))
            den = den + res[HEAD_DIM:HEAD_DIM + 1]
            acc = res[:HEAD_DIM] if acc is None else acc + res[:HEAD_DIM]
        o_t = acc * (1.0 / den)
        heads = jnp.concatenate([o_t[:, hd * tq:(hd + 1) * tq] for hd in range(A_HEADS)], axis=0)
        outs.append(heads.T)
    o_ref[0] = (jnp.concatenate(outs, axis=0) * z_ref[0]).astype(o_ref.dtype)


def _swa(sinks, q, kr, vt, z, tq):
    nb, seq, _ = q.shape
    aw = A_HEADS * HEAD_DIM
    nqb = SWA_BLOCKS_PER_STEP
    rows = nqb * tq
    return pl.pallas_call(
        functools.partial(_swa_kernel, tq, nqb),
        grid=(nb, seq // rows),
        in_specs=[
            pl.BlockSpec(memory_space=pltpu.SMEM),
            pl.BlockSpec((1, rows, aw), lambda b, i: (b, i, 0)),
            pl.BlockSpec((1, seq, PAIR), lambda b, i: (b, 0, 0)),
            pl.BlockSpec((1, seq // tq, HEAD_DIM, tq), lambda b, i: (b, 0, 0, 0)),
            pl.BlockSpec((1, rows, aw), lambda b, i: (b, i, 0)),
        ],
        out_specs=pl.BlockSpec((1, rows, aw), lambda b, i: (b, i, 0)),
        out_shape=jax.ShapeDtypeStruct((nb, seq, aw), BF16),
        compiler_params=_cparams(("arbitrary", "arbitrary")),
        name="swa",
    )(sinks, q, kr, vt, z)


def _cmp_hidden(x_ref, pe_ref, w1_ref, nch):
    top, bot = None, None
    half = CMP_STRIDE // 2
    for l in range(0, CMP_STRIDE, 2):
        rows = [x_ref[0, pl.ds(l + t, nch, stride=CMP_STRIDE), :] for t in range(2)]
        pair = lambda base: jnp.concatenate(
            [(rows[t] + pe_ref[base + l + t:base + l + t + 1, :]).astype(BF16) for t in range(2)], axis=1)
        t = _dot(pair(0), w1_ref[l // 2])
        b = _dot(pair(CMP_STRIDE), w1_ref[half + l // 2])
        top = t if top is None else top + t
        bot = b if bot is None else bot + b
    return jax.nn.gelu(top + pltpu.roll(bot, nch - 1, 0))


def _compress_kernel(kc_ref, vc_ref, pek_ref, w1k_ref, w2k_ref, pev_ref, w1v_ref, w2v_ref, ko_ref, vo_ref):
    nch = ko_ref.shape[2]
    hk = _cmp_hidden(kc_ref, pek_ref, w1k_ref, nch).astype(BF16)
    hv = _cmp_hidden(vc_ref, pev_ref, w1v_ref, nch).astype(BF16)
    for g in range(B_KV):
        gs = slice(g * CMP_HIDDEN, (g + 1) * CMP_HIDDEN)
        ko_ref[0, g] = _dot(hk[:, gs], w2k_ref[...]).astype(ko_ref.dtype)
        vo_ref[0, g] = _dot_nt(w2v_ref[...], hv[:, gs]).astype(vo_ref.dtype)


def _compress(kc, vc, pek, w1k, w2k, pev, w1v, w2v):
    nb, seq, _ = kc.shape
    nch = seq // CMP_STRIDE
    full = lambda a: pl.BlockSpec(a.shape, lambda b: (0,) * a.ndim)
    blk = pl.BlockSpec((1, seq, PAIR), lambda b: (b, 0, 0))
    return pl.pallas_call(
        _compress_kernel,
        grid=(nb,),
        in_specs=[blk, blk, full(pek), full(w1k), full(w2k), full(pev), full(w1v), full(w2v)],
        out_specs=[pl.BlockSpec((1, B_KV, nch, PAIR), lambda b: (b, 0, 0, 0)),
                   pl.BlockSpec((1, B_KV, HEAD_DIM, nch), lambda b: (b, 0, 0, 0))],
        out_shape=[jax.ShapeDtypeStruct((nb, B_KV, nch, PAIR), BF16),
                   jax.ShapeDtypeStruct((nb, B_KV, HEAD_DIM, nch), BF16)],
        compiler_params=_cparams(("arbitrary",)),
        name="compress",
    )(kc, vc, pek, w1k, w2k, pev, w1v, w2v)


def _nsa_kernel(tq, nc, nsb, q_ref, kc_ref, vct_ref, ks0_ref, ks1_ref, vs0_ref, vs1_ref, kw0_ref, kw1_ref,
                vw0_ref, vw1_ref, gt_ref, z_ref, ovl_ref, o_ref, bias_ref):
    i = pl.program_id(1)
    groups = range(B_KV)
    ks_refs, vs_refs = (ks0_ref, ks1_ref), (vs0_ref, vs1_ref)
    kw_refs, vw_refs = (kw0_ref, kw1_ref), (vw0_ref, vw1_ref)
    gw = B_REP * HEAD_DIM
    nr = B_REP * tq
    qs = [_stack_heads(q_ref[0, :, g * gw:(g + 1) * gw]) for g in groups]
    t_row = i * tq + lax.broadcasted_iota(jnp.int32, (1, tq), 1)

    nch = kc_ref.shape[2]
    cidx = lax.broadcasted_iota(jnp.int32, (nch, tq), 0)
    ok_c = (cidx < nc) & (t_row >= cidx * CMP_STRIDE + (CMP_LEN - 1))
    bias_c = _tile_lanes(jnp.where(ok_c, 0.0, NEG), B_REP)
    jr = lax.broadcasted_iota(jnp.int32, (nsb, tq), 0)
    tb = t_row // SLC_LEN
    valid = jr <= tb
    forced = (jr == 0) | (jr == tb) | (jr == tb - 1)
    n_prev = -(-(WIN_LEN - 1) // tq)
    blk_per_chunk = tq // SLC_LEN
    krow = lax.broadcasted_iota(jnp.int32, (tq, tq), 0)
    qcol = lax.broadcasted_iota(jnp.int32, (tq, tq), 1)
    causal = jnp.where(krow <= qcol, 0.0, NEG)
    edge = _tile_lanes(jnp.where(n_prev * tq + qcol - krow < WIN_LEN, 0.0, NEG), B_REP)
    causal_r = _tile_lanes(causal, B_REP)
    ones = jnp.ones((SUM_ROWS, tq), BF16)

    def win_chunk(j):
        return jnp.maximum(i - j, 0)

    def win_scores(g, j):
        off = pl.multiple_of(win_chunk(j) * tq, tq)
        s_t = _dot(kw_refs[g][0, pl.ds(off, tq), :], qs[g])
        return s_t + causal_r if j == 0 else (s_t + edge if j == n_prev else s_t)

    def slc_bias(g, c, diagonal):
        rows = [jnp.broadcast_to(bias_ref[g, pl.ds(c * blk_per_chunk + bb, 1), :], (SLC_LEN, tq))
                for bb in range(blk_per_chunk)]
        b = jnp.concatenate(rows, axis=0)
        return _tile_lanes(b + causal if diagonal else b, B_REP)

    def slc_qk(g, c):
        off = pl.multiple_of(c * tq, tq)
        return _dot(ks_refs[g][0, pl.ds(off, tq), :], qs[g])

    def sums_and_pv(v_t, e):
        res = _dot(jnp.concatenate([v_t, ones], axis=0), e)
        return res[HEAD_DIM:HEAD_DIM + 1], res[:HEAD_DIM]

    def masked_ref(m0, present):
        return m0 + jnp.where(present, 0.0, -NEG)

    s_cmp = [_dot(kc_ref[0, g], qs[g]) + bias_c for g in groups]
    s_win0 = [win_scores(g, 0) for g in groups]
    s_diag = [slc_qk(g, i) for g in groups]

    o_cmp = []
    for g in groups:
        s = s_cmp[g]
        m = jnp.max(s, axis=0, keepdims=True)
        m = jnp.where(m > 0.5 * NEG, m, 0.0)
        e = jnp.exp2(s - m)
        den = jnp.sum(e, axis=0, keepdims=True)
        p_t = (e * (1.0 / jnp.where(den > 0, den, 1.0))).astype(BF16)
        res = _dot(jnp.concatenate([vct_ref[0, g], ovl_ref[...]], axis=0), p_t)
        o_cmp.append(res[:HEAD_DIM])
        imp = res[HEAD_DIM:, 0:tq]
        for r in range(1, B_REP):
            imp = imp + res[HEAD_DIM:, r * tq:(r + 1) * tq]
        score = jnp.where(valid, jnp.where(forced, FORCE_BONUS, imp), -jnp.inf)
        bias_ref[g] = jnp.where(_topn_rows(score, valid, nsb, SLC_TOPN), 0.0, NEG)

    def write_out(slc, win):
        heads = []
        for g in groups:
            o_slc = slc[g][1] * (1.0 / slc[g][0])
            o_win = win[g][1] * (1.0 / win[g][0])
            gt = gt_ref[0, g * GATE_ROWS:(g + 1) * GATE_ROWS]
            for r in range(B_REP):
                cs = slice(r * tq, (r + 1) * tq)
                heads.append(gt[r:r + 1] * o_cmp[g][:, cs] + gt[B_REP + r:B_REP + r + 1] * o_slc[:, cs]
                             + gt[2 * B_REP + r:2 * B_REP + r + 1] * o_win[:, cs])
        o_t = jnp.concatenate(heads, axis=0)
        o_ref[0] = (o_t.T * z_ref[0]).astype(o_ref.dtype)

    win_m0 = [s_win0[g][0:1, :] for g in groups]
    win = [list(sums_and_pv(vw_refs[g][0, i], jnp.exp2(s_win0[g] - win_m0[g]).astype(BF16))) for g in groups]
    pending = []

    def finish_win(g, j, arg):
        l_c, pv = sums_and_pv(vw_refs[g][0, win_chunk(j)], jnp.exp2(arg).astype(BF16))
        win[g][0], win[g][1] = win[g][0] + l_c, win[g][1] + pv

    for j in range(1, n_prev + 1):
        for g in groups:
            pending.append((g, j, win_scores(g, j) - masked_ref(win_m0[g], i - j >= 0)))
            if len(pending) > MXU_LAG:
                finish_win(*pending.pop(0))
    while pending:
        finish_win(*pending.pop(0))

    own_blk = _tile_lanes(lax.broadcasted_iota(jnp.int32, (1, tq), 1) // SLC_LEN, B_REP)
    slc_m0 = []
    for g in groups:
        ref = s_diag[g][0:1, :]
        for bb in range(1, blk_per_chunk):
            ref = jnp.where(own_blk == bb, s_diag[g][bb * SLC_LEN:bb * SLC_LEN + 1, :], ref)
        slc_m0.append(ref)

    def slc_arg(g, c, raw, ref_row, diagonal):
        parts = []
        for bb in range(blk_per_chunk):
            rows = slice(bb * SLC_LEN, (bb + 1) * SLC_LEN)
            shift = _tile_lanes(bias_ref[g, pl.ds(c * blk_per_chunk + bb, 1), :], B_REP) - ref_row
            parts.append(raw[rows] + shift + causal_r[rows] if diagonal else raw[rows] + shift)
        return jnp.concatenate(parts, axis=0)

    slc0 = tuple(sums_and_pv(vs_refs[g][0, i],
                             jnp.exp2(slc_arg(g, i, s_diag[g], slc_m0[g], True)).astype(BF16)) for g in groups)

    def fast_pair(u, sums):
        sums = [list(x) for x in sums]
        waiting = []

        def finish(g, c, arg):
            l_c, pv = sums_and_pv(vs_refs[g][0, c], jnp.exp2(arg).astype(BF16))
            sums[g][0], sums[g][1] = sums[g][0] + l_c, sums[g][1] + pv

        for c in (2 * u, 2 * u + 1):
            for g in groups:
                waiting.append((g, c, slc_arg(g, c, slc_qk(g, c), masked_ref(slc_m0[g], c < i), False)))
                if len(waiting) > MXU_LAG:
                    finish(*waiting.pop(0))
        while waiting:
            finish(*waiting.pop(0))
        return tuple(tuple(x) for x in sums)

    slc = lax.fori_loop(0, (i + 1) // 2, fast_pair, slc0)
    write_out(slc, win)
    worst = jnp.maximum(jnp.maximum(slc[0][0], slc[1][0]), jnp.maximum(win[0][0], win[1][0]))
    safe = jnp.max(worst) < SAFE_SUM

    @pl.when(jnp.logical_not(safe))
    def _():
        span = (n_prev + 1) * tq
        c0 = jnp.maximum(i - n_prev, 0)
        start = pl.multiple_of(c0 * tq, tq)
        diff = t_row - (start + lax.broadcasted_iota(jnp.int32, (span, tq), 0))
        bias_w = _tile_lanes(jnp.where(diff >= 0, jnp.where(diff < WIN_LEN, 0.0, NEG), NEG), B_REP)
        win_x = []
        for g in groups:
            sw = _dot(kw_refs[g][0, pl.ds(start, span), :], qs[g]) + bias_w
            _, l_w, acc_w = _flash(_flash_init(HEAD_DIM, nr), sw,
                                   [vw_refs[g][0, c0 + cc] for cc in range(n_prev + 1)])
            win_x.append((l_w, acc_w))

        def exact_pair(u, carries):
            out = []
            for g in groups:
                carry = carries[g]
                for c in (2 * u, 2 * u + 1):
                    s_t = slc_qk(g, c) + slc_bias(g, c, False) + jnp.where(c < i, 0.0, NEG)
                    carry = _flash(carry, s_t, [vs_refs[g][0, c]])
                out.append(carry)
            return tuple(out)

        first = tuple(_flash(_flash_init(HEAD_DIM, nr), slc_qk(g, i) + slc_bias(g, i, True), [vs_refs[g][0, i]])
                      for g in groups)
        exact = lax.fori_loop(0, (i + 1) // 2, exact_pair, first)
        write_out([(l, acc) for (_, l, acc) in exact], win_x)


def _nsa(q, kcmp, vcmp_t, kr, vt, gates_t, z, ovl_t, nc, nsb, tq):
    nb, seq, _ = q.shape
    bw = B_HEADS * HEAD_DIM
    nch = kcmp.shape[2]
    kblk = lambda col: pl.BlockSpec((1, seq, PAIR), lambda b, i: (b, 0, col))
    vblk = lambda row: pl.BlockSpec((1, seq // tq, HEAD_DIM, tq), lambda b, i: (b, 0, row, 0))
    return pl.pallas_call(
        functools.partial(_nsa_kernel, tq, nc, nsb),
        grid=(nb, seq // tq),
        in_specs=[
            pl.BlockSpec((1, tq, bw), lambda b, i: (b, i, 1)),
            pl.BlockSpec((1, B_KV, nch, PAIR), lambda b, i: (b, 0, 0, 0)),
            pl.BlockSpec((1, B_KV, HEAD_DIM, nch), lambda b, i: (b, 0, 0, 0)),
            kblk(1), kblk(2), vblk(0), vblk(1), kblk(3), kblk(4), vblk(2), vblk(3),
            pl.BlockSpec((1, B_KV * GATE_ROWS, tq), lambda b, i: (b, 0, i)),
            pl.BlockSpec((1, tq, bw), lambda b, i: (b, i, 1)),
            pl.BlockSpec(ovl_t.shape, lambda b, i: (0, 0)),
        ],
        out_specs=pl.BlockSpec((1, tq, bw), lambda b, i: (b, i, 0)),
        out_shape=jax.ShapeDtypeStruct((nb, seq, bw), BF16),
        scratch_shapes=[pltpu.VMEM((B_KV, nsb, tq), F32)],
        compiler_params=_cparams(("arbitrary", "arbitrary")),
        name="nsa",
    )(q, kcmp, vcmp_t, kr, kr, vt, vt, kr, kr, vt, vt, gates_t, z, ovl_t)


def _moba_kernel(nblk, npair, q_ref, k_ref, vt_ref, km_ref, z_ref, o_ref, bias_ref):
    i = pl.program_id(2)
    tq = MOBA_BLOCK
    nr = 2 * tq
    ncand = bias_ref.shape[1]
    pairs = range(npair)
    lanes = [slice(p * PAIR, (p + 1) * PAIR) for p in pairs]
    qs = [_stack_heads(q_ref[0, :, lanes[p]]) for p in pairs]

    gs = []
    for p in pairs:
        km = km_ref[0, :, lanes[p]]
        pad = -nblk % SUM_ROWS
        km = jnp.concatenate([km, jnp.zeros((pad, PAIR), F32)], axis=0).astype(BF16)
        gs.append(_dot(km, qs[p])[:ncand])

    krow = lax.broadcasted_iota(jnp.int32, (tq, tq), 0)
    qcol = lax.broadcasted_iota(jnp.int32, (tq, tq), 1)
    causal = _tile_lanes(jnp.where(krow <= qcol, 0.0, NEG), 2)
    ones = jnp.ones((SUM_ROWS, tq), BF16)

    def scores(p, blk, bias):
        off = pl.multiple_of(blk * tq, tq)
        return _dot(k_ref[0, pl.ds(off, tq), lanes[p]], qs[p]) + bias

    def past_scores(p, blk):
        return scores(p, blk, bias_ref[p, pl.ds(blk, 1), :])

    def weights_times_v(e, p, blk):
        res = []
        for hh in range(2):
            v_ext = jnp.concatenate([vt_ref[0, blk, pl.ds(p * PAIR + hh * HEAD_DIM, HEAD_DIM), :], ones], axis=0)
            res.append(_dot(v_ext, e[:, hh * tq:(hh + 1) * tq]))
        res = jnp.concatenate(res, axis=1)
        return res[HEAD_DIM:HEAD_DIM + 1], res[:HEAD_DIM]

    def flash(carry, s_t, p, blk):
        m, l, acc = carry
        m_new = jnp.maximum(m, jnp.max(s_t, axis=0, keepdims=True))
        alpha = jnp.exp2(m - m_new)
        l_c, pv = weights_times_v(jnp.exp2(s_t - m_new).astype(BF16), p, blk)
        return m_new, alpha * l + l_c, alpha * acc + pv

    def write_out(results):
        outs = []
        for p in pairs:
            l_f, acc_f = results[p]
            o_t = acc_f * (1.0 / l_f)
            outs += [o_t[:, :tq], o_t[:, tq:]]
        o_ref[0] = (jnp.concatenate(outs, axis=0).T * z_ref[0]).astype(o_ref.dtype)

    s_own = [scores(p, i, causal) for p in pairs]
    jr = lax.broadcasted_iota(jnp.int32, (ncand, nr), 0)
    past = jr < i
    for p in pairs:
        sel = _topn_rows(jnp.where(past, gs[p], -jnp.inf), past, nblk, MOBA_TOPK)
        bias_ref[p] = jnp.where(sel, 0.0, NEG)
    own = []
    for p in pairs:
        m0 = s_own[p][0:1, :]
        l0, acc0 = weights_times_v(jnp.exp2(s_own[p] - m0).astype(BF16), p, i)
        own.append((m0, l0, acc0))

    def fast_blocks(blks, sums):
        sums = [list(x) for x in sums]
        pending = []

        def finish(p, blk, s_t):
            l_c, pv = weights_times_v(jnp.exp2(s_t).astype(BF16), p, blk)
            sums[p][0] = sums[p][0] + l_c
            sums[p][1] = sums[p][1] + pv

        for blk in blks:
            for p in pairs:
                pending.append((p, blk, scores(p, blk, bias_ref[p, pl.ds(blk, 1), :] - own[p][0])))
                if len(pending) > MXU_LAG:
                    finish(*pending.pop(0))
        while pending:
            finish(*pending.pop(0))
        return tuple(tuple(x) for x in sums)

    fast = lax.fori_loop(0, (i + 1) // 2, lambda u, s: fast_blocks((2 * u, 2 * u + 1), s),
                         tuple((l0, acc0) for (_, l0, acc0) in own))
    write_out(fast)
    worst = fast[0][0]
    for p in pairs[1:]:
        worst = jnp.maximum(worst, fast[p][0])
    safe = jnp.max(worst) < SAFE_SUM

    @pl.when(jnp.logical_not(safe))
    def _():
        def exact_block(blk, carries):
            return tuple(flash(carries[p], past_scores(p, blk), p, blk) for p in pairs)

        start = (jnp.full((1, nr), NEG, F32), jnp.zeros((1, nr), F32), jnp.zeros((HEAD_DIM, nr), F32))
        first = tuple(flash(start, scores(p, i, causal), p, i) for p in pairs)
        exact = lax.fori_loop(0, i, exact_block, first)
        write_out([(l, acc) for (_, l, acc) in exact])


def _moba(q, k, vt, kmean, z):
    nb, seq, width = q.shape
    tq = MOBA_BLOCK
    nblk = seq // MOBA_BLOCK
    ncand = -(-nblk // 8) * 8
    npair = MOBA_PAIRS_PER_STEP
    gw = npair * PAIR
    qblk = pl.BlockSpec((1, tq, gw), lambda b, p, i: (b, i, p))
    return pl.pallas_call(
        functools.partial(_moba_kernel, nblk, npair),
        grid=(nb, width // gw, seq // tq),
        in_specs=[
            qblk,
            pl.BlockSpec((1, seq, gw), lambda b, p, i: (b, 0, p)),
            pl.BlockSpec((1, nblk, gw, tq), lambda b, p, i: (b, 0, p, 0)),
            pl.BlockSpec((1, nblk, gw), lambda b, p, i: (b, 0, p)),
            qblk,
        ],
        out_specs=qblk,
        out_shape=jax.ShapeDtypeStruct((nb, seq, width), BF16),
        scratch_shapes=[pltpu.VMEM((npair, ncand, 2 * tq), F32)],
        compiler_params=_cparams(("arbitrary", "arbitrary", "arbitrary")),
        name="moba",
    )(q, k, vt, kmean, z)


def _even_plan(w_in, nb, seq):
    aq, akv = A_HEADS * HEAD_DIM, HEAD_DIM
    bq, bkv, bg = B_HEADS * HEAD_DIM, B_KV * HEAD_DIM, 3 * B_HEADS
    sizes = (aq, akv, akv, aq, bq, bkv, bkv, bkv, bkv, bkv, bkv, bg, bq)
    offs = np.concatenate([[0], np.cumsum(sizes)])
    qa, ka, va, za, qb, kc, vc, ks, vs, kw, vw, gb, zb = [w_in[:, offs[n]:offs[n + 1]] for n in range(len(sizes))]
    qw = aq + bq
    krw = PAIR * (1 + 2 * B_KV)
    gb4 = gb.reshape(-1, B_KV, B_REP, 3).transpose(0, 1, 3, 2).reshape(-1, B_KV, 3 * B_REP)
    gb4 = jnp.pad(gb4, ((0, 0), (0, 0), (0, GATE_ROWS - 3 * B_REP))).reshape(-1, B_KV * GATE_ROWS)
    vrows = akv + 2 * bkv
    t_width = -(-(vrows + B_KV * GATE_ROWS) // PAIR) * PAIR
    t_pad = jnp.zeros((w_in.shape[0], t_width - vrows - B_KV * GATE_ROWS), w_in.dtype)
    kheads = akv + 2 * bkv
    k_width = -(-kheads // PAIR) * PAIR
    k_pad = jnp.zeros((w_in.shape[0], k_width - kheads), w_in.dtype)
    w = jnp.concatenate([qa * (Q_SCALE * LOG2E), qb * (Q_SCALE * LOG2E), ka, ks, kw, k_pad, kc, vc, za, zb,
                         va, vs, vw, gb4, t_pad], axis=1).astype(BF16)
    groups, col = [], 0
    for width, kind in ((qw, "rope"), (k_width, "rope_dup"), (bkv, "rope"), (bkv, "plain"), (qw, "silu")):
        groups.append((col, width, kind, len(groups)))
        col += width
    groups_t = ((0, akv, "plain", 5, SWA_TQ), (akv, 2 * bkv, "plain", 6, NSA_TQ),
                (vrows, B_KV * GATE_ROWS, "sigmoid", 7, None))
    rows, tm = nb * seq, ROW_TILE
    per_b = seq // tm
    widths = [(qw, BF16), (krw, BF16), (bkv, F32), (bkv, F32), (qw, F32)]
    out_shape = [jax.ShapeDtypeStruct((rows, wd), dt) for wd, dt in widths]
    out_specs = [pl.BlockSpec((tm, wd), lambda i: (i, 0)) for wd, _ in widths]
    for n_rows, chunk in ((akv, SWA_TQ), (2 * bkv, NSA_TQ)):
        out_shape.append(jax.ShapeDtypeStruct((nb, seq // chunk, n_rows, chunk), BF16))
        out_specs.append(pl.BlockSpec((1, tm // chunk, n_rows, chunk), lambda i: (i // per_b, i % per_b, 0, 0)))
    out_shape.append(jax.ShapeDtypeStruct((nb, B_KV * GATE_ROWS, seq), F32))
    out_specs.append(pl.BlockSpec((1, B_KV * GATE_ROWS, tm), lambda i: (i // per_b, 0, i % per_b)))
    return w, (col, t_width), tuple(groups), groups_t, out_shape, out_specs


def _odd_plan(w_in, nb, seq):
    cw = C_HEADS * HEAD_DIM
    col_scale = np.ones((1, 4 * cw), np.float32)
    col_scale[:, :cw] = Q_SCALE * LOG2E
    w = (w_in * col_scale).astype(BF16)
    groups = ((0, cw, "rope", 0), (cw, cw, "rope_mean", 1), (3 * cw, cw, "silu", 2))
    groups_t = ((0, cw, "plain", 3, MOBA_BLOCK),)
    rows, tm = nb * seq, ROW_TILE
    per_b = seq // tm
    widths = [(cw, BF16), (cw, BF16), (cw, F32)]
    out_shape = [jax.ShapeDtypeStruct((rows, wd), dt) for wd, dt in widths]
    out_specs = [pl.BlockSpec((tm, wd), lambda i: (i, 0)) for wd, _ in widths]
    out_shape += [jax.ShapeDtypeStruct((nb, seq // MOBA_BLOCK, cw, MOBA_BLOCK), BF16),
                  jax.ShapeDtypeStruct((nb, seq // MOBA_BLOCK, 1, cw), F32)]
    out_specs += [pl.BlockSpec((1, tm // MOBA_BLOCK, cw, MOBA_BLOCK), lambda i: (i // per_b, i % per_b, 0, 0)),
                  pl.BlockSpec((1, tm // MOBA_BLOCK, 1, cw), lambda i: (i // per_b, i % per_b, 0, 0))]
    return w, (2 * cw, cw), groups, groups_t, out_shape, out_specs


def _rope_tables(seq):
    inv = ROPE_THETA ** (-jnp.arange(0, HEAD_DIM, 2, dtype=F32) / HEAD_DIM)
    ang = jnp.arange(seq, dtype=F32)[:, None] * inv[None, :]
    cos, sin = jnp.cos(ang), jnp.sin(ang)
    reps = PAIR // (HEAD_DIM // 2)
    sign = np.tile(np.concatenate([-np.ones(HEAD_DIM // 2), np.ones(HEAD_DIM // 2)]), PAIR // HEAD_DIM)
    return jnp.tile(cos, (1, reps)), jnp.tile(sin, (1, reps)) * jnp.asarray(sign, F32)[None, :]


def _overlap_matrix_t(seq, nch):
    nc = (seq - CMP_LEN) // CMP_STRIDE + 1
    nsb = seq // SLC_LEN
    cst = np.arange(nc) * CMP_STRIDE
    jj = np.arange(nsb)
    ov = ((cst[None, :] < (jj[:, None] + 1) * SLC_LEN) & (cst[None, :] + CMP_LEN > jj[:, None] * SLC_LEN))
    full = np.zeros((nsb, nch), np.float32)
    full[:, :nc] = ov
    return jnp.asarray(full, BF16), nc, nsb


def _cmp_weights(pe, w1, w2, transpose_out):
    pe2 = jnp.concatenate([pe, pe], axis=1)
    w1t = w1.reshape(CMP_LEN, HEAD_DIM, CMP_HIDDEN)
    zeros = jnp.zeros_like(w1t)
    w1p = jnp.concatenate([jnp.concatenate([w1t, zeros], axis=2), jnp.concatenate([zeros, w1t], axis=2)], axis=1)
    w1p = w1p.reshape(CMP_LEN // 2, 2 * PAIR, B_KV * CMP_HIDDEN)
    w2o = w2.T if transpose_out else jnp.concatenate([w2, w2], axis=1)
    return pe2, w1p.astype(BF16), w2o.astype(BF16)


def kernel(x, c, w_ada, b_ada, norm_g, w_in_even, a_sinks, cmp_pe_k, cmp_w1_k, cmp_w2_k, cmp_pe_v, cmp_w1_v, cmp_w2_v, w_out_even, w_in_odd, w_out_odd, final_g):
    nb, seq, d = x.shape
    depth = w_ada.shape[0]
    assert seq % ROW_TILE == 0 and seq % MOBA_BLOCK == 0 and NSA_TQ % SLC_LEN == 0
    assert (seq // SLC_LEN) % 8 == 0 and seq % (SWA_TQ * SWA_BLOCKS_PER_STEP) == 0 and ROW_TILE % NSA_TQ == 0
    cos_t, sin_t = _rope_tables(seq)
    ada = _ada(c, w_ada, b_ada)
    nch = seq // CMP_STRIDE
    ovl_t, nc, nsb = _overlap_matrix_t(seq, nch)
    fg = final_g.reshape(1, d)
    x2 = x.reshape(nb * seq, d)
    r3 = lambda a: a.reshape(nb, seq, a.shape[-1])
    prev = None
    for layer in range(depth):
        li = layer // 2
        ng = norm_g[layer].reshape(1, d)
        plan = _even_plan(w_in_even[li], nb, seq) if layer % 2 == 0 else _odd_plan(w_in_odd[li], nb, seq)
        w, t_cols, groups, groups_t, out_shape, out_specs = plan
        outs = _proj(x2, ada, layer, ng, cos_t, sin_t, w, t_cols, groups, groups_t, out_shape, out_specs, seq,
                     moba_mean=None if layer % 2 == 0 else 4, prev=prev)
        if prev is not None:
            x2, outs = outs[0], outs[1:]
        if layer % 2 == 0:
            q, kr, kc, vc, z, vt_a, vt_b, gates_t = outs
            q, kr, kc, vc, z = map(r3, (q, kr, kc, vc, z))
            kcmp, vcmp_t = _compress(kc, vc, *_cmp_weights(cmp_pe_k[li], cmp_w1_k[li], cmp_w2_k[li], False),
                                     *_cmp_weights(cmp_pe_v[li], cmp_w1_v[li], cmp_w2_v[li], True))
            oa = _swa(a_sinks[li], q, kr, vt_a, z, SWA_TQ)
            ob = _nsa(q, kcmp, vcmp_t, kr, vt_b, gates_t, z, ovl_t, nc, nsb, NSA_TQ)
            o_list = [oa.reshape(nb * seq, -1), ob.reshape(nb * seq, -1)]
            w_out = w_out_even[li].astype(BF16)
        else:
            q, k, z, vt, kmean = outs
            o = _moba(r3(q), r3(k), vt, kmean.reshape(nb, seq // MOBA_BLOCK, -1), r3(z))
            o_list = [o.reshape(nb * seq, -1)]
            w_out = w_out_odd[li].astype(BF16)
        prev = (o_list, w_out)
    x2 = _out_proj(x2, ada, depth - 1, prev[0], prev[1], fg, True, seq)
    return x2.reshape(nb, seq, d)
```

```python
import functools

import numpy as np
import jax
import jax.numpy as jnp
from jax import lax
from jax.experimental import pallas as pl
from jax.experimental.pallas import tpu as pltpu

D_MODEL = 1024
HEAD_DIM = 64
PAIR = 2 * HEAD_DIM
ROPE_THETA = 10000.0
RMS_EPS = 1e-6
A_HEADS = 8
A_WINDOW = 128
B_HEADS = 8
B_KV = 2
B_REP = B_HEADS // B_KV
CMP_LEN = 32
CMP_STRIDE = 16
CMP_HIDDEN = 256
SLC_LEN = 64
SLC_TOPN = 8
WIN_LEN = 512
FORCE_BONUS = 1e4
C_HEADS = 16
MOBA_BLOCK = 256
MOBA_TOPK = 3
Q_SCALE = HEAD_DIM ** -0.5
LOG2E = 1.4426950408889634

NEG = -1e30
SAFE_SUM = 2.0 ** 64
MXU_COLS = 256
ROW_TILE = 1024
SWA_TQ = 128
NSA_TQ = 256
SWA_BLOCKS_PER_STEP = 8
MOBA_PAIRS_PER_STEP = 8
MXU_LAG = 3
SUM_ROWS = 16
GATE_ROWS = 16
VMEM_LIMIT = 60 * 1024 * 1024

BF16 = jnp.bfloat16
F32 = jnp.float32


def _dot_nt(a, b):
    return lax.dot_general(a, b, (((1,), (1,)), ((), ())), preferred_element_type=F32)


def _dot(a, b):
    return jnp.dot(a, b, preferred_element_type=F32)


def _cparams(sem):
    return pltpu.CompilerParams(dimension_semantics=sem, vmem_limit_bytes=VMEM_LIMIT)


def _tile_lanes(a, n):
    return jnp.concatenate([a] * n, axis=1) if n > 1 else a


def _ada_kernel(c_ref, w_ref, b_ref, o_ref):
    c = c_ref[...]
    ca = (c * jax.nn.sigmoid(c)).astype(BF16)
    d = c.shape[-1]
    for j in range(o_ref.shape[1]):
        o_ref[0, j] = _dot(ca, w_ref[0, :, j * d:(j + 1) * d].astype(BF16)) + b_ref[0, j]


def _ada(c, w_ada, b_ada):
    depth, d, _ = w_ada.shape
    nb = c.shape[0]
    b4 = b_ada.reshape(depth, 3, 1, d)
    out = pl.pallas_call(
        _ada_kernel,
        grid=(depth,),
        in_specs=[
            pl.BlockSpec((nb, d), lambda l: (0, 0)),
            pl.BlockSpec((1, d, 3 * d), lambda l: (l, 0, 0)),
            pl.BlockSpec((1, 3, 1, d), lambda l: (l, 0, 0, 0)),
        ],
        out_specs=pl.BlockSpec((1, 3, nb, d), lambda l: (l, 0, 0, 0)),
        out_shape=jax.ShapeDtypeStruct((depth, 3, nb, d), F32),
        compiler_params=_cparams(("arbitrary",)),
        name="ada",
    )(c, w_ada, b4)
    return out.reshape(depth, 3, nb, 1, d)


def _rope(a, cos, sin_signed):
    w = a.shape[-1]
    lane = lax.broadcasted_iota(jnp.int32, a.shape, 1)
    first_half = (lane % HEAD_DIM) < (HEAD_DIM // 2)
    partner = jnp.where(first_half, pltpu.roll(a, w - HEAD_DIM // 2, 1), pltpu.roll(a, HEAD_DIM // 2, 1))
    return a * cos + partner * sin_signed


def _proj_kernel(groups, groups_t, t_cols, moba_mean, n_prev_o, *refs):
    x_ref = refs[0]
    prev_refs, refs = refs[1:1 + (n_prev_o + 2 if n_prev_o else 0)], refs[1 + (n_prev_o + 2 if n_prev_o else 0):]
    shift_ref, scale_ref, g_ref, cos_ref, sin_ref, w_ref = refs[:6]
    out_refs = refs[6:]
    t_start, t_width = t_cols

    x = x_ref[...]
    if n_prev_o:
        gate_ref, wout_ref = prev_refs[0], prev_refs[-1]
        mix, k0 = None, 0
        for o_ref in prev_refs[1:-1]:
            kw = o_ref.shape[-1]
            part = _dot(o_ref[...], wout_ref[k0:k0 + kw, :])
            mix = part if mix is None else mix + part
            k0 += kw
        x = x + gate_ref[0, 0, 0] * mix
        out_refs[0][...] = x
        out_refs = out_refs[1:]
    y = x * lax.rsqrt(jnp.mean(x * x, axis=-1, keepdims=True) + RMS_EPS)
    h = (y * g_ref[...]) * (1.0 + scale_ref[0, 0, 0]) + shift_ref[0, 0, 0]
    hb = h.astype(BF16)
    tm = hb.shape[0]
    cos, sin = cos_ref[...], sin_ref[...]
    lo_half = lax.broadcasted_iota(jnp.int32, (tm, PAIR), 1) < HEAD_DIM
    pair_plan = {}
    for (w_start, width, kind, out_idx) in groups:
        for c in range(0, width, PAIR):
            pair_plan[(w_start + c) // PAIR] = (kind, out_idx, c)
    n_cols = w_ref.shape[1]
    for s0 in range(0, n_cols, MXU_COLS):
        sw = min(MXU_COLS, n_cols - s0)
        acc = _dot(hb, w_ref[:, s0:s0 + sw])
        for h0 in range(0, sw, PAIR):
            col = s0 + h0
            a = acc[:, h0:h0 + PAIR]
            if t_start <= col < t_start + t_width:
                f0 = col - t_start
                a_t = a.T
                for (r_start, n_rows, kind, out_idx, chunk) in groups_t:
                    lo, hi = max(r_start, f0), min(r_start + n_rows, f0 + PAIR)
                    if lo >= hi:
                        continue
                    o_ref = out_refs[out_idx]
                    part = a_t[lo - f0:hi - f0]
                    if kind == "sigmoid":
                        o_ref[0, lo - r_start:hi - r_start, :] = jax.nn.sigmoid(part)
                    else:
                        for jj in range(tm // chunk):
                            o_ref[0, jj, lo - r_start:hi - r_start, :] = (
                                part[:, jj * chunk:(jj + 1) * chunk].astype(o_ref.dtype))
                continue
            kind, out_idx, c = pair_plan[col // PAIR]
            o_ref = out_refs[out_idx]
            if kind in ("rope", "rope_mean", "rope_dup"):
                a = _rope(a, cos, sin)
            elif kind == "silu":
                a = a * jax.nn.sigmoid(a)
            if kind == "rope_dup":
                b = pltpu.roll(a, HEAD_DIM, 1)
                for hh, dup in enumerate((jnp.where(lo_half, a, b), jnp.where(lo_half, b, a))):
                    dst = (2 * (c // PAIR) + hh) * PAIR
                    if dst < o_ref.shape[1]:
                        o_ref[:, dst:dst + PAIR] = dup.astype(o_ref.dtype)
                continue
            o_ref[:, c:c + PAIR] = a.astype(o_ref.dtype)
            if kind == "rope_mean":
                km_ref = out_refs[moba_mean]
                for j in range(tm // MOBA_BLOCK):
                    blk = a[j * MOBA_BLOCK:(j + 1) * MOBA_BLOCK]
                    km_ref[0, j, :, c:c + PAIR] = jnp.sum(blk, axis=0, keepdims=True) * (1.0 / MOBA_BLOCK)


def _proj(x2, ada, layer, norm_g, cos_t, sin_t, w, t_cols, groups, groups_t, out_shape, out_specs, seq,
          moba_mean=None, prev=None):
    rows, d = x2.shape
    tm = ROW_TILE
    per_b = seq // tm
    args, in_specs = [x2], [pl.BlockSpec((tm, d), lambda i: (i, 0))]
    n_prev_o = 0
    if prev is not None:
        o_list, w_out = prev
        n_prev_o = len(o_list)
        args += [ada, *o_list, w_out]
        in_specs.append(pl.BlockSpec((1, 1, 1, 1, d), lambda i: (layer - 1, 2, i // per_b, 0, 0)))
        in_specs += [pl.BlockSpec((tm, o.shape[-1]), lambda i: (i, 0)) for o in o_list]
        in_specs.append(pl.BlockSpec(w_out.shape, lambda i: (0, 0)))
        out_shape = [jax.ShapeDtypeStruct((rows, d), F32)] + list(out_shape)
        out_specs = [pl.BlockSpec((tm, d), lambda i: (i, 0))] + list(out_specs)
    args += [ada, ada, norm_g, cos_t, sin_t, w]
    in_specs += [
        pl.BlockSpec((1, 1, 1, 1, d), lambda i: (layer, 0, i // per_b, 0, 0)),
        pl.BlockSpec((1, 1, 1, 1, d), lambda i: (layer, 1, i // per_b, 0, 0)),
        pl.BlockSpec((1, d), lambda i: (0, 0)),
        pl.BlockSpec((tm, PAIR), lambda i: (i % per_b, 0)),
        pl.BlockSpec((tm, PAIR), lambda i: (i % per_b, 0)),
        pl.BlockSpec(w.shape, lambda i: (0, 0)),
    ]
    return pl.pallas_call(
        functools.partial(_proj_kernel, groups, groups_t, t_cols, moba_mean, n_prev_o),
        grid=(rows // tm,),
        in_specs=in_specs,
        out_specs=out_specs,
        out_shape=out_shape,
        compiler_params=_cparams(("arbitrary",)),
        name="proj",
    )(*args)


def _out_kernel(n_o, final, *refs):
    x_ref, gate_ref = refs[0], refs[1]
    o_refs = refs[2:2 + n_o]
    w_ref = refs[2 + n_o]
    fg_ref = refs[3 + n_o]
    out_ref = refs[4 + n_o]
    y = None
    k0 = 0
    for o_ref in o_refs:
        kw = o_ref.shape[-1]
        part = _dot(o_ref[...], w_ref[k0:k0 + kw, :])
        y = part if y is None else y + part
        k0 += kw
    xn = x_ref[...] + gate_ref[0, 0, 0] * y
    if final:
        xn = (xn * lax.rsqrt(jnp.mean(xn * xn, axis=-1, keepdims=True) + RMS_EPS)) * fg_ref[...]
    out_ref[...] = xn


def _out_proj(x2, ada, layer, o_list, w_out, final_g, final, seq):
    rows, d = x2.shape
    tm = ROW_TILE
    per_b = seq // tm
    n_o = len(o_list)
    in_specs = [
        pl.BlockSpec((tm, d), lambda i: (i, 0)),
        pl.BlockSpec((1, 1, 1, 1, d), lambda i: (layer, 2, i // per_b, 0, 0)),
    ]
    in_specs += [pl.BlockSpec((tm, o.shape[-1]), lambda i: (i, 0)) for o in o_list]
    in_specs += [pl.BlockSpec(w_out.shape, lambda i: (0, 0)), pl.BlockSpec((1, d), lambda i: (0, 0))]
    return pl.pallas_call(
        functools.partial(_out_kernel, n_o, final),
        grid=(rows // tm,),
        in_specs=in_specs,
        out_specs=pl.BlockSpec((tm, d), lambda i: (i, 0)),
        out_shape=jax.ShapeDtypeStruct((rows, d), F32),
        compiler_params=_cparams(("arbitrary",)),
        name="out_proj",
    )(x2, ada, *o_list, w_out, final_g)


def _stack_heads(q_pairs):
    row = lax.broadcasted_iota(jnp.int32, (PAIR, 1), 0)
    m_lo = (row < HEAD_DIM).astype(F32)
    m_hi = (row >= HEAD_DIM).astype(F32)
    blocks = []
    for p in range(q_pairs.shape[-1] // PAIR):
        qp_t = q_pairs[:, p * PAIR:(p + 1) * PAIR].astype(F32).T
        blocks += [qp_t * m_lo, qp_t * m_hi]
    return jnp.concatenate(blocks, axis=1).astype(BF16)


def _flash_init(d_rows, n_cols):
    return jnp.full((1, n_cols), NEG, F32), jnp.zeros((1, n_cols), F32), jnp.zeros((d_rows, n_cols), F32)


def _flash(carry, s_t, v_t_chunks):
    m, l, acc = carry
    m_new = jnp.maximum(m, jnp.max(s_t, axis=0, keepdims=True))
    alpha = jnp.exp2(m - m_new)
    e = jnp.exp2(s_t - m_new).astype(BF16)
    l = alpha * l + _dot(jnp.ones((SUM_ROWS, e.shape[0]), BF16), e)[0:1]
    acc, k0 = alpha * acc, 0
    for vt in v_t_chunks:
        acc = acc + _dot(vt, e[k0:k0 + vt.shape[1]])
        k0 += vt.shape[1]
    return m_new, l, acc


def _topn_rows(score, valid, n_cand, topn):
    sub = 8
    n_rows, n_cols = score.shape
    tiles = [score[r0:r0 + sub] for r0 in range(0, n_rows, sub)]
    ranks = [jnp.zeros((sub, n_cols), F32) for _ in tiles]
    ridx = lax.broadcasted_iota(jnp.int32, (sub, n_cols), 0)
    for jp in range(n_cand):
        row = score[jp:jp + 1, :]
        for t, tile in enumerate(tiles):
            r0 = t * sub
            if r0 > jp:
                beats = jnp.where(row >= tile, 1.0, 0.0)
            elif r0 + sub - 1 <= jp:
                beats = jnp.where(row > tile, 1.0, 0.0)
            else:
                later = jnp.where(ridx + r0 > jp, 1.0, 0.0)
                beats = jnp.where(row > tile, 1.0, jnp.where(row == tile, later, 0.0))
            ranks[t] = ranks[t] + beats
    rank = jnp.concatenate(ranks, axis=0) if len(ranks) > 1 else ranks[0]
    return valid & (rank < topn)


def _swa_kernel(tq, nqb, sink_ref, q_ref, k_ref, vt_ref, z_ref, o_ref):
    step = pl.program_id(1)
    n_prev = -(-(A_WINDOW - 1) // tq)
    krow = lax.broadcasted_iota(jnp.int32, (tq, tq), 0)
    qcol = lax.broadcasted_iota(jnp.int32, (tq, tq), 1)
    causal = _tile_lanes(jnp.where(krow <= qcol, 0.0, NEG), A_HEADS)
    edge = _tile_lanes(jnp.where(n_prev * tq + qcol - krow < A_WINDOW, 0.0, NEG), A_HEADS)
    ones = jnp.ones((SUM_ROWS, tq), BF16)
    sink = jnp.concatenate([jnp.full((1, tq), sink_ref[hd] * LOG2E, F32) for hd in range(A_HEADS)], axis=1)

    blocks = [step * nqb + bb for bb in range(nqb)]
    qs = [_stack_heads(q_ref[0, bb * tq:(bb + 1) * tq, :]) for bb in range(nqb)]
    scores = []
    for bb, i in enumerate(blocks):
        row = []
        for j in range(n_prev + 1):
            off = pl.multiple_of(jnp.maximum(i - j, 0) * tq, tq)
            s_t = _dot(k_ref[0, pl.ds(off, tq), :], qs[bb])
            s_t = s_t + causal if j == 0 else (s_t + edge if j == n_prev else s_t)
            row.append(s_t if j == 0 else s_t + jnp.where(i - j >= 0, 0.0, NEG))
        scores.append(row)

    outs = []
    for bb, i in enumerate(blocks):
        m = sink
        for s_t in scores[bb]:
            m = jnp.maximum(m, jnp.max(s_t, axis=0, keepdims=True))
        den, acc = jnp.exp2(sink - m), None
        for j, s_t in enumerate(scores[bb]):
            v_ext = jnp.concatenate([vt_ref[0, jnp.maximum(i - j, 0)], ones], axis=0)
            res = _dot(v_ext, jnp.exp2(s_t - m).astype(BF16))
            den = den + res[HEAD_DIM:HEAD_DIM + 1]
            acc = res[:HEAD_DIM] if acc is None else acc + res[:HEAD_DIM]
        o_t = acc * (1.0 / den)
        heads = jnp.concatenate([o_t[:, hd * tq:(hd + 1) * tq] for hd in range(A_HEADS)], axis=0)
        outs.append(heads.T)
    o_ref[0] = (jnp.concatenate(outs, axis=0) * z_ref[0]).astype(o_ref.dtype)


def _swa(sinks, q, kr, vt, z, tq):
    nb, seq, _ = q.shape
    aw = A_HEADS * HEAD_DIM
    nqb = SWA_BLOCKS_PER_STEP
    rows = nqb * tq
    return pl.pallas_call(
        functools.partial(_swa_kernel, tq, nqb),
        grid=(nb, seq // rows),
        in_specs=[
            pl.BlockSpec(memory_space=pltpu.SMEM),
            pl.BlockSpec((1, rows, aw), lambda b, i: (b, i, 0)),
            pl.BlockSpec((1, seq, PAIR), lambda b, i: (b, 0, 0)),
            pl.BlockSpec((1, seq // tq, HEAD_DIM, tq), lambda b, i: (b, 0, 0, 0)),
            pl.BlockSpec((1, rows, aw), lambda b, i: (b, i, 0)),
        ],
        out_specs=pl.BlockSpec((1, rows, aw), lambda b, i: (b, i, 0)),
        out_shape=jax.ShapeDtypeStruct((nb, seq, aw), BF16),
        compiler_params=_cparams(("arbitrary", "arbitrary")),
        name="swa",
    )(sinks, q, kr, vt, z)


def _cmp_hidden(x_ref, pe_ref, w1_ref, nch):
    top, bot = None, None
    half = CMP_STRIDE // 2
    for l in range(0, CMP_STRIDE, 2):
        rows = [x_ref[0, pl.ds(l + t, nch, stride=CMP_STRIDE), :] for t in range(2)]
        pair = lambda base: jnp.concatenate(
            [(rows[t] + pe_ref[base + l + t:base + l + t + 1, :]).astype(BF16) for t in range(2)], axis=1)
        t = _dot(pair(0), w1_ref[l // 2])
        b = _dot(pair(CMP_STRIDE), w1_ref[half + l // 2])
        top = t if top is None else top + t
        bot = b if bot is None else bot + b
    return jax.nn.gelu(top + pltpu.roll(bot, nch - 1, 0))


def _compress_kernel(kc_ref, vc_ref, pek_ref, w1k_ref, w2k_ref, pev_ref, w1v_ref, w2v_ref, ko_ref, vo_ref):
    nch = ko_ref.shape[2]
    hk = _cmp_hidden(kc_ref, pek_ref, w1k_ref, nch).astype(BF16)
    hv = _cmp_hidden(vc_ref, pev_ref, w1v_ref, nch).astype(BF16)
    for g in range(B_KV):
        gs = slice(g * CMP_HIDDEN, (g + 1) * CMP_HIDDEN)
        ko_ref[0, g] = _dot(hk[:, gs], w2k_ref[...]).astype(ko_ref.dtype)
        vo_ref[0, g] = _dot_nt(w2v_ref[...], hv[:, gs]).astype(vo_ref.dtype)


def _compress(kc, vc, pek, w1k, w2k, pev, w1v, w2v):
    nb, seq, _ = kc.shape
    nch = seq // CMP_STRIDE
    full = lambda a: pl.BlockSpec(a.shape, lambda b: (0,) * a.ndim)
    blk = pl.BlockSpec((1, seq, PAIR), lambda b: (b, 0, 0))
    return pl.pallas_call(
        _compress_kernel,
        grid=(nb,),
        in_specs=[blk, blk, full(pek), full(w1k), full(w2k), full(pev), full(w1v), full(w2v)],
        out_specs=[pl.BlockSpec((1, B_KV, nch, PAIR), lambda b: (b, 0, 0, 0)),
                   pl.BlockSpec((1, B_KV, HEAD_DIM, nch), lambda b: (b, 0, 0, 0))],
        out_shape=[jax.ShapeDtypeStruct((nb, B_KV, nch, PAIR), BF16),
                   jax.ShapeDtypeStruct((nb, B_KV, HEAD_DIM, nch), BF16)],
        compiler_params=_cparams(("arbitrary",)),
        name="compress",
    )(kc, vc, pek, w1k, w2k, pev, w1v, w2v)


def _nsa_kernel(tq, nc, nsb, q_ref, kc_ref, vct_ref, ks0_ref, ks1_ref, vs0_ref, vs1_ref, kw0_ref, kw1_ref,
                vw0_ref, vw1_ref, gt_ref, z_ref, ovl_ref, o_ref, bias_ref):
    i = pl.program_id(1)
    groups = range(B_KV)
    ks_refs, vs_refs = (ks0_ref, ks1_ref), (vs0_ref, vs1_ref)
    kw_refs, vw_refs = (kw0_ref, kw1_ref), (vw0_ref, vw1_ref)
    gw = B_REP * HEAD_DIM
    nr = B_REP * tq
    qs = [_stack_heads(q_ref[0, :, g * gw:(g + 1) * gw]) for g in groups]
    t_row = i * tq + lax.broadcasted_iota(jnp.int32, (1, tq), 1)

    nch = kc_ref.shape[2]
    cidx = lax.broadcasted_iota(jnp.int32, (nch, tq), 0)
    ok_c = (cidx < nc) & (t_row >= cidx * CMP_STRIDE + (CMP_LEN - 1))
    bias_c = _tile_lanes(jnp.where(ok_c, 0.0, NEG), B_REP)
    jr = lax.broadcasted_iota(jnp.int32, (nsb, tq), 0)
    tb = t_row // SLC_LEN
    valid = jr <= tb
    forced = (jr == 0) | (jr == tb) | (jr == tb - 1)
    n_prev = -(-(WIN_LEN - 1) // tq)
    blk_per_chunk = tq // SLC_LEN
    krow = lax.broadcasted_iota(jnp.int32, (tq, tq), 0)
    qcol = lax.broadcasted_iota(jnp.int32, (tq, tq), 1)
    causal = jnp.where(krow <= qcol, 0.0, NEG)
    edge = _tile_lanes(jnp.where(n_prev * tq + qcol - krow < WIN_LEN, 0.0, NEG), B_REP)
    causal_r = _tile_lanes(causal, B_REP)
    ones = jnp.ones((SUM_ROWS, tq), BF16)

    def win_chunk(j):
        return jnp.maximum(i - j, 0)

    def win_scores(g, j):
        off = pl.multiple_of(win_chunk(j) * tq, tq)
        s_t = _dot(kw_refs[g][0, pl.ds(off, tq), :], qs[g])
        return s_t + causal_r if j == 0 else (s_t + edge if j == n_prev else s_t)

    def slc_bias(g, c, diagonal):
        rows = [jnp.broadcast_to(bias_ref[g, pl.ds(c * blk_per_chunk + bb, 1), :], (SLC_LEN, tq))
                for bb in range(blk_per_chunk)]
        b = jnp.concatenate(rows, axis=0)
        return _tile_lanes(b + causal if diagonal else b, B_REP)

    def slc_qk(g, c):
        off = pl.multiple_of(c * tq, tq)
        return _dot(ks_refs[g][0, pl.ds(off, tq), :], qs[g])

    def sums_and_pv(v_t, e):
        res = _dot(jnp.concatenate([v_t, ones], axis=0), e)
        return res[HEAD_DIM:HEAD_DIM + 1], res[:HEAD_DIM]

    def masked_ref(m0, present):
        return m0 + jnp.where(present, 0.0, -NEG)

    s_cmp = [_dot(kc_ref[0, g], qs[g]) + bias_c for g in groups]
    s_win0 = [win_scores(g, 0) for g in groups]
    s_diag = [slc_qk(g, i) for g in groups]

    o_cmp = []
    for g in groups:
        s = s_cmp[g]
        m = jnp.max(s, axis=0, keepdims=True)
        m = jnp.where(m > 0.5 * NEG, m, 0.0)
        e = jnp.exp2(s - m)
        den = jnp.sum(e, axis=0, keepdims=True)
        p_t = (e * (1.0 / jnp.where(den > 0, den, 1.0))).astype(BF16)
        res = _dot(jnp.concatenate([vct_ref[0, g], ovl_ref[...]], axis=0), p_t)
        o_cmp.append(res[:HEAD_DIM])
        imp = res[HEAD_DIM:, 0:tq]
        for r in range(1, B_REP):
            imp = imp + res[HEAD_DIM:, r * tq:(r + 1) * tq]
        score = jnp.where(valid, jnp.where(forced, FORCE_BONUS, imp), -jnp.inf)
        bias_ref[g] = jnp.where(_topn_rows(score, valid, nsb, SLC_TOPN), 0.0, NEG)

    def write_out(slc, win):
        heads = []
        for g in groups:
            o_slc = slc[g][1] * (1.0 / slc[g][0])
            o_win = win[g][1] * (1.0 / win[g][0])
            gt = gt_ref[0, g * GATE_ROWS:(g + 1) * GATE_ROWS]
            for r in range(B_REP):
                cs = slice(r * tq, (r + 1) * tq)
                heads.append(gt[r:r + 1] * o_cmp[g][:, cs] + gt[B_REP + r:B_REP + r + 1] * o_slc[:, cs]
                             + gt[2 * B_REP + r:2 * B_REP + r + 1] * o_win[:, cs])
        o_t = jnp.concatenate(heads, axis=0)
        o_ref[0] = (o_t.T * z_ref[0]).astype(o_ref.dtype)

    win_m0 = [s_win0[g][0:1, :] for g in groups]
    win = [list(sums_and_pv(vw_refs[g][0, i], jnp.exp2(s_win0[g] - win_m0[g]).astype(BF16))) for g in groups]
    pending = []

    def finish_win(g, j, arg):
        l_c, pv = sums_and_pv(vw_refs[g][0, win_chunk(j)], jnp.exp2(arg).astype(BF16))
        win[g][0], win[g][1] = win[g][0] + l_c, win[g][1] + pv

    for j in range(1, n_prev + 1):
        for g in groups:
            pending.append((g, j, win_scores(g, j) - masked_ref(win_m0[g], i - j >= 0)))
            if len(pending) > MXU_LAG:
                finish_win(*pending.pop(0))
    while pending:
        finish_win(*pending.pop(0))

    own_blk = _tile_lanes(lax.broadcasted_iota(jnp.int32, (1, tq), 1) // SLC_LEN, B_REP)
    slc_m0 = []
    for g in groups:
        ref = s_diag[g][0:1, :]
        for bb in range(1, blk_per_chunk):
            ref = jnp.where(own_blk == bb, s_diag[g][bb * SLC_LEN:bb * SLC_LEN + 1, :], ref)
        slc_m0.append(ref)

    def slc_arg(g, c, raw, ref_row, diagonal):
        parts = []
        for bb in range(blk_per_chunk):
            rows = slice(bb * SLC_LEN, (bb + 1) * SLC_LEN)
            shift = _tile_lanes(bias_ref[g, pl.ds(c * blk_per_chunk + bb, 1), :], B_REP) - ref_row
            parts.append(raw[rows] + shift + causal_r[rows] if diagonal else raw[rows] + shift)
        return jnp.concatenate(parts, axis=0)

    slc0 = tuple(sums_and_pv(vs_refs[g][0, i],
                             jnp.exp2(slc_arg(g, i, s_diag[g], slc_m0[g], True)).astype(BF16)) for g in groups)

    def fast_pair(u, sums):
        sums = [list(x) for x in sums]
        waiting = []

        def finish(g, c, arg):
            l_c, pv = sums_and_pv(vs_refs[g][0, c], jnp.exp2(arg).astype(BF16))
            sums[g][0], sums[g][1] = sums[g][0] + l_c, sums[g][1] + pv

        for c in (2 * u, 2 * u + 1):
            for g in groups:
                waiting.append((g, c, slc_arg(g, c, slc_qk(g, c), masked_ref(slc_m0[g], c < i), False)))
                if len(waiting) > MXU_LAG:
                    finish(*waiting.pop(0))
        while waiting:
            finish(*waiting.pop(0))
        return tuple(tuple(x) for x in sums)

    slc = lax.fori_loop(0, (i + 1) // 2, fast_pair, slc0)
    write_out(slc, win)
    worst = jnp.maximum(jnp.maximum(slc[0][0], slc[1][0]), jnp.maximum(win[0][0], win[1][0]))
    safe = jnp.max(worst) < SAFE_SUM

    @pl.when(jnp.logical_not(safe))
    def _():
        span = (n_prev + 1) * tq
        c0 = jnp.maximum(i - n_prev, 0)
        start = pl.multiple_of(c0 * tq, tq)
        diff = t_row - (start + lax.broadcasted_iota(jnp.int32, (span, tq), 0))
        bias_w = _tile_lanes(jnp.where(diff >= 0, jnp.where(diff < WIN_LEN, 0.0, NEG), NEG), B_REP)
        win_x = []
        for g in groups:
            sw = _dot(kw_refs[g][0, pl.ds(start, span), :], qs[g]) + bias_w
            _, l_w, acc_w = _flash(_flash_init(HEAD_DIM, nr), sw,
                                   [vw_refs[g][0, c0 + cc] for cc in range(n_prev + 1)])
            win_x.append((l_w, acc_w))

        def exact_pair(u, carries):
            out = []
            for g in groups:
                carry = carries[g]
                for c in (2 * u, 2 * u + 1):
                    s_t = slc_qk(g, c) + slc_bias(g, c, False) + jnp.where(c < i, 0.0, NEG)
                    carry = _flash(carry, s_t, [vs_refs[g][0, c]])
                out.append(carry)
            return tuple(out)

        first = tuple(_flash(_flash_init(HEAD_DIM, nr), slc_qk(g, i) + slc_bias(g, i, True), [vs_refs[g][0, i]])
                      for g in groups)
        exact = lax.fori_loop(0, (i + 1) // 2, exact_pair, first)
        write_out([(l, acc) for (_, l, acc) in exact], win_x)


def _nsa(q, kcmp, vcmp_t, kr, vt, gates_t, z, ovl_t, nc, nsb, tq):
    nb, seq, _ = q.shape
    bw = B_HEADS * HEAD_DIM
    nch = kcmp.shape[2]
    kblk = lambda col: pl.BlockSpec((1, seq, PAIR), lambda b, i: (b, 0, col))
    vblk = lambda row: pl.BlockSpec((1, seq // tq, HEAD_DIM, tq), lambda b, i: (b, 0, row, 0))
    return pl.pallas_call(
        functools.partial(_nsa_kernel, tq, nc, nsb),
        grid=(nb, seq // tq),
        in_specs=[
            pl.BlockSpec((1, tq, bw), lambda b, i: (b, i, 1)),
            pl.BlockSpec((1, B_KV, nch, PAIR), lambda b, i: (b, 0, 0, 0)),
            pl.BlockSpec((1, B_KV, HEAD_DIM, nch), lambda b, i: (b, 0, 0, 0)),
            kblk(1), kblk(2), vblk(0), vblk(1), kblk(3), kblk(4), vblk(2), vblk(3),
            pl.BlockSpec((1, B_KV * GATE_ROWS, tq), lambda b, i: (b, 0, i)),
            pl.BlockSpec((1, tq, bw), lambda b, i: (b, i, 1)),
            pl.BlockSpec(ovl_t.shape, lambda b, i: (0, 0)),
        ],
        out_specs=pl.BlockSpec((1, tq, bw), lambda b, i: (b, i, 0)),
        out_shape=jax.ShapeDtypeStruct((nb, seq, bw), BF16),
        scratch_shapes=[pltpu.VMEM((B_KV, nsb, tq), F32)],
        compiler_params=_cparams(("arbitrary", "arbitrary")),
        name="nsa",
    )(q, kcmp, vcmp_t, kr, kr, vt, vt, kr, kr, vt, vt, gates_t, z, ovl_t)


def _moba_kernel(nblk, npair, q_ref, k_ref, vt_ref, km_ref, z_ref, o_ref, bias_ref):
    i = pl.program_id(2)
    tq = MOBA_BLOCK
    nr = 2 * tq
    ncand = bias_ref.shape[1]
    pairs = range(npair)
    lanes = [slice(p * PAIR, (p + 1) * PAIR) for p in pairs]
    qs = [_stack_heads(q_ref[0, :, lanes[p]]) for p in pairs]

    gs = []
    for p in pairs:
        km = km_ref[0, :, lanes[p]]
        pad = -nblk % SUM_ROWS
        km = jnp.concatenate([km, jnp.zeros((pad, PAIR), F32)], axis=0).astype(BF16)
        gs.append(_dot(km, qs[p])[:ncand])

    krow = lax.broadcasted_iota(jnp.int32, (tq, tq), 0)
    qcol = lax.broadcasted_iota(jnp.int32, (tq, tq), 1)
    causal = _tile_lanes(jnp.where(krow <= qcol, 0.0, NEG), 2)
    ones = jnp.ones((SUM_ROWS, tq), BF16)

    def scores(p, blk, bias):
        off = pl.multiple_of(blk * tq, tq)
        return _dot(k_ref[0, pl.ds(off, tq), lanes[p]], qs[p]) + bias

    def past_scores(p, blk):
        return scores(p, blk, bias_ref[p, pl.ds(blk, 1), :])

    def weights_times_v(e, p, blk):
        res = []
        for hh in range(2):
            v_ext = jnp.concatenate([vt_ref[0, blk, pl.ds(p * PAIR + hh * HEAD_DIM, HEAD_DIM), :], ones], axis=0)
            res.append(_dot(v_ext, e[:, hh * tq:(hh + 1) * tq]))
        res = jnp.concatenate(res, axis=1)
        return res[HEAD_DIM:HEAD_DIM + 1], res[:HEAD_DIM]

    def flash(carry, s_t, p, blk):
        m, l, acc = carry
        m_new = jnp.maximum(m, jnp.max(s_t, axis=0, keepdims=True))
        alpha = jnp.exp2(m - m_new)
        l_c, pv = weights_times_v(jnp.exp2(s_t - m_new).astype(BF16), p, blk)
        return m_new, alpha * l + l_c, alpha * acc + pv

    def write_out(results):
        outs = []
        for p in pairs:
            l_f, acc_f = results[p]
            o_t = acc_f * (1.0 / l_f)
            outs += [o_t[:, :tq], o_t[:, tq:]]
        o_ref[0] = (jnp.concatenate(outs, axis=0).T * z_ref[0]).astype(o_ref.dtype)

    s_own = [scores(p, i, causal) for p in pairs]
    jr = lax.broadcasted_iota(jnp.int32, (ncand, nr), 0)
    past = jr < i
    for p in pairs:
        sel = _topn_rows(jnp.where(past, gs[p], -jnp.inf), past, nblk, MOBA_TOPK)
        bias_ref[p] = jnp.where(sel, 0.0, NEG)
    own = []
    for p in pairs:
        m0 = s_own[p][0:1, :]
        l0, acc0 = weights_times_v(jnp.exp2(s_own[p] - m0).astype(BF16), p, i)
        own.append((m0, l0, acc0))

    def fast_blocks(blks, sums):
        sums = [list(x) for x in sums]
        pending = []

        def finish(p, blk, s_t):
            l_c, pv = weights_times_v(jnp.exp2(s_t).astype(BF16), p, blk)
            sums[p][0] = sums[p][0] + l_c
            sums[p][1] = sums[p][1] + pv

        for blk in blks:
            for p in pairs:
                pending.append((p, blk, scores(p, blk, bias_ref[p, pl.ds(blk, 1), :] - own[p][0])))
                if len(pending) > MXU_LAG:
                    finish(*pending.pop(0))
        while pending:
            finish(*pending.pop(0))
        return tuple(tuple(x) for x in sums)

    fast = lax.fori_loop(0, (i + 1) // 2, lambda u, s: fast_blocks((2 * u, 2 * u + 1), s),
                         tuple((l0, acc0) for (_, l0, acc0) in own))
    write_out(fast)
    worst = fast[0][0]
    for p in pairs[1:]:
        worst = jnp.maximum(worst, fast[p][0])
    safe = jnp.max(worst) < SAFE_SUM

    @pl.when(jnp.logical_not(safe))
    def _():
        def exact_block(blk, carries):
            return tuple(flash(carries[p], past_scores(p, blk), p, blk) for p in pairs)

        start = (jnp.full((1, nr), NEG, F32), jnp.zeros((1, nr), F32), jnp.zeros((HEAD_DIM, nr), F32))
        first = tuple(flash(start, scores(p, i, causal), p, i) for p in pairs)
        exact = lax.fori_loop(0, i, exact_block, first)
        write_out([(l, acc) for (_, l, acc) in exact])


def _moba(q, k, vt, kmean, z):
    nb, seq, width = q.shape
    tq = MOBA_BLOCK
    nblk = seq // MOBA_BLOCK
    ncand = -(-nblk // 8) * 8
    npair = MOBA_PAIRS_PER_STEP
    gw = npair * PAIR
    qblk = pl.BlockSpec((1, tq, gw), lambda b, p, i: (b, i, p))
    return pl.pallas_call(
        functools.partial(_moba_kernel, nblk, npair),
        grid=(nb, width // gw, seq // tq),
        in_specs=[
            qblk,
            pl.BlockSpec((1, seq, gw), lambda b, p, i: (b, 0, p)),
            pl.BlockSpec((1, nblk, gw, tq), lambda b, p, i: (b, 0, p, 0)),
            pl.BlockSpec((1, nblk, gw), lambda b, p, i: (b, 0, p)),
            qblk,
        ],
        out_specs=qblk,
        out_shape=jax.ShapeDtypeStruct((nb, seq, width), BF16),
        scratch_shapes=[pltpu.VMEM((npair, ncand, 2 * tq), F32)],
        compiler_params=_cparams(("arbitrary", "arbitrary", "arbitrary")),
        name="moba",
    )(q, k, vt, kmean, z)


def _even_plan(w_in, nb, seq):
    aq, akv = A_HEADS * HEAD_DIM, HEAD_DIM
    bq, bkv, bg = B_HEADS * HEAD_DIM, B_KV * HEAD_DIM, 3 * B_HEADS
    sizes = (aq, akv, akv, aq, bq, bkv, bkv, bkv, bkv, bkv, bkv, bg, bq)
    offs = np.concatenate([[0], np.cumsum(sizes)])
    qa, ka, va, za, qb, kc, vc, ks, vs, kw, vw, gb, zb = [w_in[:, offs[n]:offs[n + 1]] for n in range(len(sizes))]
    qw = aq + bq
    krw = PAIR * (1 + 2 * B_KV)
    gb4 = gb.reshape(-1, B_KV, B_REP, 3).transpose(0, 1, 3, 2).reshape(-1, B_KV, 3 * B_REP)
    gb4 = jnp.pad(gb4, ((0, 0), (0, 0), (0, GATE_ROWS - 3 * B_REP))).reshape(-1, B_KV * GATE_ROWS)
    vrows = akv + 2 * bkv
    t_width = -(-(vrows + B_KV * GATE_ROWS) // PAIR) * PAIR
    t_pad = jnp.zeros((w_in.shape[0], t_width - vrows - B_KV * GATE_ROWS), w_in.dtype)
    kheads = akv + 2 * bkv
    k_width = -(-kheads // PAIR) * PAIR
    k_pad = jnp.zeros((w_in.shape[0], k_width - kheads), w_in.dtype)
    w = jnp.concatenate([qa * (Q_SCALE * LOG2E), qb * (Q_SCALE * LOG2E), ka, ks, kw, k_pad, kc, vc, za, zb,
                         va, vs, vw, gb4, t_pad], axis=1).astype(BF16)
    groups, col = [], 0
    for width, kind in ((qw, "rope"), (k_width, "rope_dup"), (bkv, "rope"), (bkv, "plain"), (qw, "silu")):
        groups.append((col, width, kind, len(groups)))
        col += width
    groups_t = ((0, akv, "plain", 5, SWA_TQ), (akv, 2 * bkv, "plain", 6, NSA_TQ),
                (vrows, B_KV * GATE_ROWS, "sigmoid", 7, None))
    rows, tm = nb * seq, ROW_TILE
    per_b = seq // tm
    widths = [(qw, BF16), (krw, BF16), (bkv, F32), (bkv, F32), (qw, F32)]
    out_shape = [jax.ShapeDtypeStruct((rows, wd), dt) for wd, dt in widths]
    out_specs = [pl.BlockSpec((tm, wd), lambda i: (i, 0)) for wd, _ in widths]
    for n_rows, chunk in ((akv, SWA_TQ), (2 * bkv, NSA_TQ)):
        out_shape.append(jax.ShapeDtypeStruct((nb, seq // chunk, n_rows, chunk), BF16))
        out_specs.append(pl.BlockSpec((1, tm // chunk, n_rows, chunk), lambda i: (i // per_b, i % per_b, 0, 0)))
    out_shape.append(jax.ShapeDtypeStruct((nb, B_KV * GATE_ROWS, seq), F32))
    out_specs.append(pl.BlockSpec((1, B_KV * GATE_ROWS, tm), lambda i: (i // per_b, 0, i % per_b)))
    return w, (col, t_width), tuple(groups), groups_t, out_shape, out_specs


def _odd_plan(w_in, nb, seq):
    cw = C_HEADS * HEAD_DIM
    col_scale = np.ones((1, 4 * cw), np.float32)
    col_scale[:, :cw] = Q_SCALE * LOG2E
    w = (w_in * col_scale).astype(BF16)
    groups = ((0, cw, "rope", 0), (cw, cw, "rope_mean", 1), (3 * cw, cw, "silu", 2))
    groups_t = ((0, cw, "plain", 3, MOBA_BLOCK),)
    rows, tm = nb * seq, ROW_TILE
    per_b = seq // tm
    widths = [(cw, BF16), (cw, BF16), (cw, F32)]
    out_shape = [jax.ShapeDtypeStruct((rows, wd), dt) for wd, dt in widths]
    out_specs = [pl.BlockSpec((tm, wd), lambda i: (i, 0)) for wd, _ in widths]
    out_shape += [jax.ShapeDtypeStruct((nb, seq // MOBA_BLOCK, cw, MOBA_BLOCK), BF16),
                  jax.ShapeDtypeStruct((nb, seq // MOBA_BLOCK, 1, cw), F32)]
    out_specs += [pl.BlockSpec((1, tm // MOBA_BLOCK, cw, MOBA_BLOCK), lambda i: (i // per_b, i % per_b, 0, 0)),
                  pl.BlockSpec((1, tm // MOBA_BLOCK, 1, cw), lambda i: (i // per_b, i % per_b, 0, 0))]
    return w, (2 * cw, cw), groups, groups_t, out_shape, out_specs


def _rope_tables(seq):
    inv = ROPE_THETA ** (-jnp.arange(0, HEAD_DIM, 2, dtype=F32) / HEAD_DIM)
    ang = jnp.arange(seq, dtype=F32)[:, None] * inv[None, :]
    cos, sin = jnp.cos(ang), jnp.sin(ang)
    reps = PAIR // (HEAD_DIM // 2)
    sign = np.tile(np.concatenate([-np.ones(HEAD_DIM // 2), np.ones(HEAD_DIM // 2)]), PAIR // HEAD_DIM)
    return jnp.tile(cos, (1, reps)), jnp.tile(sin, (1, reps)) * jnp.asarray(sign, F32)[None, :]


def _overlap_matrix_t(seq, nch):
    nc = (seq - CMP_LEN) // CMP_STRIDE + 1
    nsb = seq // SLC_LEN
    cst = np.arange(nc) * CMP_STRIDE
    jj = np.arange(nsb)
    ov = ((cst[None, :] < (jj[:, None] + 1) * SLC_LEN) & (cst[None, :] + CMP_LEN > jj[:, None] * SLC_LEN))
    full = np.zeros((nsb, nch), np.float32)
    full[:, :nc] = ov
    return jnp.asarray(full, BF16), nc, nsb


def _cmp_weights(pe, w1, w2, transpose_out):
    pe2 = jnp.concatenate([pe, pe], axis=1)
    w1t = w1.reshape(CMP_LEN, HEAD_DIM, CMP_HIDDEN)
    zeros = jnp.zeros_like(w1t)
    w1p = jnp.concatenate([jnp.concatenate([w1t, zeros], axis=2), jnp.concatenate([zeros, w1t], axis=2)], axis=1)
    w1p = w1p.reshape(CMP_LEN // 2, 2 * PAIR, B_KV * CMP_HIDDEN)
    w2o = w2.T if transpose_out else jnp.concatenate([w2, w2], axis=1)
    return pe2, w1p.astype(BF16), w2o.astype(BF16)


def kernel(x, c, w_ada, b_ada, norm_g, w_in_even, a_sinks, cmp_pe_k, cmp_w1_k, cmp_w2_k, cmp_pe_v, cmp_w1_v, cmp_w2_v, w_out_even, w_in_odd, w_out_odd, final_g):
    nb, seq, d = x.shape
    depth = w_ada.shape[0]
    assert seq % ROW_TILE == 0 and seq % MOBA_BLOCK == 0 and NSA_TQ % SLC_LEN == 0
    assert (seq // SLC_LEN) % 8 == 0 and seq % (SWA_TQ * SWA_BLOCKS_PER_STEP) == 0 and ROW_TILE % NSA_TQ == 0
    cos_t, sin_t = _rope_tables(seq)
    ada = _ada(c, w_ada, b_ada)
    nch = seq // CMP_STRIDE
    ovl_t, nc, nsb = _overlap_matrix_t(seq, nch)
    fg = final_g.reshape(1, d)
    x2 = x.reshape(nb * seq, d)
    r3 = lambda a: a.reshape(nb, seq, a.shape[-1])
    prev = None
    for layer in range(depth):
        li = layer // 2
        ng = norm_g[layer].reshape(1, d)
        plan = _even_plan(w_in_even[li], nb, seq) if layer % 2 == 0 else _odd_plan(w_in_odd[li], nb, seq)
        w, t_cols, groups, groups_t, out_shape, out_specs = plan
        outs = _proj(x2, ada, layer, ng, cos_t, sin_t, w, t_cols, groups, groups_t, out_shape, out_specs, seq,
                     moba_mean=None if layer % 2 == 0 else 4, prev=prev)
        if prev is not None:
            x2, outs = outs[0], outs[1:]
        if layer % 2 == 0:
            q, kr, kc, vc, z, vt_a, vt_b, gates_t = outs
            q, kr, kc, vc, z = map(r3, (q, kr, kc, vc, z))
            kcmp, vcmp_t = _compress(kc, vc, *_cmp_weights(cmp_pe_k[li], cmp_w1_k[li], cmp_w2_k[li], False),
                                     *_cmp_weights(cmp_pe_v[li], cmp_w1_v[li], cmp_w2_v[li], True))
            oa = _swa(a_sinks[li], q, kr, vt_a, z, SWA_TQ)
            ob = _nsa(q, kcmp, vcmp_t, kr, vt_b, gates_t, z, ovl_t, nc, nsb, NSA_TQ)
            o_list = [oa.reshape(nb * seq, -1), ob.reshape(nb * seq, -1)]
            w_out = w_out_even[li].astype(BF16)
        else:
            q, k, z, vt, kmean = outs
            o = _moba(r3(q), r3(k), vt, kmean.reshape(nb, seq // MOBA_BLOCK, -1), r3(z))
            o_list = [o.reshape(nb * seq, -1)]
            w_out = w_out_odd[li].astype(BF16)
        prev = (o_list, w_out)
    x2 = _out_proj(x2, ada, depth - 1, prev[0], prev[1], fg, True, seq)
    return x2.reshape(nb, seq, d)
```

```python
import functools

import numpy as np
import jax
import jax.numpy as jnp
from jax import lax
from jax.experimental import pallas as pl
from jax.experimental.pallas import tpu as pltpu

D_MODEL = 1024
HEAD_DIM = 64
PAIR = 2 * HEAD_DIM
ROPE_THETA = 10000.0
RMS_EPS = 1e-6
A_HEADS = 8
A_WINDOW = 128
B_HEADS = 8
B_KV = 2
B_REP = B_HEADS // B_KV
CMP_LEN = 32
CMP_STRIDE = 16
CMP_HIDDEN = 256
SLC_LEN = 64
SLC_TOPN = 8
WIN_LEN = 512
FORCE_BONUS = 1e4
C_HEADS = 16
MOBA_BLOCK = 256
MOBA_TOPK = 3
Q_SCALE = HEAD_DIM ** -0.5
LOG2E = 1.4426950408889634

NEG = -1e30
SAFE_SUM = 2.0 ** 64
MXU_COLS = 256
ROW_TILE = 1024
SWA_TQ = 128
NSA_TQ = 256
SWA_BLOCKS_PER_STEP = 8
MOBA_PAIRS_PER_STEP = 8
MXU_LAG = 3
SUM_ROWS = 16
GATE_ROWS = 16
VMEM_LIMIT = 60 * 1024 * 1024

BF16 = jnp.bfloat16
F32 = jnp.float32


def _dot_nt(a, b):
    return lax.dot_general(a, b, (((1,), (1,)), ((), ())), preferred_element_type=F32)


def _dot(a, b):
    return jnp.dot(a, b, preferred_element_type=F32)


def _cparams(sem):
    return pltpu.CompilerParams(dimension_semantics=sem, vmem_limit_bytes=VMEM_LIMIT)


def _tile_lanes(a, n):
    return jnp.concatenate([a] * n, axis=1) if n > 1 else a


def _ada_kernel(c_ref, w_ref, b_ref, o_ref):
    c = c_ref[...]
    ca = (c * jax.nn.sigmoid(c)).astype(BF16)
    d = c.shape[-1]
    for j in range(o_ref.shape[1]):
        o_ref[0, j] = _dot(ca, w_ref[0, :, j * d:(j + 1) * d].astype(BF16)) + b_ref[0, j]


def _ada(c, w_ada, b_ada):
    depth, d, _ = w_ada.shape
    nb = c.shape[0]
    b4 = b_ada.reshape(depth, 3, 1, d)
    out = pl.pallas_call(
        _ada_kernel,
        grid=(depth,),
        in_specs=[
            pl.BlockSpec((nb, d), lambda l: (0, 0)),
            pl.BlockSpec((1, d, 3 * d), lambda l: (l, 0, 0)),
            pl.BlockSpec((1, 3, 1, d), lambda l: (l, 0, 0, 0)),
        ],
        out_specs=pl.BlockSpec((1, 3, nb, d), lambda l: (l, 0, 0, 0)),
        out_shape=jax.ShapeDtypeStruct((depth, 3, nb, d), F32),
        compiler_params=_cparams(("arbitrary",)),
        name="ada",
    )(c, w_ada, b4)
    return out.reshape(depth, 3, nb, 1, d)


def _rope(a, cos, sin_signed):
    w = a.shape[-1]
    lane = lax.broadcasted_iota(jnp.int32, a.shape, 1)
    first_half = (lane % HEAD_DIM) < (HEAD_DIM // 2)
    partner = jnp.where(first_half, pltpu.roll(a, w - HEAD_DIM // 2, 1), pltpu.roll(a, HEAD_DIM // 2, 1))
    return a * cos + partner * sin_signed


def _proj_kernel(groups, groups_t, t_cols, moba_mean, n_prev_o, *refs):
    x_ref = refs[0]
    prev_refs, refs = refs[1:1 + (n_prev_o + 2 if n_prev_o else 0)], refs[1 + (n_prev_o + 2 if n_prev_o else 0):]
    shift_ref, scale_ref, g_ref, cos_ref, sin_ref, w_ref = refs[:6]
    out_refs = refs[6:]
    t_start, t_width = t_cols

    x = x_ref[...]
    if n_prev_o:
        gate_ref, wout_ref = prev_refs[0], prev_refs[-1]
        mix, k0 = None, 0
        for o_ref in prev_refs[1:-1]:
            kw = o_ref.shape[-1]
            part = _dot(o_ref[...], wout_ref[k0:k0 + kw, :])
            mix = part if mix is None else mix + part
            k0 += kw
        x = x + gate_ref[0, 0, 0] * mix
        out_refs[0][...] = x
        out_refs = out_refs[1:]
    y = x * lax.rsqrt(jnp.mean(x * x, axis=-1, keepdims=True) + RMS_EPS)
    h = (y * g_ref[...]) * (1.0 + scale_ref[0, 0, 0]) + shift_ref[0, 0, 0]
    hb = h.astype(BF16)
    tm = hb.shape[0]
    cos, sin = cos_ref[...], sin_ref[...]
    lo_half = lax.broadcasted_iota(jnp.int32, (tm, PAIR), 1) < HEAD_DIM
    pair_plan = {}
    for (w_start, width, kind, out_idx) in groups:
        for c in range(0, width, PAIR):
            pair_plan[(w_start + c) // PAIR] = (kind, out_idx, c)
    n_cols = w_ref.shape[1]
    for s0 in range(0, n_cols, MXU_COLS):
        sw = min(MXU_COLS, n_cols - s0)
        acc = _dot(hb, w_ref[:, s0:s0 + sw])
        for h0 in range(0, sw, PAIR):
            col = s0 + h0
            a = acc[:, h0:h0 + PAIR]
            if t_start <= col < t_start + t_width:
                f0 = col - t_start
                a_t = a.T
                for (r_start, n_rows, kind, out_idx, chunk) in groups_t:
                    lo, hi = max(r_start, f0), min(r_start + n_rows, f0 + PAIR)
                    if lo >= hi:
                        continue
                    o_ref = out_refs[out_idx]
                    part = a_t[lo - f0:hi - f0]
                    if kind == "sigmoid":
                        o_ref[0, lo - r_start:hi - r_start, :] = jax.nn.sigmoid(part)
                    else:
                        for jj in range(tm // chunk):
                            o_ref[0, jj, lo - r_start:hi - r_start, :] = (
                                part[:, jj * chunk:(jj + 1) * chunk].astype(o_ref.dtype))
                continue
            kind, out_idx, c = pair_plan[col // PAIR]
            o_ref = out_refs[out_idx]
            if kind in ("rope", "rope_mean", "rope_dup"):
                a = _rope(a, cos, sin)
            elif kind == "silu":
                a = a * jax.nn.sigmoid(a)
            if kind == "rope_dup":
                b = pltpu.roll(a, HEAD_DIM, 1)
                for hh, dup in enumerate((jnp.where(lo_half, a, b), jnp.where(lo_half, b, a))):
                    dst = (2 * (c // PAIR) + hh) * PAIR
                    if dst < o_ref.shape[1]:
                        o_ref[:, dst:dst + PAIR] = dup.astype(o_ref.dtype)
                continue
            o_ref[:, c:c + PAIR] = a.astype(o_ref.dtype)
            if kind == "rope_mean":
                km_ref = out_refs[moba_mean]
                for j in range(tm // MOBA_BLOCK):
                    blk = a[j * MOBA_BLOCK:(j + 1) * MOBA_BLOCK]
                    km_ref[0, j, :, c:c + PAIR] = jnp.sum(blk, axis=0, keepdims=True) * (1.0 / MOBA_BLOCK)


def _proj(x2, ada, layer, norm_g, cos_t, sin_t, w, t_cols, groups, groups_t, out_shape, out_specs, seq,
          moba_mean=None, prev=None):
    rows, d = x2.shape
    tm = ROW_TILE
    per_b = seq // tm
    args, in_specs = [x2], [pl.BlockSpec((tm, d), lambda i: (i, 0))]
    n_prev_o = 0
    if prev is not None:
        o_list, w_out = prev
        n_prev_o = len(o_list)
        args += [ada, *o_list, w_out]
        in_specs.append(pl.BlockSpec((1, 1, 1, 1, d), lambda i: (layer - 1, 2, i // per_b, 0, 0)))
        in_specs += [pl.BlockSpec((tm, o.shape[-1]), lambda i: (i, 0)) for o in o_list]
        in_specs.append(pl.BlockSpec(w_out.shape, lambda i: (0, 0)))
        out_shape = [jax.ShapeDtypeStruct((rows, d), F32)] + list(out_shape)
        out_specs = [pl.BlockSpec((tm, d), lambda i: (i, 0))] + list(out_specs)
    args += [ada, ada, norm_g, cos_t, sin_t, w]
    in_specs += [
        pl.BlockSpec((1, 1, 1, 1, d), lambda i: (layer, 0, i // per_b, 0, 0)),
        pl.BlockSpec((1, 1, 1, 1, d), lambda i: (layer, 1, i // per_b, 0, 0)),
        pl.BlockSpec((1, d), lambda i: (0, 0)),
        pl.BlockSpec((tm, PAIR), lambda i: (i % per_b, 0)),
        pl.BlockSpec((tm, PAIR), lambda i: (i % per_b, 0)),
        pl.BlockSpec(w.shape, lambda i: (0, 0)),
    ]
    return pl.pallas_call(
        functools.partial(_proj_kernel, groups, groups_t, t_cols, moba_mean, n_prev_o),
        grid=(rows // tm,),
        in_specs=in_specs,
        out_specs=out_specs,
        out_shape=out_shape,
        compiler_params=_cparams(("arbitrary",)),
        name="proj",
    )(*args)


def _out_kernel(n_o, final, *refs):
    x_ref, gate_ref = refs[0], refs[1]
    o_refs = refs[2:2 + n_o]
    w_ref = refs[2 + n_o]
    fg_ref = refs[3 + n_o]
    out_ref = refs[4 + n_o]
    y = None
    k0 = 0
    for o_ref in o_refs:
        kw = o_ref.shape[-1]
        part = _dot(o_ref[...], w_ref[k0:k0 + kw, :])
        y = part if y is None else y + part
        k0 += kw
    xn = x_ref[...] + gate_ref[0, 0, 0] * y
    if final:
        xn = (xn * lax.rsqrt(jnp.mean(xn * xn, axis=-1, keepdims=True) + RMS_EPS)) * fg_ref[...]
    out_ref[...] = xn


def _out_proj(x2, ada, layer, o_list, w_out, final_g, final, seq):
    rows, d = x2.shape
    tm = ROW_TILE
    per_b = seq // tm
    n_o = len(o_list)
    in_specs = [
        pl.BlockSpec((tm, d), lambda i: (i, 0)),
        pl.BlockSpec((1, 1, 1, 1, d), lambda i: (layer, 2, i // per_b, 0, 0)),
    ]
    in_specs += [pl.BlockSpec((tm, o.shape[-1]), lambda i: (i, 0)) for o in o_list]
    in_specs += [pl.BlockSpec(w_out.shape, lambda i: (0, 0)), pl.BlockSpec((1, d), lambda i: (0, 0))]
    return pl.pallas_call(
        functools.partial(_out_kernel, n_o, final),
        grid=(rows // tm,),
        in_specs=in_specs,
        out_specs=pl.BlockSpec((tm, d), lambda i: (i, 0)),
        out_shape=jax.ShapeDtypeStruct((rows, d), F32),
        compiler_params=_cparams(("arbitrary",)),
        name="out_proj",
    )(x2, ada, *o_list, w_out, final_g)


def _stack_heads(q_pairs):
    row = lax.broadcasted_iota(jnp.int32, (PAIR, 1), 0)
    m_lo = (row < HEAD_DIM).astype(F32)
    m_hi = (row >= HEAD_DIM).astype(F32)
    blocks = []
    for p in range(q_pairs.shape[-1] // PAIR):
        qp_t = q_pairs[:, p * PAIR:(p + 1) * PAIR].astype(F32).T
        blocks += [qp_t * m_lo, qp_t * m_hi]
    return jnp.concatenate(blocks, axis=1).astype(BF16)


def _flash_init(d_rows, n_cols):
    return jnp.full((1, n_cols), NEG, F32), jnp.zeros((1, n_cols), F32), jnp.zeros((d_rows, n_cols), F32)


def _flash(carry, s_t, v_t_chunks):
    m, l, acc = carry
    m_new = jnp.maximum(m, jnp.max(s_t, axis=0, keepdims=True))
    alpha = jnp.exp2(m - m_new)
    e = jnp.exp2(s_t - m_new).astype(BF16)
    l = alpha * l + _dot(jnp.ones((SUM_ROWS, e.shape[0]), BF16), e)[0:1]
    acc, k0 = alpha * acc, 0
    for vt in v_t_chunks:
        acc = acc + _dot(vt, e[k0:k0 + vt.shape[1]])
        k0 += vt.shape[1]
    return m_new, l, acc


def _topn_rows(score, valid, n_cand, topn):
    sub = 8
    n_rows, n_cols = score.shape
    tiles = [score[r0:r0 + sub] for r0 in range(0, n_rows, sub)]
    ranks = [jnp.zeros((sub, n_cols), F32) for _ in tiles]
    ridx = lax.broadcasted_iota(jnp.int32, (sub, n_cols), 0)
    for jp in range(n_cand):
        row = score[jp:jp + 1, :]
        for t, tile in enumerate(tiles):
            r0 = t * sub
            if r0 > jp:
                beats = jnp.where(row >= tile, 1.0, 0.0)
            elif r0 + sub - 1 <= jp:
                beats = jnp.where(row > tile, 1.0, 0.0)
            else:
                later = jnp.where(ridx + r0 > jp, 1.0, 0.0)
                beats = jnp.where(row > tile, 1.0, jnp.where(row == tile, later, 0.0))
            ranks[t] = ranks[t] + beats
    rank = jnp.concatenate(ranks, axis=0) if len(ranks) > 1 else ranks[0]
    return valid & (rank < topn)


def _swa_kernel(tq, nqb, sink_ref, q_ref, k_ref, vt_ref, z_ref, o_ref):
    step = pl.program_id(1)
    n_prev = -(-(A_WINDOW - 1) // tq)
    krow = lax.broadcasted_iota(jnp.int32, (tq, tq), 0)
    qcol = lax.broadcasted_iota(jnp.int32, (tq, tq), 1)
    causal = _tile_lanes(jnp.where(krow <= qcol, 0.0, NEG), A_HEADS)
    edge = _tile_lanes(jnp.where(n_prev * tq + qcol - krow < A_WINDOW, 0.0, NEG), A_HEADS)
    ones = jnp.ones((SUM_ROWS, tq), BF16)
    sink = jnp.concatenate([jnp.full((1, tq), sink_ref[hd] * LOG2E, F32) for hd in range(A_HEADS)], axis=1)

    blocks = [step * nqb + bb for bb in range(nqb)]
    qs = [_stack_heads(q_ref[0, bb * tq:(bb + 1) * tq, :]) for bb in range(nqb)]
    scores = []
    for bb, i in enumerate(blocks):
        row = []
        for j in range(n_prev + 1):
            off = pl.multiple_of(jnp.maximum(i - j, 0) * tq, tq)
            s_t = _dot(k_ref[0, pl.ds(off, tq), :], qs[bb])
            s_t = s_t + causal if j == 0 else (s_t + edge if j == n_prev else s_t)
            row.append(s_t if j == 0 else s_t + jnp.where(i - j >= 0, 0.0, NEG))
        scores.append(row)

    outs = []
    for bb, i in enumerate(blocks):
        m = sink
        for s_t in scores[bb]:
            m = jnp.maximum(m, jnp.max(s_t, axis=0, keepdims=True))
        den, acc = jnp.exp2(sink - m), None
        for j, s_t in enumerate(scores[bb]):
            v_ext = jnp.concatenate([vt_ref[0, jnp.maximum(i - j, 0)], ones], axis=0)
            res = _dot(v_ext, jnp.exp2(s_t - m).astype(BF16))
            den = den + res[HEAD_DIM:HEAD_DIM + 1]
            acc = res[:HEAD_DIM] if acc is None else acc + res[:HEAD_DIM]
        o_t = acc * (1.0 / den)
        heads = jnp.concatenate([o_t[:, hd * tq:(hd + 1) * tq] for hd in range(A_HEADS)], axis=0)
        outs.append(heads.T)
    o_ref[0] = (jnp.concatenate(outs, axis=0) * z_ref[0]).astype(o_ref.dtype)


def _swa(sinks, q, kr, vt, z, tq):
    nb, seq, _ = q.shape
    aw = A_HEADS * HEAD_DIM
    nqb = SWA_BLOCKS_PER_STEP
    rows = nqb * tq
    return pl.pallas_call(
        functools.partial(_swa_kernel, tq, nqb),
        grid=(nb, seq // rows),
        in_specs=[
            pl.BlockSpec(memory_space=pltpu.SMEM),
            pl.BlockSpec((1, rows, aw), lambda b, i: (b, i, 0)),
            pl.BlockSpec((1, seq, PAIR), lambda b, i: (b, 0, 0)),
            pl.BlockSpec((1, seq // tq, HEAD_DIM, tq), lambda b, i: (b, 0, 0, 0)),
            pl.BlockSpec((1, rows, aw), lambda b, i: (b, i, 0)),
        ],
        out_specs=pl.BlockSpec((1, rows, aw), lambda b, i: (b, i, 0)),
        out_shape=jax.ShapeDtypeStruct((nb, seq, aw), BF16),
        compiler_params=_cparams(("arbitrary", "arbitrary")),
        name="swa",
    )(sinks, q, kr, vt, z)


def _cmp_hidden(x_ref, pe_ref, w1_ref, nch):
    top, bot = None, None
    half = CMP_STRIDE // 2
    for l in range(0, CMP_STRIDE, 2):
        rows = [x_ref[0, pl.ds(l + t, nch, stride=CMP_STRIDE), :] for t in range(2)]
        pair = lambda base: jnp.concatenate(
            [(rows[t] + pe_ref[base + l + t:base + l + t + 1, :]).astype(BF16) for t in range(2)], axis=1)
        t = _dot(pair(0), w1_ref[l // 2])
        b = _dot(pair(CMP_STRIDE), w1_ref[half + l // 2])
        top = t if top is None else top + t
        bot = b if bot is None else bot + b
    return jax.nn.gelu(top + pltpu.roll(bot, nch - 1, 0))


def _compress_kernel(kc_ref, vc_ref, pek_ref, w1k_ref, w2k_ref, pev_ref, w1v_ref, w2v_ref, ko_ref, vo_ref):
    nch = ko_ref.shape[2]
    hk = _cmp_hidden(kc_ref, pek_ref, w1k_ref, nch).astype(BF16)
    hv = _cmp_hidden(vc_ref, pev_ref, w1v_ref, nch).astype(BF16)
    for g in range(B_KV):
        gs = slice(g * CMP_HIDDEN, (g + 1) * CMP_HIDDEN)
        ko_ref[0, g] = _dot(hk[:, gs], w2k_ref[...]).astype(ko_ref.dtype)
        vo_ref[0, g] = _dot_nt(w2v_ref[...], hv[:, gs]).astype(vo_ref.dtype)


def _compress(kc, vc, pek, w1k, w2k, pev, w1v, w2v):
    nb, seq, _ = kc.shape
    nch = seq // CMP_STRIDE
    full = lambda a: pl.BlockSpec(a.shape, lambda b: (0,) * a.ndim)
    blk = pl.BlockSpec((1, seq, PAIR), lambda b: (b, 0, 0))
    return pl.pallas_call(
        _compress_kernel,
        grid=(nb,),
        in_specs=[blk, blk, full(pek), full(w1k), full(w2k), full(pev), full(w1v), full(w2v)],
        out_specs=[pl.BlockSpec((1, B_KV, nch, PAIR), lambda b: (b, 0, 0, 0)),
                   pl.BlockSpec((1, B_KV, HEAD_DIM, nch), lambda b: (b, 0, 0, 0))],
        out_shape=[jax.ShapeDtypeStruct((nb, B_KV, nch, PAIR), BF16),
                   jax.ShapeDtypeStruct((nb, B_KV, HEAD_DIM, nch), BF16)],
        compiler_params=_cparams(("arbitrary",)),
        name="compress",
    )(kc, vc, pek, w1k, w2k, pev, w1v, w2v)


def _nsa_kernel(tq, nc, nsb, q_ref, kc_ref, vct_ref, ks0_ref, ks1_ref, vs0_ref, vs1_ref, kw0_ref, kw1_ref,
                vw0_ref, vw1_ref, gt_ref, z_ref, ovl_ref, o_ref, bias_ref):
    i = pl.program_id(1)
    groups = range(B_KV)
    ks_refs, vs_refs = (ks0_ref, ks1_ref), (vs0_ref, vs1_ref)
    kw_refs, vw_refs = (kw0_ref, kw1_ref), (vw0_ref, vw1_ref)
    gw = B_REP * HEAD_DIM
    nr = B_REP * tq
    qs = [_stack_heads(q_ref[0, :, g * gw:(g + 1) * gw]) for g in groups]
    t_row = i * tq + lax.broadcasted_iota(jnp.int32, (1, tq), 1)

    nch = kc_ref.shape[2]
    cidx = lax.broadcasted_iota(jnp.int32, (nch, tq), 0)
    ok_c = (cidx < nc) & (t_row >= cidx * CMP_STRIDE + (CMP_LEN - 1))
    bias_c = _tile_lanes(jnp.where(ok_c, 0.0, NEG), B_REP)
    jr = lax.broadcasted_iota(jnp.int32, (nsb, tq), 0)
    tb = t_row // SLC_LEN
    valid = jr <= tb
    forced = (jr == 0) | (jr == tb) | (jr == tb - 1)
    n_prev = -(-(WIN_LEN - 1) // tq)
    blk_per_chunk = tq // SLC_LEN
    krow = lax.broadcasted_iota(jnp.int32, (tq, tq), 0)
    qcol = lax.broadcasted_iota(jnp.int32, (tq, tq), 1)
    causal = jnp.where(krow <= qcol, 0.0, NEG)
    edge = _tile_lanes(jnp.where(n_prev * tq + qcol - krow < WIN_LEN, 0.0, NEG), B_REP)
    causal_r = _tile_lanes(causal, B_REP)
    ones = jnp.ones((SUM_ROWS, tq), BF16)

    def win_chunk(j):
        return jnp.maximum(i - j, 0)

    def win_scores(g, j):
        off = pl.multiple_of(win_chunk(j) * tq, tq)
        s_t = _dot(kw_refs[g][0, pl.ds(off, tq), :], qs[g])
        return s_t + causal_r if j == 0 else (s_t + edge if j == n_prev else s_t)

    def slc_bias(g, c, diagonal):
        rows = [jnp.broadcast_to(bias_ref[g, pl.ds(c * blk_per_chunk + bb, 1), :], (SLC_LEN, tq))
                for bb in range(blk_per_chunk)]
        b = jnp.concatenate(rows, axis=0)
        return _tile_lanes(b + causal if diagonal else b, B_REP)

    def slc_qk(g, c):
        off = pl.multiple_of(c * tq, tq)
        return _dot(ks_refs[g][0, pl.ds(off, tq), :], qs[g])

    def sums_and_pv(v_t, e):
        res = _dot(jnp.concatenate([v_t, ones], axis=0), e)
        return res[HEAD_DIM:HEAD_DIM + 1], res[:HEAD_DIM]

    def masked_ref(m0, present):
        return m0 + jnp.where(present, 0.0, -NEG)

    s_cmp = [_dot(kc_ref[0, g], qs[g]) + bias_c for g in groups]
    s_win0 = [win_scores(g, 0) for g in groups]
    s_diag = [slc_qk(g, i) for g in groups]

    o_cmp = []
    for g in groups:
        s = s_cmp[g]
        m = jnp.max(s, axis=0, keepdims=True)
        m = jnp.where(m > 0.5 * NEG, m, 0.0)
        e = jnp.exp2(s - m)
        den = jnp.sum(e, axis=0, keepdims=True)
        p_t = (e * (1.0 / jnp.where(den > 0, den, 1.0))).astype(BF16)
        res = _dot(jnp.concatenate([vct_ref[0, g], ovl_ref[...]], axis=0), p_t)
        o_cmp.append(res[:HEAD_DIM])
        imp = res[HEAD_DIM:, 0:tq]
        for r in range(1, B_REP):
            imp = imp + res[HEAD_DIM:, r * tq:(r + 1) * tq]
        score = jnp.where(valid, jnp.where(forced, FORCE_BONUS, imp), -jnp.inf)
        bias_ref[g] = jnp.where(_topn_rows(score, valid, nsb, SLC_TOPN), 0.0, NEG)

    def write_out(slc, win):
        heads = []
        for g in groups:
            o_slc = slc[g][1] * (1.0 / slc[g][0])
            o_win = win[g][1] * (1.0 / win[g][0])
            gt = gt_ref[0, g * GATE_ROWS:(g + 1) * GATE_ROWS]
            for r in range(B_REP):
                cs = slice(r * tq, (r + 1) * tq)
                heads.append(gt[r:r + 1] * o_cmp[g][:, cs] + gt[B_REP + r:B_REP + r + 1] * o_slc[:, cs]
                             + gt[2 * B_REP + r:2 * B_REP + r + 1] * o_win[:, cs])
        o_t = jnp.concatenate(heads, axis=0)
        o_ref[0] = (o_t.T * z_ref[0]).astype(o_ref.dtype)

    win_m0 = [s_win0[g][0:1, :] for g in groups]
    win = [list(sums_and_pv(vw_refs[g][0, i], jnp.exp2(s_win0[g] - win_m0[g]).astype(BF16))) for g in groups]
    pending = []

    def finish_win(g, j, arg):
        l_c, pv = sums_and_pv(vw_refs[g][0, win_chunk(j)], jnp.exp2(arg).astype(BF16))
        win[g][0], win[g][1] = win[g][0] + l_c, win[g][1] + pv

    for j in range(1, n_prev + 1):
        for g in groups:
            pending.append((g, j, win_scores(g, j) - masked_ref(win_m0[g], i - j >= 0)))
            if len(pending) > MXU_LAG:
                finish_win(*pending.pop(0))
    while pending:
        finish_win(*pending.pop(0))

    own_blk = _tile_lanes(lax.broadcasted_iota(jnp.int32, (1, tq), 1) // SLC_LEN, B_REP)
    slc_m0 = []
    for g in groups:
        ref = s_diag[g][0:1, :]
        for bb in range(1, blk_per_chunk):
            ref = jnp.where(own_blk == bb, s_diag[g][bb * SLC_LEN:bb * SLC_LEN + 1, :], ref)
        slc_m0.append(ref)

    def slc_arg(g, c, raw, ref_row, diagonal):
        parts = []
        for bb in range(blk_per_chunk):
            rows = slice(bb * SLC_LEN, (bb + 1) * SLC_LEN)
            shift = _tile_lanes(bias_ref[g, pl.ds(c * blk_per_chunk + bb, 1), :], B_REP) - ref_row
            parts.append(raw[rows] + shift + causal_r[rows] if diagonal else raw[rows] + shift)
        return jnp.concatenate(parts, axis=0)

    slc0 = tuple(sums_and_pv(vs_refs[g][0, i],
                             jnp.exp2(slc_arg(g, i, s_diag[g], slc_m0[g], True)).astype(BF16)) for g in groups)

    def fast_pair(u, sums):
        sums = [list(x) for x in sums]
        waiting = []

        def finish(g, c, arg):
            l_c, pv = sums_and_pv(vs_refs[g][0, c], jnp.exp2(arg).astype(BF16))
            sums[g][0], sums[g][1] = sums[g][0] + l_c, sums[g][1] + pv

        for c in (2 * u, 2 * u + 1):
            for g in groups:
                waiting.append((g, c, slc_arg(g, c, slc_qk(g, c), masked_ref(slc_m0[g], c < i), False)))
                if len(waiting) > MXU_LAG:
                    finish(*waiting.pop(0))
        while waiting:
            finish(*waiting.pop(0))
        return tuple(tuple(x) for x in sums)

    slc = lax.fori_loop(0, (i + 1) // 2, fast_pair, slc0)
    write_out(slc, win)
    worst = jnp.maximum(jnp.maximum(slc[0][0], slc[1][0]), jnp.maximum(win[0][0], win[1][0]))
    safe = jnp.max(worst) < SAFE_SUM

    @pl.when(jnp.logical_not(safe))
    def _():
        span = (n_prev + 1) * tq
        c0 = jnp.maximum(i - n_prev, 0)
        start = pl.multiple_of(c0 * tq, tq)
        diff = t_row - (start + lax.broadcasted_iota(jnp.int32, (span, tq), 0))
        bias_w = _tile_lanes(jnp.where(diff >= 0, jnp.where(diff < WIN_LEN, 0.0, NEG), NEG), B_REP)
        win_x = []
        for g in groups:
            sw = _dot(kw_refs[g][0, pl.ds(start, span), :], qs[g]) + bias_w
            _, l_w, acc_w = _flash(_flash_init(HEAD_DIM, nr), sw,
                                   [vw_refs[g][0, c0 + cc] for cc in range(n_prev + 1)])
            win_x.append((l_w, acc_w))

        def exact_pair(u, carries):
            out = []
            for g in groups:
                carry = carries[g]
                for c in (2 * u, 2 * u + 1):
                    s_t = slc_qk(g, c) + slc_bias(g, c, False) + jnp.where(c < i, 0.0, NEG)
                    carry = _flash(carry, s_t, [vs_refs[g][0, c]])
                out.append(carry)
            return tuple(out)

        first = tuple(_flash(_flash_init(HEAD_DIM, nr), slc_qk(g, i) + slc_bias(g, i, True), [vs_refs[g][0, i]])
                      for g in groups)
        exact = lax.fori_loop(0, (i + 1) // 2, exact_pair, first)
        write_out([(l, acc) for (_, l, acc) in exact], win_x)


def _nsa(q, kcmp, vcmp_t, kr, vt, gates_t, z, ovl_t, nc, nsb, tq):
    nb, seq, _ = q.shape
    bw = B_HEADS * HEAD_DIM
    nch = kcmp.shape[2]
    kblk = lambda col: pl.BlockSpec((1, seq, PAIR), lambda b, i: (b, 0, col))
    vblk = lambda row: pl.BlockSpec((1, seq // tq, HEAD_DIM, tq), lambda b, i: (b, 0, row, 0))
    return pl.pallas_call(
        functools.partial(_nsa_kernel, tq, nc, nsb),
        grid=(nb, seq // tq),
        in_specs=[
            pl.BlockSpec((1, tq, bw), lambda b, i: (b, i, 1)),
            pl.BlockSpec((1, B_KV, nch, PAIR), lambda b, i: (b, 0, 0, 0)),
            pl.BlockSpec((1, B_KV, HEAD_DIM, nch), lambda b, i: (b, 0, 0, 0)),
            kblk(1), kblk(2), vblk(0), vblk(1), kblk(3), kblk(4), vblk(2), vblk(3),
            pl.BlockSpec((1, B_KV * GATE_ROWS, tq), lambda b, i: (b, 0, i)),
            pl.BlockSpec((1, tq, bw), lambda b, i: (b, i, 1)),
            pl.BlockSpec(ovl_t.shape, lambda b, i: (0, 0)),
        ],
        out_specs=pl.BlockSpec((1, tq, bw), lambda b, i: (b, i, 0)),
        out_shape=jax.ShapeDtypeStruct((nb, seq, bw), BF16),
        scratch_shapes=[pltpu.VMEM((B_KV, nsb, tq), F32)],
        compiler_params=_cparams(("arbitrary", "arbitrary")),
        name="nsa",
    )(q, kcmp, vcmp_t, kr, kr, vt, vt, kr, kr, vt, vt, gates_t, z, ovl_t)


def _moba_kernel(nblk, npair, q_ref, k_ref, vt_ref, km_ref, z_ref, o_ref, bias_ref):
    i = pl.program_id(2)
    tq = MOBA_BLOCK
    nr = 2 * tq
    ncand = bias_ref.shape[1]
    pairs = range(npair)
    lanes = [slice(p * PAIR, (p + 1) * PAIR) for p in pairs]
    qs = [_stack_heads(q_ref[0, :, lanes[p]]) for p in pairs]

    gs = []
    for p in pairs:
        km = km_ref[0, :, lanes[p]]
        pad = -nblk % SUM_ROWS
        km = jnp.concatenate([km, jnp.zeros((pad, PAIR), F32)], axis=0).astype(BF16)
        gs.append(_dot(km, qs[p])[:ncand])

    krow = lax.broadcasted_iota(jnp.int32, (tq, tq), 0)
    qcol = lax.broadcasted_iota(jnp.int32, (tq, tq), 1)
    causal = _tile_lanes(jnp.where(krow <= qcol, 0.0, NEG), 2)
    ones = jnp.ones((SUM_ROWS, tq), BF16)

    def scores(p, blk, bias):
        off = pl.multiple_of(blk * tq, tq)
        return _dot(k_ref[0, pl.ds(off, tq), lanes[p]], qs[p]) + bias

    def past_scores(p, blk):
        return scores(p, blk, bias_ref[p, pl.ds(blk, 1), :])

    def weights_times_v(e, p, blk):
        res = []
        for hh in range(2):
            v_ext = jnp.concatenate([vt_ref[0, blk, pl.ds(p * PAIR + hh * HEAD_DIM, HEAD_DIM), :], ones], axis=0)
            res.append(_dot(v_ext, e[:, hh * tq:(hh + 1) * tq]))
        res = jnp.concatenate(res, axis=1)
        return res[HEAD_DIM:HEAD_DIM + 1], res[:HEAD_DIM]

    def flash(carry, s_t, p, blk):
        m, l, acc = carry
        m_new = jnp.maximum(m, jnp.max(s_t, axis=0, keepdims=True))
        alpha = jnp.exp2(m - m_new)
        l_c, pv = weights_times_v(jnp.exp2(s_t - m_new).astype(BF16), p, blk)
        return m_new, alpha * l + l_c, alpha * acc + pv

    def write_out(results):
        outs = []
        for p in pairs:
            l_f, acc_f = results[p]
            o_t = acc_f * (1.0 / l_f)
            outs += [o_t[:, :tq], o_t[:, tq:]]
        o_ref[0] = (jnp.concatenate(outs, axis=0).T * z_ref[0]).astype(o_ref.dtype)

    s_own = [scores(p, i, causal) for p in pairs]
    jr = lax.broadcasted_iota(jnp.int32, (ncand, nr), 0)
    past = jr < i
    for p in pairs:
        sel = _topn_rows(jnp.where(past, gs[p], -jnp.inf), past, nblk, MOBA_TOPK)
        bias_ref[p] = jnp.where(sel, 0.0, NEG)
    own = []
    for p in pairs:
        m0 = s_own[p][0:1, :]
        l0, acc0 = weights_times_v(jnp.exp2(s_own[p] - m0).astype(BF16), p, i)
        own.append((m0, l0, acc0))

    def fast_blocks(blks, sums):
        sums = [list(x) for x in sums]
        pending = []

        def finish(p, blk, s_t):
            l_c, pv = weights_times_v(jnp.exp2(s_t).astype(BF16), p, blk)
            sums[p][0] = sums[p][0] + l_c
            sums[p][1] = sums[p][1] + pv

        for blk in blks:
            for p in pairs:
                pending.append((p, blk, scores(p, blk, bias_ref[p, pl.ds(blk, 1), :] - own[p][0])))
                if len(pending) > MXU_LAG:
                    finish(*pending.pop(0))
        while pending:
            finish(*pending.pop(0))
        return tuple(tuple(x) for x in sums)

    fast = lax.fori_loop(0, (i + 1) // 2, lambda u, s: fast_blocks((2 * u, 2 * u + 1), s),
                         tuple((l0, acc0) for (_, l0, acc0) in own))
    write_out(fast)
    worst = fast[0][0]
    for p in pairs[1:]:
        worst = jnp.maximum(worst, fast[p][0])
    safe = jnp.max(worst) < SAFE_SUM

    @pl.when(jnp.logical_not(safe))
    def _():
        def exact_block(blk, carries):
            return tuple(flash(carries[p], past_scores(p, blk), p, blk) for p in pairs)

        start = (jnp.full((1, nr), NEG, F32), jnp.zeros((1, nr), F32), jnp.zeros((HEAD_DIM, nr), F32))
        first = tuple(flash(start, scores(p, i, causal), p, i) for p in pairs)
        exact = lax.fori_loop(0, i, exact_block, first)
        write_out([(l, acc) for (_, l, acc) in exact])


def _moba(q, k, vt, kmean, z):
    nb, seq, width = q.shape
    tq = MOBA_BLOCK
    nblk = seq // MOBA_BLOCK
    ncand = -(-nblk // 8) * 8
    npair = MOBA_PAIRS_PER_STEP
    gw = npair * PAIR
    qblk = pl.BlockSpec((1, tq, gw), lambda b, p, i: (b, i, p))
    return pl.pallas_call(
        functools.partial(_moba_kernel, nblk, npair),
        grid=(nb, width // gw, seq // tq),
        in_specs=[
            qblk,
            pl.BlockSpec((1, seq, gw), lambda b, p, i: (b, 0, p)),
            pl.BlockSpec((1, nblk, gw, tq), lambda b, p, i: (b, 0, p, 0)),
            pl.BlockSpec((1, nblk, gw), lambda b, p, i: (b, 0, p)),
            qblk,
        ],
        out_specs=qblk,
        out_shape=jax.ShapeDtypeStruct((nb, seq, width), BF16),
        scratch_shapes=[pltpu.VMEM((npair, ncand, 2 * tq), F32)],
        compiler_params=_cparams(("arbitrary", "arbitrary", "arbitrary")),
        name="moba",
    )(q, k, vt, kmean, z)


def _even_plan(w_in, nb, seq):
    aq, akv = A_HEADS * HEAD_DIM, HEAD_DIM
    bq, bkv, bg = B_HEADS * HEAD_DIM, B_KV * HEAD_DIM, 3 * B_HEADS
    sizes = (aq, akv, akv, aq, bq, bkv, bkv, bkv, bkv, bkv, bkv, bg, bq)
    offs = np.concatenate([[0], np.cumsum(sizes)])
    qa, ka, va, za, qb, kc, vc, ks, vs, kw, vw, gb, zb = [w_in[:, offs[n]:offs[n + 1]] for n in range(len(sizes))]
    qw = aq + bq
    krw = PAIR * (1 + 2 * B_KV)
    gb4 = gb.reshape(-1, B_KV, B_REP, 3).transpose(0, 1, 3, 2).reshape(-1, B_KV, 3 * B_REP)
    gb4 = jnp.pad(gb4, ((0, 0), (0, 0), (0, GATE_ROWS - 3 * B_REP))).reshape(-1, B_KV * GATE_ROWS)
    vrows = akv + 2 * bkv
    t_width = -(-(vrows + B_KV * GATE_ROWS) // PAIR) * PAIR
    t_pad = jnp.zeros((w_in.shape[0], t_width - vrows - B_KV * GATE_ROWS), w_in.dtype)
    kheads = akv + 2 * bkv
    k_width = -(-kheads // PAIR) * PAIR
    k_pad = jnp.zeros((w_in.shape[0], k_width - kheads), w_in.dtype)
    w = jnp.concatenate([qa * (Q_SCALE * LOG2E), qb * (Q_SCALE * LOG2E), ka, ks, kw, k_pad, kc, vc, za, zb,
                         va, vs, vw, gb4, t_pad], axis=1).astype(BF16)
    groups, col = [], 0
    for width, kind in ((qw, "rope"), (k_width, "rope_dup"), (bkv, "rope"), (bkv, "plain"), (qw, "silu")):
        groups.append((col, width, kind, len(groups)))
        col += width
    groups_t = ((0, akv, "plain", 5, SWA_TQ), (akv, 2 * bkv, "plain", 6, NSA_TQ),
                (vrows, B_KV * GATE_ROWS, "sigmoid", 7, None))
    rows, tm = nb * seq, ROW_TILE
    per_b = seq // tm
    widths = [(qw, BF16), (krw, BF16), (bkv, F32), (bkv, F32), (qw, BF16)]
    out_shape = [jax.ShapeDtypeStruct((rows, wd), dt) for wd, dt in widths]
    out_specs = [pl.BlockSpec((tm, wd), lambda i: (i, 0)) for wd, _ in widths]
    for n_rows, chunk in ((akv, SWA_TQ), (2 * bkv, NSA_TQ)):
        out_shape.append(jax.ShapeDtypeStruct((nb, seq // chunk, n_rows, chunk), BF16))
        out_specs.append(pl.BlockSpec((1, tm // chunk, n_rows, chunk), lambda i: (i // per_b, i % per_b, 0, 0)))
    out_shape.append(jax.ShapeDtypeStruct((nb, B_KV * GATE_ROWS, seq), F32))
    out_specs.append(pl.BlockSpec((1, B_KV * GATE_ROWS, tm), lambda i: (i // per_b, 0, i % per_b)))
    return w, (col, t_width), tuple(groups), groups_t, out_shape, out_specs


def _odd_plan(w_in, nb, seq):
    cw = C_HEADS * HEAD_DIM
    col_scale = np.ones((1, 4 * cw), np.float32)
    col_scale[:, :cw] = Q_SCALE * LOG2E
    w = (w_in * col_scale).astype(BF16)
    groups = ((0, cw, "rope", 0), (cw, cw, "rope_mean", 1), (3 * cw, cw, "silu", 2))
    groups_t = ((0, cw, "plain", 3, MOBA_BLOCK),)
    rows, tm = nb * seq, ROW_TILE
    per_b = seq // tm
    widths = [(cw, BF16), (cw, BF16), (cw, BF16)]
    out_shape = [jax.ShapeDtypeStruct((rows, wd), dt) for wd, dt in widths]
    out_specs = [pl.BlockSpec((tm, wd), lambda i: (i, 0)) for wd, _ in widths]
    out_shape += [jax.ShapeDtypeStruct((nb, seq // MOBA_BLOCK, cw, MOBA_BLOCK), BF16),
                  jax.ShapeDtypeStruct((nb, seq // MOBA_BLOCK, 1, cw), F32)]
    out_specs += [pl.BlockSpec((1, tm // MOBA_BLOCK, cw, MOBA_BLOCK), lambda i: (i // per_b, i % per_b, 0, 0)),
                  pl.BlockSpec((1, tm // MOBA_BLOCK, 1, cw), lambda i: (i // per_b, i % per_b, 0, 0))]
    return w, (2 * cw, cw), groups, groups_t, out_shape, out_specs


def _rope_tables(seq):
    inv = ROPE_THETA ** (-jnp.arange(0, HEAD_DIM, 2, dtype=F32) / HEAD_DIM)
    ang = jnp.arange(seq, dtype=F32)[:, None] * inv[None, :]
    cos, sin = jnp.cos(ang), jnp.sin(ang)
    reps = PAIR // (HEAD_DIM // 2)
    sign = np.tile(np.concatenate([-np.ones(HEAD_DIM // 2), np.ones(HEAD_DIM // 2)]), PAIR // HEAD_DIM)
    return jnp.tile(cos, (1, reps)), jnp.tile(sin, (1, reps)) * jnp.asarray(sign, F32)[None, :]


def _overlap_matrix_t(seq, nch):
    nc = (seq - CMP_LEN) // CMP_STRIDE + 1
    nsb = seq // SLC_LEN
    cst = np.arange(nc) * CMP_STRIDE
    jj = np.arange(nsb)
    ov = ((cst[None, :] < (jj[:, None] + 1) * SLC_LEN) & (cst[None, :] + CMP_LEN > jj[:, None] * SLC_LEN))
    full = np.zeros((nsb, nch), np.float32)
    full[:, :nc] = ov
    return jnp.asarray(full, BF16), nc, nsb


def _cmp_weights(pe, w1, w2, transpose_out):
    pe2 = jnp.concatenate([pe, pe], axis=1)
    w1t = w1.reshape(CMP_LEN, HEAD_DIM, CMP_HIDDEN)
    zeros = jnp.zeros_like(w1t)
    w1p = jnp.concatenate([jnp.concatenate([w1t, zeros], axis=2), jnp.concatenate([zeros, w1t], axis=2)], axis=1)
    w1p = w1p.reshape(CMP_LEN // 2, 2 * PAIR, B_KV * CMP_HIDDEN)
    w2o = w2.T if transpose_out else jnp.concatenate([w2, w2], axis=1)
    return pe2, w1p.astype(BF16), w2o.astype(BF16)


def kernel(x, c, w_ada, b_ada, norm_g, w_in_even, a_sinks, cmp_pe_k, cmp_w1_k, cmp_w2_k, cmp_pe_v, cmp_w1_v, cmp_w2_v, w_out_even, w_in_odd, w_out_odd, final_g):
    nb, seq, d = x.shape
    depth = w_ada.shape[0]
    assert seq % ROW_TILE == 0 and seq % MOBA_BLOCK == 0 and NSA_TQ % SLC_LEN == 0
    assert (seq // SLC_LEN) % 8 == 0 and seq % (SWA_TQ * SWA_BLOCKS_PER_STEP) == 0 and ROW_TILE % NSA_TQ == 0
    cos_t, sin_t = _rope_tables(seq)
    ada = _ada(c, w_ada, b_ada)
    nch = seq // CMP_STRIDE
    ovl_t, nc, nsb = _overlap_matrix_t(seq, nch)
    fg = final_g.reshape(1, d)
    x2 = x.reshape(nb * seq, d)
    r3 = lambda a: a.reshape(nb, seq, a.shape[-1])
    prev = None
    for layer in range(depth):
        li = layer // 2
        ng = norm_g[layer].reshape(1, d)
        plan = _even_plan(w_in_even[li], nb, seq) if layer % 2 == 0 else _odd_plan(w_in_odd[li], nb, seq)
        w, t_cols, groups, groups_t, out_shape, out_specs = plan
        outs = _proj(x2, ada, layer, ng, cos_t, sin_t, w, t_cols, groups, groups_t, out_shape, out_specs, seq,
                     moba_mean=None if layer % 2 == 0 else 4, prev=prev)
        if prev is not None:
            x2, outs = outs[0], outs[1:]
        if layer % 2 == 0:
            q, kr, kc, vc, z, vt_a, vt_b, gates_t = outs
            q, kr, kc, vc, z = map(r3, (q, kr, kc, vc, z))
            kcmp, vcmp_t = _compress(kc, vc, *_cmp_weights(cmp_pe_k[li], cmp_w1_k[li], cmp_w2_k[li], False),
                                     *_cmp_weights(cmp_pe_v[li], cmp_w1_v[li], cmp_w2_v[li], True))
            oa = _swa(a_sinks[li], q, kr, vt_a, z, SWA_TQ)
            ob = _nsa(q, kcmp, vcmp_t, kr, vt_b, gates_t, z, ovl_t, nc, nsb, NSA_TQ)
            o_list = [oa.reshape(nb * seq, -1), ob.reshape(nb * seq, -1)]
            w_out = w_out_even[li].astype(BF16)
        else:
            q, k, z, vt, kmean = outs
            o = _moba(r3(q), r3(k), vt, kmean.reshape(nb, seq // MOBA_BLOCK, -1), r3(z))
            o_list = [o.reshape(nb * seq, -1)]
            w_out = w_out_odd[li].astype(BF16)
        prev = (o_list, w_out)
    x2 = _out_proj(x2, ada, depth - 1, prev[0], prev[1], fg, True, seq)
    return x2.reshape(nb, seq, d)
```
